```python
import jax, jax.numpy as jnp
from jax import lax
import numpy as np

D_MODEL = 4096
BATCH = 1
SEQ = 16384
DEPTH = 1
DEC_BATCH = 8
DEC_SEQ = 64
PAST_LEN = 1024

CHUNK = 64
MIX_W = D_MODEL
RW_W = MIX_W // 2
RW_HEAD = 64
RW_HEADS = RW_W // RW_HEAD
D_DECAY = 96
D_AAA = 96
D_GATE = 256
RW_COLS = 3 * RW_W + D_DECAY + D_AAA + D_GATE
SA_W = MIX_W - RW_W
SA_HEAD = 128
SA_HEADS = SA_W // SA_HEAD
SA_KV_HEADS = 4
IDX_HEADS = 16
IDX_DIM = 64
TOPK_MAX = 256
QBLOCK = 128
SA_COLS = SA_W + 2 * SA_KV_HEADS * SA_HEAD + IDX_HEADS * IDX_DIM + IDX_DIM + IDX_HEADS
IN_COLS = RW_COLS + SA_COLS
FFN_DIM = 2 * D_MODEL
CONV_W = 3
ROPE_THETA = 10000.0
NORM_EPS = 1e-6
LNX_EPS = 64e-5

kernel_name = 'hymba_rwkv7_dsa_convffn_stream_step'


def rms_norm(x, g):
    xf = x.astype(jnp.float32)
    y = xf * lax.rsqrt(jnp.mean(xf * xf, axis=-1, keepdims=True) + NORM_EPS)
    return (y * g.astype(jnp.float32)).astype(x.dtype)


def rope(x, pos):
    half = x.shape[-1] // 2
    inv = ROPE_THETA ** (-jnp.arange(half, dtype=jnp.float32) / half)
    ang = pos.astype(jnp.float32)[:, None] * inv[None, :]
    cos = jnp.cos(ang)[None, :, None, :]
    sin = jnp.sin(ang)[None, :, None, :]
    xf = x.astype(jnp.float32)
    x1, x2 = xf[..., :half], xf[..., half:]
    return jnp.concatenate([x1 * cos - x2 * sin, x2 * cos + x1 * sin], axis=-1).astype(x.dtype)


def split_cols(p, sizes):
    return jnp.split(p, np.cumsum(sizes)[:-1].tolist(), axis=-1)


def wkv7_scan(S0, r, decay, k, v, kk, b):
    def step(S, inp):
        r_t, d_t, k_t, v_t, kk_t, b_t = inp
        sa = jnp.einsum('bhvk,bhk->bhv', S, kk_t)
        S = S * d_t[:, :, None, :] - sa[..., None] * b_t[:, :, None, :] + v_t[..., None] * k_t[:, :, None, :]
        return S, jnp.einsum('bhvk,bhk->bhv', S, r_t)
    xs = tuple(jnp.moveaxis(t, 1, 0) for t in (r, decay, k, v, kk, b))
    S, y = lax.scan(step, S0, xs)
    return S, jnp.moveaxis(y, 0, 1)


def rwkv7_mixer(p_rw, shift0, wkv0, mu_shift, w_decay_up, decay_bias, w_a_up, a_bias, w_gate_up,
                k_k, k_a, r_k, lnx_w, lnx_b):
    f32 = jnp.float32
    B, T, _ = p_rw.shape
    prev = jnp.concatenate([shift0.astype(p_rw.dtype), p_rw[:, :-1]], axis=1)
    pm = p_rw + (prev - p_rw) * mu_shift
    r, wd, k, v, ad, gd = split_cols(pm, (RW_W, D_DECAY, RW_W, RW_W, D_AAA, D_GATE))
    w_log = -jax.nn.softplus(-(decay_bias + jnp.tanh(wd) @ w_decay_up).astype(f32)) - 0.5
    decay = jnp.exp(-jnp.exp(w_log))
    a = jax.nn.sigmoid((a_bias + ad @ w_a_up).astype(f32))
    g = (jax.nn.sigmoid(gd) @ w_gate_up).astype(f32)
    k = k.astype(f32)
    kk = (k * k_k.astype(f32)).reshape(B, T, RW_HEADS, RW_HEAD)
    kk = kk * lax.rsqrt(jnp.maximum(jnp.sum(kk * kk, axis=-1, keepdims=True), 1e-24))
    k = k * (1.0 + (a - 1.0) * k_a.astype(f32))
    hd = lambda t: t.astype(f32).reshape(B, T, RW_HEADS, RW_HEAD)
    r, decay, k, v, a = hd(r), hd(decay), hd(k), hd(v), hd(a)
    wkv1, y = wkv7_scan(wkv0.astype(f32), r, decay, k, v, kk, kk * a)
    mean = jnp.mean(y, axis=-1, keepdims=True)
    var = jnp.mean(jnp.square(y - mean), axis=-1, keepdims=True)
    y = ((y - mean) * lax.rsqrt(var + LNX_EPS)).reshape(B, T, RW_W) * lnx_w.astype(f32) + lnx_b.astype(f32)
    bonus = jnp.sum(r * k * r_k.astype(f32), axis=-1, keepdims=True) * v
    out = (y + bonus.reshape(B, T, RW_W)) * g
    return out.astype(p_rw.dtype), p_rw[:, -1:], wkv1.astype(wkv0.dtype)


def dsa_block(q, qi, wi, q_pos, k_all, v_all, ki_all, topk):
    f32 = jnp.float32
    B, Tq = q.shape[:2]
    L = k_all.shape[1]
    s = jnp.einsum('bthd,bsd->bths', qi.astype(f32), ki_all.astype(f32))
    score = jnp.einsum('bths,bth->bts', jax.nn.relu(s), wi.astype(f32))
    q_chunk = q_pos // CHUNK
    admissible = (jnp.arange(L) // CHUNK)[None, :] <= q_chunk[:, None]
    score = jnp.where(admissible[None], score, -jnp.inf)
    _, idx = lax.top_k(score, topk)
    valid = (idx // CHUNK) <= q_chunk[None, :, None]
    gather = jax.vmap(lambda t, i: t[i])
    kg = gather(k_all, idx)
    vg = gather(v_all, idx)
    qg = q.reshape(B, Tq, SA_KV_HEADS, SA_HEADS // SA_KV_HEADS, SA_HEAD)
    logits = jnp.einsum('btngd,btknd->btngk', qg, kg, preferred_element_type=f32) * (SA_HEAD ** -0.5)
    logits = jnp.where(valid[:, :, None, None, :], logits, -jnp.inf)
    p = jax.nn.softmax(logits, axis=-1)
    o = jnp.einsum('btngk,btknd->btngd', p.astype(vg.dtype), vg)
    return o.reshape(B, Tq, SA_W)


def dsa_mixer(p_sa, pos, past_k, past_v, past_ki):
    B, T, _ = p_sa.shape
    q, k, v, qi, ki, wi = split_cols(
        p_sa, (SA_W, SA_KV_HEADS * SA_HEAD, SA_KV_HEADS * SA_HEAD, IDX_HEADS * IDX_DIM, IDX_DIM, IDX_HEADS))
    q = rope(q.reshape(B, T, SA_HEADS, SA_HEAD), pos)
    k = rope(k.reshape(B, T, SA_KV_HEADS, SA_HEAD), pos)
    v = v.reshape(B, T, SA_KV_HEADS, SA_HEAD)
    qi = rope(qi.reshape(B, T, IDX_HEADS, IDX_DIM), pos)
    ki = rope(ki.reshape(B, T, 1, IDX_DIM), pos)[:, :, 0]
    wi = wi * ((IDX_HEADS * IDX_DIM) ** -0.5)
    k_all = jnp.concatenate([past_k.astype(k.dtype), k], axis=1)
    v_all = jnp.concatenate([past_v.astype(v.dtype), v], axis=1)
    ki_all = jnp.concatenate([past_ki.astype(ki.dtype), ki], axis=1)
    topk = min(TOPK_MAX, k_all.shape[1] // 4)
    if T % QBLOCK == 0:
        nb = T // QBLOCK
        blk = lambda t: jnp.moveaxis(t.reshape(B, nb, QBLOCK, *t.shape[2:]), 1, 0)
        o = lax.map(lambda xs: dsa_block(xs[0], xs[1], xs[2], xs[3], k_all, v_all, ki_all, topk),
                    (blk(q), blk(qi), blk(wi), pos.reshape(nb, QBLOCK)))
        o = jnp.moveaxis(o, 0, 1).reshape(B, T, SA_W)
    else:
        o = dsa_block(q, qi, wi, pos, k_all, v_all, ki_all, topk)
    return o, k, v, ki


def stream_layer(x, pos, past_k, past_v, past_ki, wkv0, shift0, conv0,
                 norm_mix_pre, norm_mix_post, norm_ffn_pre, norm_ffn_post, w_in, mu_shift,
                 w_decay_up, decay_bias, w_a_up, a_bias, w_gate_up, k_k, k_a, r_k, lnx_w, lnx_b,
                 w_out, w_up, conv_w, conv_b, w_down):
    T = x.shape[1]
    h = rms_norm(x, norm_mix_pre)
    p = h @ w_in
    o_rw, shift1, wkv1 = rwkv7_mixer(p[..., :RW_COLS], shift0, wkv0, mu_shift, w_decay_up, decay_bias,
                                     w_a_up, a_bias, w_gate_up, k_k, k_a, r_k, lnx_w, lnx_b)
    o_sa, k_new, v_new, ki_new = dsa_mixer(p[..., RW_COLS:], pos, past_k, past_v, past_ki)
    mix = jnp.concatenate([o_rw, o_sa.astype(o_rw.dtype)], axis=-1) @ w_out
    x = x + rms_norm(mix, norm_mix_post)
    h = rms_norm(x, norm_ffn_pre)
    u = h @ w_up
    u_pad = jnp.concatenate([conv0.astype(u.dtype), u], axis=1)
    uc = sum(u_pad[:, j:j + T] * conv_w[j] for j in range(CONV_W)) + conv_b
    gate, val = jnp.split(uc, 2, axis=-1)
    f = (jax.nn.gelu(gate, approximate=True) * val) @ w_down
    x = x + rms_norm(f, norm_ffn_post)
    return x, (k_new, v_new, ki_new, wkv1, shift1, u_pad[:, -(CONV_W - 1):])


def run_stream(x, past_k, past_v, past_ki, wkv0, shift0, conv0, weights):
    pos = past_k.shape[2] + jnp.arange(x.shape[1], dtype=jnp.int32)
    outs = []
    for l in range(DEPTH):
        lw = [w[l] for w in weights]
        x, st = stream_layer(x, pos, past_k[l], past_v[l], past_ki[l], wkv0[l], shift0[l], conv0[l], *lw)
        outs.append(st)
    return x, [jnp.stack(s) for s in zip(*outs)]


def setup_inputs(seed: int = 0) -> dict:
    key = jax.random.key(seed)
    ks = iter(jax.random.split(key, 40))
    nrm = lambda shape, scale: scale * jax.random.normal(next(ks), shape, jnp.float32)
    Dp = DEPTH
    return {
        'x_prompt': nrm((BATCH, SEQ, D_MODEL), 1.0),
        'x_sample': nrm((DEC_BATCH, DEC_SEQ, D_MODEL), 1.0),
        'cache_k': nrm((Dp, DEC_BATCH, PAST_LEN, SA_KV_HEADS, SA_HEAD), 1.0),
        'cache_v': nrm((Dp, DEC_BATCH, PAST_LEN, SA_KV_HEADS, SA_HEAD), 1.0),
        'cache_kidx': nrm((Dp, DEC_BATCH, PAST_LEN, IDX_DIM), 1.0),
        'state_wkv': nrm((Dp, DEC_BATCH, RW_HEADS, RW_HEAD, RW_HEAD), 0.1),
        'state_shift': nrm((Dp, DEC_BATCH, 1, RW_COLS), 1.0),
        'state_conv': nrm((Dp, DEC_BATCH, CONV_W - 1, 2 * FFN_DIM), 1.0),
        'norm_mix_pre': 1.0 + nrm((Dp, D_MODEL), 0.02),
        'norm_mix_post': 1.0 + nrm((Dp, D_MODEL), 0.02),
        'norm_ffn_pre': 1.0 + nrm((Dp, D_MODEL), 0.02),
        'norm_ffn_post': 1.0 + nrm((Dp, D_MODEL), 0.02),
        'w_in': nrm((Dp, D_MODEL, IN_COLS), D_MODEL ** -0.5),
        'mu_shift': jax.random.uniform(next(ks), (Dp, RW_COLS), jnp.float32),
        'w_decay_up': nrm((Dp, D_DECAY, RW_W), D_DECAY ** -0.5),
        'decay_bias': nrm((Dp, RW_W), 0.5),
        'w_a_up': nrm((Dp, D_AAA, RW_W), D_AAA ** -0.5),
        'a_bias': nrm((Dp, RW_W), 0.1),
        'w_gate_up': nrm((Dp, D_GATE, RW_W), D_GATE ** -0.5),
        'k_k': 1.0 + nrm((Dp, RW_W), 0.1),
        'k_a': 1.0 + nrm((Dp, RW_W), 0.1),
        'r_k': nrm((Dp, RW_HEADS, RW_HEAD), 0.1),
        'lnx_w': 1.0 + nrm((Dp, RW_W), 0.02),
        'lnx_b': nrm((Dp, RW_W), 0.02),
        'w_out': nrm((Dp, MIX_W, D_MODEL), MIX_W ** -0.5),
        'w_up': nrm((Dp, D_MODEL, 2 * FFN_DIM), D_MODEL ** -0.5),
        'conv_w': nrm((Dp, CONV_W, 2 * FFN_DIM), CONV_W ** -0.5),
        'conv_b': nrm((Dp, 2 * FFN_DIM), 0.02),
        'w_down': nrm((Dp, FFN_DIM, D_MODEL), FFN_DIM ** -0.5),
    }


def reference(x_prompt, x_sample, cache_k, cache_v, cache_kidx, state_wkv, state_shift, state_conv,
              norm_mix_pre, norm_mix_post, norm_ffn_pre, norm_ffn_post, w_in, mu_shift,
              w_decay_up, decay_bias, w_a_up, a_bias, w_gate_up, k_k, k_a, r_k, lnx_w, lnx_b,
              w_out, w_up, conv_w, conv_b, w_down):
    weights = (norm_mix_pre, norm_mix_post, norm_ffn_pre, norm_ffn_post, w_in, mu_shift,
               w_decay_up, decay_bias, w_a_up, a_bias, w_gate_up, k_k, k_a, r_k, lnx_w, lnx_b,
               w_out, w_up, conv_w, conv_b, w_down)
    dt = x_prompt.dtype
    B = x_prompt.shape[0]
    zk = jnp.zeros((DEPTH, B, 0, SA_KV_HEADS, SA_HEAD), dt)
    zki = jnp.zeros((DEPTH, B, 0, IDX_DIM), dt)
    zwkv = jnp.zeros((DEPTH, B, RW_HEADS, RW_HEAD, RW_HEAD), dt)
    zshift = jnp.zeros((DEPTH, B, 1, RW_COLS), dt)
    zconv = jnp.zeros((DEPTH, B, CONV_W - 1, 2 * FFN_DIM), dt)
    y_prompt, p_new = run_stream(x_prompt, zk, zk, zki, zwkv, zshift, zconv, weights)
    p_k, p_v, p_kidx, p_wkv, p_shift, p_conv = p_new
    y_sample, s_new = run_stream(x_sample, cache_k, cache_v, cache_kidx, state_wkv, state_shift, state_conv, weights)
    s_k, s_v, s_kidx, s_wkv, s_shift, s_conv = s_new
    return (y_prompt, y_sample, p_k, p_v, p_kidx, p_wkv, p_shift, p_conv,
            s_k, s_v, s_kidx, s_wkv, s_shift, s_conv)
```

```python
import functools

import numpy as np
import jax
import jax.numpy as jnp
from jax import lax
from jax.experimental import pallas as pl
from jax.experimental.pallas import tpu as pltpu

F32 = jnp.float32
MXU_DTYPE = jnp.bfloat16
HI = lax.Precision.HIGHEST

CHUNK = 64
RW_HEAD = 64
D_DECAY = 96
D_AAA = 96
D_GATE = 256
SA_HEAD = 128
SA_KV_HEADS = 4
IDX_HEADS = 16
IDX_DIM = 64
TOPK_MAX = 256
CONV_W = 3
ROPE_THETA = 10000.0
NORM_EPS = 1e-6
LNX_EPS = 64e-5

LANES = 128
V7X_VMEM_LIMIT_BYTES = 60000 * 1024
INT_MIN = -(2 ** 31)
NEG_BIG = -1e30


def _cparams(n_grid, vmem_bytes):
    limit = int(min(V7X_VMEM_LIMIT_BYTES, max(32 * 1024 * 1024, vmem_bytes)))
    return pltpu.CompilerParams(dimension_semantics=("arbitrary",) * n_grid, vmem_limit_bytes=limit)


def _nt(a, b, precision=None):
    return lax.dot_general(a, b, (((1,), (1,)), ((), ())), precision=precision, preferred_element_type=F32)


def _tn(a, b, precision=None):
    return lax.dot_general(a, b, (((0,), (0,)), ((), ())), precision=precision, preferred_element_type=F32)


def _mx(x):
    return x.astype(MXU_DTYPE)


def _rmsnorm_kernel(x_ref, g_ref, o_ref):
    x = x_ref[...]
    y = x * lax.rsqrt(jnp.mean(x * x, axis=-1, keepdims=True) + NORM_EPS)
    o_ref[...] = (y * g_ref[...]).astype(o_ref.dtype)


def _rmsnorm(x, g, tm):
    m, d = x.shape
    return pl.pallas_call(
        _rmsnorm_kernel,
        grid=(m // tm,),
        in_specs=[pl.BlockSpec((tm, d), lambda i: (i, 0)), pl.BlockSpec((1, d), lambda i: (0, 0))],
        out_specs=pl.BlockSpec((tm, d), lambda i: (i, 0)),
        out_shape=jax.ShapeDtypeStruct((m, d), MXU_DTYPE),
        compiler_params=_cparams(1, 6 * tm * d * 4),
        name="rmsnorm",
    )(x, g)


def _mm_kernel(a_ref, b_ref, o_ref):
    o_ref[...] = jnp.dot(a_ref[...], b_ref[...], preferred_element_type=F32)


def _matmul(a, b, tm, tn):
    m, k = a.shape
    n = b.shape[1]
    isz = jnp.dtype(MXU_DTYPE).itemsize
    vmem = 2 * (tm * k * isz + k * tn * isz + tm * tn * 4) + 2 * tm * tn * 4 + (8 << 20)
    return pl.pallas_call(
        _mm_kernel,
        grid=(m // tm, n // tn),
        in_specs=[pl.BlockSpec((tm, k), lambda i, j: (i, 0)), pl.BlockSpec((k, tn), lambda i, j: (0, j))],
        out_specs=pl.BlockSpec((tm, tn), lambda i, j: (i, j)),
        out_shape=jax.ShapeDtypeStruct((m, n), F32),
        compiler_params=_cparams(2, vmem),
        name="in_proj",
    )(a, b)


def _mm_norm_kernel(a_ref, w_ref, x_ref, gpost_ref, gnext_ref, *rest, nk, emit_next):
    if emit_next:
        x1_ref, h_ref, acc_ref = rest
    else:
        x1_ref, acc_ref = rest
    k = pl.program_id(1)

    @pl.when(k == 0)
    def _():
        acc_ref[...] = jnp.zeros_like(acc_ref)

    acc_ref[...] += jnp.dot(a_ref[...], w_ref[...], preferred_element_type=F32)

    @pl.when(k == nk - 1)
    def _():
        y = acc_ref[...]
        y = y * lax.rsqrt(jnp.mean(y * y, axis=-1, keepdims=True) + NORM_EPS) * gpost_ref[...]
        x1 = x_ref[...] + y
        x1_ref[...] = x1
        if emit_next:
            h = x1 * lax.rsqrt(jnp.mean(x1 * x1, axis=-1, keepdims=True) + NORM_EPS) * gnext_ref[...]
            h_ref[...] = h.astype(h_ref.dtype)


def _matmul_norm_residual(a, w, x, g_post, g_next, tm, tk, emit_next, name):
    m, kdim = a.shape
    d = w.shape[1]
    nk = kdim // tk
    isz = jnp.dtype(MXU_DTYPE).itemsize
    out_shape = [jax.ShapeDtypeStruct((m, d), F32)]
    out_specs = [pl.BlockSpec((tm, d), lambda i, k: (i, 0))]
    if emit_next:
        out_shape.append(jax.ShapeDtypeStruct((m, d), MXU_DTYPE))
        out_specs.append(pl.BlockSpec((tm, d), lambda i, k: (i, 0)))
    vmem = 2 * (tm * tk * isz + tk * d * isz + 2 * tm * d * 4 + tm * d * isz) + 3 * tm * d * 4 + (4 << 20)
    res = pl.pallas_call(
        functools.partial(_mm_norm_kernel, nk=nk, emit_next=emit_next),
        grid=(m // tm, nk),
        in_specs=[
            pl.BlockSpec((tm, tk), lambda i, k: (i, k)),
            pl.BlockSpec((tk, d), lambda i, k: (k, 0)),
            pl.BlockSpec((tm, d), lambda i, k: (i, 0)),
            pl.BlockSpec((1, d), lambda i, k: (0, 0)),
            pl.BlockSpec((1, d), lambda i, k: (0, 0)),
        ],
        out_specs=out_specs,
        out_shape=out_shape,
        scratch_shapes=[pltpu.VMEM((tm, d), F32)],
        compiler_params=_cparams(2, vmem),
        name=name,
    )(a, w, x, g_post, g_next)
    return res if emit_next else (res[0], None)


def _gelu_tanh(x):
    return 0.5 * x * (1.0 + jnp.tanh(0.7978845608028654 * (x + 0.044715 * x * x * x)))


def _ffn_up_kernel(h_ref, wg_ref, wv_ref, c0g_ref, c0v_ref, cwg_ref, cwv_ref, cbg_ref, cbv_ref,
                   a_ref, cg_ref, cv_ref, carry_g, carry_v, *, blocks_per_stream, tm):
    i = pl.program_id(1) % blocks_per_stream

    @pl.when(i == 0)
    def _():
        carry_g[...] = c0g_ref[...]
        carry_v[...] = c0v_ref[...]

    h = h_ref[...]
    row = lax.broadcasted_iota(jnp.int32, (tm, 1), 0)

    def conv(u, carry_ref, cw_ref, cb_ref, out_ref):
        p = carry_ref[...]
        u1 = jnp.where(row == 0, p[1:2], pltpu.roll(u, 1, axis=0))
        u2 = jnp.where(row == 0, p[0:1], jnp.where(row == 1, p[1:2], pltpu.roll(u, 2, axis=0)))
        cw = cw_ref[...]
        last = u[tm - 2:tm]
        carry_ref[...] = last
        out_ref[...] = last
        return u2 * cw[0:1] + u1 * cw[1:2] + u * cw[2:3] + cb_ref[...]

    gate = conv(jnp.dot(h, wg_ref[...], preferred_element_type=F32), carry_g, cwg_ref, cbg_ref, cg_ref)
    val = conv(jnp.dot(h, wv_ref[...], preferred_element_type=F32), carry_v, cwv_ref, cbv_ref, cv_ref)
    a_ref[...] = (_gelu_tanh(gate) * val).astype(a_ref.dtype)


def _ffn_up(h, w_up, conv0, conv_w, conv_b, n_streams, tm, tn):
    m, d = h.shape
    f2 = w_up.shape[1]
    f = f2 // 2
    nj = f // tn
    nr = m // tm
    bps = nr // n_streams
    isz = jnp.dtype(MXU_DTYPE).itemsize
    vmem = 2 * (tm * d * isz + 2 * d * tn * isz + tm * tn * isz) + 10 * tm * tn * 4 + (4 << 20)
    col = lambda off: (lambda j, r: (0, off + j))
    st = lambda off: (lambda j, r: (r // bps, 0, off + j))
    a, cg, cv = pl.pallas_call(
        functools.partial(_ffn_up_kernel, blocks_per_stream=bps, tm=tm),
        grid=(nj, nr),
        in_specs=[
            pl.BlockSpec((tm, d), lambda j, r: (r, 0)),
            pl.BlockSpec((d, tn), col(0)),
            pl.BlockSpec((d, tn), col(nj)),
            pl.BlockSpec((None, CONV_W - 1, tn), st(0)),
            pl.BlockSpec((None, CONV_W - 1, tn), st(nj)),
            pl.BlockSpec((CONV_W, tn), col(0)),
            pl.BlockSpec((CONV_W, tn), col(nj)),
            pl.BlockSpec((1, tn), col(0)),
            pl.BlockSpec((1, tn), col(nj)),
        ],
        out_specs=[
            pl.BlockSpec((tm, tn), lambda j, r: (r, j)),
            pl.BlockSpec((None, CONV_W - 1, tn), st(0)),
            pl.BlockSpec((None, CONV_W - 1, tn), st(0)),
        ],
        out_shape=[
            jax.ShapeDtypeStruct((m, f), MXU_DTYPE),
            jax.ShapeDtypeStruct((n_streams, CONV_W - 1, f), F32),
            jax.ShapeDtypeStruct((n_streams, CONV_W - 1, f), F32),
        ],
        scratch_shapes=[pltpu.VMEM((CONV_W - 1, tn), F32), pltpu.VMEM((CONV_W - 1, tn), F32)],
        compiler_params=_cparams(2, vmem),
        name="ffn_up_conv",
    )(h, w_up, w_up, conv0, conv0, conv_w, conv_w, conv_b, conv_b)
    return a, jnp.concatenate([cg, cv], axis=-1)


RW_GROUP = 4 * LANES


def _softplus(z):
    return jnp.maximum(z, 0.0) + jnp.log(1.0 + jnp.exp(-jnp.abs(z)))


def _sigmoid(z):
    return 1.0 / (1.0 + jnp.exp(-z))


def _wkv_pair(r, k, v, kk, b, logd, s_bd, consts):
    m0, m1, tril_incl, strict, incl, eye = consts
    c = r.shape[0]
    cum = jnp.dot(tril_incl, logd, precision=HI, preferred_element_type=F32)
    cum_end = cum[c - 1:c]
    e_inc = jnp.exp(cum)
    e_prev = jnp.exp(cum - logd)
    e_neg = jnp.exp(-cum)
    e_tail = jnp.exp(cum_end - cum)
    stack = lambda x: jnp.concatenate([x * m0, x * m1], axis=0)
    kks, rs = stack(kk * e_prev), stack(r * e_inc)
    ks, bs = stack(k * e_neg), stack(b * e_neg)
    khs, bhs = stack(k * e_tail), stack(b * e_tail)
    vs = stack(v)
    zero = jnp.zeros((), F32)
    a_kk = jnp.where(strict, _nt(kks, ks, HI), zero)
    low = jnp.where(strict, _nt(kks, bs, HI), zero)
    a_rk = jnp.where(incl, _nt(rs, ks, HI), zero)
    a_rb = jnp.where(incl, _nt(rs, bs, HI), zero)
    pw = -low
    tinv = eye + pw
    n_sq = int(np.log2(c)) - 1
    for _ in range(n_sq):
        pw = jnp.dot(pw, pw, precision=HI, preferred_element_type=F32)
        tinv = tinv + jnp.dot(tinv, pw, precision=HI, preferred_element_type=F32)
    rhs = _nt(kks, s_bd, HI) + jnp.dot(a_kk, vs, precision=HI, preferred_element_type=F32)
    u = jnp.dot(tinv, rhs, precision=HI, preferred_element_type=F32)
    ys = (_nt(rs, s_bd, HI) + jnp.dot(a_rk, vs, precision=HI, preferred_element_type=F32)
          - jnp.dot(a_rb, u, precision=HI, preferred_element_type=F32))
    y = ys[:c] + ys[c:]
    s_new = s_bd * jnp.exp(cum_end) + _tn(vs, khs, HI) - _tn(u, bhs, HI)
    return y, s_new


def _rwkv_kernel(pr_ref, pk_ref, pv_ref, ps_ref, sr_ref, sk_ref, sv_ref, ss_ref, wkv0_ref,
                 mur_ref, muk_ref, muv_ref, mus_ref, wdec_ref, dbias_ref, wa_ref, abias_ref, wg_ref,
                 kk_ref, ka_ref, rk_ref, lnw_ref, lnb_ref,
                 o_ref, wkv1_ref,
                 cr_ref, ck_ref, cv_ref, cs_ref, state_ref, *, n_chunks):
    c = pl.program_id(2)
    C = CHUNK
    npairs = RW_GROUP // LANES

    lane = lax.broadcasted_iota(jnp.int32, (1, LANES), 1)
    m0 = (lane < RW_HEAD).astype(F32)
    m1 = 1.0 - m0
    ri = lax.broadcasted_iota(jnp.int32, (2 * C, 2 * C), 0)
    ci = lax.broadcasted_iota(jnp.int32, (2 * C, 2 * C), 1)
    same = (ri // C) == (ci // C)
    strict = same & ((ri % C) > (ci % C))
    incl = same & ((ri % C) >= (ci % C))
    eye = (ri == ci).astype(F32)
    ti = lax.broadcasted_iota(jnp.int32, (C, C), 0)
    tj = lax.broadcasted_iota(jnp.int32, (C, C), 1)
    tril_incl = (ti >= tj).astype(F32)
    seg = ((ri // RW_HEAD) == (ci // RW_HEAD)).astype(F32)
    consts = (m0, m1, tril_incl, strict, incl, eye)

    @pl.when(c == 0)
    def _():
        cr_ref[...] = sr_ref[...]
        ck_ref[...] = sk_ref[...]
        cv_ref[...] = sv_ref[...]
        cs_ref[...] = ss_ref[...]
        z = jnp.zeros((RW_HEAD, RW_HEAD), F32)
        for p in range(npairs):
            s0 = wkv0_ref[2 * p]
            s1 = wkv0_ref[2 * p + 1]
            state_ref[p] = jnp.concatenate(
                [jnp.concatenate([s0, z], axis=1), jnp.concatenate([z, s1], axis=1)], axis=0)

    row = lax.broadcasted_iota(jnp.int32, (C, 1), 0)

    def shifted(p_ref, carry_ref, mu_ref):
        p = p_ref[...]
        prev = jnp.where(row == 0, carry_ref[...], pltpu.roll(p, 1, axis=0))
        carry_ref[...] = p[C - 1:C]
        return p + (prev - p) * mu_ref[...]

    r = shifted(pr_ref, cr_ref, mur_ref)
    k = shifted(pk_ref, ck_ref, muk_ref)
    v = shifted(pv_ref, cv_ref, muv_ref)
    sm = shifted(ps_ref, cs_ref, mus_ref)
    wd, ad, gd = sm[:, 0:LANES], sm[:, LANES:2 * LANES], sm[:, 2 * LANES:4 * LANES]

    dec_in = dbias_ref[...] + jnp.dot(_mx(jnp.tanh(wd)), wdec_ref[...], preferred_element_type=F32)
    w_log = -_softplus(-dec_in) - 0.5
    logd = -jnp.exp(w_log)
    a = _sigmoid(abias_ref[...] + jnp.dot(_mx(ad), wa_ref[...], preferred_element_type=F32))
    g = jnp.dot(_mx(_sigmoid(gd)), wg_ref[...], preferred_element_type=F32)
    kk = k * kk_ref[...]
    k2 = k * (1.0 + (a - 1.0) * ka_ref[...])
    rkr = r * k2 * rk_ref[...]

    for p in range(npairs):
        sl = slice(p * LANES, (p + 1) * LANES)
        kkp = kk[:, sl]
        nrm = jnp.dot(kkp * kkp, seg, precision=HI, preferred_element_type=F32)
        kkp = kkp * lax.rsqrt(jnp.maximum(nrm, 1e-24))
        y, s_new = _wkv_pair(r[:, sl], k2[:, sl], v[:, sl], kkp, kkp * a[:, sl], logd[:, sl],
                             state_ref[p], consts)
        state_ref[p] = s_new
        mean = jnp.dot(y, seg, precision=HI, preferred_element_type=F32) * (1.0 / RW_HEAD)
        yc = y - mean
        var = jnp.dot(yc * yc, seg, precision=HI, preferred_element_type=F32) * (1.0 / RW_HEAD)
        yn = yc * lax.rsqrt(var + LNX_EPS) * lnw_ref[:, sl] + lnb_ref[:, sl]
        bonus = jnp.dot(rkr[:, sl], seg, precision=HI, preferred_element_type=F32) * v[:, sl]
        o_ref[:, sl] = ((yn + bonus) * g[:, sl]).astype(o_ref.dtype)

    @pl.when(c == n_chunks - 1)
    def _():
        for p in range(npairs):
            s = state_ref[p]
            wkv1_ref[2 * p] = s[:RW_HEAD, :RW_HEAD]
            wkv1_ref[2 * p + 1] = s[RW_HEAD:, RW_HEAD:]


def _rwkv(pfull, cols, shift0, wkv0, w, n_streams, t):
    m = pfull.shape[0]
    rw = w["k_k"].shape[1]
    ng = rw // RW_GROUP
    nc = t // CHUNK
    hg = RW_GROUP // RW_HEAD
    rowblk = lambda off: (lambda b, g, c: (b * nc + c, off + g))
    fixed = lambda off: (lambda b, g, c: (b * nc + c, off))
    st = lambda b, g, c: (b, 0, g)
    st0 = lambda b, g, c: (b, 0, 0)
    wcol = lambda b, g, c: (0, g)
    in_specs = [
        pl.BlockSpec((CHUNK, RW_GROUP), rowblk(cols["r"])),
        pl.BlockSpec((CHUNK, RW_GROUP), rowblk(cols["k"])),
        pl.BlockSpec((CHUNK, RW_GROUP), rowblk(cols["v"])),
        pl.BlockSpec((CHUNK, RW_GROUP), fixed(cols["small"])),
        pl.BlockSpec((None, 1, RW_GROUP), st),
        pl.BlockSpec((None, 1, RW_GROUP), st),
        pl.BlockSpec((None, 1, RW_GROUP), st),
        pl.BlockSpec((None, 1, RW_GROUP), st0),
        pl.BlockSpec((None, hg, RW_HEAD, RW_HEAD), lambda b, g, c: (b, g, 0, 0)),
        pl.BlockSpec((1, RW_GROUP), wcol), pl.BlockSpec((1, RW_GROUP), wcol), pl.BlockSpec((1, RW_GROUP), wcol),
        pl.BlockSpec((1, RW_GROUP), lambda b, g, c: (0, 0)),
        pl.BlockSpec((LANES, RW_GROUP), wcol), pl.BlockSpec((1, RW_GROUP), wcol),
        pl.BlockSpec((LANES, RW_GROUP), wcol), pl.BlockSpec((1, RW_GROUP), wcol),
        pl.BlockSpec((2 * LANES, RW_GROUP), wcol),
        pl.BlockSpec((1, RW_GROUP), wcol), pl.BlockSpec((1, RW_GROUP), wcol), pl.BlockSpec((1, RW_GROUP), wcol),
        pl.BlockSpec((1, RW_GROUP), wcol), pl.BlockSpec((1, RW_GROUP), wcol),
    ]
    o, wkv1 = pl.pallas_call(
        functools.partial(_rwkv_kernel, n_chunks=nc),
        grid=(n_streams, ng, nc),
        in_specs=in_specs,
        out_specs=[
            pl.BlockSpec((CHUNK, RW_GROUP), lambda b, g, c: (b * nc + c, g)),
            pl.BlockSpec((None, hg, RW_HEAD, RW_HEAD), lambda b, g, c: (b, g, 0, 0)),
        ],
        out_shape=[
            jax.ShapeDtypeStruct((m, rw), MXU_DTYPE),
            jax.ShapeDtypeStruct((n_streams, rw // RW_HEAD, RW_HEAD, RW_HEAD), F32),
        ],
        scratch_shapes=[pltpu.VMEM((1, RW_GROUP), F32)] * 4
        + [pltpu.VMEM((RW_GROUP // LANES, LANES, LANES), F32)],
        compiler_params=_cparams(3, 32 << 20),
        name="rwkv7_chunked",
    )(pfull, pfull, pfull, pfull, shift0["r"], shift0["k"], shift0["v"], shift0["small"], wkv0,
      w["mu_r"], w["mu_k"], w["mu_v"], w["mu_small"], w["w_decay_up"], w["decay_bias"], w["w_a_up"], w["a_bias"],
      w["w_gate_up"], w["k_k"], w["k_a"], w["r_k"], w["lnx_w"], w["lnx_b"])
    return o, wkv1


def _rope_kernel(q_ref, k_ref, qi_ref, kiw_ref, ca_ref, sa_ref, cb_ref, sb1_ref, sb2_ref,
                 qo_ref, ko_ref, qio_ref, kiwo_ref):
    ca, sa = ca_ref[...], sa_ref[...]
    cb, sb1, sb2 = cb_ref[...], sb1_ref[...], sb2_ref[...]

    def rope_head(x):
        return x * ca + pltpu.roll(x, SA_HEAD // 2, axis=1) * sa

    def rope_idx(x):
        return (x * cb + pltpu.roll(x, LANES - IDX_DIM // 2, axis=1) * sb1
                + pltpu.roll(x, IDX_DIM // 2, axis=1) * sb2)

    for h in range(q_ref.shape[1] // LANES):
        sl = slice(h * LANES, (h + 1) * LANES)
        qo_ref[:, sl] = rope_head(q_ref[:, sl]).astype(qo_ref.dtype)
    for h in range(k_ref.shape[1] // LANES):
        sl = slice(h * LANES, (h + 1) * LANES)
        ko_ref[:, sl] = rope_head(k_ref[:, sl])
    for hp in range(qi_ref.shape[1] // LANES):
        y = rope_idx(qi_ref[:, hp * LANES:(hp + 1) * LANES]).astype(qio_ref.dtype)
        qio_ref[2 * hp] = y[:, :IDX_DIM]
        qio_ref[2 * hp + 1] = y[:, IDX_DIM:]
    x = kiw_ref[...]
    lane = lax.broadcasted_iota(jnp.int32, (1, LANES), 1)
    kiwo_ref[...] = jnp.where(lane < IDX_DIM, rope_idx(x), x * float((IDX_HEADS * IDX_DIM) ** -0.5))


def _rope_tables(pos):
    def ang(half):
        inv = ROPE_THETA ** (-jnp.arange(half, dtype=F32) / half)
        return pos.astype(F32)[:, None] * inv[None, :]
    aa = ang(SA_HEAD // 2)
    ca = jnp.concatenate([jnp.cos(aa), jnp.cos(aa)], axis=1)
    sa = jnp.concatenate([-jnp.sin(aa), jnp.sin(aa)], axis=1)
    ab = ang(IDX_DIM // 2)
    z = jnp.zeros_like(ab)
    cb = jnp.concatenate([jnp.cos(ab)] * 4, axis=1)
    sb1 = jnp.concatenate([-jnp.sin(ab), z, -jnp.sin(ab), z], axis=1)
    sb2 = jnp.concatenate([z, jnp.sin(ab), z, jnp.sin(ab)], axis=1)
    return ca, sa, cb, sb1, sb2


def _rope(pfull, cols, tables, n_streams, t, tm, sa_w, kv_w, qi_w):
    m = pfull.shape[0]
    bps = t // tm
    tab = pl.BlockSpec((tm, LANES), lambda r: (r % bps, 0))
    return pl.pallas_call(
        _rope_kernel,
        grid=(m // tm,),
        in_specs=[
            pl.BlockSpec((tm, sa_w), lambda r: (r, cols["q"])),
            pl.BlockSpec((tm, kv_w), lambda r: (r, cols["ksa"])),
            pl.BlockSpec((tm, qi_w), lambda r: (r, cols["qi"])),
            pl.BlockSpec((tm, LANES), lambda r: (r, cols["kiw"])),
            tab, tab, tab, tab, tab,
        ],
        out_specs=[
            pl.BlockSpec((tm, sa_w), lambda r: (r, 0)),
            pl.BlockSpec((tm, kv_w), lambda r: (r, 0)),
            pl.BlockSpec((None, IDX_HEADS, tm, IDX_DIM), lambda r: (r // bps, 0, r % bps, 0)),
            pl.BlockSpec((tm, LANES), lambda r: (r, 0)),
        ],
        out_shape=[
            jax.ShapeDtypeStruct((m, sa_w), MXU_DTYPE),
            jax.ShapeDtypeStruct((m, kv_w), F32),
            jax.ShapeDtypeStruct((n_streams, IDX_HEADS, t, IDX_DIM), MXU_DTYPE),
            jax.ShapeDtypeStruct((m, LANES), F32),
        ],
        compiler_params=_cparams(1, 32 << 20),
        name="rope",
    )(pfull, pfull, pfull, pfull, *tables)


def _index_kernel(qi_ref, w_ref, kit_ref, mask_ref, keys_ref, wb_ref, *, tq, kb, nkb, past, topk):
    i = pl.program_id(1)
    n_adm = jnp.minimum(nkb, (past + (i + 1) * tq + kb - 1) // kb)
    row = lax.broadcasted_iota(jnp.int32, (tq, 1), 0)
    chunk_bits = CHUNK.bit_length() - 1
    lim = (((past + i * tq + row) >> chunk_bits) + 1) << chunk_bits
    lane = lax.broadcasted_iota(jnp.int32, (1, kb), 1)
    ncol = kb // LANES

    w = w_ref[...]
    for h in range(IDX_HEADS):
        wb_ref[h] = jnp.broadcast_to(w[:, IDX_DIM + h:IDX_DIM + h + 1], (tq, LANES))

    def score_block(j, carry):
        kt = kit_ref[j]
        acc = [jnp.zeros((tq, LANES), F32) for _ in range(ncol)]
        for h in range(IDX_HEADS):
            s = jnp.dot(qi_ref[h], kt, preferred_element_type=F32)
            wb = wb_ref[h]
            for cidx in range(ncol):
                acc[cidx] = acc[cidx] + jnp.maximum(s[:, cidx * LANES:(cidx + 1) * LANES], 0.0) * wb
        score = jnp.concatenate(acc, axis=1) + 0.0
        bits = lax.bitcast_convert_type(score, jnp.int32)
        key = bits ^ ((bits >> 31) & jnp.int32(0x7FFFFFFF))
        keys_ref[j] = jnp.where(j * kb + lane < lim, key, jnp.int32(INT_MIN))
        return carry

    lax.fori_loop(0, n_adm, score_block, 0)

    def bisect(p, prefix):
        cand = prefix + lax.shift_left(jnp.int32(1), 31 - p)

        def count(j, cnt):
            x = jnp.where(keys_ref[j] >= cand, 1.0, 0.0)
            for cidx in range(ncol):
                cnt = cnt + x[:, cidx * LANES:(cidx + 1) * LANES]
            return cnt

        cnt = lax.fori_loop(0, n_adm, count, jnp.zeros((tq, LANES), F32))
        total = jnp.sum(cnt, axis=1, keepdims=True)
        return jnp.where(total >= float(topk), cand, prefix)

    thr = lax.fori_loop(0, 32, bisect, jnp.full((tq, 1), INT_MIN, jnp.int32))
    thr = jnp.maximum(thr, jnp.int32(INT_MIN + 1))

    def write(j, carry):
        mask_ref[j] = jnp.where(keys_ref[j] >= thr, 1, 0).astype(mask_ref.dtype)
        return carry

    lax.fori_loop(0, n_adm, write, 0)

    def clear(j, carry):
        mask_ref[j] = jnp.zeros((tq, kb), mask_ref.dtype)
        return carry

    lax.fori_loop(n_adm, nkb, clear, 0)


def _index_mask(qi, w, kit, n_streams, t, tq, kb, past, topk):
    nkb = kit.shape[1]
    nq = t // tq
    vmem = nkb * tq * kb * 4 + 2 * nkb * tq * kb + 2 * nkb * IDX_DIM * kb * 2 + IDX_HEADS * tq * LANES * 4 * 3 + (8 << 20)
    return pl.pallas_call(
        functools.partial(_index_kernel, tq=tq, kb=kb, nkb=nkb, past=past, topk=topk),
        grid=(n_streams, nq),
        in_specs=[
            pl.BlockSpec((None, IDX_HEADS, tq, IDX_DIM), lambda b, i: (b, 0, i, 0)),
            pl.BlockSpec((tq, LANES), lambda b, i: (b * nq + i, 0)),
            pl.BlockSpec((None, nkb, IDX_DIM, kb), lambda b, i: (b, 0, 0, 0)),
        ],
        out_specs=pl.BlockSpec((None, nkb, tq, kb), lambda b, i: (b, 0, i, 0)),
        out_shape=jax.ShapeDtypeStruct((n_streams, nkb, t, kb), jnp.int8),
        scratch_shapes=[pltpu.VMEM((nkb, tq, kb), jnp.int32), pltpu.VMEM((IDX_HEADS, tq, LANES), F32)],
        compiler_params=_cparams(2, vmem),
        name="indexer_topk_mask",
    )(qi, w, kit)


def _attn_kernel(q_ref, k_ref, v_ref, m_ref, o_ref, mx_ref, l_ref, acc_ref, *, tq, tk, nkb, past, group):
    i = pl.program_id(2)
    j = pl.program_id(3)
    jmax = jnp.minimum(nkb, (past + (i + 1) * tq + tk - 1) // tk) - 1

    @pl.when(j == 0)
    def _():
        mx_ref[...] = jnp.full(mx_ref.shape, NEG_BIG, F32)
        l_ref[...] = jnp.zeros(l_ref.shape, F32)
        acc_ref[...] = jnp.zeros(acc_ref.shape, F32)

    @pl.when(j <= jmax)
    def _():
        sel = m_ref[...].astype(F32) > 0.0
        kblk = k_ref[...]
        vblk = v_ref[...]
        for g in range(group):
            s = _nt(q_ref[:, g * SA_HEAD:(g + 1) * SA_HEAD], kblk) * float(SA_HEAD ** -0.5)
            s = jnp.where(sel, s, NEG_BIG)
            m_old = mx_ref[g]
            m_new = jnp.maximum(m_old, jnp.max(s, axis=1, keepdims=True))
            alpha = jnp.exp(m_old - m_new)
            p = jnp.where(sel, jnp.exp(s - m_new), 0.0)
            l_ref[g] = alpha * l_ref[g] + jnp.sum(p, axis=1, keepdims=True)
            acc_ref[g] = alpha * acc_ref[g] + jnp.dot(_mx(p), vblk, preferred_element_type=F32)
            mx_ref[g] = m_new

    @pl.when(j == nkb - 1)
    def _():
        for g in range(group):
            o_ref[:, g * SA_HEAD:(g + 1) * SA_HEAD] = (acc_ref[g] / l_ref[g]).astype(o_ref.dtype)


def _attention(q, k_all, v_all, mask, n_streams, t, tq, tk, past):
    m, sa_w = q.shape
    nkb = mask.shape[1]
    nq = t // tq
    group = sa_w // (SA_KV_HEADS * SA_HEAD)
    gw = group * SA_HEAD

    def jclamp(i, j):
        return jnp.minimum(j, jnp.minimum(nkb, (past + (i + 1) * tq + tk - 1) // tk) - 1)

    return pl.pallas_call(
        functools.partial(_attn_kernel, tq=tq, tk=tk, nkb=nkb, past=past, group=group),
        grid=(n_streams, SA_KV_HEADS, nq, nkb),
        in_specs=[
            pl.BlockSpec((tq, gw), lambda b, n, i, j: (b * nq + i, n)),
            pl.BlockSpec((None, tk, SA_HEAD), lambda b, n, i, j: (b, jclamp(i, j), n)),
            pl.BlockSpec((None, tk, SA_HEAD), lambda b, n, i, j: (b, jclamp(i, j), n)),
            pl.BlockSpec((None, None, tq, tk), lambda b, n, i, j: (b, jclamp(i, j), i, 0)),
        ],
        out_specs=pl.BlockSpec((tq, gw), lambda b, n, i, j: (b * nq + i, n)),
        out_shape=jax.ShapeDtypeStruct((m, sa_w), MXU_DTYPE),
        scratch_shapes=[pltpu.VMEM((group, tq, 1), F32), pltpu.VMEM((group, tq, 1), F32),
                        pltpu.VMEM((group, tq, SA_HEAD), F32)],
        compiler_params=_cparams(4, 32 << 20),
        name="masked_flash_attention",
    )(q, k_all, v_all, mask)


def _prepare_weights(w_in, mu_shift, w_decay_up, decay_bias, w_a_up, a_bias, w_gate_up, k_k, k_a, r_k,
                     lnx_w, lnx_b, w_out, w_up, conv_w, conv_b, w_down, d_model):
    rw = w_decay_up.shape[1]
    rw_cols = 3 * rw + D_DECAY + D_AAA + D_GATE
    sa_w = d_model - rw
    kv_w = SA_KV_HEADS * SA_HEAD
    qi_w = IDX_HEADS * IDX_DIM
    o = {}
    o["r"] = 0
    o["wd"] = rw
    o["k"] = rw + D_DECAY
    o["v"] = 2 * rw + D_DECAY
    o["ad"] = 3 * rw + D_DECAY
    o["gd"] = 3 * rw + D_DECAY + D_AAA
    o["q"] = rw_cols
    o["ksa"] = rw_cols + sa_w
    o["vsa"] = o["ksa"] + kv_w
    o["qi"] = o["vsa"] + kv_w
    o["kiw"] = o["qi"] + qi_w
    kiw_w = IDX_DIM + IDX_HEADS

    def seg(x, name, width, pad_to=None):
        s = x[..., o[name]:o[name] + width]
        if pad_to is not None and pad_to > width:
            s = jnp.pad(s, [(0, 0)] * (s.ndim - 1) + [(0, pad_to - width)])
        return s

    def rw_small(x):
        return jnp.concatenate([seg(x, "wd", D_DECAY, LANES), seg(x, "ad", D_AAA, LANES), seg(x, "gd", D_GATE)], -1)

    w_in_l = jnp.concatenate([
        seg(w_in, "r", rw), seg(w_in, "k", rw), seg(w_in, "v", rw), seg(w_in, "q", sa_w), seg(w_in, "qi", qi_w),
        seg(w_in, "ksa", kv_w), seg(w_in, "vsa", kv_w), rw_small(w_in), seg(w_in, "kiw", kiw_w, LANES)], axis=1)
    starts = {"r": 0, "k": rw, "v": 2 * rw, "q": 3 * rw, "qi": 3 * rw + sa_w}
    starts["ksa"] = starts["qi"] + qi_w
    starts["vsa"] = starts["ksa"] + kv_w
    starts["small"] = starts["vsa"] + kv_w
    starts["kiw"] = starts["small"] + 4 * LANES
    pad_rows = lambda x, n: jnp.pad(x, ((0, n - x.shape[0]), (0, 0)))
    row = lambda x: x.reshape(1, -1).astype(F32)
    mu = mu_shift.reshape(1, -1)
    w = {
        "w_in": _mx(w_in_l),
        "mu_r": seg(mu, "r", rw), "mu_k": seg(mu, "k", rw), "mu_v": seg(mu, "v", rw), "mu_small": rw_small(mu),
        "w_decay_up": _mx(pad_rows(w_decay_up, LANES)), "decay_bias": row(decay_bias),
        "w_a_up": _mx(pad_rows(w_a_up, LANES)), "a_bias": row(a_bias),
        "w_gate_up": _mx(w_gate_up),
        "k_k": row(k_k), "k_a": row(k_a), "r_k": row(r_k), "lnx_w": row(lnx_w), "lnx_b": row(lnx_b),
        "w_out": _mx(w_out), "w_up": _mx(w_up), "conv_w": conv_w, "conv_b": row(conv_b), "w_down": _mx(w_down),
    }
    dims = {"rw": rw, "rw_cols": rw_cols, "sa_w": sa_w, "kv_w": kv_w, "qi_w": qi_w, "orig": o, "starts": starts,
            "seg": seg, "rw_small": rw_small}
    return w, dims


def _layer(x, past_k, past_v, past_ki, wkv0, shift0, conv0, norms, w, dims, tiles):
    bsz, t, d = x.shape
    past = past_k.shape[1]
    m = bsz * t
    rw, sa_w, kv_w, qi_w = dims["rw"], dims["sa_w"], dims["kv_w"], dims["qi_w"]
    starts, seg, rw_small = dims["starts"], dims["seg"], dims["rw_small"]
    assert t % CHUNK == 0 and past % CHUNK == 0
    tm, tq, kb = tiles["tm"], tiles["tq"], tiles["kb"]
    g_mix_pre, g_mix_post, g_ffn_pre, g_ffn_post = norms

    x2 = x.reshape(m, d)
    h1 = _rmsnorm(x2, g_mix_pre, tiles["tm_norm"])
    pfull = _matmul(h1, w["w_in"], tiles["tm_in"], tiles["tn_in"])

    sh = shift0.reshape(bsz, 1, -1)
    shift_l = {"r": seg(sh, "r", rw), "k": seg(sh, "k", rw), "v": seg(sh, "v", rw), "small": rw_small(sh)}
    cols_rw = {n: starts[n] // RW_GROUP for n in ("r", "k", "v", "small")}
    o_rw, wkv1 = _rwkv(pfull, cols_rw, shift_l, wkv0, w, bsz, t)

    pos = past + jnp.arange(t, dtype=jnp.int32)
    cols_sa = {"q": starts["q"] // sa_w, "ksa": starts["ksa"] // kv_w, "qi": starts["qi"] // qi_w,
               "kiw": starts["kiw"] // LANES}
    q_r, k_r, qi_r, kiw_r = _rope(pfull, cols_sa, _rope_tables(pos), bsz, t, tq, sa_w, kv_w, qi_w)
    k_new = k_r.reshape(bsz, t, kv_w)
    v_new = pfull[:, starts["vsa"]:starts["vsa"] + kv_w].reshape(bsz, t, kv_w)
    ki_new = kiw_r[:, :IDX_DIM].reshape(bsz, t, IDX_DIM)
    l_all = past + t
    lp = -(-l_all // kb) * kb
    nkb = lp // kb
    cat = lambda old, new: jnp.pad(
        jnp.concatenate([_mx(old.reshape(bsz, past, new.shape[-1])), _mx(new)], axis=1),
        ((0, 0), (0, lp - l_all), (0, 0)))
    k_all, v_all, ki_all = cat(past_k, k_new), cat(past_v, v_new), cat(past_ki, ki_new)
    kit = ki_all.reshape(bsz, nkb, kb, IDX_DIM).transpose(0, 1, 3, 2)
    topk = min(TOPK_MAX, l_all // 4)
    mask = _index_mask(qi_r, kiw_r, kit, bsz, t, tq, kb, past, topk)
    o_sa = _attention(q_r, k_all, v_all, mask, bsz, t, tq, kb, past)

    mix_in = jnp.concatenate([o_rw, o_sa], axis=1)
    x1, h2 = _matmul_norm_residual(mix_in, w["w_out"], x2, g_mix_post, g_ffn_pre, tm, tiles["tk"], True, "out_proj")
    act, conv1 = _ffn_up(h2, w["w_up"], conv0, w["conv_w"], w["conv_b"], bsz, tiles["tm_up"], tiles["tn_up"])
    x_out, _ = _matmul_norm_residual(act, w["w_down"], x1, g_ffn_post, g_ffn_post, tm, tiles["tk"], False, "ffn_down")

    orig = dims["orig"]
    last = pfull.reshape(bsz, t, -1)[:, t - 1:t]
    sm = last[..., starts["small"]:starts["small"] + 4 * LANES]
    shift1 = jnp.concatenate([
        last[..., starts["r"]:starts["r"] + rw], sm[..., :D_DECAY],
        last[..., starts["k"]:starts["k"] + rw], last[..., starts["v"]:starts["v"] + rw],
        sm[..., LANES:LANES + D_AAA], sm[..., 2 * LANES:2 * LANES + D_GATE]], axis=-1)
    del orig
    new = (k_new.reshape(bsz, t, SA_KV_HEADS, SA_HEAD), v_new.reshape(bsz, t, SA_KV_HEADS, SA_HEAD), ki_new,
           wkv1, shift1, conv1)
    return x_out.reshape(bsz, t, d), new


def _tiles(bsz, t):
    big = t >= 1024
    return {
        "tm_norm": 256 if big else CHUNK,
        "tm_in": 1024 if big else bsz * t,
        "tn_in": 640,
        "tm": 256 if big else CHUNK,
        "tk": 512,
        "tm_up": 512 if big else CHUNK,
        "tn_up": 512,
        "tq": 256 if big else CHUNK,
        "kb": 512,
    }


def _run_stream(x, past_k, past_v, past_ki, wkv0, shift0, conv0, norm_w, weights, dims, depth):
    outs = []
    tiles = _tiles(x.shape[0], x.shape[1])
    for l in range(depth):
        w_l = {n: (v[l] if n != "conv_w" else v[l]) for n, v in weights.items()}
        norms = tuple(g[l].reshape(1, -1) for g in norm_w)
        x, st = _layer(x, past_k[l], past_v[l], past_ki[l], wkv0[l], shift0[l], conv0[l], norms, w_l, dims[l], tiles)
        outs.append(st)
    return x, [jnp.stack(s) for s in zip(*outs)]


def kernel(x_prompt, x_sample, cache_k, cache_v, cache_kidx, state_wkv, state_shift, state_conv, norm_mix_pre, norm_mix_post, norm_ffn_pre, norm_ffn_post, w_in, mu_shift, w_decay_up, decay_bias, w_a_up, a_bias, w_gate_up, k_k, k_a, r_k, lnx_w, lnx_b, w_out, w_up, conv_w, conv_b, w_down):
    depth, d_model = norm_mix_pre.shape
    per_layer = [_prepare_weights(w_in[l], mu_shift[l], w_decay_up[l], decay_bias[l], w_a_up[l], a_bias[l],
                                  w_gate_up[l], k_k[l], k_a[l], r_k[l], lnx_w[l], lnx_b[l], w_out[l], w_up[l],
                                  conv_w[l], conv_b[l], w_down[l], d_model) for l in range(depth)]
    weights = {n: [pw[0][n] for pw in per_layer] for n in per_layer[0][0]}
    dims = [pw[1] for pw in per_layer]
    norm_w = (norm_mix_pre, norm_mix_post, norm_ffn_pre, norm_ffn_post)

    dt = x_prompt.dtype
    bp = x_prompt.shape[0]
    rw_heads = state_wkv.shape[2]
    zk = jnp.zeros((depth, bp, 0, SA_KV_HEADS, SA_HEAD), dt)
    zki = jnp.zeros((depth, bp, 0, IDX_DIM), dt)
    zwkv = jnp.zeros((depth, bp, rw_heads, RW_HEAD, RW_HEAD), dt)
    zshift = jnp.zeros((depth, bp, 1, state_shift.shape[-1]), dt)
    zconv = jnp.zeros((depth, bp, CONV_W - 1, state_conv.shape[-1]), dt)
    y_prompt, p_new = _run_stream(x_prompt, zk, zk, zki, zwkv, zshift, zconv, norm_w, weights, dims, depth)
    y_sample, s_new = _run_stream(x_sample, cache_k, cache_v, cache_kidx, state_wkv, state_shift, state_conv,
                                  norm_w, weights, dims, depth)
    return (y_prompt, y_sample, *p_new, *s_new)
```

```python
import functools

import numpy as np
import jax
import jax.numpy as jnp
from jax import lax
from jax.experimental import pallas as pl
from jax.experimental.pallas import tpu as pltpu

F32 = jnp.float32
MXU_DTYPE = jnp.bfloat16
HI = lax.Precision.HIGHEST

CHUNK = 64
RW_HEAD = 64
D_DECAY = 96
D_AAA = 96
D_GATE = 256
SA_HEAD = 128
SA_KV_HEADS = 4
IDX_HEADS = 16
IDX_DIM = 64
TOPK_MAX = 256
CONV_W = 3
ROPE_THETA = 10000.0
NORM_EPS = 1e-6
LNX_EPS = 64e-5

LANES = 128
V7X_VMEM_LIMIT_BYTES = 60000 * 1024
INT_MIN = -(2 ** 31)
NEG_BIG = -1e30


def _cparams(n_grid, vmem_bytes):
    limit = int(min(V7X_VMEM_LIMIT_BYTES, max(32 * 1024 * 1024, vmem_bytes)))
    return pltpu.CompilerParams(dimension_semantics=("arbitrary",) * n_grid, vmem_limit_bytes=limit)


def _nt(a, b, precision=None):
    return lax.dot_general(a, b, (((1,), (1,)), ((), ())), precision=precision, preferred_element_type=F32)


def _tn(a, b, precision=None):
    return lax.dot_general(a, b, (((0,), (0,)), ((), ())), precision=precision, preferred_element_type=F32)


def _mx(x):
    return x.astype(MXU_DTYPE)


def _rmsnorm_kernel(x_ref, g_ref, o_ref):
    x = x_ref[...]
    y = x * lax.rsqrt(jnp.mean(x * x, axis=-1, keepdims=True) + NORM_EPS)
    o_ref[...] = (y * g_ref[...]).astype(o_ref.dtype)


def _rmsnorm(x, g, tm):
    m, d = x.shape
    return pl.pallas_call(
        _rmsnorm_kernel,
        grid=(m // tm,),
        in_specs=[pl.BlockSpec((tm, d), lambda i: (i, 0)), pl.BlockSpec((1, d), lambda i: (0, 0))],
        out_specs=pl.BlockSpec((tm, d), lambda i: (i, 0)),
        out_shape=jax.ShapeDtypeStruct((m, d), MXU_DTYPE),
        compiler_params=_cparams(1, 6 * tm * d * 4),
        name="rmsnorm",
    )(x, g)


def _mm_kernel(a_ref, b_ref, o_ref):
    o_ref[...] = jnp.dot(a_ref[...], b_ref[...], preferred_element_type=F32)


def _matmul(a, b, tm, tn):
    m, k = a.shape
    n = b.shape[1]
    isz = jnp.dtype(MXU_DTYPE).itemsize
    vmem = 2 * (tm * k * isz + k * tn * isz + tm * tn * 4) + 2 * tm * tn * 4 + (8 << 20)
    return pl.pallas_call(
        _mm_kernel,
        grid=(m // tm, n // tn),
        in_specs=[pl.BlockSpec((tm, k), lambda i, j: (i, 0)), pl.BlockSpec((k, tn), lambda i, j: (0, j))],
        out_specs=pl.BlockSpec((tm, tn), lambda i, j: (i, j)),
        out_shape=jax.ShapeDtypeStruct((m, n), F32),
        compiler_params=_cparams(2, vmem),
        name="in_proj",
    )(a, b)


def _mm_norm_kernel(a_ref, w_ref, x_ref, gpost_ref, gnext_ref, *rest, nk, emit_next):
    if emit_next:
        x1_ref, h_ref, acc_ref = rest
    else:
        x1_ref, acc_ref = rest
    k = pl.program_id(1)

    @pl.when(k == 0)
    def _():
        acc_ref[...] = jnp.zeros_like(acc_ref)

    acc_ref[...] += jnp.dot(a_ref[...], w_ref[...], preferred_element_type=F32)

    @pl.when(k == nk - 1)
    def _():
        y = acc_ref[...]
        y = y * lax.rsqrt(jnp.mean(y * y, axis=-1, keepdims=True) + NORM_EPS) * gpost_ref[...]
        x1 = x_ref[...] + y
        x1_ref[...] = x1
        if emit_next:
            h = x1 * lax.rsqrt(jnp.mean(x1 * x1, axis=-1, keepdims=True) + NORM_EPS) * gnext_ref[...]
            h_ref[...] = h.astype(h_ref.dtype)


def _matmul_norm_residual(a, w, x, g_post, g_next, tm, tk, emit_next, name):
    m, kdim = a.shape
    d = w.shape[1]
    nk = kdim // tk
    isz = jnp.dtype(MXU_DTYPE).itemsize
    out_shape = [jax.ShapeDtypeStruct((m, d), F32)]
    out_specs = [pl.BlockSpec((tm, d), lambda i, k: (i, 0))]
    if emit_next:
        out_shape.append(jax.ShapeDtypeStruct((m, d), MXU_DTYPE))
        out_specs.append(pl.BlockSpec((tm, d), lambda i, k: (i, 0)))
    vmem = 2 * (tm * tk * isz + tk * d * isz + 2 * tm * d * 4 + tm * d * isz) + 3 * tm * d * 4 + (4 << 20)
    res = pl.pallas_call(
        functools.partial(_mm_norm_kernel, nk=nk, emit_next=emit_next),
        grid=(m // tm, nk),
        in_specs=[
            pl.BlockSpec((tm, tk), lambda i, k: (i, k)),
            pl.BlockSpec((tk, d), lambda i, k: (k, 0)),
            pl.BlockSpec((tm, d), lambda i, k: (i, 0)),
            pl.BlockSpec((1, d), lambda i, k: (0, 0)),
            pl.BlockSpec((1, d), lambda i, k: (0, 0)),
        ],
        out_specs=out_specs,
        out_shape=out_shape,
        scratch_shapes=[pltpu.VMEM((tm, d), F32)],
        compiler_params=_cparams(2, vmem),
        name=name,
    )(a, w, x, g_post, g_next)
    return res if emit_next else (res[0], None)


def _gelu_tanh(x):
    return 0.5 * x * (1.0 + jnp.tanh(0.7978845608028654 * (x + 0.044715 * x * x * x)))


def _ffn_up_kernel(h_ref, wg_ref, wv_ref, c0g_ref, c0v_ref, cwg_ref, cwv_ref, cbg_ref, cbv_ref,
                   a_ref, cg_ref, cv_ref, carry_g, carry_v, *, blocks_per_stream, tm):
    i = pl.program_id(1) % blocks_per_stream

    @pl.when(i == 0)
    def _():
        carry_g[...] = c0g_ref[...]
        carry_v[...] = c0v_ref[...]

    h = h_ref[...]
    row = lax.broadcasted_iota(jnp.int32, (tm, 1), 0)

    def conv(u, carry_ref, cw_ref, cb_ref, out_ref):
        p = carry_ref[...]
        u1 = jnp.where(row == 0, p[1:2], pltpu.roll(u, 1, axis=0))
        u2 = jnp.where(row == 0, p[0:1], jnp.where(row == 1, p[1:2], pltpu.roll(u, 2, axis=0)))
        cw = cw_ref[...]
        last = u[tm - 2:tm]
        carry_ref[...] = last
        out_ref[...] = last
        return u2 * cw[0:1] + u1 * cw[1:2] + u * cw[2:3] + cb_ref[...]

    gate = conv(jnp.dot(h, wg_ref[...], preferred_element_type=F32), carry_g, cwg_ref, cbg_ref, cg_ref)
    val = conv(jnp.dot(h, wv_ref[...], preferred_element_type=F32), carry_v, cwv_ref, cbv_ref, cv_ref)
    a_ref[...] = (_gelu_tanh(gate) * val).astype(a_ref.dtype)


def _ffn_up(h, w_up, conv0, conv_w, conv_b, n_streams, tm, tn):
    m, d = h.shape
    f2 = w_up.shape[1]
    f = f2 // 2
    nj = f // tn
    nr = m // tm
    bps = nr // n_streams
    isz = jnp.dtype(MXU_DTYPE).itemsize
    vmem = 2 * (tm * d * isz + 2 * d * tn * isz + tm * tn * isz) + 10 * tm * tn * 4 + (4 << 20)
    col = lambda off: (lambda j, r: (0, off + j))
    st = lambda off: (lambda j, r: (r // bps, 0, off + j))
    a, cg, cv = pl.pallas_call(
        functools.partial(_ffn_up_kernel, blocks_per_stream=bps, tm=tm),
        grid=(nj, nr),
        in_specs=[
            pl.BlockSpec((tm, d), lambda j, r: (r, 0)),
            pl.BlockSpec((d, tn), col(0)),
            pl.BlockSpec((d, tn), col(nj)),
            pl.BlockSpec((None, CONV_W - 1, tn), st(0)),
            pl.BlockSpec((None, CONV_W - 1, tn), st(nj)),
            pl.BlockSpec((CONV_W, tn), col(0)),
            pl.BlockSpec((CONV_W, tn), col(nj)),
            pl.BlockSpec((1, tn), col(0)),
            pl.BlockSpec((1, tn), col(nj)),
        ],
        out_specs=[
            pl.BlockSpec((tm, tn), lambda j, r: (r, j)),
            pl.BlockSpec((None, CONV_W - 1, tn), st(0)),
            pl.BlockSpec((None, CONV_W - 1, tn), st(0)),
        ],
        out_shape=[
            jax.ShapeDtypeStruct((m, f), MXU_DTYPE),
            jax.ShapeDtypeStruct((n_streams, CONV_W - 1, f), F32),
            jax.ShapeDtypeStruct((n_streams, CONV_W - 1, f), F32),
        ],
        scratch_shapes=[pltpu.VMEM((CONV_W - 1, tn), F32), pltpu.VMEM((CONV_W - 1, tn), F32)],
        compiler_params=_cparams(2, vmem),
        name="ffn_up_conv",
    )(h, w_up, w_up, conv0, conv0, conv_w, conv_w, conv_b, conv_b)
    return a, jnp.concatenate([cg, cv], axis=-1)


RW_GROUP = 4 * LANES


def _softplus(z):
    return jnp.maximum(z, 0.0) + jnp.log(1.0 + jnp.exp(-jnp.abs(z)))


def _sigmoid(z):
    return 1.0 / (1.0 + jnp.exp(-z))


def _split(x):
    hi = x.astype(MXU_DTYPE)
    return hi, (x - hi.astype(F32)).astype(MXU_DTYPE)


def _dg(a, b, dims):
    return lax.dot_general(_mx(a), _mx(b), (dims, ((), ())), preferred_element_type=F32)


_NN = ((1,), (0,))
_NT = ((1,), (1,))
_TN = ((0,), (0,))


def _segsum(x, seg):
    hi, lo = _split(x)
    return jnp.dot(hi, seg, preferred_element_type=F32) + jnp.dot(lo, seg, preferred_element_type=F32)


def _each(f, *lists):
    return [f(*args) for args in zip(*lists)]


def _wkv_chunk(r, k, v, kk, b, cum, logd, s_bd, consts):
    m0, m1, tri_mask = consts
    c = r[0].shape[0]
    c2 = 2 * c
    stack = lambda x: jnp.concatenate([x * m0, x * m1], axis=0)
    g_end = _each(lambda cm: jnp.exp(cm[c - 1:c]), cum)
    e_inc = _each(jnp.exp, cum)
    e_prev = _each(lambda cm, ld: jnp.exp(cm - ld), cum, logd)
    e_neg = _each(lambda cm: jnp.exp(-cm), cum)
    e_tail = _each(lambda g, e: g * e, g_end, e_neg)
    vs = _each(stack, v)
    lhs = _each(lambda kk_, r_, ep, ei: jnp.concatenate([stack(kk_ * ep), stack(r_ * ei)], axis=0),
                kk, r, e_prev, e_inc)
    rhs = _each(lambda k_, b_, en: jnp.concatenate([stack(k_ * en), stack(b_ * en)], axis=0), k, b, e_neg)
    tails = _each(lambda k_, b_, et: jnp.concatenate([stack(k_ * et), stack(b_ * et)], axis=0), k, b, e_tail)
    amat = _each(lambda x, y: jnp.where(tri_mask, _dg(x, y, _NT), 0.0), lhs, rhs)
    sp = _each(lambda x, s: _dg(x, s, _NT), lhs, s_bd)
    av = _each(lambda a_, v_: _dg(a_[:, :c2], v_, _NN), amat, vs)
    pw = _each(lambda a_: -a_[:c2, c2:], amat)
    x = _each(lambda s_, a_: s_[:c2] + a_[:c2], sp, av)
    n_lvl = int(np.log2(c))
    for lvl in range(n_lvl - 1):
        z = _each(lambda p_, x_: _dg(p_, jnp.concatenate([p_, x_], axis=1), _NN), pw, x)
        pw = _each(lambda z_: z_[:, :c2], z)
        x = _each(lambda x_, z_: x_ + z_[:, c2:], x, z)
    u = _each(lambda p_, x_: x_ + _dg(p_, x_, _NN), pw, x)
    ys = _each(lambda s_, a_, am, u_: s_[c2:] + a_[c2:] - _dg(am[c2:, c2:], u_, _NN), sp, av, amat, u)
    y = _each(lambda y_: y_[:c] + y_[c:], ys)
    s_new = _each(lambda s, g, v_, u_, t_: s * g + _dg(jnp.concatenate([v_, -u_], axis=0), t_, _TN),
                  s_bd, g_end, vs, u, tails)
    return y, s_new


def _rwkv_kernel(pr_ref, pk_ref, pv_ref, ps_ref, sr_ref, sk_ref, sv_ref, ss_ref, wkv0_ref,
                 mur_ref, muk_ref, muv_ref, mus_ref, wdec_ref, dbias_ref, wa_ref, abias_ref, wg_ref,
                 kk_ref, ka_ref, rk_ref, lnw_ref, lnb_ref,
                 o_ref, wkv1_ref,
                 cr_ref, ck_ref, cv_ref, cs_ref, state_ref, *, n_chunks):
    c = pl.program_id(2)
    C = CHUNK
    npairs = RW_GROUP // LANES

    lane = lax.broadcasted_iota(jnp.int32, (1, LANES), 1)
    m0 = (lane < RW_HEAD).astype(F32)
    m1 = 1.0 - m0
    ri = lax.broadcasted_iota(jnp.int32, (4 * C, 4 * C), 0)
    ci = lax.broadcasted_iota(jnp.int32, (4 * C, 4 * C), 1)
    same_head = ((ri // C) % 2) == ((ci // C) % 2)
    tri_mask = same_head & ((ri % C) + ri // (2 * C) > (ci % C))
    li = lax.broadcasted_iota(jnp.int32, (LANES, LANES), 0)
    lj = lax.broadcasted_iota(jnp.int32, (LANES, LANES), 1)
    seg = ((li // RW_HEAD) == (lj // RW_HEAD)).astype(MXU_DTYPE)
    consts = (m0, m1, tri_mask)

    @pl.when(c == 0)
    def _():
        cr_ref[...] = sr_ref[...]
        ck_ref[...] = sk_ref[...]
        cv_ref[...] = sv_ref[...]
        cs_ref[...] = ss_ref[...]
        z = jnp.zeros((RW_HEAD, RW_HEAD), F32)
        for p in range(npairs):
            s0 = wkv0_ref[2 * p]
            s1 = wkv0_ref[2 * p + 1]
            state_ref[p] = jnp.concatenate(
                [jnp.concatenate([s0, z], axis=1), jnp.concatenate([z, s1], axis=1)], axis=0)

    row = lax.broadcasted_iota(jnp.int32, (C, 1), 0)

    def shifted(p_ref, carry_ref, mu_ref):
        p = p_ref[...]
        prev = jnp.where(row == 0, carry_ref[...], pltpu.roll(p, 1, axis=0))
        carry_ref[...] = p[C - 1:C]
        return p + (prev - p) * mu_ref[...]

    r = shifted(pr_ref, cr_ref, mur_ref)
    k = shifted(pk_ref, ck_ref, muk_ref)
    v = shifted(pv_ref, cv_ref, muv_ref)
    sm = shifted(ps_ref, cs_ref, mus_ref)
    wd, ad, gd = sm[:, 0:LANES], sm[:, LANES:2 * LANES], sm[:, 2 * LANES:4 * LANES]

    dec_in = dbias_ref[...] + jnp.dot(_mx(jnp.tanh(wd)), wdec_ref[...], preferred_element_type=F32)
    w_log = -_softplus(-dec_in) - 0.5
    logd = -jnp.exp(w_log)
    a = _sigmoid(abias_ref[...] + jnp.dot(_mx(ad), wa_ref[...], preferred_element_type=F32))
    g = jnp.dot(_mx(_sigmoid(gd)), wg_ref[...], preferred_element_type=F32)
    kk = k * kk_ref[...]
    k2 = k * (1.0 + (a - 1.0) * ka_ref[...])
    rkr = r * k2 * rk_ref[...]
    cum = logd
    for sft in (1, 2, 4, 8, 16, 32):
        cum = cum + jnp.where(row >= sft, pltpu.roll(cum, sft, axis=0), 0.0)

    pairs = lambda x: [x[:, p * LANES:(p + 1) * LANES] for p in range(npairs)]
    r_p, k_p, v_p, a_p, cum_p, logd_p = pairs(r), pairs(k2), pairs(v), pairs(a), pairs(cum), pairs(logd)
    sums = _each(lambda kk_, rkr_: _segsum(jnp.concatenate([kk_ * kk_, rkr_], axis=0), seg), pairs(kk), pairs(rkr))
    kk_p = _each(lambda kk_, s_: kk_ * lax.rsqrt(jnp.maximum(s_[:C], 1e-24)), pairs(kk), sums)
    b_p = _each(lambda kk_, a_: kk_ * a_, kk_p, a_p)
    y_p, s_new = _wkv_chunk(r_p, k_p, v_p, kk_p, b_p, cum_p, logd_p, [state_ref[p] for p in range(npairs)], consts)
    for p in range(npairs):
        state_ref[p] = s_new[p]
    yc = _each(lambda y_: y_ - _segsum(y_, seg) * (1.0 / RW_HEAD), y_p)
    var = _each(lambda yc_: _segsum(yc_ * yc_, seg) * (1.0 / RW_HEAD), yc)
    for p in range(npairs):
        sl = slice(p * LANES, (p + 1) * LANES)
        yn = yc[p] * lax.rsqrt(var[p] + LNX_EPS) * lnw_ref[:, sl] + lnb_ref[:, sl]
        o_ref[:, sl] = ((yn + sums[p][C:] * v_p[p]) * g[:, sl]).astype(o_ref.dtype)

    @pl.when(c == n_chunks - 1)
    def _():
        for p in range(npairs):
            s = state_ref[p]
            wkv1_ref[2 * p] = s[:RW_HEAD, :RW_HEAD]
            wkv1_ref[2 * p + 1] = s[RW_HEAD:, RW_HEAD:]


def _rwkv(pfull, cols, shift0, wkv0, w, n_streams, t):
    m = pfull.shape[0]
    rw = w["k_k"].shape[1]
    ng = rw // RW_GROUP
    nc = t // CHUNK
    hg = RW_GROUP // RW_HEAD
    rowblk = lambda off: (lambda b, g, c: (b * nc + c, off + g))
    fixed = lambda off: (lambda b, g, c: (b * nc + c, off))
    st = lambda b, g, c: (b, 0, g)
    st0 = lambda b, g, c: (b, 0, 0)
    wcol = lambda b, g, c: (0, g)
    in_specs = [
        pl.BlockSpec((CHUNK, RW_GROUP), rowblk(cols["r"])),
        pl.BlockSpec((CHUNK, RW_GROUP), rowblk(cols["k"])),
        pl.BlockSpec((CHUNK, RW_GROUP), rowblk(cols["v"])),
        pl.BlockSpec((CHUNK, RW_GROUP), fixed(cols["small"])),
        pl.BlockSpec((None, 1, RW_GROUP), st),
        pl.BlockSpec((None, 1, RW_GROUP), st),
        pl.BlockSpec((None, 1, RW_GROUP), st),
        pl.BlockSpec((None, 1, RW_GROUP), st0),
        pl.BlockSpec((None, hg, RW_HEAD, RW_HEAD), lambda b, g, c: (b, g, 0, 0)),
        pl.BlockSpec((1, RW_GROUP), wcol), pl.BlockSpec((1, RW_GROUP), wcol), pl.BlockSpec((1, RW_GROUP), wcol),
        pl.BlockSpec((1, RW_GROUP), lambda b, g, c: (0, 0)),
        pl.BlockSpec((LANES, RW_GROUP), wcol), pl.BlockSpec((1, RW_GROUP), wcol),
        pl.BlockSpec((LANES, RW_GROUP), wcol), pl.BlockSpec((1, RW_GROUP), wcol),
        pl.BlockSpec((2 * LANES, RW_GROUP), wcol),
        pl.BlockSpec((1, RW_GROUP), wcol), pl.BlockSpec((1, RW_GROUP), wcol), pl.BlockSpec((1, RW_GROUP), wcol),
        pl.BlockSpec((1, RW_GROUP), wcol), pl.BlockSpec((1, RW_GROUP), wcol),
    ]
    o, wkv1 = pl.pallas_call(
        functools.partial(_rwkv_kernel, n_chunks=nc),
        grid=(n_streams, ng, nc),
        in_specs=in_specs,
        out_specs=[
            pl.BlockSpec((CHUNK, RW_GROUP), lambda b, g, c: (b * nc + c, g)),
            pl.BlockSpec((None, hg, RW_HEAD, RW_HEAD), lambda b, g, c: (b, g, 0, 0)),
        ],
        out_shape=[
            jax.ShapeDtypeStruct((m, rw), MXU_DTYPE),
            jax.ShapeDtypeStruct((n_streams, rw // RW_HEAD, RW_HEAD, RW_HEAD), F32),
        ],
        scratch_shapes=[pltpu.VMEM((1, RW_GROUP), F32)] * 4
        + [pltpu.VMEM((RW_GROUP // LANES, LANES, LANES), F32)],
        compiler_params=_cparams(3, 32 << 20),
        name="rwkv7_chunked",
    )(pfull, pfull, pfull, pfull, shift0["r"], shift0["k"], shift0["v"], shift0["small"], wkv0,
      w["mu_r"], w["mu_k"], w["mu_v"], w["mu_small"], w["w_decay_up"], w["decay_bias"], w["w_a_up"], w["a_bias"],
      w["w_gate_up"], w["k_k"], w["k_a"], w["r_k"], w["lnx_w"], w["lnx_b"])
    return o, wkv1


def _rope_kernel(q_ref, k_ref, qi_ref, kiw_ref, ca_ref, sa_ref, cb_ref, sb1_ref, sb2_ref,
                 qo_ref, ko_ref, qio_ref, kiwo_ref):
    ca, sa = ca_ref[...], sa_ref[...]
    cb, sb1, sb2 = cb_ref[...], sb1_ref[...], sb2_ref[...]

    def rope_head(x):
        return x * ca + pltpu.roll(x, SA_HEAD // 2, axis=1) * sa

    def rope_idx(x):
        return (x * cb + pltpu.roll(x, LANES - IDX_DIM // 2, axis=1) * sb1
                + pltpu.roll(x, IDX_DIM // 2, axis=1) * sb2)

    q_scale = float(SA_HEAD ** -0.5 * np.log2(np.e))
    for h in range(q_ref.shape[1] // LANES):
        sl = slice(h * LANES, (h + 1) * LANES)
        qo_ref[:, sl] = (rope_head(q_ref[:, sl]) * q_scale).astype(qo_ref.dtype)
    for h in range(k_ref.shape[1] // LANES):
        sl = slice(h * LANES, (h + 1) * LANES)
        ko_ref[:, sl] = rope_head(k_ref[:, sl])
    for hp in range(qi_ref.shape[1] // LANES):
        y = rope_idx(qi_ref[:, hp * LANES:(hp + 1) * LANES]).astype(qio_ref.dtype)
        qio_ref[2 * hp] = y[:, :IDX_DIM]
        qio_ref[2 * hp + 1] = y[:, IDX_DIM:]
    x = kiw_ref[...]
    lane = lax.broadcasted_iota(jnp.int32, (1, LANES), 1)
    kiwo_ref[...] = jnp.where(lane < IDX_DIM, rope_idx(x), x * float((IDX_HEADS * IDX_DIM) ** -0.5))


def _rope_tables(pos):
    def ang(half):
        inv = ROPE_THETA ** (-jnp.arange(half, dtype=F32) / half)
        return pos.astype(F32)[:, None] * inv[None, :]
    aa = ang(SA_HEAD // 2)
    ca = jnp.concatenate([jnp.cos(aa), jnp.cos(aa)], axis=1)
    sa = jnp.concatenate([-jnp.sin(aa), jnp.sin(aa)], axis=1)
    ab = ang(IDX_DIM // 2)
    z = jnp.zeros_like(ab)
    cb = jnp.concatenate([jnp.cos(ab)] * 4, axis=1)
    sb1 = jnp.concatenate([-jnp.sin(ab), z, -jnp.sin(ab), z], axis=1)
    sb2 = jnp.concatenate([z, jnp.sin(ab), z, jnp.sin(ab)], axis=1)
    return ca, sa, cb, sb1, sb2


def _rope(pfull, cols, tables, n_streams, t, tm, sa_w, kv_w, qi_w):
    m = pfull.shape[0]
    bps = t // tm
    tab = pl.BlockSpec((tm, LANES), lambda r: (r % bps, 0))
    return pl.pallas_call(
        _rope_kernel,
        grid=(m // tm,),
        in_specs=[
            pl.BlockSpec((tm, sa_w), lambda r: (r, cols["q"])),
            pl.BlockSpec((tm, kv_w), lambda r: (r, cols["ksa"])),
            pl.BlockSpec((tm, qi_w), lambda r: (r, cols["qi"])),
            pl.BlockSpec((tm, LANES), lambda r: (r, cols["kiw"])),
            tab, tab, tab, tab, tab,
        ],
        out_specs=[
            pl.BlockSpec((tm, sa_w), lambda r: (r, 0)),
            pl.BlockSpec((tm, kv_w), lambda r: (r, 0)),
            pl.BlockSpec((None, IDX_HEADS, tm, IDX_DIM), lambda r: (r // bps, 0, r % bps, 0)),
            pl.BlockSpec((tm, LANES), lambda r: (r, 0)),
        ],
        out_shape=[
            jax.ShapeDtypeStruct((m, sa_w), MXU_DTYPE),
            jax.ShapeDtypeStruct((m, kv_w), F32),
            jax.ShapeDtypeStruct((n_streams, IDX_HEADS, t, IDX_DIM), MXU_DTYPE),
            jax.ShapeDtypeStruct((m, LANES), F32),
        ],
        compiler_params=_cparams(1, 32 << 20),
        name="rope",
    )(pfull, pfull, pfull, pfull, *tables)


def _index_kernel(qi_ref, w_ref, kit_ref, mask_ref, keys_ref, wb_ref, *, tq, kb, nkb, past, topk):
    i = pl.program_id(1)
    n_adm = jnp.minimum(nkb, (past + (i + 1) * tq + kb - 1) // kb)
    row = lax.broadcasted_iota(jnp.int32, (tq, 1), 0)
    chunk_bits = CHUNK.bit_length() - 1
    lim = (((past + i * tq + row) >> chunk_bits) + 1) << chunk_bits
    lane = lax.broadcasted_iota(jnp.int32, (1, kb), 1)
    ncol = kb // LANES

    w = w_ref[...]
    for h in range(IDX_HEADS):
        wb_ref[h] = jnp.broadcast_to(w[:, IDX_DIM + h:IDX_DIM + h + 1], (tq, LANES))

    def score_block(j, carry):
        kt = kit_ref[j]
        acc = [jnp.zeros((tq, LANES), F32) for _ in range(ncol)]
        for h in range(IDX_HEADS):
            s = jnp.dot(qi_ref[h], kt, preferred_element_type=F32)
            wb = wb_ref[h]
            for cidx in range(ncol):
                acc[cidx] = acc[cidx] + jnp.maximum(s[:, cidx * LANES:(cidx + 1) * LANES], 0.0) * wb
        score = jnp.concatenate(acc, axis=1) + 0.0
        bits = lax.bitcast_convert_type(score, jnp.int32)
        key = bits ^ ((bits >> 31) & jnp.int32(0x7FFFFFFF))
        keys_ref[j] = jnp.where(j * kb + lane < lim, key, jnp.int32(INT_MIN))
        return carry

    lax.fori_loop(0, n_adm, score_block, 0)

    def bisect(p, prefix):
        cand = prefix + lax.shift_left(jnp.int32(1), 31 - p)

        def count(j, cnt):
            x = jnp.where(keys_ref[j] >= cand, 1.0, 0.0)
            for cidx in range(ncol):
                cnt = cnt + x[:, cidx * LANES:(cidx + 1) * LANES]
            return cnt

        cnt = lax.fori_loop(0, n_adm, count, jnp.zeros((tq, LANES), F32))
        total = jnp.sum(cnt, axis=1, keepdims=True)
        return jnp.where(total >= float(topk), cand, prefix)

    thr = lax.fori_loop(0, 32, bisect, jnp.full((tq, 1), INT_MIN, jnp.int32))
    thr = jnp.maximum(thr, jnp.int32(INT_MIN + 1))

    def write(j, carry):
        mask_ref[j] = jnp.where(keys_ref[j] >= thr, 1, 0).astype(mask_ref.dtype)
        return carry

    lax.fori_loop(0, n_adm, write, 0)

    def clear(j, carry):
        mask_ref[j] = jnp.zeros((tq, kb), mask_ref.dtype)
        return carry

    lax.fori_loop(n_adm, nkb, clear, 0)


def _index_mask(qi, w, kit, n_streams, t, tq, kb, past, topk):
    nkb = kit.shape[1]
    nq = t // tq
    vmem = nkb * tq * kb * 4 + 2 * nkb * tq * kb + 2 * nkb * IDX_DIM * kb * 2 + IDX_HEADS * tq * LANES * 4 * 3 + (8 << 20)
    return pl.pallas_call(
        functools.partial(_index_kernel, tq=tq, kb=kb, nkb=nkb, past=past, topk=topk),
        grid=(n_streams, nq),
        in_specs=[
            pl.BlockSpec((None, IDX_HEADS, tq, IDX_DIM), lambda b, i: (b, 0, i, 0)),
            pl.BlockSpec((tq, LANES), lambda b, i: (b * nq + i, 0)),
            pl.BlockSpec((None, nkb, IDX_DIM, kb), lambda b, i: (b, 0, 0, 0)),
        ],
        out_specs=pl.BlockSpec((None, nkb, tq, kb), lambda b, i: (b, 0, i, 0)),
        out_shape=jax.ShapeDtypeStruct((n_streams, nkb, t, kb), jnp.int8),
        scratch_shapes=[pltpu.VMEM((nkb, tq, kb), jnp.int32), pltpu.VMEM((IDX_HEADS, tq, LANES), F32)],
        compiler_params=_cparams(2, vmem),
        name="indexer_topk_mask",
    )(qi, w, kit)


def _attn_kernel(it_ref, jt_ref, last_ref, q_ref, k_ref, v_ref, m_ref, o_ref, qs_ref, mx_ref, l_ref, acc_ref,
                 *, group):
    step = pl.program_id(1)
    tq = q_ref.shape[0]
    tk = k_ref.shape[0]
    n_kv = k_ref.shape[1] // SA_HEAD
    rows = group * tq
    n_lane_tiles = tk // LANES

    @pl.when(jt_ref[step] == 0)
    def _():
        for n in range(n_kv):
            for g in range(group):
                h = n * group + g
                qs_ref[n, g * tq:(g + 1) * tq, :] = q_ref[:, h * SA_HEAD:(h + 1) * SA_HEAD]
        mx_ref[...] = jnp.full(mx_ref.shape, NEG_BIG, F32)
        l_ref[...] = jnp.zeros(l_ref.shape, F32)
        acc_ref[...] = jnp.zeros(acc_ref.shape, F32)

    bias = jnp.where(m_ref[...].astype(F32) > 0.0, 0.0, NEG_BIG)
    kv = range(n_kv)
    s = [_nt(qs_ref[n], k_ref[:, n * SA_HEAD:(n + 1) * SA_HEAD]) for n in kv]
    s = [(x.reshape(group, tq, tk) + bias[None]).reshape(rows, tk) for x in s]
    tiles = [[x[:, c * LANES:(c + 1) * LANES] for c in range(n_lane_tiles)] for x in s]
    m_old = [mx_ref[n] for n in kv]
    m_new = [jnp.maximum(mo, jnp.max(functools.reduce(jnp.maximum, t), axis=1, keepdims=True))
             for mo, t in zip(m_old, tiles)]
    alpha = [jnp.exp2(mo - mn) for mo, mn in zip(m_old, m_new)]
    p = [[jnp.exp2(x - mn) for x in t] for t, mn in zip(tiles, m_new)]
    for n in kv:
        mx_ref[n] = m_new[n]
        l_ref[n] = alpha[n] * l_ref[n] + functools.reduce(jnp.add, p[n])
    pv = [jnp.dot(_mx(jnp.concatenate(p[n], axis=1)), v_ref[:, n * SA_HEAD:(n + 1) * SA_HEAD],
                  preferred_element_type=F32) for n in kv]
    for n in kv:
        acc_ref[n] = alpha[n] * acc_ref[n] + pv[n]

    @pl.when(last_ref[step] == 1)
    def _():
        for n in range(n_kv):
            o = acc_ref[n] / jnp.sum(l_ref[n], axis=1, keepdims=True)
            for g in range(group):
                h = n * group + g
                o_ref[:, h * SA_HEAD:(h + 1) * SA_HEAD] = o[g * tq:(g + 1) * tq].astype(o_ref.dtype)


def _attn_tiles(t, tq, tk, nkb, past):
    ii, jj, last = [], [], []
    for i in range(t // tq):
        n_adm = min(nkb, -(-(past + (i + 1) * tq) // tk))
        for j in range(n_adm):
            ii.append(i)
            jj.append(j)
            last.append(int(j == n_adm - 1))
    return tuple(jnp.asarray(np.array(x, np.int32)) for x in (ii, jj, last))


def _attention(q, k_all, v_all, mask, n_streams, t, tq, tk, past):
    m, sa_w = q.shape
    kv_w = k_all.shape[2]
    nkb = mask.shape[1]
    nq = t // tq
    n_heads = sa_w // SA_HEAD
    group = n_heads // SA_KV_HEADS
    it, jt, last = _attn_tiles(t, tq, tk, nkb, past)
    grid_spec = pltpu.PrefetchScalarGridSpec(
        num_scalar_prefetch=3,
        grid=(n_streams, int(it.shape[0])),
        in_specs=[
            pl.BlockSpec((tq, sa_w), lambda b, s, it, jt, lt: (b * nq + it[s], 0)),
            pl.BlockSpec((None, tk, kv_w), lambda b, s, it, jt, lt: (b, jt[s], 0)),
            pl.BlockSpec((None, tk, kv_w), lambda b, s, it, jt, lt: (b, jt[s], 0)),
            pl.BlockSpec((None, None, tq, tk), lambda b, s, it, jt, lt: (b, jt[s], it[s], 0)),
        ],
        out_specs=pl.BlockSpec((tq, sa_w), lambda b, s, it, jt, lt: (b * nq + it[s], 0)),
        scratch_shapes=[pltpu.VMEM((SA_KV_HEADS, group * tq, SA_HEAD), MXU_DTYPE)]
        + [pltpu.VMEM((SA_KV_HEADS, group * tq, SA_HEAD), F32)] * 3,
    )
    return pl.pallas_call(
        functools.partial(_attn_kernel, group=group),
        grid_spec=grid_spec,
        out_shape=jax.ShapeDtypeStruct((m, sa_w), MXU_DTYPE),
        compiler_params=_cparams(2, 40 << 20),
        name="masked_flash_attention",
    )(it, jt, last, q, k_all, v_all, mask)


def _prepare_weights(w_in, mu_shift, w_decay_up, decay_bias, w_a_up, a_bias, w_gate_up, k_k, k_a, r_k,
                     lnx_w, lnx_b, w_out, w_up, conv_w, conv_b, w_down, d_model):
    rw = w_decay_up.shape[1]
    rw_cols = 3 * rw + D_DECAY + D_AAA + D_GATE
    sa_w = d_model - rw
    kv_w = SA_KV_HEADS * SA_HEAD
    qi_w = IDX_HEADS * IDX_DIM
    o = {}
    o["r"] = 0
    o["wd"] = rw
    o["k"] = rw + D_DECAY
    o["v"] = 2 * rw + D_DECAY
    o["ad"] = 3 * rw + D_DECAY
    o["gd"] = 3 * rw + D_DECAY + D_AAA
    o["q"] = rw_cols
    o["ksa"] = rw_cols + sa_w
    o["vsa"] = o["ksa"] + kv_w
    o["qi"] = o["vsa"] + kv_w
    o["kiw"] = o["qi"] + qi_w
    kiw_w = IDX_DIM + IDX_HEADS

    def seg(x, name, width, pad_to=None):
        s = x[..., o[name]:o[name] + width]
        if pad_to is not None and pad_to > width:
            s = jnp.pad(s, [(0, 0)] * (s.ndim - 1) + [(0, pad_to - width)])
        return s

    def rw_small(x):
        return jnp.concatenate([seg(x, "wd", D_DECAY, LANES), seg(x, "ad", D_AAA, LANES), seg(x, "gd", D_GATE)], -1)

    w_in_l = jnp.concatenate([
        seg(w_in, "r", rw), seg(w_in, "k", rw), seg(w_in, "v", rw), seg(w_in, "q", sa_w), seg(w_in, "qi", qi_w),
        seg(w_in, "ksa", kv_w), seg(w_in, "vsa", kv_w), rw_small(w_in), seg(w_in, "kiw", kiw_w, LANES)], axis=1)
    starts = {"r": 0, "k": rw, "v": 2 * rw, "q": 3 * rw, "qi": 3 * rw + sa_w}
    starts["ksa"] = starts["qi"] + qi_w
    starts["vsa"] = starts["ksa"] + kv_w
    starts["small"] = starts["vsa"] + kv_w
    starts["kiw"] = starts["small"] + 4 * LANES
    pad_rows = lambda x, n: jnp.pad(x, ((0, n - x.shape[0]), (0, 0)))
    row = lambda x: x.reshape(1, -1).astype(F32)
    mu = mu_shift.reshape(1, -1)
    w = {
        "w_in": _mx(w_in_l),
        "mu_r": seg(mu, "r", rw), "mu_k": seg(mu, "k", rw), "mu_v": seg(mu, "v", rw), "mu_small": rw_small(mu),
        "w_decay_up": _mx(pad_rows(w_decay_up, LANES)), "decay_bias": row(decay_bias),
        "w_a_up": _mx(pad_rows(w_a_up, LANES)), "a_bias": row(a_bias),
        "w_gate_up": _mx(w_gate_up),
        "k_k": row(k_k), "k_a": row(k_a), "r_k": row(r_k), "lnx_w": row(lnx_w), "lnx_b": row(lnx_b),
        "w_out": _mx(w_out), "w_up": _mx(w_up), "conv_w": conv_w, "conv_b": row(conv_b), "w_down": _mx(w_down),
    }
    dims = {"rw": rw, "rw_cols": rw_cols, "sa_w": sa_w, "kv_w": kv_w, "qi_w": qi_w, "orig": o, "starts": starts,
            "seg": seg, "rw_small": rw_small}
    return w, dims


def _layer(x, past_k, past_v, past_ki, wkv0, shift0, conv0, norms, w, dims, tiles):
    bsz, t, d = x.shape
    past = past_k.shape[1]
    m = bsz * t
    rw, sa_w, kv_w, qi_w = dims["rw"], dims["sa_w"], dims["kv_w"], dims["qi_w"]
    starts, seg, rw_small = dims["starts"], dims["seg"], dims["rw_small"]
    assert t % CHUNK == 0 and past % CHUNK == 0
    tm, tq, kb = tiles["tm"], tiles["tq"], tiles["kb"]
    g_mix_pre, g_mix_post, g_ffn_pre, g_ffn_post = norms

    x2 = x.reshape(m, d)
    h1 = _rmsnorm(x2, g_mix_pre, tiles["tm_norm"])
    pfull = _matmul(h1, w["w_in"], tiles["tm_in"], tiles["tn_in"])

    sh = shift0.reshape(bsz, 1, -1)
    shift_l = {"r": seg(sh, "r", rw), "k": seg(sh, "k", rw), "v": seg(sh, "v", rw), "small": rw_small(sh)}
    cols_rw = {n: starts[n] // RW_GROUP for n in ("r", "k", "v", "small")}
    o_rw, wkv1 = _rwkv(pfull, cols_rw, shift_l, wkv0, w, bsz, t)

    pos = past + jnp.arange(t, dtype=jnp.int32)
    cols_sa = {"q": starts["q"] // sa_w, "ksa": starts["ksa"] // kv_w, "qi": starts["qi"] // qi_w,
               "kiw": starts["kiw"] // LANES}
    q_r, k_r, qi_r, kiw_r = _rope(pfull, cols_sa, _rope_tables(pos), bsz, t, tq, sa_w, kv_w, qi_w)
    k_new = k_r.reshape(bsz, t, kv_w)
    v_new = pfull[:, starts["vsa"]:starts["vsa"] + kv_w].reshape(bsz, t, kv_w)
    ki_new = kiw_r[:, :IDX_DIM].reshape(bsz, t, IDX_DIM)
    l_all = past + t
    lp = -(-l_all // kb) * kb
    nkb = lp // kb
    cat = lambda old, new: jnp.pad(
        jnp.concatenate([_mx(old.reshape(bsz, past, new.shape[-1])), _mx(new)], axis=1),
        ((0, 0), (0, lp - l_all), (0, 0)))
    k_all, v_all, ki_all = cat(past_k, k_new), cat(past_v, v_new), cat(past_ki, ki_new)
    kit = ki_all.reshape(bsz, nkb, kb, IDX_DIM).transpose(0, 1, 3, 2)
    topk = min(TOPK_MAX, l_all // 4)
    mask = _index_mask(qi_r, kiw_r, kit, bsz, t, tq, kb, past, topk)
    o_sa = _attention(q_r, k_all, v_all, mask, bsz, t, tq, kb, past)

    mix_in = jnp.concatenate([o_rw, o_sa], axis=1)
    x1, h2 = _matmul_norm_residual(mix_in, w["w_out"], x2, g_mix_post, g_ffn_pre, tm, tiles["tk"], True, "out_proj")
    act, conv1 = _ffn_up(h2, w["w_up"], conv0, w["conv_w"], w["conv_b"], bsz, tiles["tm_up"], tiles["tn_up"])
    x_out, _ = _matmul_norm_residual(act, w["w_down"], x1, g_ffn_post, g_ffn_post, tm, tiles["tk"], False, "ffn_down")

    orig = dims["orig"]
    last = pfull.reshape(bsz, t, -1)[:, t - 1:t]
    sm = last[..., starts["small"]:starts["small"] + 4 * LANES]
    shift1 = jnp.concatenate([
        last[..., starts["r"]:starts["r"] + rw], sm[..., :D_DECAY],
        last[..., starts["k"]:starts["k"] + rw], last[..., starts["v"]:starts["v"] + rw],
        sm[..., LANES:LANES + D_AAA], sm[..., 2 * LANES:2 * LANES + D_GATE]], axis=-1)
    del orig
    new = (k_new.reshape(bsz, t, SA_KV_HEADS, SA_HEAD), v_new.reshape(bsz, t, SA_KV_HEADS, SA_HEAD), ki_new,
           wkv1, shift1, conv1)
    return x_out.reshape(bsz, t, d), new


def _tiles(bsz, t):
    big = t >= 1024
    return {
        "tm_norm": 256 if big else CHUNK,
        "tm_in": 1024 if big else bsz * t,
        "tn_in": 640,
        "tm": 256 if big else CHUNK,
        "tk": 512,
        "tm_up": 512 if big else CHUNK,
        "tn_up": 512,
        "tq": 256 if big else CHUNK,
        "kb": 512,
    }


def _run_stream(x, past_k, past_v, past_ki, wkv0, shift0, conv0, norm_w, weights, dims, depth):
    outs = []
    tiles = _tiles(x.shape[0], x.shape[1])
    for l in range(depth):
        w_l = {n: (v[l] if n != "conv_w" else v[l]) for n, v in weights.items()}
        norms = tuple(g[l].reshape(1, -1) for g in norm_w)
        x, st = _layer(x, past_k[l], past_v[l], past_ki[l], wkv0[l], shift0[l], conv0[l], norms, w_l, dims[l], tiles)
        outs.append(st)
    return x, [jnp.stack(s) for s in zip(*outs)]


def kernel(x_prompt, x_sample, cache_k, cache_v, cache_kidx, state_wkv, state_shift, state_conv, norm_mix_pre, norm_mix_post, norm_ffn_pre, norm_ffn_post, w_in, mu_shift, w_decay_up, decay_bias, w_a_up, a_bias, w_gate_up, k_k, k_a, r_k, lnx_w, lnx_b, w_out, w_up, conv_w, conv_b, w_down):
    depth, d_model = norm_mix_pre.shape
    per_layer = [_prepare_weights(w_in[l], mu_shift[l], w_decay_up[l], decay_bias[l], w_a_up[l], a_bias[l],
                                  w_gate_up[l], k_k[l], k_a[l], r_k[l], lnx_w[l], lnx_b[l], w_out[l], w_up[l],
                                  conv_w[l], conv_b[l], w_down[l], d_model) for l in range(depth)]
    weights = {n: [pw[0][n] for pw in per_layer] for n in per_layer[0][0]}
    dims = [pw[1] for pw in per_layer]
    norm_w = (norm_mix_pre, norm_mix_post, norm_ffn_pre, norm_ffn_post)

    dt = x_prompt.dtype
    bp = x_prompt.shape[0]
    rw_heads = state_wkv.shape[2]
    zk = jnp.zeros((depth, bp, 0, SA_KV_HEADS, SA_HEAD), dt)
    zki = jnp.zeros((depth, bp, 0, IDX_DIM), dt)
    zwkv = jnp.zeros((depth, bp, rw_heads, RW_HEAD, RW_HEAD), dt)
    zshift = jnp.zeros((depth, bp, 1, state_shift.shape[-1]), dt)
    zconv = jnp.zeros((depth, bp, CONV_W - 1, state_conv.shape[-1]), dt)
    y_prompt, p_new = _run_stream(x_prompt, zk, zk, zki, zwkv, zshift, zconv, norm_w, weights, dims, depth)
    y_sample, s_new = _run_stream(x_sample, cache_k, cache_v, cache_kidx, state_wkv, state_shift, state_conv,
                                  norm_w, weights, dims, depth)
    return (y_prompt, y_sample, *p_new, *s_new)
```

```python
import functools

import numpy as np
import jax
import jax.numpy as jnp
from jax import lax
from jax.experimental import pallas as pl
from jax.experimental.pallas import tpu as pltpu

F32 = jnp.float32
MXU_DTYPE = jnp.bfloat16
HI = lax.Precision.HIGHEST

CHUNK = 64
RW_HEAD = 64
D_DECAY = 96
D_AAA = 96
D_GATE = 256
SA_HEAD = 128
SA_KV_HEADS = 4
IDX_HEADS = 16
IDX_DIM = 64
TOPK_MAX = 256
CONV_W = 3
ROPE_THETA = 10000.0
NORM_EPS = 1e-6
LNX_EPS = 64e-5

LANES = 128
V7X_VMEM_LIMIT_BYTES = 60000 * 1024
INT_MIN = -(2 ** 31)
NEG_BIG = -1e30


def _cparams(n_grid, vmem_bytes):
    limit = int(min(V7X_VMEM_LIMIT_BYTES, max(32 * 1024 * 1024, vmem_bytes)))
    return pltpu.CompilerParams(dimension_semantics=("arbitrary",) * n_grid, vmem_limit_bytes=limit)


def _nt(a, b, precision=None):
    return lax.dot_general(a, b, (((1,), (1,)), ((), ())), precision=precision, preferred_element_type=F32)


def _tn(a, b, precision=None):
    return lax.dot_general(a, b, (((0,), (0,)), ((), ())), precision=precision, preferred_element_type=F32)


def _mx(x):
    return x.astype(MXU_DTYPE)


def _rmsnorm_kernel(x_ref, g_ref, o_ref):
    x = x_ref[...]
    y = x * lax.rsqrt(jnp.mean(x * x, axis=-1, keepdims=True) + NORM_EPS)
    o_ref[...] = (y * g_ref[...]).astype(o_ref.dtype)


def _rmsnorm(x, g, tm):
    m, d = x.shape
    return pl.pallas_call(
        _rmsnorm_kernel,
        grid=(m // tm,),
        in_specs=[pl.BlockSpec((tm, d), lambda i: (i, 0)), pl.BlockSpec((1, d), lambda i: (0, 0))],
        out_specs=pl.BlockSpec((tm, d), lambda i: (i, 0)),
        out_shape=jax.ShapeDtypeStruct((m, d), MXU_DTYPE),
        compiler_params=_cparams(1, 6 * tm * d * 4),
        name="rmsnorm",
    )(x, g)


def _mm_kernel(a_ref, b_ref, o_ref):
    o_ref[...] = jnp.dot(a_ref[...], b_ref[...], preferred_element_type=F32)


def _matmul(a, b, tm, tn):
    m, k = a.shape
    n = b.shape[1]
    isz = jnp.dtype(MXU_DTYPE).itemsize
    vmem = 2 * (tm * k * isz + k * tn * isz + tm * tn * 4) + 2 * tm * tn * 4 + (8 << 20)
    return pl.pallas_call(
        _mm_kernel,
        grid=(m // tm, n // tn),
        in_specs=[pl.BlockSpec((tm, k), lambda i, j: (i, 0)), pl.BlockSpec((k, tn), lambda i, j: (0, j))],
        out_specs=pl.BlockSpec((tm, tn), lambda i, j: (i, j)),
        out_shape=jax.ShapeDtypeStruct((m, n), F32),
        compiler_params=_cparams(2, vmem),
        name="in_proj",
    )(a, b)


MM_NORM_COLS = 1024
MM_NORM_ROWS = 128


def _mm_norm_kernel(a_ref, w_ref, x_ref, gpost_ref, gnext_ref, x1_ref, *rest, nk, emit_next):
    k = pl.program_id(1)
    tm, d = x1_ref.shape
    a = a_ref[...]
    for c in range(0, d, MM_NORM_COLS):
        part = jnp.dot(a, w_ref[:, c:c + MM_NORM_COLS], preferred_element_type=F32)

        @pl.when(k == 0)
        def _():
            x1_ref[:, c:c + MM_NORM_COLS] = part

        @pl.when(k > 0)
        def _():
            x1_ref[:, c:c + MM_NORM_COLS] += part

    @pl.when(k == nk - 1)
    def _():
        step = min(tm, MM_NORM_ROWS)
        for r in range(0, tm, step):
            rows = slice(r, r + step)
            y = x1_ref[rows, :]
            y = y * lax.rsqrt(jnp.mean(y * y, axis=-1, keepdims=True) + NORM_EPS) * gpost_ref[...]
            x1 = x_ref[rows, :] + y
            x1_ref[rows, :] = x1
            if emit_next:
                h = x1 * lax.rsqrt(jnp.mean(x1 * x1, axis=-1, keepdims=True) + NORM_EPS) * gnext_ref[...]
                rest[0][rows, :] = h.astype(rest[0].dtype)


def _matmul_norm_residual(a, w, x, g_post, g_next, tm, tk, emit_next, name):
    m, kdim = a.shape
    d = w.shape[1]
    nk = kdim // tk
    isz = jnp.dtype(MXU_DTYPE).itemsize
    out_shape = [jax.ShapeDtypeStruct((m, d), F32)]
    out_specs = [pl.BlockSpec((tm, d), lambda i, k: (i, 0))]
    if emit_next:
        out_shape.append(jax.ShapeDtypeStruct((m, d), MXU_DTYPE))
        out_specs.append(pl.BlockSpec((tm, d), lambda i, k: (i, 0)))
    vmem = (2 * (tm * tk * isz + tk * d * isz + tm * d * 4 + emit_next * tm * d * isz) + tm * d * 4
            + 2 * tm * MM_NORM_COLS * 4 + 6 * MM_NORM_ROWS * d * 4 + (4 << 20))
    res = pl.pallas_call(
        functools.partial(_mm_norm_kernel, nk=nk, emit_next=emit_next),
        grid=(m // tm, nk),
        in_specs=[
            pl.BlockSpec((tm, tk), lambda i, k: (i, k)),
            pl.BlockSpec((tk, d), lambda i, k: (k, 0)),
            pl.BlockSpec((tm, d), lambda i, k: (i, 0), pipeline_mode=pl.Buffered(1)),
            pl.BlockSpec((1, d), lambda i, k: (0, 0)),
            pl.BlockSpec((1, d), lambda i, k: (0, 0)),
        ],
        out_specs=out_specs,
        out_shape=out_shape,
        compiler_params=_cparams(2, vmem),
        name=name,
    )(a, w, x, g_post, g_next)
    return res if emit_next else (res[0], None)


def _gelu_tanh(x):
    return 0.5 * x * (1.0 + jnp.tanh(0.7978845608028654 * (x + 0.044715 * x * x * x)))


def _ffn_up_kernel(h_ref, wg_ref, wv_ref, c0g_ref, c0v_ref, cwg_ref, cwv_ref, cbg_ref, cbv_ref,
                   a_ref, cg_ref, cv_ref, carry_g, carry_v, *, blocks_per_stream, tm):
    i = pl.program_id(1) % blocks_per_stream

    @pl.when(i == 0)
    def _():
        carry_g[...] = c0g_ref[...]
        carry_v[...] = c0v_ref[...]

    h = h_ref[...]
    row = lax.broadcasted_iota(jnp.int32, (tm, 1), 0)

    def conv(u, carry_ref, cw_ref, cb_ref, out_ref):
        p = carry_ref[...]
        u1 = jnp.where(row == 0, p[1:2], pltpu.roll(u, 1, axis=0))
        u2 = jnp.where(row == 0, p[0:1], jnp.where(row == 1, p[1:2], pltpu.roll(u, 2, axis=0)))
        cw = cw_ref[...]
        last = u[tm - 2:tm]
        carry_ref[...] = last
        out_ref[...] = last
        return u2 * cw[0:1] + u1 * cw[1:2] + u * cw[2:3] + cb_ref[...]

    gate = conv(jnp.dot(h, wg_ref[...], preferred_element_type=F32), carry_g, cwg_ref, cbg_ref, cg_ref)
    val = conv(jnp.dot(h, wv_ref[...], preferred_element_type=F32), carry_v, cwv_ref, cbv_ref, cv_ref)
    a_ref[...] = (_gelu_tanh(gate) * val).astype(a_ref.dtype)


def _ffn_up(h, w_up, conv0, conv_w, conv_b, n_streams, tm, tn):
    m, d = h.shape
    f2 = w_up.shape[1]
    f = f2 // 2
    nj = f // tn
    nr = m // tm
    bps = nr // n_streams
    isz = jnp.dtype(MXU_DTYPE).itemsize
    vmem = 2 * (tm * d * isz + 2 * d * tn * isz + tm * tn * isz) + 10 * tm * tn * 4 + (4 << 20)
    col = lambda off: (lambda j, r: (0, off + j))
    st = lambda off: (lambda j, r: (r // bps, 0, off + j))
    a, cg, cv = pl.pallas_call(
        functools.partial(_ffn_up_kernel, blocks_per_stream=bps, tm=tm),
        grid=(nj, nr),
        in_specs=[
            pl.BlockSpec((tm, d), lambda j, r: (r, 0)),
            pl.BlockSpec((d, tn), col(0)),
            pl.BlockSpec((d, tn), col(nj)),
            pl.BlockSpec((None, CONV_W - 1, tn), st(0)),
            pl.BlockSpec((None, CONV_W - 1, tn), st(nj)),
            pl.BlockSpec((CONV_W, tn), col(0)),
            pl.BlockSpec((CONV_W, tn), col(nj)),
            pl.BlockSpec((1, tn), col(0)),
            pl.BlockSpec((1, tn), col(nj)),
        ],
        out_specs=[
            pl.BlockSpec((tm, tn), lambda j, r: (r, j)),
            pl.BlockSpec((None, CONV_W - 1, tn), st(0)),
            pl.BlockSpec((None, CONV_W - 1, tn), st(0)),
        ],
        out_shape=[
            jax.ShapeDtypeStruct((m, f), MXU_DTYPE),
            jax.ShapeDtypeStruct((n_streams, CONV_W - 1, f), F32),
            jax.ShapeDtypeStruct((n_streams, CONV_W - 1, f), F32),
        ],
        scratch_shapes=[pltpu.VMEM((CONV_W - 1, tn), F32), pltpu.VMEM((CONV_W - 1, tn), F32)],
        compiler_params=_cparams(2, vmem),
        name="ffn_up_conv",
    )(h, w_up, w_up, conv0, conv0, conv_w, conv_w, conv_b, conv_b)
    return a, jnp.concatenate([cg, cv], axis=-1)


RW_GROUP = 8 * LANES
RW_SMALL = 4 * LANES


def _softplus(z):
    return jnp.maximum(z, 0.0) + jnp.log(1.0 + jnp.exp(-jnp.abs(z)))


def _sigmoid(z):
    return 1.0 / (1.0 + jnp.exp(-z))


def _split(x):
    hi = x.astype(MXU_DTYPE)
    return hi, (x - hi.astype(F32)).astype(MXU_DTYPE)


def _dg(a, b, dims):
    return lax.dot_general(_mx(a), _mx(b), (dims, ((), ())), preferred_element_type=F32)


_NN = ((1,), (0,))
_NT = ((1,), (1,))
_TN = ((0,), (0,))


def _segsum(x, seg):
    hi, lo = _split(x)
    return jnp.dot(hi, seg, preferred_element_type=F32) + jnp.dot(lo, seg, preferred_element_type=F32)


def _each(f, *lists):
    return [f(*args) for args in zip(*lists)]


def _wkv_chunk(r, k, v, kk, b, cum, logd, s_bd, consts):
    m0, m1, tri_mask = consts
    c = r[0].shape[0]
    c2 = 2 * c
    stack = lambda x: jnp.concatenate([x * m0, x * m1], axis=0)
    g_end = _each(lambda cm: jnp.exp(cm[c - 1:c]), cum)
    e_inc = _each(jnp.exp, cum)
    e_prev = _each(lambda cm, ld: jnp.exp(cm - ld), cum, logd)
    e_neg = _each(lambda cm: jnp.exp(-cm), cum)
    e_tail = _each(lambda g, e: g * e, g_end, e_neg)
    vs = _each(stack, v)
    lhs = _each(lambda kk_, r_, ep, ei: jnp.concatenate([stack(kk_ * ep), stack(r_ * ei)], axis=0),
                kk, r, e_prev, e_inc)
    rhs = _each(lambda k_, b_, en: jnp.concatenate([stack(k_ * en), stack(b_ * en)], axis=0), k, b, e_neg)
    tails = _each(lambda k_, b_, et: jnp.concatenate([stack(k_ * et), stack(b_ * et)], axis=0), k, b, e_tail)
    amat = _each(lambda x, y: jnp.where(tri_mask, _dg(x, y, _NT), 0.0), lhs, rhs)
    sp = _each(lambda x, s: _dg(x, s, _NT), lhs, s_bd)
    av = _each(lambda a_, v_: _dg(a_[:, :c2], v_, _NN), amat, vs)
    pw = _each(lambda a_: -a_[:c2, c2:], amat)
    x = _each(lambda s_, a_: s_[:c2] + a_[:c2], sp, av)
    n_lvl = int(np.log2(c))
    for lvl in range(n_lvl - 1):
        z = _each(lambda p_, x_: _dg(p_, jnp.concatenate([p_, x_], axis=1), _NN), pw, x)
        pw = _each(lambda z_: z_[:, :c2], z)
        x = _each(lambda x_, z_: x_ + z_[:, c2:], x, z)
    u = _each(lambda p_, x_: x_ + _dg(p_, x_, _NN), pw, x)
    ys = _each(lambda s_, a_, am, u_: s_[c2:] + a_[c2:] - _dg(am[c2:, c2:], u_, _NN), sp, av, amat, u)
    y = _each(lambda y_: y_[:c] + y_[c:], ys)
    s_new = _each(lambda s, g, v_, u_, t_: s * g + _dg(jnp.concatenate([v_, -u_], axis=0), t_, _TN),
                  s_bd, g_end, vs, u, tails)
    return y, s_new


def _rwkv_kernel(pr_ref, pk_ref, pv_ref, ps_ref, sr_ref, sk_ref, sv_ref, ss_ref, wkv0_ref,
                 mur_ref, muk_ref, muv_ref, mus_ref, wdec_ref, dbias_ref, wa_ref, abias_ref, wg_ref,
                 kk_ref, ka_ref, rk_ref, lnw_ref, lnb_ref,
                 o_ref, wkv1_ref,
                 cr_ref, ck_ref, cv_ref, cs_ref, state_ref, *, n_chunks):
    c = pl.program_id(2)
    C = CHUNK
    npairs = RW_GROUP // LANES

    lane = lax.broadcasted_iota(jnp.int32, (1, LANES), 1)
    m0 = (lane < RW_HEAD).astype(F32)
    m1 = 1.0 - m0
    ri = lax.broadcasted_iota(jnp.int32, (4 * C, 4 * C), 0)
    ci = lax.broadcasted_iota(jnp.int32, (4 * C, 4 * C), 1)
    same_head = ((ri // C) % 2) == ((ci // C) % 2)
    tri_mask = same_head & ((ri % C) + ri // (2 * C) > (ci % C))
    li = lax.broadcasted_iota(jnp.int32, (LANES, LANES), 0)
    lj = lax.broadcasted_iota(jnp.int32, (LANES, LANES), 1)
    seg = ((li // RW_HEAD) == (lj // RW_HEAD)).astype(MXU_DTYPE)
    consts = (m0, m1, tri_mask)

    @pl.when(c == 0)
    def _():
        cr_ref[...] = sr_ref[...]
        ck_ref[...] = sk_ref[...]
        cv_ref[...] = sv_ref[...]
        cs_ref[...] = ss_ref[...]
        z = jnp.zeros((RW_HEAD, RW_HEAD), F32)
        for p in range(npairs):
            s0 = wkv0_ref[2 * p]
            s1 = wkv0_ref[2 * p + 1]
            state_ref[p] = jnp.concatenate(
                [jnp.concatenate([s0, z], axis=1), jnp.concatenate([z, s1], axis=1)], axis=0)

    row = lax.broadcasted_iota(jnp.int32, (C, 1), 0)

    def shifted(p_ref, carry_ref, mu_ref):
        p = p_ref[...]
        prev = jnp.where(row == 0, carry_ref[...], pltpu.roll(p, 1, axis=0))
        carry_ref[...] = p[C - 1:C]
        return p + (prev - p) * mu_ref[...]

    r = shifted(pr_ref, cr_ref, mur_ref)
    k = shifted(pk_ref, ck_ref, muk_ref)
    v = shifted(pv_ref, cv_ref, muv_ref)
    sm = shifted(ps_ref, cs_ref, mus_ref)
    wd, ad, gd = sm[:, 0:LANES], sm[:, LANES:2 * LANES], sm[:, 2 * LANES:4 * LANES]

    dec_in = dbias_ref[...] + jnp.dot(_mx(jnp.tanh(wd)), wdec_ref[...], preferred_element_type=F32)
    w_log = -_softplus(-dec_in) - 0.5
    logd = -jnp.exp(w_log)
    a = _sigmoid(abias_ref[...] + jnp.dot(_mx(ad), wa_ref[...], preferred_element_type=F32))
    g = jnp.dot(_mx(_sigmoid(gd)), wg_ref[...], preferred_element_type=F32)
    kk = k * kk_ref[...]
    k2 = k * (1.0 + (a - 1.0) * ka_ref[...])
    rkr = r * k2 * rk_ref[...]
    cum = logd
    for sft in (1, 2, 4, 8, 16, 32):
        cum = cum + jnp.where(row >= sft, pltpu.roll(cum, sft, axis=0), 0.0)

    pairs = lambda x: [x[:, p * LANES:(p + 1) * LANES] for p in range(npairs)]
    r_p, k_p, v_p, a_p, cum_p, logd_p = pairs(r), pairs(k2), pairs(v), pairs(a), pairs(cum), pairs(logd)
    sums = _each(lambda kk_, rkr_: _segsum(jnp.concatenate([kk_ * kk_, rkr_], axis=0), seg), pairs(kk), pairs(rkr))
    kk_p = _each(lambda kk_, s_: kk_ * lax.rsqrt(jnp.maximum(s_[:C], 1e-24)), pairs(kk), sums)
    b_p = _each(lambda kk_, a_: kk_ * a_, kk_p, a_p)
    y_p, s_new = _wkv_chunk(r_p, k_p, v_p, kk_p, b_p, cum_p, logd_p, [state_ref[p] for p in range(npairs)], consts)
    for p in range(npairs):
        state_ref[p] = s_new[p]
    yc = _each(lambda y_: y_ - _segsum(y_, seg) * (1.0 / RW_HEAD), y_p)
    var = _each(lambda yc_: _segsum(yc_ * yc_, seg) * (1.0 / RW_HEAD), yc)
    for p in range(npairs):
        sl = slice(p * LANES, (p + 1) * LANES)
        yn = yc[p] * lax.rsqrt(var[p] + LNX_EPS) * lnw_ref[:, sl] + lnb_ref[:, sl]
        o_ref[:, sl] = ((yn + sums[p][C:] * v_p[p]) * g[:, sl]).astype(o_ref.dtype)

    @pl.when(c == n_chunks - 1)
    def _():
        for p in range(npairs):
            s = state_ref[p]
            wkv1_ref[2 * p] = s[:RW_HEAD, :RW_HEAD]
            wkv1_ref[2 * p + 1] = s[RW_HEAD:, RW_HEAD:]


def _rwkv(pfull, cols, shift0, wkv0, w, n_streams, t):
    m = pfull.shape[0]
    rw = w["k_k"].shape[1]
    ng = rw // RW_GROUP
    nc = t // CHUNK
    hg = RW_GROUP // RW_HEAD
    rowblk = lambda off: (lambda b, g, c: (b * nc + c, off + g))
    fixed = lambda off: (lambda b, g, c: (b * nc + c, off))
    st = lambda b, g, c: (b, 0, g)
    st0 = lambda b, g, c: (b, 0, 0)
    wcol = lambda b, g, c: (0, g)
    in_specs = [
        pl.BlockSpec((CHUNK, RW_GROUP), rowblk(cols["r"])),
        pl.BlockSpec((CHUNK, RW_GROUP), rowblk(cols["k"])),
        pl.BlockSpec((CHUNK, RW_GROUP), rowblk(cols["v"])),
        pl.BlockSpec((CHUNK, RW_SMALL), fixed(cols["small"])),
        pl.BlockSpec((None, 1, RW_GROUP), st),
        pl.BlockSpec((None, 1, RW_GROUP), st),
        pl.BlockSpec((None, 1, RW_GROUP), st),
        pl.BlockSpec((None, 1, RW_SMALL), st0),
        pl.BlockSpec((None, hg, RW_HEAD, RW_HEAD), lambda b, g, c: (b, g, 0, 0)),
        pl.BlockSpec((1, RW_GROUP), wcol), pl.BlockSpec((1, RW_GROUP), wcol), pl.BlockSpec((1, RW_GROUP), wcol),
        pl.BlockSpec((1, RW_SMALL), lambda b, g, c: (0, 0)),
        pl.BlockSpec((LANES, RW_GROUP), wcol), pl.BlockSpec((1, RW_GROUP), wcol),
        pl.BlockSpec((LANES, RW_GROUP), wcol), pl.BlockSpec((1, RW_GROUP), wcol),
        pl.BlockSpec((2 * LANES, RW_GROUP), wcol),
        pl.BlockSpec((1, RW_GROUP), wcol), pl.BlockSpec((1, RW_GROUP), wcol), pl.BlockSpec((1, RW_GROUP), wcol),
        pl.BlockSpec((1, RW_GROUP), wcol), pl.BlockSpec((1, RW_GROUP), wcol),
    ]
    o, wkv1 = pl.pallas_call(
        functools.partial(_rwkv_kernel, n_chunks=nc),
        grid=(n_streams, ng, nc),
        in_specs=in_specs,
        out_specs=[
            pl.BlockSpec((CHUNK, RW_GROUP), lambda b, g, c: (b * nc + c, g)),
            pl.BlockSpec((None, hg, RW_HEAD, RW_HEAD), lambda b, g, c: (b, g, 0, 0)),
        ],
        out_shape=[
            jax.ShapeDtypeStruct((m, rw), MXU_DTYPE),
            jax.ShapeDtypeStruct((n_streams, rw // RW_HEAD, RW_HEAD, RW_HEAD), F32),
        ],
        scratch_shapes=[pltpu.VMEM((1, RW_GROUP), F32)] * 3 + [pltpu.VMEM((1, RW_SMALL), F32)]
        + [pltpu.VMEM((RW_GROUP // LANES, LANES, LANES), F32)],
        compiler_params=_cparams(3, 32 << 20),
        name="rwkv7_chunked",
    )(pfull, pfull, pfull, pfull, shift0["r"], shift0["k"], shift0["v"], shift0["small"], wkv0,
      w["mu_r"], w["mu_k"], w["mu_v"], w["mu_small"], w["w_decay_up"], w["decay_bias"], w["w_a_up"], w["a_bias"],
      w["w_gate_up"], w["k_k"], w["k_a"], w["r_k"], w["lnx_w"], w["lnx_b"])
    return o, wkv1


def _rope_kernel(q_ref, k_ref, qi_ref, kiw_ref, ca_ref, sa_ref, cb_ref, sb1_ref, sb2_ref,
                 qo_ref, ko_ref, qio_ref, kiwo_ref):
    ca, sa = ca_ref[...], sa_ref[...]
    cb, sb1, sb2 = cb_ref[...], sb1_ref[...], sb2_ref[...]

    def rope_head(x):
        return x * ca + pltpu.roll(x, SA_HEAD // 2, axis=1) * sa

    def rope_idx(x):
        return (x * cb + pltpu.roll(x, LANES - IDX_DIM // 2, axis=1) * sb1
                + pltpu.roll(x, IDX_DIM // 2, axis=1) * sb2)

    q_scale = float(SA_HEAD ** -0.5 * np.log2(np.e))
    for h in range(q_ref.shape[1] // LANES):
        sl = slice(h * LANES, (h + 1) * LANES)
        qo_ref[:, sl] = (rope_head(q_ref[:, sl]) * q_scale).astype(qo_ref.dtype)
    for h in range(k_ref.shape[1] // LANES):
        sl = slice(h * LANES, (h + 1) * LANES)
        ko_ref[:, sl] = rope_head(k_ref[:, sl])
    for hp in range(qi_ref.shape[1] // LANES):
        y = rope_idx(qi_ref[:, hp * LANES:(hp + 1) * LANES]).astype(qio_ref.dtype)
        qio_ref[2 * hp] = y[:, :IDX_DIM]
        qio_ref[2 * hp + 1] = y[:, IDX_DIM:]
    x = kiw_ref[...]
    lane = lax.broadcasted_iota(jnp.int32, (1, LANES), 1)
    kiwo_ref[...] = jnp.where(lane < IDX_DIM, rope_idx(x), x * float((IDX_HEADS * IDX_DIM) ** -0.5))


def _rope_tables(pos):
    def ang(half):
        inv = ROPE_THETA ** (-jnp.arange(half, dtype=F32) / half)
        return pos.astype(F32)[:, None] * inv[None, :]
    aa = ang(SA_HEAD // 2)
    ca = jnp.concatenate([jnp.cos(aa), jnp.cos(aa)], axis=1)
    sa = jnp.concatenate([-jnp.sin(aa), jnp.sin(aa)], axis=1)
    ab = ang(IDX_DIM // 2)
    z = jnp.zeros_like(ab)
    cb = jnp.concatenate([jnp.cos(ab)] * 4, axis=1)
    sb1 = jnp.concatenate([-jnp.sin(ab), z, -jnp.sin(ab), z], axis=1)
    sb2 = jnp.concatenate([z, jnp.sin(ab), z, jnp.sin(ab)], axis=1)
    return ca, sa, cb, sb1, sb2


def _rope(pfull, cols, tables, n_streams, t, tm, sa_w, kv_w, qi_w):
    m = pfull.shape[0]
    bps = t // tm
    tab = pl.BlockSpec((tm, LANES), lambda r: (r % bps, 0))
    return pl.pallas_call(
        _rope_kernel,
        grid=(m // tm,),
        in_specs=[
            pl.BlockSpec((tm, sa_w), lambda r: (r, cols["q"])),
            pl.BlockSpec((tm, kv_w), lambda r: (r, cols["ksa"])),
            pl.BlockSpec((tm, qi_w), lambda r: (r, cols["qi"])),
            pl.BlockSpec((tm, LANES), lambda r: (r, cols["kiw"])),
            tab, tab, tab, tab, tab,
        ],
        out_specs=[
            pl.BlockSpec((tm, sa_w), lambda r: (r, 0)),
            pl.BlockSpec((tm, kv_w), lambda r: (r, 0)),
            pl.BlockSpec((None, IDX_HEADS, tm, IDX_DIM), lambda r: (r // bps, 0, r % bps, 0)),
            pl.BlockSpec((tm, LANES), lambda r: (r, 0)),
        ],
        out_shape=[
            jax.ShapeDtypeStruct((m, sa_w), MXU_DTYPE),
            jax.ShapeDtypeStruct((m, kv_w), F32),
            jax.ShapeDtypeStruct((n_streams, IDX_HEADS, t, IDX_DIM), MXU_DTYPE),
            jax.ShapeDtypeStruct((m, LANES), F32),
        ],
        compiler_params=_cparams(1, 32 << 20),
        name="rope",
    )(pfull, pfull, pfull, pfull, *tables)


IDX_HEAD_GROUP = 4
IDX_VALUE_PASSES = 16


def _f32_to_key(x):
    bits = lax.bitcast_convert_type(x, jnp.int32)
    return bits ^ ((bits >> 31) & jnp.int32(0x7FFFFFFF))


def _key_to_f32(key):
    return lax.bitcast_convert_type(key ^ ((key >> 31) & jnp.int32(0x7FFFFFFF)), F32)


def _index_kernel(qi_ref, w_ref, kit_ref, mask_ref, keys_ref, wb_ref, *, tq, kb, nkb, past, topk):
    i = pl.program_id(1)
    n_adm = jnp.minimum(nkb, (past + (i + 1) * tq + kb - 1) // kb)
    row = lax.broadcasted_iota(jnp.int32, (tq, 1), 0)
    chunk_bits = CHUNK.bit_length() - 1
    lim = (((past + i * tq + row) >> chunk_bits) + 1) << chunk_bits
    lane = lax.broadcasted_iota(jnp.int32, (1, LANES), 1)
    ncol = kb // LANES
    hg = IDX_HEAD_GROUP

    w = w_ref[...]
    for h in range(IDX_HEADS):
        wb_ref[h] = jnp.broadcast_to(w[:, IDX_DIM + h:IDX_DIM + h + 1], (tq, LANES))

    def score_block(j, carry):
        smax, smin = carry
        kt = kit_ref[j]
        acc = [jnp.zeros((tq, LANES), F32) for _ in range(ncol)]
        for g0 in range(0, IDX_HEADS, hg):
            s = jnp.dot(qi_ref[g0:g0 + hg].reshape(hg * tq, IDX_DIM), kt, preferred_element_type=F32)
            for g in range(hg):
                wb = wb_ref[g0 + g]
                for c in range(ncol):
                    acc[c] = acc[c] + jnp.maximum(s[g * tq:(g + 1) * tq, c * LANES:(c + 1) * LANES], 0.0) * wb
        for c in range(ncol):
            score = acc[c] + 0.0
            adm = j * kb + c * LANES + lane < lim
            keys_ref[j, :, c * LANES:(c + 1) * LANES] = jnp.where(adm, _f32_to_key(score), jnp.int32(INT_MIN))
            smax = jnp.maximum(smax, jnp.where(adm, score, -jnp.inf))
            smin = jnp.minimum(smin, jnp.where(adm, score, jnp.inf))
        return smax, smin

    smax, smin = lax.fori_loop(0, n_adm, score_block,
                               (jnp.full((tq, LANES), -jnp.inf, F32), jnp.full((tq, LANES), jnp.inf, F32)))
    lo0 = _f32_to_key(jnp.min(smin, axis=1, keepdims=True))
    hi0 = _f32_to_key(jnp.max(smax, axis=1, keepdims=True)) + 1

    def count_ge(thr):
        def body(j, cnt):
            x = jnp.where(keys_ref[j] >= thr, 1.0, 0.0)
            for c in range(ncol):
                cnt = cnt + x[:, c * LANES:(c + 1) * LANES]
            return cnt
        return jnp.sum(lax.fori_loop(0, n_adm, body, jnp.zeros((tq, LANES), F32)), axis=1, keepdims=True)

    def midpoint(lo, hi):
        return (lo >> 1) + (hi >> 1) + (lo & hi & 1)

    def unfinished(lo, hi, cnt_lo):
        active = (midpoint(lo, hi) != lo) & (cnt_lo != float(topk))
        return jnp.max(jnp.where(active, 1.0, 0.0))

    def bisect(state):
        p, lo, hi, cnt_lo, _ = state
        mid_v = _f32_to_key(0.5 * _key_to_f32(lo) + 0.5 * _key_to_f32(hi))
        by_value = (p < IDX_VALUE_PASSES) & (mid_v > lo) & (mid_v < hi)
        mid = jnp.where(by_value, mid_v, midpoint(lo, hi))
        cnt = count_ge(mid)
        ge = cnt >= float(topk)
        lo, hi, cnt_lo = jnp.where(ge, mid, lo), jnp.where(ge, hi, mid), jnp.where(ge, cnt, cnt_lo)
        return p + 1, lo, hi, cnt_lo, unfinished(lo, hi, cnt_lo)

    cnt0 = jnp.full((tq, 1), -1.0, F32)
    state = (jnp.int32(0), lo0, hi0, cnt0, unfinished(lo0, hi0, cnt0))
    thr = lax.while_loop(lambda st: st[4] > 0.5, bisect, state)[1]

    def write(j, carry):
        mask_ref[j] = jnp.where(keys_ref[j] >= thr, 1, 0).astype(mask_ref.dtype)
        return carry

    lax.fori_loop(0, n_adm, write, 0)

    def clear(j, carry):
        mask_ref[j] = jnp.zeros((tq, kb), mask_ref.dtype)
        return carry

    lax.fori_loop(n_adm, nkb, clear, 0)


def _index_mask(qi, w, kit, n_streams, t, tq, kb, past, topk):
    nkb = kit.shape[1]
    nq = t // tq
    vmem = nkb * tq * kb * 4 + 2 * nkb * tq * kb + 2 * nkb * IDX_DIM * kb * 2 + IDX_HEADS * tq * LANES * 4 * 3 + (8 << 20)
    return pl.pallas_call(
        functools.partial(_index_kernel, tq=tq, kb=kb, nkb=nkb, past=past, topk=topk),
        grid=(n_streams, nq),
        in_specs=[
            pl.BlockSpec((None, IDX_HEADS, tq, IDX_DIM), lambda b, i: (b, 0, i, 0)),
            pl.BlockSpec((tq, LANES), lambda b, i: (b * nq + i, 0)),
            pl.BlockSpec((None, nkb, IDX_DIM, kb), lambda b, i: (b, 0, 0, 0)),
        ],
        out_specs=pl.BlockSpec((None, nkb, tq, kb), lambda b, i: (b, 0, i, 0)),
        out_shape=jax.ShapeDtypeStruct((n_streams, nkb, t, kb), jnp.int8),
        scratch_shapes=[pltpu.VMEM((nkb, tq, kb), jnp.int32), pltpu.VMEM((IDX_HEADS, tq, LANES), F32)],
        compiler_params=_cparams(2, vmem),
        name="indexer_topk_mask",
    )(qi, w, kit)


def _attn_kernel(it_ref, jt_ref, last_ref, q_ref, k_ref, v_ref, m_ref, o_ref, qs_ref, mx_ref, l_ref, acc_ref,
                 *, group):
    step = pl.program_id(1)
    tq = q_ref.shape[0]
    tk = k_ref.shape[0]
    n_kv = k_ref.shape[1] // SA_HEAD
    rows = group * tq
    n_lane_tiles = tk // LANES

    @pl.when(jt_ref[step] == 0)
    def _():
        for n in range(n_kv):
            for g in range(group):
                h = n * group + g
                qs_ref[n, g * tq:(g + 1) * tq, :] = q_ref[:, h * SA_HEAD:(h + 1) * SA_HEAD]
        mx_ref[...] = jnp.full(mx_ref.shape, NEG_BIG, F32)
        l_ref[...] = jnp.zeros(l_ref.shape, F32)
        acc_ref[...] = jnp.zeros(acc_ref.shape, F32)

    bias = jnp.where(m_ref[...].astype(F32) > 0.0, 0.0, NEG_BIG)
    kv = range(n_kv)
    s = [_nt(qs_ref[n], k_ref[:, n * SA_HEAD:(n + 1) * SA_HEAD]) for n in kv]
    s = [(x.reshape(group, tq, tk) + bias[None]).reshape(rows, tk) for x in s]
    tiles = [[x[:, c * LANES:(c + 1) * LANES] for c in range(n_lane_tiles)] for x in s]
    m_old = [mx_ref[n] for n in kv]
    m_new = [jnp.maximum(mo, jnp.max(functools.reduce(jnp.maximum, t), axis=1, keepdims=True))
             for mo, t in zip(m_old, tiles)]
    alpha = [jnp.exp2(mo - mn) for mo, mn in zip(m_old, m_new)]
    p = [[jnp.exp2(x - mn) for x in t] for t, mn in zip(tiles, m_new)]
    for n in kv:
        mx_ref[n] = m_new[n]
        l_ref[n] = alpha[n] * l_ref[n] + functools.reduce(jnp.add, p[n])
    pv = [jnp.dot(_mx(jnp.concatenate(p[n], axis=1)), v_ref[:, n * SA_HEAD:(n + 1) * SA_HEAD],
                  preferred_element_type=F32) for n in kv]
    for n in kv:
        acc_ref[n] = alpha[n] * acc_ref[n] + pv[n]

    @pl.when(last_ref[step] == 1)
    def _():
        for n in range(n_kv):
            o = acc_ref[n] / jnp.sum(l_ref[n], axis=1, keepdims=True)
            for g in range(group):
                h = n * group + g
                o_ref[:, h * SA_HEAD:(h + 1) * SA_HEAD] = o[g * tq:(g + 1) * tq].astype(o_ref.dtype)


def _attn_tiles(t, tq, tk, nkb, past):
    ii, jj, last = [], [], []
    for i in range(t // tq):
        n_adm = min(nkb, -(-(past + (i + 1) * tq) // tk))
        for j in range(n_adm):
            ii.append(i)
            jj.append(j)
            last.append(int(j == n_adm - 1))
    return tuple(jnp.asarray(np.array(x, np.int32)) for x in (ii, jj, last))


def _attention(q, k_all, v_all, mask, n_streams, t, tq, tk, past):
    m, sa_w = q.shape
    kv_w = k_all.shape[2]
    nkb = mask.shape[1]
    nq = t // tq
    n_heads = sa_w // SA_HEAD
    group = n_heads // SA_KV_HEADS
    it, jt, last = _attn_tiles(t, tq, tk, nkb, past)
    grid_spec = pltpu.PrefetchScalarGridSpec(
        num_scalar_prefetch=3,
        grid=(n_streams, int(it.shape[0])),
        in_specs=[
            pl.BlockSpec((tq, sa_w), lambda b, s, it, jt, lt: (b * nq + it[s], 0)),
            pl.BlockSpec((None, tk, kv_w), lambda b, s, it, jt, lt: (b, jt[s], 0)),
            pl.BlockSpec((None, tk, kv_w), lambda b, s, it, jt, lt: (b, jt[s], 0)),
            pl.BlockSpec((None, None, tq, tk), lambda b, s, it, jt, lt: (b, jt[s], it[s], 0)),
        ],
        out_specs=pl.BlockSpec((tq, sa_w), lambda b, s, it, jt, lt: (b * nq + it[s], 0)),
        scratch_shapes=[pltpu.VMEM((SA_KV_HEADS, group * tq, SA_HEAD), MXU_DTYPE)]
        + [pltpu.VMEM((SA_KV_HEADS, group * tq, SA_HEAD), F32)] * 3,
    )
    return pl.pallas_call(
        functools.partial(_attn_kernel, group=group),
        grid_spec=grid_spec,
        out_shape=jax.ShapeDtypeStruct((m, sa_w), MXU_DTYPE),
        compiler_params=_cparams(2, 40 << 20),
        name="masked_flash_attention",
    )(it, jt, last, q, k_all, v_all, mask)


def _prepare_weights(w_in, mu_shift, w_decay_up, decay_bias, w_a_up, a_bias, w_gate_up, k_k, k_a, r_k,
                     lnx_w, lnx_b, w_out, w_up, conv_w, conv_b, w_down, d_model):
    rw = w_decay_up.shape[1]
    rw_cols = 3 * rw + D_DECAY + D_AAA + D_GATE
    sa_w = d_model - rw
    kv_w = SA_KV_HEADS * SA_HEAD
    qi_w = IDX_HEADS * IDX_DIM
    o = {}
    o["r"] = 0
    o["wd"] = rw
    o["k"] = rw + D_DECAY
    o["v"] = 2 * rw + D_DECAY
    o["ad"] = 3 * rw + D_DECAY
    o["gd"] = 3 * rw + D_DECAY + D_AAA
    o["q"] = rw_cols
    o["ksa"] = rw_cols + sa_w
    o["vsa"] = o["ksa"] + kv_w
    o["qi"] = o["vsa"] + kv_w
    o["kiw"] = o["qi"] + qi_w
    kiw_w = IDX_DIM + IDX_HEADS

    def seg(x, name, width, pad_to=None):
        s = x[..., o[name]:o[name] + width]
        if pad_to is not None and pad_to > width:
            s = jnp.pad(s, [(0, 0)] * (s.ndim - 1) + [(0, pad_to - width)])
        return s

    def rw_small(x):
        return jnp.concatenate([seg(x, "wd", D_DECAY, LANES), seg(x, "ad", D_AAA, LANES), seg(x, "gd", D_GATE)], -1)

    w_in_l = jnp.concatenate([
        seg(w_in, "r", rw), seg(w_in, "k", rw), seg(w_in, "v", rw), seg(w_in, "q", sa_w), seg(w_in, "qi", qi_w),
        seg(w_in, "ksa", kv_w), seg(w_in, "vsa", kv_w), rw_small(w_in), seg(w_in, "kiw", kiw_w, LANES)], axis=1)
    starts = {"r": 0, "k": rw, "v": 2 * rw, "q": 3 * rw, "qi": 3 * rw + sa_w}
    starts["ksa"] = starts["qi"] + qi_w
    starts["vsa"] = starts["ksa"] + kv_w
    starts["small"] = starts["vsa"] + kv_w
    starts["kiw"] = starts["small"] + 4 * LANES
    pad_rows = lambda x, n: jnp.pad(x, ((0, n - x.shape[0]), (0, 0)))
    row = lambda x: x.reshape(1, -1).astype(F32)
    mu = mu_shift.reshape(1, -1)
    w = {
        "w_in": _mx(w_in_l),
        "mu_r": seg(mu, "r", rw), "mu_k": seg(mu, "k", rw), "mu_v": seg(mu, "v", rw), "mu_small": rw_small(mu),
        "w_decay_up": _mx(pad_rows(w_decay_up, LANES)), "decay_bias": row(decay_bias),
        "w_a_up": _mx(pad_rows(w_a_up, LANES)), "a_bias": row(a_bias),
        "w_gate_up": _mx(w_gate_up),
        "k_k": row(k_k), "k_a": row(k_a), "r_k": row(r_k), "lnx_w": row(lnx_w), "lnx_b": row(lnx_b),
        "w_out": _mx(w_out), "w_up": _mx(w_up), "conv_w": conv_w, "conv_b": row(conv_b), "w_down": _mx(w_down),
    }
    dims = {"rw": rw, "rw_cols": rw_cols, "sa_w": sa_w, "kv_w": kv_w, "qi_w": qi_w, "orig": o, "starts": starts,
            "seg": seg, "rw_small": rw_small}
    return w, dims


def _layer(x, past_k, past_v, past_ki, wkv0, shift0, conv0, norms, w, dims, tiles):
    bsz, t, d = x.shape
    past = past_k.shape[1]
    m = bsz * t
    rw, sa_w, kv_w, qi_w = dims["rw"], dims["sa_w"], dims["kv_w"], dims["qi_w"]
    starts, seg, rw_small = dims["starts"], dims["seg"], dims["rw_small"]
    assert t % CHUNK == 0 and past % CHUNK == 0
    tm, tq, kb = tiles["tm"], tiles["tq"], tiles["kb"]
    g_mix_pre, g_mix_post, g_ffn_pre, g_ffn_post = norms

    x2 = x.reshape(m, d)
    h1 = _rmsnorm(x2, g_mix_pre, tiles["tm_norm"])
    pfull = _matmul(h1, w["w_in"], tiles["tm_in"], tiles["tn_in"])

    sh = shift0.reshape(bsz, 1, -1)
    shift_l = {"r": seg(sh, "r", rw), "k": seg(sh, "k", rw), "v": seg(sh, "v", rw), "small": rw_small(sh)}
    cols_rw = {n: starts[n] // RW_GROUP for n in ("r", "k", "v")}
    cols_rw["small"] = starts["small"] // RW_SMALL
    o_rw, wkv1 = _rwkv(pfull, cols_rw, shift_l, wkv0, w, bsz, t)

    pos = past + jnp.arange(t, dtype=jnp.int32)
    cols_sa = {"q": starts["q"] // sa_w, "ksa": starts["ksa"] // kv_w, "qi": starts["qi"] // qi_w,
               "kiw": starts["kiw"] // LANES}
    q_r, k_r, qi_r, kiw_r = _rope(pfull, cols_sa, _rope_tables(pos), bsz, t, tq, sa_w, kv_w, qi_w)
    k_new = k_r.reshape(bsz, t, kv_w)
    v_new = pfull[:, starts["vsa"]:starts["vsa"] + kv_w].reshape(bsz, t, kv_w)
    ki_new = kiw_r[:, :IDX_DIM].reshape(bsz, t, IDX_DIM)
    l_all = past + t
    lp = -(-l_all // kb) * kb
    nkb = lp // kb
    cat = lambda old, new: jnp.pad(
        jnp.concatenate([_mx(old.reshape(bsz, past, new.shape[-1])), _mx(new)], axis=1),
        ((0, 0), (0, lp - l_all), (0, 0)))
    k_all, v_all, ki_all = cat(past_k, k_new), cat(past_v, v_new), cat(past_ki, ki_new)
    kit = ki_all.reshape(bsz, nkb, kb, IDX_DIM).transpose(0, 1, 3, 2)
    topk = min(TOPK_MAX, l_all // 4)
    mask = _index_mask(qi_r, kiw_r, kit, bsz, t, tq, kb, past, topk)
    o_sa = _attention(q_r, k_all, v_all, mask, bsz, t, tq, kb, past)

    mix_in = jnp.concatenate([o_rw, o_sa], axis=1)
    x1, h2 = _matmul_norm_residual(mix_in, w["w_out"], x2, g_mix_post, g_ffn_pre, tm, tiles["tk"], True, "out_proj")
    act, conv1 = _ffn_up(h2, w["w_up"], conv0, w["conv_w"], w["conv_b"], bsz, tiles["tm_up"], tiles["tn_up"])
    x_out, _ = _matmul_norm_residual(act, w["w_down"], x1, g_ffn_post, g_ffn_post, tm, tiles["tk_down"], False,
                                     "ffn_down")

    orig = dims["orig"]
    last = pfull.reshape(bsz, t, -1)[:, t - 1:t]
    sm = last[..., starts["small"]:starts["small"] + 4 * LANES]
    shift1 = jnp.concatenate([
        last[..., starts["r"]:starts["r"] + rw], sm[..., :D_DECAY],
        last[..., starts["k"]:starts["k"] + rw], last[..., starts["v"]:starts["v"] + rw],
        sm[..., LANES:LANES + D_AAA], sm[..., 2 * LANES:2 * LANES + D_GATE]], axis=-1)
    del orig
    new = (k_new.reshape(bsz, t, SA_KV_HEADS, SA_HEAD), v_new.reshape(bsz, t, SA_KV_HEADS, SA_HEAD), ki_new,
           wkv1, shift1, conv1)
    return x_out.reshape(bsz, t, d), new


def _tiles(bsz, t):
    big = t >= 1024
    return {
        "tm_norm": 256 if big else CHUNK,
        "tm_in": 1024 if big else bsz * t,
        "tn_in": 640,
        "tm": 512 if big else CHUNK,
        "tk": 512,
        "tk_down": 1024,
        "tm_up": 512 if big else CHUNK,
        "tn_up": 512,
        "tq": 256 if big else CHUNK,
        "kb": 512,
    }


def _run_stream(x, past_k, past_v, past_ki, wkv0, shift0, conv0, norm_w, weights, dims, depth):
    outs = []
    tiles = _tiles(x.shape[0], x.shape[1])
    for l in range(depth):
        w_l = {n: (v[l] if n != "conv_w" else v[l]) for n, v in weights.items()}
        norms = tuple(g[l].reshape(1, -1) for g in norm_w)
        x, st = _layer(x, past_k[l], past_v[l], past_ki[l], wkv0[l], shift0[l], conv0[l], norms, w_l, dims[l], tiles)
        outs.append(st)
    return x, [jnp.stack(s) for s in zip(*outs)]


def kernel(x_prompt, x_sample, cache_k, cache_v, cache_kidx, state_wkv, state_shift, state_conv, norm_mix_pre, norm_mix_post, norm_ffn_pre, norm_ffn_post, w_in, mu_shift, w_decay_up, decay_bias, w_a_up, a_bias, w_gate_up, k_k, k_a, r_k, lnx_w, lnx_b, w_out, w_up, conv_w, conv_b, w_down):
    depth, d_model = norm_mix_pre.shape
    per_layer = [_prepare_weights(w_in[l], mu_shift[l], w_decay_up[l], decay_bias[l], w_a_up[l], a_bias[l],
                                  w_gate_up[l], k_k[l], k_a[l], r_k[l], lnx_w[l], lnx_b[l], w_out[l], w_up[l],
                                  conv_w[l], conv_b[l], w_down[l], d_model) for l in range(depth)]
    weights = {n: [pw[0][n] for pw in per_layer] for n in per_layer[0][0]}
    dims = [pw[1] for pw in per_layer]
    norm_w = (norm_mix_pre, norm_mix_post, norm_ffn_pre, norm_ffn_post)

    dt = x_prompt.dtype
    bp = x_prompt.shape[0]
    rw_heads = state_wkv.shape[2]
    zk = jnp.zeros((depth, bp, 0, SA_KV_HEADS, SA_HEAD), dt)
    zki = jnp.zeros((depth, bp, 0, IDX_DIM), dt)
    zwkv = jnp.zeros((depth, bp, rw_heads, RW_HEAD, RW_HEAD), dt)
    zshift = jnp.zeros((depth, bp, 1, state_shift.shape[-1]), dt)
    zconv = jnp.zeros((depth, bp, CONV_W - 1, state_conv.shape[-1]), dt)
    y_prompt, p_new = _run_stream(x_prompt, zk, zk, zki, zwkv, zshift, zconv, norm_w, weights, dims, depth)
    y_sample, s_new = _run_stream(x_sample, cache_k, cache_v, cache_kidx, state_wkv, state_shift, state_conv,
                                  norm_w, weights, dims, depth)
    return (y_prompt, y_sample, *p_new, *s_new)
```

```python
import functools

import numpy as np
import jax
import jax.numpy as jnp
from jax import lax
from jax.experimental import pallas as pl
from jax.experimental.pallas import tpu as pltpu

F32 = jnp.float32
MXU_DTYPE = jnp.bfloat16
HI = lax.Precision.HIGHEST

CHUNK = 64
RW_HEAD = 64
D_DECAY = 96
D_AAA = 96
D_GATE = 256
SA_HEAD = 128
SA_KV_HEADS = 4
IDX_HEADS = 16
IDX_DIM = 64
TOPK_MAX = 256
CONV_W = 3
ROPE_THETA = 10000.0
NORM_EPS = 1e-6
LNX_EPS = 64e-5

LANES = 128
V7X_VMEM_LIMIT_BYTES = 60000 * 1024
INT_MIN = -(2 ** 31)
NEG_BIG = -1e30


def _cparams(n_grid, vmem_bytes):
    limit = int(min(V7X_VMEM_LIMIT_BYTES, max(32 * 1024 * 1024, vmem_bytes)))
    return pltpu.CompilerParams(dimension_semantics=("arbitrary",) * n_grid, vmem_limit_bytes=limit)


def _nt(a, b, precision=None):
    return lax.dot_general(a, b, (((1,), (1,)), ((), ())), precision=precision, preferred_element_type=F32)


def _tn(a, b, precision=None):
    return lax.dot_general(a, b, (((0,), (0,)), ((), ())), precision=precision, preferred_element_type=F32)


def _mx(x):
    return x.astype(MXU_DTYPE)


def _rmsnorm_kernel(x_ref, g_ref, o_ref):
    x = x_ref[...]
    y = x * lax.rsqrt(jnp.mean(x * x, axis=-1, keepdims=True) + NORM_EPS)
    o_ref[...] = (y * g_ref[...]).astype(o_ref.dtype)


def _rmsnorm(x, g, tm):
    m, d = x.shape
    return pl.pallas_call(
        _rmsnorm_kernel,
        grid=(m // tm,),
        in_specs=[pl.BlockSpec((tm, d), lambda i: (i, 0)), pl.BlockSpec((1, d), lambda i: (0, 0))],
        out_specs=pl.BlockSpec((tm, d), lambda i: (i, 0)),
        out_shape=jax.ShapeDtypeStruct((m, d), MXU_DTYPE),
        compiler_params=_cparams(1, 6 * tm * d * 4),
        name="rmsnorm",
    )(x, g)


def _mm_kernel(a_ref, b_ref, o_ref):
    o_ref[...] = jnp.dot(a_ref[...], b_ref[...], preferred_element_type=F32)


def _matmul(a, b, tm, tn):
    m, k = a.shape
    n = b.shape[1]
    isz = jnp.dtype(MXU_DTYPE).itemsize
    vmem = 2 * (tm * k * isz + k * tn * isz + tm * tn * 4) + 2 * tm * tn * 4 + (8 << 20)
    return pl.pallas_call(
        _mm_kernel,
        grid=(m // tm, n // tn),
        in_specs=[pl.BlockSpec((tm, k), lambda i, j: (i, 0)), pl.BlockSpec((k, tn), lambda i, j: (0, j))],
        out_specs=pl.BlockSpec((tm, tn), lambda i, j: (i, j)),
        out_shape=jax.ShapeDtypeStruct((m, n), F32),
        compiler_params=_cparams(2, vmem),
        name="in_proj",
    )(a, b)


MM_NORM_COLS = 1024
MM_NORM_ROWS = 128


def _mm_norm_kernel(a_ref, w_ref, x_ref, gpost_ref, x1_ref, *, nk):
    k = pl.program_id(1)
    tm, d = x1_ref.shape
    a = a_ref[...]
    for c in range(0, d, MM_NORM_COLS):
        part = jnp.dot(a, w_ref[:, c:c + MM_NORM_COLS], preferred_element_type=F32)

        @pl.when(k == 0)
        def _():
            x1_ref[:, c:c + MM_NORM_COLS] = part

        @pl.when(k > 0)
        def _():
            x1_ref[:, c:c + MM_NORM_COLS] += part

    @pl.when(k == nk - 1)
    def _():
        step = min(tm, MM_NORM_ROWS)
        for r in range(0, tm, step):
            rows = slice(r, r + step)
            y = x1_ref[rows, :]
            y = y * lax.rsqrt(jnp.mean(y * y, axis=-1, keepdims=True) + NORM_EPS) * gpost_ref[...]
            x1_ref[rows, :] = x_ref[rows, :] + y


def _matmul_norm_residual(a, w, x, g_post, tm, tk, name):
    m, kdim = a.shape
    d = w.shape[1]
    nk = kdim // tk
    isz = jnp.dtype(MXU_DTYPE).itemsize
    vmem = (2 * (tm * tk * isz + tk * d * isz + tm * d * 4) + tm * d * 4
            + 2 * tm * MM_NORM_COLS * 4 + 6 * MM_NORM_ROWS * d * 4 + (4 << 20))
    return pl.pallas_call(
        functools.partial(_mm_norm_kernel, nk=nk),
        grid=(m // tm, nk),
        in_specs=[
            pl.BlockSpec((tm, tk), lambda i, k: (i, k)),
            pl.BlockSpec((tk, d), lambda i, k: (k, 0)),
            pl.BlockSpec((tm, d), lambda i, k: (i, 0), pipeline_mode=pl.Buffered(1)),
            pl.BlockSpec((1, d), lambda i, k: (0, 0)),
        ],
        out_specs=pl.BlockSpec((tm, d), lambda i, k: (i, 0)),
        out_shape=jax.ShapeDtypeStruct((m, d), F32),
        compiler_params=_cparams(2, vmem),
        name=name,
    )(a, w, x, g_post)


def _gelu_tanh(x):
    return 0.5 * x * (1.0 + jnp.tanh(0.7978845608028654 * (x + 0.044715 * x * x * x)))


def _ffn_up_kernel(h_ref, wg_ref, wv_ref, c0g_ref, c0v_ref, cwg_ref, cwv_ref, cbg_ref, cbv_ref,
                   a_ref, cg_ref, cv_ref, carry_g, carry_v, *, blocks_per_stream, tm):
    i = pl.program_id(1) % blocks_per_stream

    @pl.when(i == 0)
    def _():
        carry_g[...] = c0g_ref[...]
        carry_v[...] = c0v_ref[...]

    h = h_ref[...]
    row = lax.broadcasted_iota(jnp.int32, (tm, 1), 0)

    def conv(u, carry_ref, cw_ref, cb_ref, out_ref):
        p = carry_ref[...]
        u1 = jnp.where(row == 0, p[1:2], pltpu.roll(u, 1, axis=0))
        u2 = jnp.where(row == 0, p[0:1], jnp.where(row == 1, p[1:2], pltpu.roll(u, 2, axis=0)))
        cw = cw_ref[...]
        last = u[tm - 2:tm]
        carry_ref[...] = last
        out_ref[...] = last
        return u2 * cw[0:1] + u1 * cw[1:2] + u * cw[2:3] + cb_ref[...]

    gate = conv(jnp.dot(h, wg_ref[...], preferred_element_type=F32), carry_g, cwg_ref, cbg_ref, cg_ref)
    val = conv(jnp.dot(h, wv_ref[...], preferred_element_type=F32), carry_v, cwv_ref, cbv_ref, cv_ref)
    a_ref[...] = (_gelu_tanh(gate) * val).astype(a_ref.dtype)


def _ffn_up(h, w_up, conv0, conv_w, conv_b, n_streams, tm, tn):
    m, d = h.shape
    f2 = w_up.shape[1]
    f = f2 // 2
    nj = f // tn
    nr = m // tm
    bps = nr // n_streams
    isz = jnp.dtype(MXU_DTYPE).itemsize
    vmem = 2 * (tm * d * isz + 2 * d * tn * isz + tm * tn * isz) + 10 * tm * tn * 4 + (4 << 20)
    col = lambda off: (lambda j, r: (0, off + j))
    st = lambda off: (lambda j, r: (r // bps, 0, off + j))
    a, cg, cv = pl.pallas_call(
        functools.partial(_ffn_up_kernel, blocks_per_stream=bps, tm=tm),
        grid=(nj, nr),
        in_specs=[
            pl.BlockSpec((tm, d), lambda j, r: (r, 0)),
            pl.BlockSpec((d, tn), col(0)),
            pl.BlockSpec((d, tn), col(nj)),
            pl.BlockSpec((None, CONV_W - 1, tn), st(0)),
            pl.BlockSpec((None, CONV_W - 1, tn), st(nj)),
            pl.BlockSpec((CONV_W, tn), col(0)),
            pl.BlockSpec((CONV_W, tn), col(nj)),
            pl.BlockSpec((1, tn), col(0)),
            pl.BlockSpec((1, tn), col(nj)),
        ],
        out_specs=[
            pl.BlockSpec((tm, tn), lambda j, r: (r, j)),
            pl.BlockSpec((None, CONV_W - 1, tn), st(0)),
            pl.BlockSpec((None, CONV_W - 1, tn), st(0)),
        ],
        out_shape=[
            jax.ShapeDtypeStruct((m, f), MXU_DTYPE),
            jax.ShapeDtypeStruct((n_streams, CONV_W - 1, f), F32),
            jax.ShapeDtypeStruct((n_streams, CONV_W - 1, f), F32),
        ],
        scratch_shapes=[pltpu.VMEM((CONV_W - 1, tn), F32), pltpu.VMEM((CONV_W - 1, tn), F32)],
        compiler_params=_cparams(2, vmem),
        name="ffn_up_conv",
    )(h, w_up, w_up, conv0, conv0, conv_w, conv_w, conv_b, conv_b)
    return a, jnp.concatenate([cg, cv], axis=-1)


RW_GROUP = 8 * LANES
RW_SMALL = 4 * LANES


def _softplus(z):
    return jnp.maximum(z, 0.0) + jnp.log(1.0 + jnp.exp(-jnp.abs(z)))


def _sigmoid(z):
    return 1.0 / (1.0 + jnp.exp(-z))


def _split(x):
    hi = x.astype(MXU_DTYPE)
    return hi, (x - hi.astype(F32)).astype(MXU_DTYPE)


def _dg(a, b, dims):
    return lax.dot_general(_mx(a), _mx(b), (dims, ((), ())), preferred_element_type=F32)


_NN = ((1,), (0,))
_NT = ((1,), (1,))
_TN = ((0,), (0,))


def _segsum(x, seg):
    hi, lo = _split(x)
    return jnp.dot(hi, seg, preferred_element_type=F32) + jnp.dot(lo, seg, preferred_element_type=F32)


def _each(f, *lists):
    return [f(*args) for args in zip(*lists)]


def _wkv_chunk(r, k, v, kk, b, cum, logd, s_bd, consts):
    m0, m1, tri_mask = consts
    c = r[0].shape[0]
    c2 = 2 * c
    stack = lambda x: jnp.concatenate([x * m0, x * m1], axis=0)
    g_end = _each(lambda cm: jnp.exp(cm[c - 1:c]), cum)
    e_inc = _each(jnp.exp, cum)
    e_prev = _each(lambda cm, ld: jnp.exp(cm - ld), cum, logd)
    e_neg = _each(lambda cm: jnp.exp(-cm), cum)
    e_tail = _each(lambda g, e: g * e, g_end, e_neg)
    vs = _each(stack, v)
    lhs = _each(lambda kk_, r_, ep, ei: jnp.concatenate([stack(kk_ * ep), stack(r_ * ei)], axis=0),
                kk, r, e_prev, e_inc)
    rhs = _each(lambda k_, b_, en: jnp.concatenate([stack(k_ * en), stack(b_ * en)], axis=0), k, b, e_neg)
    tails = _each(lambda k_, b_, et: jnp.concatenate([stack(k_ * et), stack(b_ * et)], axis=0), k, b, e_tail)
    amat = _each(lambda x, y: jnp.where(tri_mask, _dg(x, y, _NT), 0.0), lhs, rhs)
    sp = _each(lambda x, s: _dg(x, s, _NT), lhs, s_bd)
    av = _each(lambda a_, v_: _dg(a_[:, :c2], v_, _NN), amat, vs)
    pw = _each(lambda a_: -a_[:c2, c2:], amat)
    x = _each(lambda s_, a_: s_[:c2] + a_[:c2], sp, av)
    n_lvl = int(np.log2(c))
    for lvl in range(n_lvl - 1):
        z = _each(lambda p_, x_: _dg(p_, jnp.concatenate([p_, x_], axis=1), _NN), pw, x)
        pw = _each(lambda z_: z_[:, :c2], z)
        x = _each(lambda x_, z_: x_ + z_[:, c2:], x, z)
    u = _each(lambda p_, x_: x_ + _dg(p_, x_, _NN), pw, x)
    ys = _each(lambda s_, a_, am, u_: s_[c2:] + a_[c2:] - _dg(am[c2:, c2:], u_, _NN), sp, av, amat, u)
    y = _each(lambda y_: y_[:c] + y_[c:], ys)
    s_new = _each(lambda s, g, v_, u_, t_: s * g + _dg(jnp.concatenate([v_, -u_], axis=0), t_, _TN),
                  s_bd, g_end, vs, u, tails)
    return y, s_new


def _rwkv_kernel(pr_ref, pk_ref, pv_ref, ps_ref, sr_ref, sk_ref, sv_ref, ss_ref, wkv0_ref,
                 mur_ref, muk_ref, muv_ref, mus_ref, wdec_ref, dbias_ref, wa_ref, abias_ref, wg_ref,
                 kk_ref, ka_ref, rk_ref, lnw_ref, lnb_ref,
                 o_ref, wkv1_ref,
                 cr_ref, ck_ref, cv_ref, cs_ref, state_ref, *, n_chunks):
    c = pl.program_id(2)
    C = CHUNK
    npairs = RW_GROUP // LANES

    lane = lax.broadcasted_iota(jnp.int32, (1, LANES), 1)
    m0 = (lane < RW_HEAD).astype(F32)
    m1 = 1.0 - m0
    ri = lax.broadcasted_iota(jnp.int32, (4 * C, 4 * C), 0)
    ci = lax.broadcasted_iota(jnp.int32, (4 * C, 4 * C), 1)
    same_head = ((ri // C) % 2) == ((ci // C) % 2)
    tri_mask = same_head & ((ri % C) + ri // (2 * C) > (ci % C))
    li = lax.broadcasted_iota(jnp.int32, (LANES, LANES), 0)
    lj = lax.broadcasted_iota(jnp.int32, (LANES, LANES), 1)
    seg = ((li // RW_HEAD) == (lj // RW_HEAD)).astype(MXU_DTYPE)
    consts = (m0, m1, tri_mask)

    @pl.when(c == 0)
    def _():
        cr_ref[...] = sr_ref[...]
        ck_ref[...] = sk_ref[...]
        cv_ref[...] = sv_ref[...]
        cs_ref[...] = ss_ref[...]
        z = jnp.zeros((RW_HEAD, RW_HEAD), F32)
        for p in range(npairs):
            s0 = wkv0_ref[2 * p]
            s1 = wkv0_ref[2 * p + 1]
            state_ref[p] = jnp.concatenate(
                [jnp.concatenate([s0, z], axis=1), jnp.concatenate([z, s1], axis=1)], axis=0)

    row = lax.broadcasted_iota(jnp.int32, (C, 1), 0)

    def shifted(p_ref, carry_ref, mu_ref):
        p = p_ref[...]
        prev = jnp.where(row == 0, carry_ref[...], pltpu.roll(p, 1, axis=0))
        carry_ref[...] = p[C - 1:C]
        return p + (prev - p) * mu_ref[...]

    r = shifted(pr_ref, cr_ref, mur_ref)
    k = shifted(pk_ref, ck_ref, muk_ref)
    v = shifted(pv_ref, cv_ref, muv_ref)
    sm = shifted(ps_ref, cs_ref, mus_ref)
    wd, ad, gd = sm[:, 0:LANES], sm[:, LANES:2 * LANES], sm[:, 2 * LANES:4 * LANES]

    dec_in = dbias_ref[...] + jnp.dot(_mx(jnp.tanh(wd)), wdec_ref[...], preferred_element_type=F32)
    w_log = -_softplus(-dec_in) - 0.5
    logd = -jnp.exp(w_log)
    a = _sigmoid(abias_ref[...] + jnp.dot(_mx(ad), wa_ref[...], preferred_element_type=F32))
    g = jnp.dot(_mx(_sigmoid(gd)), wg_ref[...], preferred_element_type=F32)
    kk = k * kk_ref[...]
    k2 = k * (1.0 + (a - 1.0) * ka_ref[...])
    rkr = r * k2 * rk_ref[...]
    cum = logd
    for sft in (1, 2, 4, 8, 16, 32):
        cum = cum + jnp.where(row >= sft, pltpu.roll(cum, sft, axis=0), 0.0)

    pairs = lambda x: [x[:, p * LANES:(p + 1) * LANES] for p in range(npairs)]
    r_p, k_p, v_p, a_p, cum_p, logd_p = pairs(r), pairs(k2), pairs(v), pairs(a), pairs(cum), pairs(logd)
    sums = _each(lambda kk_, rkr_: _segsum(jnp.concatenate([kk_ * kk_, rkr_], axis=0), seg), pairs(kk), pairs(rkr))
    kk_p = _each(lambda kk_, s_: kk_ * lax.rsqrt(jnp.maximum(s_[:C], 1e-24)), pairs(kk), sums)
    b_p = _each(lambda kk_, a_: kk_ * a_, kk_p, a_p)
    y_p, s_new = _wkv_chunk(r_p, k_p, v_p, kk_p, b_p, cum_p, logd_p, [state_ref[p] for p in range(npairs)], consts)
    for p in range(npairs):
        state_ref[p] = s_new[p]
    yc = _each(lambda y_: y_ - _segsum(y_, seg) * (1.0 / RW_HEAD), y_p)
    var = _each(lambda yc_: _segsum(yc_ * yc_, seg) * (1.0 / RW_HEAD), yc)
    for p in range(npairs):
        sl = slice(p * LANES, (p + 1) * LANES)
        yn = yc[p] * lax.rsqrt(var[p] + LNX_EPS) * lnw_ref[:, sl] + lnb_ref[:, sl]
        o_ref[:, sl] = ((yn + sums[p][C:] * v_p[p]) * g[:, sl]).astype(o_ref.dtype)

    @pl.when(c == n_chunks - 1)
    def _():
        for p in range(npairs):
            s = state_ref[p]
            wkv1_ref[2 * p] = s[:RW_HEAD, :RW_HEAD]
            wkv1_ref[2 * p + 1] = s[RW_HEAD:, RW_HEAD:]


def _rwkv(pfull, cols, shift0, wkv0, w, n_streams, t):
    m = pfull.shape[0]
    rw = w["k_k"].shape[1]
    ng = rw // RW_GROUP
    nc = t // CHUNK
    hg = RW_GROUP // RW_HEAD
    rowblk = lambda off: (lambda b, g, c: (b * nc + c, off + g))
    fixed = lambda off: (lambda b, g, c: (b * nc + c, off))
    st = lambda b, g, c: (b, 0, g)
    st0 = lambda b, g, c: (b, 0, 0)
    wcol = lambda b, g, c: (0, g)
    in_specs = [
        pl.BlockSpec((CHUNK, RW_GROUP), rowblk(cols["r"])),
        pl.BlockSpec((CHUNK, RW_GROUP), rowblk(cols["k"])),
        pl.BlockSpec((CHUNK, RW_GROUP), rowblk(cols["v"])),
        pl.BlockSpec((CHUNK, RW_SMALL), fixed(cols["small"])),
        pl.BlockSpec((None, 1, RW_GROUP), st),
        pl.BlockSpec((None, 1, RW_GROUP), st),
        pl.BlockSpec((None, 1, RW_GROUP), st),
        pl.BlockSpec((None, 1, RW_SMALL), st0),
        pl.BlockSpec((None, hg, RW_HEAD, RW_HEAD), lambda b, g, c: (b, g, 0, 0)),
        pl.BlockSpec((1, RW_GROUP), wcol), pl.BlockSpec((1, RW_GROUP), wcol), pl.BlockSpec((1, RW_GROUP), wcol),
        pl.BlockSpec((1, RW_SMALL), lambda b, g, c: (0, 0)),
        pl.BlockSpec((LANES, RW_GROUP), wcol), pl.BlockSpec((1, RW_GROUP), wcol),
        pl.BlockSpec((LANES, RW_GROUP), wcol), pl.BlockSpec((1, RW_GROUP), wcol),
        pl.BlockSpec((2 * LANES, RW_GROUP), wcol),
        pl.BlockSpec((1, RW_GROUP), wcol), pl.BlockSpec((1, RW_GROUP), wcol), pl.BlockSpec((1, RW_GROUP), wcol),
        pl.BlockSpec((1, RW_GROUP), wcol), pl.BlockSpec((1, RW_GROUP), wcol),
    ]
    o, wkv1 = pl.pallas_call(
        functools.partial(_rwkv_kernel, n_chunks=nc),
        grid=(n_streams, ng, nc),
        in_specs=in_specs,
        out_specs=[
            pl.BlockSpec((CHUNK, RW_GROUP), lambda b, g, c: (b * nc + c, g)),
            pl.BlockSpec((None, hg, RW_HEAD, RW_HEAD), lambda b, g, c: (b, g, 0, 0)),
        ],
        out_shape=[
            jax.ShapeDtypeStruct((m, rw), MXU_DTYPE),
            jax.ShapeDtypeStruct((n_streams, rw // RW_HEAD, RW_HEAD, RW_HEAD), F32),
        ],
        scratch_shapes=[pltpu.VMEM((1, RW_GROUP), F32)] * 3 + [pltpu.VMEM((1, RW_SMALL), F32)]
        + [pltpu.VMEM((RW_GROUP // LANES, LANES, LANES), F32)],
        compiler_params=_cparams(3, 32 << 20),
        name="rwkv7_chunked",
    )(pfull, pfull, pfull, pfull, shift0["r"], shift0["k"], shift0["v"], shift0["small"], wkv0,
      w["mu_r"], w["mu_k"], w["mu_v"], w["mu_small"], w["w_decay_up"], w["decay_bias"], w["w_a_up"], w["a_bias"],
      w["w_gate_up"], w["k_k"], w["k_a"], w["r_k"], w["lnx_w"], w["lnx_b"])
    return o, wkv1


def _rope_kernel(q_ref, k_ref, qi_ref, kiw_ref, ca_ref, sa_ref, cb_ref, sb1_ref, sb2_ref,
                 qo_ref, ko_ref, qio_ref, kiwo_ref):
    ca, sa = ca_ref[...], sa_ref[...]
    cb, sb1, sb2 = cb_ref[...], sb1_ref[...], sb2_ref[...]

    def rope_head(x):
        return x * ca + pltpu.roll(x, SA_HEAD // 2, axis=1) * sa

    def rope_idx(x):
        return (x * cb + pltpu.roll(x, LANES - IDX_DIM // 2, axis=1) * sb1
                + pltpu.roll(x, IDX_DIM // 2, axis=1) * sb2)

    q_scale = float(SA_HEAD ** -0.5 * np.log2(np.e))
    for h in range(q_ref.shape[1] // LANES):
        sl = slice(h * LANES, (h + 1) * LANES)
        qo_ref[:, sl] = (rope_head(q_ref[:, sl]) * q_scale).astype(qo_ref.dtype)
    for h in range(k_ref.shape[1] // LANES):
        sl = slice(h * LANES, (h + 1) * LANES)
        ko_ref[:, sl] = rope_head(k_ref[:, sl])
    for hp in range(qi_ref.shape[1] // LANES):
        sl = slice(hp * LANES, (hp + 1) * LANES)
        qio_ref[:, sl] = rope_idx(qi_ref[:, sl]).astype(qio_ref.dtype)
    x = kiw_ref[...]
    lane = lax.broadcasted_iota(jnp.int32, (1, LANES), 1)
    kiwo_ref[...] = jnp.where(lane < IDX_DIM, rope_idx(x), x * float((IDX_HEADS * IDX_DIM) ** -0.5))


def _rope_tables(pos):
    def ang(half):
        inv = ROPE_THETA ** (-jnp.arange(half, dtype=F32) / half)
        return pos.astype(F32)[:, None] * inv[None, :]
    aa = ang(SA_HEAD // 2)
    ca = jnp.concatenate([jnp.cos(aa), jnp.cos(aa)], axis=1)
    sa = jnp.concatenate([-jnp.sin(aa), jnp.sin(aa)], axis=1)
    ab = ang(IDX_DIM // 2)
    z = jnp.zeros_like(ab)
    cb = jnp.concatenate([jnp.cos(ab)] * 4, axis=1)
    sb1 = jnp.concatenate([-jnp.sin(ab), z, -jnp.sin(ab), z], axis=1)
    sb2 = jnp.concatenate([z, jnp.sin(ab), z, jnp.sin(ab)], axis=1)
    return ca, sa, cb, sb1, sb2


def _rope(pfull, cols, tables, n_streams, t, tm, sa_w, kv_w, qi_w):
    m = pfull.shape[0]
    bps = t // tm
    tab = pl.BlockSpec((tm, LANES), lambda r: (r % bps, 0))
    return pl.pallas_call(
        _rope_kernel,
        grid=(m // tm,),
        in_specs=[
            pl.BlockSpec((tm, sa_w), lambda r: (r, cols["q"])),
            pl.BlockSpec((tm, kv_w), lambda r: (r, cols["ksa"])),
            pl.BlockSpec((tm, qi_w), lambda r: (r, cols["qi"])),
            pl.BlockSpec((tm, LANES), lambda r: (r, cols["kiw"])),
            tab, tab, tab, tab, tab,
        ],
        out_specs=[
            pl.BlockSpec((tm, sa_w), lambda r: (r, 0)),
            pl.BlockSpec((tm, kv_w), lambda r: (r, 0)),
            pl.BlockSpec((tm, qi_w), lambda r: (r, 0)),
            pl.BlockSpec((tm, LANES), lambda r: (r, 0)),
        ],
        out_shape=[
            jax.ShapeDtypeStruct((m, sa_w), MXU_DTYPE),
            jax.ShapeDtypeStruct((m, kv_w), F32),
            jax.ShapeDtypeStruct((m, qi_w), MXU_DTYPE),
            jax.ShapeDtypeStruct((m, LANES), F32),
        ],
        compiler_params=_cparams(1, 32 << 20),
        name="rope",
    )(pfull, pfull, pfull, pfull, *tables)


IDX_HEAD_GROUP = 4
IDX_VALUE_PASSES = 16


FOLD_CHAINS = 4


def _fold_sublane_tiles(x, op):
    parts = [x[r:r + 8] for r in range(0, x.shape[0], 8)]
    chains = [functools.reduce(op, parts[c::FOLD_CHAINS]) for c in range(min(FOLD_CHAINS, len(parts)))]
    return functools.reduce(op, chains)


def _f32_to_key(x):
    bits = lax.bitcast_convert_type(x, jnp.int32)
    return bits ^ ((bits >> 31) & jnp.int32(0x7FFFFFFF))


def _key_to_f32(key):
    return lax.bitcast_convert_type(key ^ ((key >> 31) & jnp.int32(0x7FFFFFFF)), F32)


def _index_kernel(qit_ref, w_ref, ki_ref, mask_ref, keys_ref, *, tq, kb, nkb, past, topk):
    i = pl.program_id(1)
    n_adm = jnp.minimum(nkb, (past + (i + 1) * tq + kb - 1) // kb)
    chunk_bits = CHUNK.bit_length() - 1
    qpos = past + i * tq + lax.broadcasted_iota(jnp.int32, (1, tq), 1)
    lim = ((qpos >> chunk_bits) + 1) << chunk_bits
    krow = lax.broadcasted_iota(jnp.int32, (kb, 1), 0)
    hg = IDX_HEAD_GROUP
    fold = _fold_sublane_tiles

    def score_block(j, carry):
        smax, smin = carry
        kblk = ki_ref[j]
        acc = jnp.zeros((kb, tq), F32)
        for g0 in range(0, IDX_HEADS, hg):
            s = jnp.dot(kblk, qit_ref[:, g0 * tq:(g0 + hg) * tq], preferred_element_type=F32)
            for g in range(hg):
                acc = acc + jnp.maximum(s[:, g * tq:(g + 1) * tq], 0.0) * w_ref[g0 + g:g0 + g + 1, :]
        score = acc + 0.0
        adm = j * kb + krow < lim
        keys_ref[j] = jnp.where(adm, _f32_to_key(score), jnp.int32(INT_MIN))
        smax = jnp.maximum(smax, fold(jnp.where(adm, score, -jnp.inf), jnp.maximum))
        smin = jnp.minimum(smin, fold(jnp.where(adm, score, jnp.inf), jnp.minimum))
        return smax, smin

    smax, smin = lax.fori_loop(0, n_adm, score_block,
                               (jnp.full((8, tq), -jnp.inf, F32), jnp.full((8, tq), jnp.inf, F32)))
    lo0 = _f32_to_key(jnp.min(smin, axis=0, keepdims=True))
    hi0 = _f32_to_key(jnp.max(smax, axis=0, keepdims=True)) + 1

    def count_ge(thr):
        def body(j, cnt):
            return cnt + fold(jnp.where(keys_ref[j] >= thr, 1.0, 0.0), jnp.add)
        return jnp.sum(lax.fori_loop(0, n_adm, body, jnp.zeros((8, tq), F32)), axis=0, keepdims=True)

    def midpoint(lo, hi):
        return (lo >> 1) + (hi >> 1) + (lo & hi & 1)

    def unfinished(lo, hi, cnt_lo):
        active = (midpoint(lo, hi) != lo) & (cnt_lo != float(topk))
        return jnp.max(jnp.where(active, 1.0, 0.0))

    def bisect(state):
        p, lo, hi, cnt_lo, _ = state
        mid_v = _f32_to_key(0.5 * _key_to_f32(lo) + 0.5 * _key_to_f32(hi))
        by_value = (p < IDX_VALUE_PASSES) & (mid_v > lo) & (mid_v < hi)
        mid = jnp.where(by_value, mid_v, midpoint(lo, hi))
        cnt = count_ge(mid)
        ge = cnt >= float(topk)
        lo, hi, cnt_lo = jnp.where(ge, mid, lo), jnp.where(ge, hi, mid), jnp.where(ge, cnt, cnt_lo)
        return p + 1, lo, hi, cnt_lo, unfinished(lo, hi, cnt_lo)

    cnt0 = jnp.full((1, tq), -1.0, F32)
    state = (jnp.int32(0), lo0, hi0, cnt0, unfinished(lo0, hi0, cnt0))
    thr = lax.while_loop(lambda st: st[4] > 0.5, bisect, state)[1]

    def write(j, carry):
        mask_ref[j] = jnp.where(keys_ref[j] >= thr, 1, 0).astype(mask_ref.dtype)
        return carry

    lax.fori_loop(0, n_adm, write, 0)

    def clear(j, carry):
        mask_ref[j] = jnp.zeros((kb, tq), mask_ref.dtype)
        return carry

    lax.fori_loop(n_adm, nkb, clear, 0)


def _index_mask(qit, w, ki, n_streams, t, tq, kb, past, topk):
    nkb = ki.shape[1]
    nq = t // tq
    isz = jnp.dtype(MXU_DTYPE).itemsize
    vmem = (nkb * kb * tq * 4 + 2 * nkb * kb * tq + 2 * nkb * kb * LANES * isz + 2 * IDX_DIM * IDX_HEADS * tq * isz
            + 6 * kb * IDX_HEAD_GROUP * tq * 4 + (8 << 20))
    return pl.pallas_call(
        functools.partial(_index_kernel, tq=tq, kb=kb, nkb=nkb, past=past, topk=topk),
        grid=(n_streams, nq),
        in_specs=[
            pl.BlockSpec((None, None, IDX_DIM, IDX_HEADS * tq), lambda b, i: (b, i, 0, 0)),
            pl.BlockSpec((None, IDX_HEADS, tq), lambda b, i: (b, 0, i)),
            pl.BlockSpec((None, nkb, kb, IDX_DIM), lambda b, i: (b, 0, 0, 0)),
        ],
        out_specs=pl.BlockSpec((None, nkb, kb, tq), lambda b, i: (b, 0, 0, i)),
        out_shape=jax.ShapeDtypeStruct((n_streams, nkb, kb, t), jnp.int8),
        scratch_shapes=[pltpu.VMEM((nkb, kb, tq), jnp.int32)],
        compiler_params=_cparams(2, vmem),
        name="indexer_topk_mask",
    )(qit, w, ki)


def _attn_kernel(it_ref, jt_ref, last_ref, qt_ref, k_ref, vt_ref, m_ref, ot_ref, qs_ref, mx_ref, l_ref, acc_ref,
                 *, group):
    step = pl.program_id(1)
    tq = qt_ref.shape[1]
    tk = k_ref.shape[0]
    n_kv = k_ref.shape[1] // SA_HEAD
    fold = _fold_sublane_tiles

    @pl.when(jt_ref[step] == 0)
    def _():
        for n in range(n_kv):
            for g in range(group):
                h = n * group + g
                qs_ref[n, :, g * tq:(g + 1) * tq] = qt_ref[h * SA_HEAD:(h + 1) * SA_HEAD, :]
        mx_ref[...] = jnp.full(mx_ref.shape, NEG_BIG, F32)
        l_ref[...] = jnp.zeros(l_ref.shape, F32)
        acc_ref[...] = jnp.zeros(acc_ref.shape, F32)

    bias = jnp.where(m_ref[...].astype(F32) > 0.0, 0.0, NEG_BIG)
    bias = jnp.concatenate([bias] * group, axis=1)
    kv = range(n_kv)
    s = [jnp.dot(k_ref[:, n * SA_HEAD:(n + 1) * SA_HEAD], qs_ref[n], preferred_element_type=F32) + bias for n in kv]
    m_old = [mx_ref[n] for n in kv]
    m_new = [jnp.maximum(mo, jnp.max(fold(x, jnp.maximum), axis=0, keepdims=True)) for mo, x in zip(m_old, s)]
    alpha = [jnp.exp2(mo - mn) for mo, mn in zip(m_old, m_new)]
    p = [jnp.exp2(x - mn) for x, mn in zip(s, m_new)]
    for n in kv:
        mx_ref[n] = m_new[n]
        l_ref[n] = alpha[n] * l_ref[n] + fold(p[n], jnp.add)
    pv = [jnp.dot(vt_ref[n * SA_HEAD:(n + 1) * SA_HEAD, :], _mx(p[n]), preferred_element_type=F32) for n in kv]
    for n in kv:
        acc_ref[n] = alpha[n] * acc_ref[n] + pv[n]

    @pl.when(last_ref[step] == 1)
    def _():
        for n in range(n_kv):
            o = acc_ref[n] / jnp.sum(l_ref[n], axis=0, keepdims=True)
            for g in range(group):
                h = n * group + g
                ot_ref[h * SA_HEAD:(h + 1) * SA_HEAD, :] = o[:, g * tq:(g + 1) * tq].astype(ot_ref.dtype)


def _attn_tiles(t, tq, tk, nkb, past):
    ii, jj, last = [], [], []
    for i in range(t // tq):
        n_adm = min(nkb, -(-(past + (i + 1) * tq) // tk))
        for j in range(n_adm):
            ii.append(i)
            jj.append(j)
            last.append(int(j == n_adm - 1))
    return tuple(jnp.asarray(np.array(x, np.int32)) for x in (ii, jj, last))


def _attention(qt, k_all, vt_all, mask, t, tq, tk, past):
    n_streams, sa_w, _ = qt.shape
    kv_w = k_all.shape[2]
    nkb = mask.shape[1]
    group = sa_w // kv_w
    it, jt, last = _attn_tiles(t, tq, tk, nkb, past)
    grid_spec = pltpu.PrefetchScalarGridSpec(
        num_scalar_prefetch=3,
        grid=(n_streams, int(it.shape[0])),
        in_specs=[
            pl.BlockSpec((None, sa_w, tq), lambda b, s, it, jt, lt: (b, 0, it[s])),
            pl.BlockSpec((None, tk, kv_w), lambda b, s, it, jt, lt: (b, jt[s], 0)),
            pl.BlockSpec((None, kv_w, tk), lambda b, s, it, jt, lt: (b, 0, jt[s])),
            pl.BlockSpec((None, None, tk, tq), lambda b, s, it, jt, lt: (b, jt[s], 0, it[s])),
        ],
        out_specs=pl.BlockSpec((None, sa_w, tq), lambda b, s, it, jt, lt: (b, 0, it[s])),
        scratch_shapes=[pltpu.VMEM((SA_KV_HEADS, SA_HEAD, group * tq), MXU_DTYPE),
                        pltpu.VMEM((SA_KV_HEADS, 1, group * tq), F32),
                        pltpu.VMEM((SA_KV_HEADS, 8, group * tq), F32),
                        pltpu.VMEM((SA_KV_HEADS, SA_HEAD, group * tq), F32)],
    )
    return pl.pallas_call(
        functools.partial(_attn_kernel, group=group),
        grid_spec=grid_spec,
        out_shape=jax.ShapeDtypeStruct((n_streams, sa_w, t), MXU_DTYPE),
        compiler_params=_cparams(2, 40 << 20),
        name="masked_flash_attention",
    )(it, jt, last, qt, k_all, vt_all, mask)


def _prepare_weights(w_in, mu_shift, w_decay_up, decay_bias, w_a_up, a_bias, w_gate_up, k_k, k_a, r_k,
                     lnx_w, lnx_b, w_out, w_up, conv_w, conv_b, w_down, d_model):
    rw = w_decay_up.shape[1]
    rw_cols = 3 * rw + D_DECAY + D_AAA + D_GATE
    sa_w = d_model - rw
    kv_w = SA_KV_HEADS * SA_HEAD
    qi_w = IDX_HEADS * IDX_DIM
    o = {}
    o["r"] = 0
    o["wd"] = rw
    o["k"] = rw + D_DECAY
    o["v"] = 2 * rw + D_DECAY
    o["ad"] = 3 * rw + D_DECAY
    o["gd"] = 3 * rw + D_DECAY + D_AAA
    o["q"] = rw_cols
    o["ksa"] = rw_cols + sa_w
    o["vsa"] = o["ksa"] + kv_w
    o["qi"] = o["vsa"] + kv_w
    o["kiw"] = o["qi"] + qi_w
    kiw_w = IDX_DIM + IDX_HEADS

    def seg(x, name, width, pad_to=None):
        s = x[..., o[name]:o[name] + width]
        if pad_to is not None and pad_to > width:
            s = jnp.pad(s, [(0, 0)] * (s.ndim - 1) + [(0, pad_to - width)])
        return s

    def rw_small(x):
        return jnp.concatenate([seg(x, "wd", D_DECAY, LANES), seg(x, "ad", D_AAA, LANES), seg(x, "gd", D_GATE)], -1)

    w_in_l = jnp.concatenate([
        seg(w_in, "r", rw), seg(w_in, "k", rw), seg(w_in, "v", rw), seg(w_in, "q", sa_w), seg(w_in, "qi", qi_w),
        seg(w_in, "ksa", kv_w), seg(w_in, "vsa", kv_w), rw_small(w_in), seg(w_in, "kiw", kiw_w, LANES)], axis=1)
    starts = {"r": 0, "k": rw, "v": 2 * rw, "q": 3 * rw, "qi": 3 * rw + sa_w}
    starts["ksa"] = starts["qi"] + qi_w
    starts["vsa"] = starts["ksa"] + kv_w
    starts["small"] = starts["vsa"] + kv_w
    starts["kiw"] = starts["small"] + 4 * LANES
    pad_rows = lambda x, n: jnp.pad(x, ((0, n - x.shape[0]), (0, 0)))
    row = lambda x: x.reshape(1, -1).astype(F32)
    mu = mu_shift.reshape(1, -1)
    w = {
        "w_in": _mx(w_in_l),
        "mu_r": seg(mu, "r", rw), "mu_k": seg(mu, "k", rw), "mu_v": seg(mu, "v", rw), "mu_small": rw_small(mu),
        "w_decay_up": _mx(pad_rows(w_decay_up, LANES)), "decay_bias": row(decay_bias),
        "w_a_up": _mx(pad_rows(w_a_up, LANES)), "a_bias": row(a_bias),
        "w_gate_up": _mx(w_gate_up),
        "k_k": row(k_k), "k_a": row(k_a), "r_k": row(r_k), "lnx_w": row(lnx_w), "lnx_b": row(lnx_b),
        "w_out": _mx(w_out), "w_up": _mx(w_up), "conv_w": conv_w, "conv_b": row(conv_b), "w_down": _mx(w_down),
    }
    dims = {"rw": rw, "rw_cols": rw_cols, "sa_w": sa_w, "kv_w": kv_w, "qi_w": qi_w, "orig": o, "starts": starts,
            "seg": seg, "rw_small": rw_small}
    return w, dims


def _layer(x, past_k, past_v, past_ki, wkv0, shift0, conv0, norms, w, dims, tiles):
    bsz, t, d = x.shape
    past = past_k.shape[1]
    m = bsz * t
    rw, sa_w, kv_w, qi_w = dims["rw"], dims["sa_w"], dims["kv_w"], dims["qi_w"]
    starts, seg, rw_small = dims["starts"], dims["seg"], dims["rw_small"]
    assert t % CHUNK == 0 and past % CHUNK == 0
    tm, tq, kb = tiles["tm"], tiles["tq"], tiles["kb"]
    g_mix_pre, g_mix_post, g_ffn_pre, g_ffn_post = norms

    x2 = x.reshape(m, d)
    h1 = _rmsnorm(x2, g_mix_pre, tiles["tm_norm"])
    pfull = _matmul(h1, w["w_in"], tiles["tm_in"], tiles["tn_in"])

    sh = shift0.reshape(bsz, 1, -1)
    shift_l = {"r": seg(sh, "r", rw), "k": seg(sh, "k", rw), "v": seg(sh, "v", rw), "small": rw_small(sh)}
    cols_rw = {n: starts[n] // RW_GROUP for n in ("r", "k", "v")}
    cols_rw["small"] = starts["small"] // RW_SMALL
    o_rw, wkv1 = _rwkv(pfull, cols_rw, shift_l, wkv0, w, bsz, t)

    pos = past + jnp.arange(t, dtype=jnp.int32)
    cols_sa = {"q": starts["q"] // sa_w, "ksa": starts["ksa"] // kv_w, "qi": starts["qi"] // qi_w,
               "kiw": starts["kiw"] // LANES}
    q_r, k_r, qi_r, kiw_r = _rope(pfull, cols_sa, _rope_tables(pos), bsz, t, tq, sa_w, kv_w, qi_w)
    k_new = k_r.reshape(bsz, t, kv_w)
    v_new = pfull[:, starts["vsa"]:starts["vsa"] + kv_w].reshape(bsz, t, kv_w)
    ki_new = kiw_r[:, :IDX_DIM].reshape(bsz, t, IDX_DIM)
    l_all = past + t
    lp = -(-l_all // kb) * kb
    nkb = lp // kb
    cat = lambda old, new: jnp.pad(
        jnp.concatenate([_mx(old.reshape(bsz, past, new.shape[-1])), _mx(new)], axis=1),
        ((0, 0), (0, lp - l_all), (0, 0)))
    k_all, v_all, ki_all = cat(past_k, k_new), cat(past_v, v_new), cat(past_ki, ki_new)
    nq = t // tq
    qit = (qi_r.reshape(bsz, nq, tq, IDX_HEADS, IDX_DIM).transpose(0, 1, 4, 3, 2)
           .reshape(bsz, nq, IDX_DIM, IDX_HEADS * tq))
    w_idx = kiw_r[:, IDX_DIM:IDX_DIM + IDX_HEADS].reshape(bsz, t, IDX_HEADS).transpose(0, 2, 1)
    topk = min(TOPK_MAX, l_all // 4)
    mask = _index_mask(qit, w_idx, ki_all.reshape(bsz, nkb, kb, IDX_DIM), bsz, t, tq, kb, past, topk)
    o_sa_t = _attention(q_r.reshape(bsz, t, sa_w).transpose(0, 2, 1), k_all, v_all.transpose(0, 2, 1), mask,
                        t, tq, kb, past)
    o_sa = o_sa_t.transpose(0, 2, 1).reshape(m, sa_w)

    mix_in = jnp.concatenate([o_rw, o_sa], axis=1)
    x1 = _matmul_norm_residual(mix_in, w["w_out"], x2, g_mix_post, tm, tiles["tk"], "out_proj")
    h2 = _rmsnorm(x1, g_ffn_pre, tiles["tm_norm"])
    act, conv1 = _ffn_up(h2, w["w_up"], conv0, w["conv_w"], w["conv_b"], bsz, tiles["tm_up"], tiles["tn_up"])
    x_out = _matmul_norm_residual(act, w["w_down"], x1, g_ffn_post, tm, tiles["tk"], "ffn_down")

    orig = dims["orig"]
    last = pfull.reshape(bsz, t, -1)[:, t - 1:t]
    sm = last[..., starts["small"]:starts["small"] + 4 * LANES]
    shift1 = jnp.concatenate([
        last[..., starts["r"]:starts["r"] + rw], sm[..., :D_DECAY],
        last[..., starts["k"]:starts["k"] + rw], last[..., starts["v"]:starts["v"] + rw],
        sm[..., LANES:LANES + D_AAA], sm[..., 2 * LANES:2 * LANES + D_GATE]], axis=-1)
    del orig
    new = (k_new.reshape(bsz, t, SA_KV_HEADS, SA_HEAD), v_new.reshape(bsz, t, SA_KV_HEADS, SA_HEAD), ki_new,
           wkv1, shift1, conv1)
    return x_out.reshape(bsz, t, d), new


def _tiles(bsz, t):
    big = t >= 1024
    return {
        "tm_norm": 256 if big else CHUNK,
        "tm_in": 1024 if big else bsz * t,
        "tn_in": 640,
        "tm": 512 if big else CHUNK,
        "tk": 1024,
        "tm_up": 512 if big else CHUNK,
        "tn_up": 512,
        "tq": 256 if big else CHUNK,
        "kb": 512,
    }


def _run_stream(x, past_k, past_v, past_ki, wkv0, shift0, conv0, norm_w, weights, dims, depth):
    outs = []
    tiles = _tiles(x.shape[0], x.shape[1])
    for l in range(depth):
        w_l = {n: (v[l] if n != "conv_w" else v[l]) for n, v in weights.items()}
        norms = tuple(g[l].reshape(1, -1) for g in norm_w)
        x, st = _layer(x, past_k[l], past_v[l], past_ki[l], wkv0[l], shift0[l], conv0[l], norms, w_l, dims[l], tiles)
        outs.append(st)
    return x, [jnp.stack(s) for s in zip(*outs)]


def kernel(x_prompt, x_sample, cache_k, cache_v, cache_kidx, state_wkv, state_shift, state_conv, norm_mix_pre, norm_mix_post, norm_ffn_pre, norm_ffn_post, w_in, mu_shift, w_decay_up, decay_bias, w_a_up, a_bias, w_gate_up, k_k, k_a, r_k, lnx_w, lnx_b, w_out, w_up, conv_w, conv_b, w_down):
    depth, d_model = norm_mix_pre.shape
    per_layer = [_prepare_weights(w_in[l], mu_shift[l], w_decay_up[l], decay_bias[l], w_a_up[l], a_bias[l],
                                  w_gate_up[l], k_k[l], k_a[l], r_k[l], lnx_w[l], lnx_b[l], w_out[l], w_up[l],
                                  conv_w[l], conv_b[l], w_down[l], d_model) for l in range(depth)]
    weights = {n: [pw[0][n] for pw in per_layer] for n in per_layer[0][0]}
    dims = [pw[1] for pw in per_layer]
    norm_w = (norm_mix_pre, norm_mix_post, norm_ffn_pre, norm_ffn_post)

    dt = x_prompt.dtype
    bp = x_prompt.shape[0]
    rw_heads = state_wkv.shape[2]
    zk = jnp.zeros((depth, bp, 0, SA_KV_HEADS, SA_HEAD), dt)
    zki = jnp.zeros((depth, bp, 0, IDX_DIM), dt)
    zwkv = jnp.zeros((depth, bp, rw_heads, RW_HEAD, RW_HEAD), dt)
    zshift = jnp.zeros((depth, bp, 1, state_shift.shape[-1]), dt)
    zconv = jnp.zeros((depth, bp, CONV_W - 1, state_conv.shape[-1]), dt)
    y_prompt, p_new = _run_stream(x_prompt, zk, zk, zki, zwkv, zshift, zconv, norm_w, weights, dims, depth)
    y_sample, s_new = _run_stream(x_sample, cache_k, cache_v, cache_kidx, state_wkv, state_shift, state_conv,
                                  norm_w, weights, dims, depth)
    return (y_prompt, y_sample, *p_new, *s_new)
```

```python
import functools

import numpy as np
import jax
import jax.numpy as jnp
from jax import lax
from jax.experimental import pallas as pl
from jax.experimental.pallas import tpu as pltpu

F32 = jnp.float32
MXU_DTYPE = jnp.bfloat16
HI = lax.Precision.HIGHEST

CHUNK = 64
RW_HEAD = 64
D_DECAY = 96
D_AAA = 96
D_GATE = 256
SA_HEAD = 128
SA_KV_HEADS = 4
IDX_HEADS = 16
IDX_DIM = 64
TOPK_MAX = 256
CONV_W = 3
ROPE_THETA = 10000.0
NORM_EPS = 1e-6
LNX_EPS = 64e-5

LANES = 128
V7X_VMEM_LIMIT_BYTES = 60000 * 1024
INT_MIN = -(2 ** 31)
NEG_BIG = -1e30


def _cparams(n_grid, vmem_bytes):
    limit = int(min(V7X_VMEM_LIMIT_BYTES, max(32 * 1024 * 1024, vmem_bytes)))
    return pltpu.CompilerParams(dimension_semantics=("arbitrary",) * n_grid, vmem_limit_bytes=limit)


def _nt(a, b, precision=None):
    return lax.dot_general(a, b, (((1,), (1,)), ((), ())), precision=precision, preferred_element_type=F32)


def _tn(a, b, precision=None):
    return lax.dot_general(a, b, (((0,), (0,)), ((), ())), precision=precision, preferred_element_type=F32)


def _mx(x):
    return x.astype(MXU_DTYPE)


def _rmsnorm_kernel(x_ref, g_ref, o_ref):
    x = x_ref[...]
    y = x * lax.rsqrt(jnp.mean(x * x, axis=-1, keepdims=True) + NORM_EPS)
    o_ref[...] = (y * g_ref[...]).astype(o_ref.dtype)


def _rmsnorm(x, g, tm):
    m, d = x.shape
    return pl.pallas_call(
        _rmsnorm_kernel,
        grid=(m // tm,),
        in_specs=[pl.BlockSpec((tm, d), lambda i: (i, 0)), pl.BlockSpec((1, d), lambda i: (0, 0))],
        out_specs=pl.BlockSpec((tm, d), lambda i: (i, 0)),
        out_shape=jax.ShapeDtypeStruct((m, d), MXU_DTYPE),
        compiler_params=_cparams(1, 6 * tm * d * 4),
        name="rmsnorm",
    )(x, g)


def _mm_kernel(a_ref, b_ref, o_ref):
    o_ref[...] = jnp.dot(a_ref[...], b_ref[...], preferred_element_type=F32)


def _matmul(a, b, tm, tn):
    m, k = a.shape
    n = b.shape[1]
    isz = jnp.dtype(MXU_DTYPE).itemsize
    vmem = 2 * (tm * k * isz + k * tn * isz + tm * tn * 4) + 2 * tm * tn * 4 + (8 << 20)
    return pl.pallas_call(
        _mm_kernel,
        grid=(m // tm, n // tn),
        in_specs=[pl.BlockSpec((tm, k), lambda i, j: (i, 0)), pl.BlockSpec((k, tn), lambda i, j: (0, j))],
        out_specs=pl.BlockSpec((tm, tn), lambda i, j: (i, j)),
        out_shape=jax.ShapeDtypeStruct((m, n), F32),
        compiler_params=_cparams(2, vmem),
        name="in_proj",
    )(a, b)


MM_NORM_COLS = 1024
MM_NORM_ROWS = 128


def _mm_norm_kernel(a_ref, w_ref, x_ref, gpost_ref, x1_ref, *, nk):
    k = pl.program_id(1)
    tm, d = x1_ref.shape
    a = a_ref[...]
    for c in range(0, d, MM_NORM_COLS):
        part = jnp.dot(a, w_ref[:, c:c + MM_NORM_COLS], preferred_element_type=F32)

        @pl.when(k == 0)
        def _():
            x1_ref[:, c:c + MM_NORM_COLS] = part

        @pl.when(k > 0)
        def _():
            x1_ref[:, c:c + MM_NORM_COLS] += part

    @pl.when(k == nk - 1)
    def _():
        step = min(tm, MM_NORM_ROWS)
        for r in range(0, tm, step):
            rows = slice(r, r + step)
            y = x1_ref[rows, :]
            y = y * lax.rsqrt(jnp.mean(y * y, axis=-1, keepdims=True) + NORM_EPS) * gpost_ref[...]
            x1_ref[rows, :] = x_ref[rows, :] + y


def _matmul_norm_residual(a, w, x, g_post, tm, tk, name):
    m, kdim = a.shape
    d = w.shape[1]
    nk = kdim // tk
    isz = jnp.dtype(MXU_DTYPE).itemsize
    vmem = (2 * (tm * tk * isz + tk * d * isz + tm * d * 4) + tm * d * 4
            + 2 * tm * MM_NORM_COLS * 4 + 6 * MM_NORM_ROWS * d * 4 + (4 << 20))
    return pl.pallas_call(
        functools.partial(_mm_norm_kernel, nk=nk),
        grid=(m // tm, nk),
        in_specs=[
            pl.BlockSpec((tm, tk), lambda i, k: (i, k)),
            pl.BlockSpec((tk, d), lambda i, k: (k, 0)),
            pl.BlockSpec((tm, d), lambda i, k: (i, 0), pipeline_mode=pl.Buffered(1)),
            pl.BlockSpec((1, d), lambda i, k: (0, 0)),
        ],
        out_specs=pl.BlockSpec((tm, d), lambda i, k: (i, 0)),
        out_shape=jax.ShapeDtypeStruct((m, d), F32),
        compiler_params=_cparams(2, vmem),
        name=name,
    )(a, w, x, g_post)


def _gelu_tanh(x):
    return 0.5 * x * (1.0 + jnp.tanh(0.7978845608028654 * (x + 0.044715 * x * x * x)))


def _ffn_up_kernel(h_ref, wg_ref, wv_ref, c0g_ref, c0v_ref, cwg_ref, cwv_ref, cbg_ref, cbv_ref,
                   a_ref, cg_ref, cv_ref, carry_g, carry_v, *, blocks_per_stream, tm):
    i = pl.program_id(1) % blocks_per_stream

    @pl.when(i == 0)
    def _():
        carry_g[...] = c0g_ref[...]
        carry_v[...] = c0v_ref[...]

    h = h_ref[...]
    row = lax.broadcasted_iota(jnp.int32, (tm, 1), 0)

    def conv(u, carry_ref, cw_ref, cb_ref, out_ref):
        p = carry_ref[...]
        u1 = jnp.where(row == 0, p[1:2], pltpu.roll(u, 1, axis=0))
        u2 = jnp.where(row == 0, p[0:1], jnp.where(row == 1, p[1:2], pltpu.roll(u, 2, axis=0)))
        cw = cw_ref[...]
        last = u[tm - 2:tm]
        carry_ref[...] = last
        out_ref[...] = last
        return u2 * cw[0:1] + u1 * cw[1:2] + u * cw[2:3] + cb_ref[...]

    gate = conv(jnp.dot(h, wg_ref[...], preferred_element_type=F32), carry_g, cwg_ref, cbg_ref, cg_ref)
    val = conv(jnp.dot(h, wv_ref[...], preferred_element_type=F32), carry_v, cwv_ref, cbv_ref, cv_ref)
    a_ref[...] = (_gelu_tanh(gate) * val).astype(a_ref.dtype)


def _ffn_up(h, w_up, conv0, conv_w, conv_b, n_streams, tm, tn):
    m, d = h.shape
    f2 = w_up.shape[1]
    f = f2 // 2
    nj = f // tn
    nr = m // tm
    bps = nr // n_streams
    isz = jnp.dtype(MXU_DTYPE).itemsize
    vmem = 2 * (tm * d * isz + 2 * d * tn * isz + tm * tn * isz) + 10 * tm * tn * 4 + (4 << 20)
    col = lambda off: (lambda j, r: (0, off + j))
    st = lambda off: (lambda j, r: (r // bps, 0, off + j))
    a, cg, cv = pl.pallas_call(
        functools.partial(_ffn_up_kernel, blocks_per_stream=bps, tm=tm),
        grid=(nj, nr),
        in_specs=[
            pl.BlockSpec((tm, d), lambda j, r: (r, 0)),
            pl.BlockSpec((d, tn), col(0)),
            pl.BlockSpec((d, tn), col(nj)),
            pl.BlockSpec((None, CONV_W - 1, tn), st(0)),
            pl.BlockSpec((None, CONV_W - 1, tn), st(nj)),
            pl.BlockSpec((CONV_W, tn), col(0)),
            pl.BlockSpec((CONV_W, tn), col(nj)),
            pl.BlockSpec((1, tn), col(0)),
            pl.BlockSpec((1, tn), col(nj)),
        ],
        out_specs=[
            pl.BlockSpec((tm, tn), lambda j, r: (r, j)),
            pl.BlockSpec((None, CONV_W - 1, tn), st(0)),
            pl.BlockSpec((None, CONV_W - 1, tn), st(0)),
        ],
        out_shape=[
            jax.ShapeDtypeStruct((m, f), MXU_DTYPE),
            jax.ShapeDtypeStruct((n_streams, CONV_W - 1, f), F32),
            jax.ShapeDtypeStruct((n_streams, CONV_W - 1, f), F32),
        ],
        scratch_shapes=[pltpu.VMEM((CONV_W - 1, tn), F32), pltpu.VMEM((CONV_W - 1, tn), F32)],
        compiler_params=_cparams(2, vmem),
        name="ffn_up_conv",
    )(h, w_up, w_up, conv0, conv0, conv_w, conv_w, conv_b, conv_b)
    return a, jnp.concatenate([cg, cv], axis=-1)


RW_GROUP = 8 * LANES
RW_SMALL = 4 * LANES


def _softplus(z):
    return jnp.maximum(z, 0.0) + jnp.log(1.0 + jnp.exp(-jnp.abs(z)))


def _sigmoid(z):
    return 1.0 / (1.0 + jnp.exp(-z))


def _split(x):
    hi = x.astype(MXU_DTYPE)
    return hi, (x - hi.astype(F32)).astype(MXU_DTYPE)


def _dg(a, b, dims):
    return lax.dot_general(_mx(a), _mx(b), (dims, ((), ())), preferred_element_type=F32)


_NN = ((1,), (0,))
_NT = ((1,), (1,))
_TN = ((0,), (0,))


def _segsum(x, seg):
    hi, lo = _split(x)
    return jnp.dot(hi, seg, preferred_element_type=F32) + jnp.dot(lo, seg, preferred_element_type=F32)


def _each(f, *lists):
    return [f(*args) for args in zip(*lists)]


def _wkv_chunk(r, k, v, kk, b, cum, logd, s_bd, consts):
    m0, m1, tri_mask = consts
    c = r[0].shape[0]
    c2 = 2 * c
    stack = lambda x: jnp.concatenate([x * m0, x * m1], axis=0)
    g_end = _each(lambda cm: jnp.exp(cm[c - 1:c]), cum)
    e_inc = _each(jnp.exp, cum)
    e_prev = _each(lambda cm, ld: jnp.exp(cm - ld), cum, logd)
    e_neg = _each(lambda cm: jnp.exp(-cm), cum)
    e_tail = _each(lambda g, e: g * e, g_end, e_neg)
    vs = _each(stack, v)
    lhs = _each(lambda kk_, r_, ep, ei: jnp.concatenate([stack(kk_ * ep), stack(r_ * ei)], axis=0),
                kk, r, e_prev, e_inc)
    rhs = _each(lambda k_, b_, en: jnp.concatenate([stack(k_ * en), stack(b_ * en)], axis=0), k, b, e_neg)
    tails = _each(lambda k_, b_, et: jnp.concatenate([stack(k_ * et), stack(b_ * et)], axis=0), k, b, e_tail)
    amat = _each(lambda x, y: jnp.where(tri_mask, _dg(x, y, _NT), 0.0), lhs, rhs)
    sp = _each(lambda x, s: _dg(x, s, _NT), lhs, s_bd)
    av = _each(lambda a_, v_: _dg(a_[:, :c2], v_, _NN), amat, vs)
    pw = _each(lambda a_: -a_[:c2, c2:], amat)
    x = _each(lambda s_, a_: s_[:c2] + a_[:c2], sp, av)
    n_lvl = int(np.log2(c))
    for lvl in range(n_lvl - 1):
        z = _each(lambda p_, x_: _dg(p_, jnp.concatenate([p_, x_], axis=1), _NN), pw, x)
        pw = _each(lambda z_: z_[:, :c2], z)
        x = _each(lambda x_, z_: x_ + z_[:, c2:], x, z)
    u = _each(lambda p_, x_: x_ + _dg(p_, x_, _NN), pw, x)
    ys = _each(lambda s_, a_, am, u_: s_[c2:] + a_[c2:] - _dg(am[c2:, c2:], u_, _NN), sp, av, amat, u)
    y = _each(lambda y_: y_[:c] + y_[c:], ys)
    s_new = _each(lambda s, g, v_, u_, t_: s * g + _dg(jnp.concatenate([v_, -u_], axis=0), t_, _TN),
                  s_bd, g_end, vs, u, tails)
    return y, s_new


def _rwkv_kernel(pr_ref, pk_ref, pv_ref, ps_ref, sr_ref, sk_ref, sv_ref, ss_ref, wkv0_ref,
                 mur_ref, muk_ref, muv_ref, mus_ref, wdec_ref, dbias_ref, wa_ref, abias_ref, wg_ref,
                 kk_ref, ka_ref, rk_ref, lnw_ref, lnb_ref,
                 o_ref, wkv1_ref,
                 cr_ref, ck_ref, cv_ref, cs_ref, state_ref, *, n_chunks):
    c = pl.program_id(2)
    C = CHUNK
    npairs = RW_GROUP // LANES

    lane = lax.broadcasted_iota(jnp.int32, (1, LANES), 1)
    m0 = (lane < RW_HEAD).astype(F32)
    m1 = 1.0 - m0
    ri = lax.broadcasted_iota(jnp.int32, (4 * C, 4 * C), 0)
    ci = lax.broadcasted_iota(jnp.int32, (4 * C, 4 * C), 1)
    same_head = ((ri // C) % 2) == ((ci // C) % 2)
    tri_mask = same_head & ((ri % C) + ri // (2 * C) > (ci % C))
    li = lax.broadcasted_iota(jnp.int32, (LANES, LANES), 0)
    lj = lax.broadcasted_iota(jnp.int32, (LANES, LANES), 1)
    seg = ((li // RW_HEAD) == (lj // RW_HEAD)).astype(MXU_DTYPE)
    consts = (m0, m1, tri_mask)

    @pl.when(c == 0)
    def _():
        cr_ref[...] = sr_ref[...]
        ck_ref[...] = sk_ref[...]
        cv_ref[...] = sv_ref[...]
        cs_ref[...] = ss_ref[...]
        z = jnp.zeros((RW_HEAD, RW_HEAD), F32)
        for p in range(npairs):
            s0 = wkv0_ref[2 * p]
            s1 = wkv0_ref[2 * p + 1]
            state_ref[p] = jnp.concatenate(
                [jnp.concatenate([s0, z], axis=1), jnp.concatenate([z, s1], axis=1)], axis=0)

    row = lax.broadcasted_iota(jnp.int32, (C, 1), 0)

    def shifted(p_ref, carry_ref, mu_ref):
        p = p_ref[...]
        prev = jnp.where(row == 0, carry_ref[...], pltpu.roll(p, 1, axis=0))
        carry_ref[...] = p[C - 1:C]
        return p + (prev - p) * mu_ref[...]

    r = shifted(pr_ref, cr_ref, mur_ref)
    k = shifted(pk_ref, ck_ref, muk_ref)
    v = shifted(pv_ref, cv_ref, muv_ref)
    sm = shifted(ps_ref, cs_ref, mus_ref)
    wd, ad, gd = sm[:, 0:LANES], sm[:, LANES:2 * LANES], sm[:, 2 * LANES:4 * LANES]

    dec_in = dbias_ref[...] + jnp.dot(_mx(jnp.tanh(wd)), wdec_ref[...], preferred_element_type=F32)
    w_log = -_softplus(-dec_in) - 0.5
    logd = -jnp.exp(w_log)
    a = _sigmoid(abias_ref[...] + jnp.dot(_mx(ad), wa_ref[...], preferred_element_type=F32))
    g = jnp.dot(_mx(_sigmoid(gd)), wg_ref[...], preferred_element_type=F32)
    kk = k * kk_ref[...]
    k2 = k * (1.0 + (a - 1.0) * ka_ref[...])
    rkr = r * k2 * rk_ref[...]
    cum = logd
    for sft in (1, 2, 4, 8, 16, 32):
        cum = cum + jnp.where(row >= sft, pltpu.roll(cum, sft, axis=0), 0.0)

    pairs = lambda x: [x[:, p * LANES:(p + 1) * LANES] for p in range(npairs)]
    r_p, k_p, v_p, a_p, cum_p, logd_p = pairs(r), pairs(k2), pairs(v), pairs(a), pairs(cum), pairs(logd)
    sums = _each(lambda kk_, rkr_: _segsum(jnp.concatenate([kk_ * kk_, rkr_], axis=0), seg), pairs(kk), pairs(rkr))
    kk_p = _each(lambda kk_, s_: kk_ * lax.rsqrt(jnp.maximum(s_[:C], 1e-24)), pairs(kk), sums)
    b_p = _each(lambda kk_, a_: kk_ * a_, kk_p, a_p)
    y_p, s_new = _wkv_chunk(r_p, k_p, v_p, kk_p, b_p, cum_p, logd_p, [state_ref[p] for p in range(npairs)], consts)
    for p in range(npairs):
        state_ref[p] = s_new[p]
    yc = _each(lambda y_: y_ - _segsum(y_, seg) * (1.0 / RW_HEAD), y_p)
    var = _each(lambda yc_: _segsum(yc_ * yc_, seg) * (1.0 / RW_HEAD), yc)
    for p in range(npairs):
        sl = slice(p * LANES, (p + 1) * LANES)
        yn = yc[p] * lax.rsqrt(var[p] + LNX_EPS) * lnw_ref[:, sl] + lnb_ref[:, sl]
        o_ref[:, sl] = ((yn + sums[p][C:] * v_p[p]) * g[:, sl]).astype(o_ref.dtype)

    @pl.when(c == n_chunks - 1)
    def _():
        for p in range(npairs):
            s = state_ref[p]
            wkv1_ref[2 * p] = s[:RW_HEAD, :RW_HEAD]
            wkv1_ref[2 * p + 1] = s[RW_HEAD:, RW_HEAD:]


def _rwkv(pfull, cols, shift0, wkv0, w, n_streams, t):
    m = pfull.shape[0]
    rw = w["k_k"].shape[1]
    ng = rw // RW_GROUP
    nc = t // CHUNK
    hg = RW_GROUP // RW_HEAD
    rowblk = lambda off: (lambda b, g, c: (b * nc + c, off + g))
    fixed = lambda off: (lambda b, g, c: (b * nc + c, off))
    st = lambda b, g, c: (b, 0, g)
    st0 = lambda b, g, c: (b, 0, 0)
    wcol = lambda b, g, c: (0, g)
    in_specs = [
        pl.BlockSpec((CHUNK, RW_GROUP), rowblk(cols["r"])),
        pl.BlockSpec((CHUNK, RW_GROUP), rowblk(cols["k"])),
        pl.BlockSpec((CHUNK, RW_GROUP), rowblk(cols["v"])),
        pl.BlockSpec((CHUNK, RW_SMALL), fixed(cols["small"])),
        pl.BlockSpec((None, 1, RW_GROUP), st),
        pl.BlockSpec((None, 1, RW_GROUP), st),
        pl.BlockSpec((None, 1, RW_GROUP), st),
        pl.BlockSpec((None, 1, RW_SMALL), st0),
        pl.BlockSpec((None, hg, RW_HEAD, RW_HEAD), lambda b, g, c: (b, g, 0, 0)),
        pl.BlockSpec((1, RW_GROUP), wcol), pl.BlockSpec((1, RW_GROUP), wcol), pl.BlockSpec((1, RW_GROUP), wcol),
        pl.BlockSpec((1, RW_SMALL), lambda b, g, c: (0, 0)),
        pl.BlockSpec((LANES, RW_GROUP), wcol), pl.BlockSpec((1, RW_GROUP), wcol),
        pl.BlockSpec((LANES, RW_GROUP), wcol), pl.BlockSpec((1, RW_GROUP), wcol),
        pl.BlockSpec((2 * LANES, RW_GROUP), wcol),
        pl.BlockSpec((1, RW_GROUP), wcol), pl.BlockSpec((1, RW_GROUP), wcol), pl.BlockSpec((1, RW_GROUP), wcol),
        pl.BlockSpec((1, RW_GROUP), wcol), pl.BlockSpec((1, RW_GROUP), wcol),
    ]
    o, wkv1 = pl.pallas_call(
        functools.partial(_rwkv_kernel, n_chunks=nc),
        grid=(n_streams, ng, nc),
        in_specs=in_specs,
        out_specs=[
            pl.BlockSpec((CHUNK, RW_GROUP), lambda b, g, c: (b * nc + c, g)),
            pl.BlockSpec((None, hg, RW_HEAD, RW_HEAD), lambda b, g, c: (b, g, 0, 0)),
        ],
        out_shape=[
            jax.ShapeDtypeStruct((m, rw), MXU_DTYPE),
            jax.ShapeDtypeStruct((n_streams, rw // RW_HEAD, RW_HEAD, RW_HEAD), F32),
        ],
        scratch_shapes=[pltpu.VMEM((1, RW_GROUP), F32)] * 3 + [pltpu.VMEM((1, RW_SMALL), F32)]
        + [pltpu.VMEM((RW_GROUP // LANES, LANES, LANES), F32)],
        compiler_params=_cparams(3, 32 << 20),
        name="rwkv7_chunked",
    )(pfull, pfull, pfull, pfull, shift0["r"], shift0["k"], shift0["v"], shift0["small"], wkv0,
      w["mu_r"], w["mu_k"], w["mu_v"], w["mu_small"], w["w_decay_up"], w["decay_bias"], w["w_a_up"], w["a_bias"],
      w["w_gate_up"], w["k_k"], w["k_a"], w["r_k"], w["lnx_w"], w["lnx_b"])
    return o, wkv1


def _rope_kernel(q_ref, k_ref, qi_ref, kiw_ref, ca_ref, sa_ref, cb_ref, sb1_ref, sb2_ref,
                 qo_ref, ko_ref, qio_ref, kiwo_ref):
    ca, sa = ca_ref[...], sa_ref[...]
    cb, sb1, sb2 = cb_ref[...], sb1_ref[...], sb2_ref[...]

    def rope_head(x):
        return x * ca + pltpu.roll(x, SA_HEAD // 2, axis=1) * sa

    def rope_idx(x):
        return (x * cb + pltpu.roll(x, LANES - IDX_DIM // 2, axis=1) * sb1
                + pltpu.roll(x, IDX_DIM // 2, axis=1) * sb2)

    q_scale = float(SA_HEAD ** -0.5 * np.log2(np.e))
    for h in range(q_ref.shape[1] // LANES):
        sl = slice(h * LANES, (h + 1) * LANES)
        qo_ref[:, sl] = (rope_head(q_ref[:, sl]) * q_scale).astype(qo_ref.dtype)
    for h in range(k_ref.shape[1] // LANES):
        sl = slice(h * LANES, (h + 1) * LANES)
        ko_ref[:, sl] = rope_head(k_ref[:, sl])
    for hp in range(qi_ref.shape[1] // LANES):
        sl = slice(hp * LANES, (hp + 1) * LANES)
        qio_ref[:, sl] = rope_idx(qi_ref[:, sl]).astype(qio_ref.dtype)
    x = kiw_ref[...]
    lane = lax.broadcasted_iota(jnp.int32, (1, LANES), 1)
    kiwo_ref[...] = jnp.where(lane < IDX_DIM, rope_idx(x), x * float((IDX_HEADS * IDX_DIM) ** -0.5))


def _rope_tables(pos):
    def ang(half):
        inv = ROPE_THETA ** (-jnp.arange(half, dtype=F32) / half)
        return pos.astype(F32)[:, None] * inv[None, :]
    aa = ang(SA_HEAD // 2)
    ca = jnp.concatenate([jnp.cos(aa), jnp.cos(aa)], axis=1)
    sa = jnp.concatenate([-jnp.sin(aa), jnp.sin(aa)], axis=1)
    ab = ang(IDX_DIM // 2)
    z = jnp.zeros_like(ab)
    cb = jnp.concatenate([jnp.cos(ab)] * 4, axis=1)
    sb1 = jnp.concatenate([-jnp.sin(ab), z, -jnp.sin(ab), z], axis=1)
    sb2 = jnp.concatenate([z, jnp.sin(ab), z, jnp.sin(ab)], axis=1)
    return ca, sa, cb, sb1, sb2


def _rope(pfull, cols, tables, n_streams, t, tm, sa_w, kv_w, qi_w):
    m = pfull.shape[0]
    bps = t // tm
    tab = pl.BlockSpec((tm, LANES), lambda r: (r % bps, 0))
    return pl.pallas_call(
        _rope_kernel,
        grid=(m // tm,),
        in_specs=[
            pl.BlockSpec((tm, sa_w), lambda r: (r, cols["q"])),
            pl.BlockSpec((tm, kv_w), lambda r: (r, cols["ksa"])),
            pl.BlockSpec((tm, qi_w), lambda r: (r, cols["qi"])),
            pl.BlockSpec((tm, LANES), lambda r: (r, cols["kiw"])),
            tab, tab, tab, tab, tab,
        ],
        out_specs=[
            pl.BlockSpec((tm, sa_w), lambda r: (r, 0)),
            pl.BlockSpec((tm, kv_w), lambda r: (r, 0)),
            pl.BlockSpec((tm, qi_w), lambda r: (r, 0)),
            pl.BlockSpec((tm, LANES), lambda r: (r, 0)),
        ],
        out_shape=[
            jax.ShapeDtypeStruct((m, sa_w), MXU_DTYPE),
            jax.ShapeDtypeStruct((m, kv_w), F32),
            jax.ShapeDtypeStruct((m, qi_w), MXU_DTYPE),
            jax.ShapeDtypeStruct((m, LANES), F32),
        ],
        compiler_params=_cparams(1, 32 << 20),
        name="rope",
    )(pfull, pfull, pfull, pfull, *tables)


IDX_HEAD_GROUP = 4
IDX_VALUE_PASSES = 16


FOLD_CHAINS = 4


def _fold_sublane_tiles(x, op):
    parts = [x[r:r + 8] for r in range(0, x.shape[0], 8)]
    chains = [functools.reduce(op, parts[c::FOLD_CHAINS]) for c in range(min(FOLD_CHAINS, len(parts)))]
    return functools.reduce(op, chains)


def _f32_to_key(x):
    bits = lax.bitcast_convert_type(x, jnp.int32)
    return bits ^ ((bits >> 31) & jnp.int32(0x7FFFFFFF))


def _key_to_f32(key):
    return lax.bitcast_convert_type(key ^ ((key >> 31) & jnp.int32(0x7FFFFFFF)), F32)


def _index_kernel(qit_ref, w_ref, ki_ref, mask_ref, keys_ref, *, tq, kb, nkb, past, topk):
    i = pl.program_id(1)
    n_adm = jnp.minimum(nkb, (past + (i + 1) * tq + kb - 1) // kb)
    chunk_bits = CHUNK.bit_length() - 1
    qpos = past + i * tq + lax.broadcasted_iota(jnp.int32, (1, tq), 1)
    lim = ((qpos >> chunk_bits) + 1) << chunk_bits
    krow = lax.broadcasted_iota(jnp.int32, (kb, 1), 0)
    hg = IDX_HEAD_GROUP
    fold = _fold_sublane_tiles

    def score_block(j, carry):
        smax, smin = carry
        kblk = ki_ref[j]
        acc = jnp.zeros((kb, tq), F32)
        for g0 in range(0, IDX_HEADS, hg):
            s = jnp.dot(kblk, qit_ref[:, g0 * tq:(g0 + hg) * tq], preferred_element_type=F32)
            for g in range(hg):
                acc = acc + jnp.maximum(s[:, g * tq:(g + 1) * tq], 0.0) * w_ref[g0 + g:g0 + g + 1, :]
        score = acc + 0.0
        adm = j * kb + krow < lim
        keys_ref[j] = jnp.where(adm, _f32_to_key(score), jnp.int32(INT_MIN))
        smax = jnp.maximum(smax, fold(jnp.where(adm, score, -jnp.inf), jnp.maximum))
        smin = jnp.minimum(smin, fold(jnp.where(adm, score, jnp.inf), jnp.minimum))
        return smax, smin

    smax, smin = lax.fori_loop(0, n_adm, score_block,
                               (jnp.full((8, tq), -jnp.inf, F32), jnp.full((8, tq), jnp.inf, F32)))
    lo0 = _f32_to_key(jnp.min(smin, axis=0, keepdims=True))
    hi0 = _f32_to_key(jnp.max(smax, axis=0, keepdims=True)) + 1

    def count_ge(thr):
        def body(j, cnt):
            return cnt + fold(jnp.where(keys_ref[j] >= thr, 1.0, 0.0), jnp.add)
        return jnp.sum(lax.fori_loop(0, n_adm, body, jnp.zeros((8, tq), F32)), axis=0, keepdims=True)

    def midpoint(lo, hi):
        return (lo >> 1) + (hi >> 1) + (lo & hi & 1)

    def unfinished(lo, hi, cnt_lo):
        active = (midpoint(lo, hi) != lo) & (cnt_lo != float(topk))
        return jnp.max(jnp.where(active, 1.0, 0.0))

    def bisect(state):
        p, lo, hi, cnt_lo, _ = state
        mid_v = _f32_to_key(0.5 * _key_to_f32(lo) + 0.5 * _key_to_f32(hi))
        by_value = (p < IDX_VALUE_PASSES) & (mid_v > lo) & (mid_v < hi)
        mid = jnp.where(by_value, mid_v, midpoint(lo, hi))
        cnt = count_ge(mid)
        ge = cnt >= float(topk)
        lo, hi, cnt_lo = jnp.where(ge, mid, lo), jnp.where(ge, hi, mid), jnp.where(ge, cnt, cnt_lo)
        return p + 1, lo, hi, cnt_lo, unfinished(lo, hi, cnt_lo)

    cnt0 = jnp.full((1, tq), -1.0, F32)
    state = (jnp.int32(0), lo0, hi0, cnt0, unfinished(lo0, hi0, cnt0))
    thr = lax.while_loop(lambda st: st[4] > 0.5, bisect, state)[1]

    def write(j, carry):
        mask_ref[j] = jnp.where(keys_ref[j] >= thr, 1.0, 0.0).T.astype(mask_ref.dtype)
        return carry

    lax.fori_loop(0, n_adm, write, 0)

    def clear(j, carry):
        mask_ref[j] = jnp.zeros((tq, kb), mask_ref.dtype)
        return carry

    lax.fori_loop(n_adm, nkb, clear, 0)


def _index_mask(qit, w, ki, n_streams, t, tq, kb, past, topk):
    nkb = ki.shape[1]
    nq = t // tq
    isz = jnp.dtype(MXU_DTYPE).itemsize
    vmem = (nkb * kb * tq * 4 + 2 * nkb * kb * tq + 2 * nkb * kb * LANES * isz + 2 * IDX_DIM * IDX_HEADS * tq * isz
            + 6 * kb * IDX_HEAD_GROUP * tq * 4 + (8 << 20))
    return pl.pallas_call(
        functools.partial(_index_kernel, tq=tq, kb=kb, nkb=nkb, past=past, topk=topk),
        grid=(n_streams, nq),
        in_specs=[
            pl.BlockSpec((None, None, IDX_DIM, IDX_HEADS * tq), lambda b, i: (b, i, 0, 0)),
            pl.BlockSpec((None, IDX_HEADS, tq), lambda b, i: (b, 0, i)),
            pl.BlockSpec((None, nkb, kb, IDX_DIM), lambda b, i: (b, 0, 0, 0)),
        ],
        out_specs=pl.BlockSpec((None, nkb, tq, kb), lambda b, i: (b, 0, i, 0)),
        out_shape=jax.ShapeDtypeStruct((n_streams, nkb, t, kb), jnp.int8),
        scratch_shapes=[pltpu.VMEM((nkb, kb, tq), jnp.int32)],
        compiler_params=_cparams(2, vmem),
        name="indexer_topk_mask",
    )(qit, w, ki)


def _attn_kernel(it_ref, jt_ref, last_ref, q_ref, k_ref, v_ref, m_ref, o_ref, qs_ref, mx_ref, l_ref, acc_ref,
                 *, group):
    step = pl.program_id(1)
    tq = q_ref.shape[0]
    tk = k_ref.shape[0]
    n_kv = k_ref.shape[1] // SA_HEAD
    rows = group * tq
    n_lane_tiles = tk // LANES

    @pl.when(jt_ref[step] == 0)
    def _():
        for n in range(n_kv):
            for g in range(group):
                h = n * group + g
                qs_ref[n, g * tq:(g + 1) * tq, :] = q_ref[:, h * SA_HEAD:(h + 1) * SA_HEAD]
        mx_ref[...] = jnp.full(mx_ref.shape, NEG_BIG, F32)
        l_ref[...] = jnp.zeros(l_ref.shape, F32)
        acc_ref[...] = jnp.zeros(acc_ref.shape, F32)

    bias = jnp.where(m_ref[...].astype(F32) > 0.0, 0.0, NEG_BIG)
    kv = range(n_kv)
    s = [_nt(qs_ref[n], k_ref[:, n * SA_HEAD:(n + 1) * SA_HEAD]) for n in kv]
    s = [(x.reshape(group, tq, tk) + bias[None]).reshape(rows, tk) for x in s]
    tiles = [[x[:, c * LANES:(c + 1) * LANES] for c in range(n_lane_tiles)] for x in s]
    m_old = [mx_ref[n] for n in kv]
    m_new = [jnp.maximum(mo, jnp.max(functools.reduce(jnp.maximum, t), axis=1, keepdims=True))
             for mo, t in zip(m_old, tiles)]
    alpha = [jnp.exp2(mo - mn) for mo, mn in zip(m_old, m_new)]
    p = [[jnp.exp2(x - mn) for x in t] for t, mn in zip(tiles, m_new)]
    for n in kv:
        mx_ref[n] = m_new[n]
        l_ref[n] = alpha[n] * l_ref[n] + functools.reduce(jnp.add, p[n])
    pv = [jnp.dot(_mx(jnp.concatenate(p[n], axis=1)), v_ref[:, n * SA_HEAD:(n + 1) * SA_HEAD],
                  preferred_element_type=F32) for n in kv]
    for n in kv:
        acc_ref[n] = alpha[n] * acc_ref[n] + pv[n]

    @pl.when(last_ref[step] == 1)
    def _():
        for n in range(n_kv):
            o = acc_ref[n] / jnp.sum(l_ref[n], axis=1, keepdims=True)
            for g in range(group):
                h = n * group + g
                o_ref[:, h * SA_HEAD:(h + 1) * SA_HEAD] = o[g * tq:(g + 1) * tq].astype(o_ref.dtype)


def _attn_tiles(t, tq, tk, nkb, past):
    ii, jj, last = [], [], []
    for i in range(t // tq):
        n_adm = min(nkb, -(-(past + (i + 1) * tq) // tk))
        for j in range(n_adm):
            ii.append(i)
            jj.append(j)
            last.append(int(j == n_adm - 1))
    return tuple(jnp.asarray(np.array(x, np.int32)) for x in (ii, jj, last))


def _attention(q, k_all, v_all, mask, n_streams, t, tq, tk, past):
    m, sa_w = q.shape
    kv_w = k_all.shape[2]
    nkb = mask.shape[1]
    nq = t // tq
    n_heads = sa_w // SA_HEAD
    group = n_heads // SA_KV_HEADS
    it, jt, last = _attn_tiles(t, tq, tk, nkb, past)
    grid_spec = pltpu.PrefetchScalarGridSpec(
        num_scalar_prefetch=3,
        grid=(n_streams, int(it.shape[0])),
        in_specs=[
            pl.BlockSpec((tq, sa_w), lambda b, s, it, jt, lt: (b * nq + it[s], 0)),
            pl.BlockSpec((None, tk, kv_w), lambda b, s, it, jt, lt: (b, jt[s], 0)),
            pl.BlockSpec((None, tk, kv_w), lambda b, s, it, jt, lt: (b, jt[s], 0)),
            pl.BlockSpec((None, None, tq, tk), lambda b, s, it, jt, lt: (b, jt[s], it[s], 0)),
        ],
        out_specs=pl.BlockSpec((tq, sa_w), lambda b, s, it, jt, lt: (b * nq + it[s], 0)),
        scratch_shapes=[pltpu.VMEM((SA_KV_HEADS, group * tq, SA_HEAD), MXU_DTYPE)]
        + [pltpu.VMEM((SA_KV_HEADS, group * tq, SA_HEAD), F32)] * 3,
    )
    return pl.pallas_call(
        functools.partial(_attn_kernel, group=group),
        grid_spec=grid_spec,
        out_shape=jax.ShapeDtypeStruct((m, sa_w), MXU_DTYPE),
        compiler_params=_cparams(2, 40 << 20),
        name="masked_flash_attention",
    )(it, jt, last, q, k_all, v_all, mask)


def _prepare_weights(w_in, mu_shift, w_decay_up, decay_bias, w_a_up, a_bias, w_gate_up, k_k, k_a, r_k,
                     lnx_w, lnx_b, w_out, w_up, conv_w, conv_b, w_down, d_model):
    rw = w_decay_up.shape[1]
    rw_cols = 3 * rw + D_DECAY + D_AAA + D_GATE
    sa_w = d_model - rw
    kv_w = SA_KV_HEADS * SA_HEAD
    qi_w = IDX_HEADS * IDX_DIM
    o = {}
    o["r"] = 0
    o["wd"] = rw
    o["k"] = rw + D_DECAY
    o["v"] = 2 * rw + D_DECAY
    o["ad"] = 3 * rw + D_DECAY
    o["gd"] = 3 * rw + D_DECAY + D_AAA
    o["q"] = rw_cols
    o["ksa"] = rw_cols + sa_w
    o["vsa"] = o["ksa"] + kv_w
    o["qi"] = o["vsa"] + kv_w
    o["kiw"] = o["qi"] + qi_w
    kiw_w = IDX_DIM + IDX_HEADS

    def seg(x, name, width, pad_to=None):
        s = x[..., o[name]:o[name] + width]
        if pad_to is not None and pad_to > width:
            s = jnp.pad(s, [(0, 0)] * (s.ndim - 1) + [(0, pad_to - width)])
        return s

    def rw_small(x):
        return jnp.concatenate([seg(x, "wd", D_DECAY, LANES), seg(x, "ad", D_AAA, LANES), seg(x, "gd", D_GATE)], -1)

    w_in_l = jnp.concatenate([
        seg(w_in, "r", rw), seg(w_in, "k", rw), seg(w_in, "v", rw), seg(w_in, "q", sa_w), seg(w_in, "qi", qi_w),
        seg(w_in, "ksa", kv_w), seg(w_in, "vsa", kv_w), rw_small(w_in), seg(w_in, "kiw", kiw_w, LANES)], axis=1)
    starts = {"r": 0, "k": rw, "v": 2 * rw, "q": 3 * rw, "qi": 3 * rw + sa_w}
    starts["ksa"] = starts["qi"] + qi_w
    starts["vsa"] = starts["ksa"] + kv_w
    starts["small"] = starts["vsa"] + kv_w
    starts["kiw"] = starts["small"] + 4 * LANES
    pad_rows = lambda x, n: jnp.pad(x, ((0, n - x.shape[0]), (0, 0)))
    row = lambda x: x.reshape(1, -1).astype(F32)
    mu = mu_shift.reshape(1, -1)
    w = {
        "w_in": _mx(w_in_l),
        "mu_r": seg(mu, "r", rw), "mu_k": seg(mu, "k", rw), "mu_v": seg(mu, "v", rw), "mu_small": rw_small(mu),
        "w_decay_up": _mx(pad_rows(w_decay_up, LANES)), "decay_bias": row(decay_bias),
        "w_a_up": _mx(pad_rows(w_a_up, LANES)), "a_bias": row(a_bias),
        "w_gate_up": _mx(w_gate_up),
        "k_k": row(k_k), "k_a": row(k_a), "r_k": row(r_k), "lnx_w": row(lnx_w), "lnx_b": row(lnx_b),
        "w_out": _mx(w_out), "w_up": _mx(w_up), "conv_w": conv_w, "conv_b": row(conv_b), "w_down": _mx(w_down),
    }
    dims = {"rw": rw, "rw_cols": rw_cols, "sa_w": sa_w, "kv_w": kv_w, "qi_w": qi_w, "orig": o, "starts": starts,
            "seg": seg, "rw_small": rw_small}
    return w, dims


def _layer(x, past_k, past_v, past_ki, wkv0, shift0, conv0, norms, w, dims, tiles):
    bsz, t, d = x.shape
    past = past_k.shape[1]
    m = bsz * t
    rw, sa_w, kv_w, qi_w = dims["rw"], dims["sa_w"], dims["kv_w"], dims["qi_w"]
    starts, seg, rw_small = dims["starts"], dims["seg"], dims["rw_small"]
    assert t % CHUNK == 0 and past % CHUNK == 0
    tm, tq, kb = tiles["tm"], tiles["tq"], tiles["kb"]
    g_mix_pre, g_mix_post, g_ffn_pre, g_ffn_post = norms

    x2 = x.reshape(m, d)
    h1 = _rmsnorm(x2, g_mix_pre, tiles["tm_norm"])
    pfull = _matmul(h1, w["w_in"], tiles["tm_in"], tiles["tn_in"])

    sh = shift0.reshape(bsz, 1, -1)
    shift_l = {"r": seg(sh, "r", rw), "k": seg(sh, "k", rw), "v": seg(sh, "v", rw), "small": rw_small(sh)}
    cols_rw = {n: starts[n] // RW_GROUP for n in ("r", "k", "v")}
    cols_rw["small"] = starts["small"] // RW_SMALL
    o_rw, wkv1 = _rwkv(pfull, cols_rw, shift_l, wkv0, w, bsz, t)

    pos = past + jnp.arange(t, dtype=jnp.int32)
    cols_sa = {"q": starts["q"] // sa_w, "ksa": starts["ksa"] // kv_w, "qi": starts["qi"] // qi_w,
               "kiw": starts["kiw"] // LANES}
    q_r, k_r, qi_r, kiw_r = _rope(pfull, cols_sa, _rope_tables(pos), bsz, t, tq, sa_w, kv_w, qi_w)
    k_new = k_r.reshape(bsz, t, kv_w)
    v_new = pfull[:, starts["vsa"]:starts["vsa"] + kv_w].reshape(bsz, t, kv_w)
    ki_new = kiw_r[:, :IDX_DIM].reshape(bsz, t, IDX_DIM)
    l_all = past + t
    lp = -(-l_all // kb) * kb
    nkb = lp // kb
    cat = lambda old, new: jnp.pad(
        jnp.concatenate([_mx(old.reshape(bsz, past, new.shape[-1])), _mx(new)], axis=1),
        ((0, 0), (0, lp - l_all), (0, 0)))
    k_all, v_all, ki_all = cat(past_k, k_new), cat(past_v, v_new), cat(past_ki, ki_new)
    nq = t // tq
    qit = (qi_r.reshape(bsz, nq, tq, IDX_HEADS, IDX_DIM).transpose(0, 1, 4, 3, 2)
           .reshape(bsz, nq, IDX_DIM, IDX_HEADS * tq))
    w_idx = kiw_r[:, IDX_DIM:IDX_DIM + IDX_HEADS].reshape(bsz, t, IDX_HEADS).transpose(0, 2, 1)
    topk = min(TOPK_MAX, l_all // 4)
    mask = _index_mask(qit, w_idx, ki_all.reshape(bsz, nkb, kb, IDX_DIM), bsz, t, tq, kb, past, topk)
    o_sa = _attention(q_r, k_all, v_all, mask, bsz, t, tq, kb, past)

    mix_in = jnp.concatenate([o_rw, o_sa], axis=1)
    x1 = _matmul_norm_residual(mix_in, w["w_out"], x2, g_mix_post, tm, tiles["tk"], "out_proj")
    h2 = _rmsnorm(x1, g_ffn_pre, tiles["tm_norm"])
    act, conv1 = _ffn_up(h2, w["w_up"], conv0, w["conv_w"], w["conv_b"], bsz, tiles["tm_up"], tiles["tn_up"])
    x_out = _matmul_norm_residual(act, w["w_down"], x1, g_ffn_post, tm, tiles["tk"], "ffn_down")

    orig = dims["orig"]
    last = pfull.reshape(bsz, t, -1)[:, t - 1:t]
    sm = last[..., starts["small"]:starts["small"] + 4 * LANES]
    shift1 = jnp.concatenate([
        last[..., starts["r"]:starts["r"] + rw], sm[..., :D_DECAY],
        last[..., starts["k"]:starts["k"] + rw], last[..., starts["v"]:starts["v"] + rw],
        sm[..., LANES:LANES + D_AAA], sm[..., 2 * LANES:2 * LANES + D_GATE]], axis=-1)
    del orig
    new = (k_new.reshape(bsz, t, SA_KV_HEADS, SA_HEAD), v_new.reshape(bsz, t, SA_KV_HEADS, SA_HEAD), ki_new,
           wkv1, shift1, conv1)
    return x_out.reshape(bsz, t, d), new


def _tiles(bsz, t):
    big = t >= 1024
    return {
        "tm_norm": 256 if big else CHUNK,
        "tm_in": 1024 if big else bsz * t,
        "tn_in": 640,
        "tm": 512 if big else CHUNK,
        "tk": 1024,
        "tm_up": 512 if big else CHUNK,
        "tn_up": 512,
        "tq": 256 if big else CHUNK,
        "kb": 512,
    }


def _run_stream(x, past_k, past_v, past_ki, wkv0, shift0, conv0, norm_w, weights, dims, depth):
    outs = []
    tiles = _tiles(x.shape[0], x.shape[1])
    for l in range(depth):
        w_l = {n: (v[l] if n != "conv_w" else v[l]) for n, v in weights.items()}
        norms = tuple(g[l].reshape(1, -1) for g in norm_w)
        x, st = _layer(x, past_k[l], past_v[l], past_ki[l], wkv0[l], shift0[l], conv0[l], norms, w_l, dims[l], tiles)
        outs.append(st)
    return x, [jnp.stack(s) for s in zip(*outs)]


def kernel(x_prompt, x_sample, cache_k, cache_v, cache_kidx, state_wkv, state_shift, state_conv, norm_mix_pre, norm_mix_post, norm_ffn_pre, norm_ffn_post, w_in, mu_shift, w_decay_up, decay_bias, w_a_up, a_bias, w_gate_up, k_k, k_a, r_k, lnx_w, lnx_b, w_out, w_up, conv_w, conv_b, w_down):
    depth, d_model = norm_mix_pre.shape
    per_layer = [_prepare_weights(w_in[l], mu_shift[l], w_decay_up[l], decay_bias[l], w_a_up[l], a_bias[l],
                                  w_gate_up[l], k_k[l], k_a[l], r_k[l], lnx_w[l], lnx_b[l], w_out[l], w_up[l],
                                  conv_w[l], conv_b[l], w_down[l], d_model) for l in range(depth)]
    weights = {n: [pw[0][n] for pw in per_layer] for n in per_layer[0][0]}
    dims = [pw[1] for pw in per_layer]
    norm_w = (norm_mix_pre, norm_mix_post, norm_ffn_pre, norm_ffn_post)

    dt = x_prompt.dtype
    bp = x_prompt.shape[0]
    rw_heads = state_wkv.shape[2]
    zk = jnp.zeros((depth, bp, 0, SA_KV_HEADS, SA_HEAD), dt)
    zki = jnp.zeros((depth, bp, 0, IDX_DIM), dt)
    zwkv = jnp.zeros((depth, bp, rw_heads, RW_HEAD, RW_HEAD), dt)
    zshift = jnp.zeros((depth, bp, 1, state_shift.shape[-1]), dt)
    zconv = jnp.zeros((depth, bp, CONV_W - 1, state_conv.shape[-1]), dt)
    y_prompt, p_new = _run_stream(x_prompt, zk, zk, zki, zwkv, zshift, zconv, norm_w, weights, dims, depth)
    y_sample, s_new = _run_stream(x_sample, cache_k, cache_v, cache_kidx, state_wkv, state_shift, state_conv,
                                  norm_w, weights, dims, depth)
    return (y_prompt, y_sample, *p_new, *s_new)
```

```python
import functools

import numpy as np
import jax
import jax.numpy as jnp
from jax import lax
from jax.experimental import pallas as pl
from jax.experimental.pallas import tpu as pltpu

F32 = jnp.float32
MXU_DTYPE = jnp.bfloat16
HI = lax.Precision.HIGHEST

CHUNK = 64
RW_HEAD = 64
D_DECAY = 96
D_AAA = 96
D_GATE = 256
SA_HEAD = 128
SA_KV_HEADS = 4
IDX_HEADS = 16
IDX_DIM = 64
TOPK_MAX = 256
CONV_W = 3
ROPE_THETA = 10000.0
NORM_EPS = 1e-6
LNX_EPS = 64e-5

LANES = 128
V7X_VMEM_LIMIT_BYTES = 60000 * 1024
INT_MIN = -(2 ** 31)
NEG_BIG = -1e30


def _cparams(n_grid, vmem_bytes):
    limit = int(min(V7X_VMEM_LIMIT_BYTES, max(32 * 1024 * 1024, vmem_bytes)))
    return pltpu.CompilerParams(dimension_semantics=("arbitrary",) * n_grid, vmem_limit_bytes=limit)


def _nt(a, b, precision=None):
    return lax.dot_general(a, b, (((1,), (1,)), ((), ())), precision=precision, preferred_element_type=F32)


def _tn(a, b, precision=None):
    return lax.dot_general(a, b, (((0,), (0,)), ((), ())), precision=precision, preferred_element_type=F32)


def _mx(x):
    return x.astype(MXU_DTYPE)


def _rmsnorm_kernel(x_ref, g_ref, o_ref):
    x = x_ref[...]
    y = x * lax.rsqrt(jnp.mean(x * x, axis=-1, keepdims=True) + NORM_EPS)
    o_ref[...] = (y * g_ref[...]).astype(o_ref.dtype)


def _rmsnorm(x, g, tm):
    m, d = x.shape
    return pl.pallas_call(
        _rmsnorm_kernel,
        grid=(m // tm,),
        in_specs=[pl.BlockSpec((tm, d), lambda i: (i, 0)), pl.BlockSpec((1, d), lambda i: (0, 0))],
        out_specs=pl.BlockSpec((tm, d), lambda i: (i, 0)),
        out_shape=jax.ShapeDtypeStruct((m, d), MXU_DTYPE),
        compiler_params=_cparams(1, 6 * tm * d * 4),
        name="rmsnorm",
    )(x, g)


def _mm_kernel(a_ref, b_ref, o_ref):
    o_ref[...] = jnp.dot(a_ref[...], b_ref[...], preferred_element_type=F32)


IN_PROJ_TN = 1024


def _matmul(a, b, tm, tn):
    m, k = a.shape
    n = b.shape[1]
    isz = jnp.dtype(MXU_DTYPE).itemsize
    vmem = 2 * (tm * k * isz + k * tn * isz + tm * tn * 4) + 2 * tm * tn * 4 + (8 << 20)
    return pl.pallas_call(
        _mm_kernel,
        grid=(m // tm, n // tn),
        in_specs=[pl.BlockSpec((tm, k), lambda i, j: (i, 0)), pl.BlockSpec((k, tn), lambda i, j: (0, j))],
        out_specs=pl.BlockSpec((tm, tn), lambda i, j: (i, j)),
        out_shape=jax.ShapeDtypeStruct((m, n), F32),
        compiler_params=_cparams(2, vmem),
        name="in_proj",
    )(a, b)


MM_NORM_ROWS = 128


def _mm_norm_kernel(a_ref, w_ref, x_ref, gpost_ref, x1_ref, *, nn):
    n = pl.program_id(1)
    tm, d = x1_ref.shape
    tn = w_ref.shape[1]
    x1_ref[:, pl.ds(pl.multiple_of(n * tn, LANES), tn)] = jnp.dot(a_ref[...], w_ref[...],
                                                                  preferred_element_type=F32)

    @pl.when(n == nn - 1)
    def _():
        step = min(tm, MM_NORM_ROWS)
        for r in range(0, tm, step):
            rows = slice(r, r + step)
            y = x1_ref[rows, :]
            y = y * lax.rsqrt(jnp.mean(y * y, axis=-1, keepdims=True) + NORM_EPS) * gpost_ref[...]
            x1_ref[rows, :] = x_ref[rows, :] + y


def _matmul_norm_residual(a, w, x, g_post, tm, tn, name):
    m, kdim = a.shape
    d = w.shape[1]
    nn = d // tn
    isz = jnp.dtype(MXU_DTYPE).itemsize
    vmem = (2 * (tm * kdim * isz + kdim * tn * isz + tm * d * 4) + tm * d * 4
            + 2 * tm * tn * 4 + 6 * MM_NORM_ROWS * d * 4 + (4 << 20))
    return pl.pallas_call(
        functools.partial(_mm_norm_kernel, nn=nn),
        grid=(m // tm, nn),
        in_specs=[
            pl.BlockSpec((tm, kdim), lambda i, n: (i, 0)),
            pl.BlockSpec((kdim, tn), lambda i, n: (0, n)),
            pl.BlockSpec((tm, d), lambda i, n: (i, 0), pipeline_mode=pl.Buffered(1)),
            pl.BlockSpec((1, d), lambda i, n: (0, 0)),
        ],
        out_specs=pl.BlockSpec((tm, d), lambda i, n: (i, 0)),
        out_shape=jax.ShapeDtypeStruct((m, d), F32),
        compiler_params=_cparams(2, vmem),
        name=name,
    )(a, w, x, g_post)


def _gelu_tanh(x):
    return 0.5 * x * (1.0 + jnp.tanh(0.7978845608028654 * (x + 0.044715 * x * x * x)))


def _ffn_up_kernel(h_ref, wg_ref, wv_ref, c0g_ref, c0v_ref, cwg_ref, cwv_ref, cbg_ref, cbv_ref,
                   a_ref, cg_ref, cv_ref, carry_g, carry_v, *, blocks_per_stream, tm):
    i = pl.program_id(1) % blocks_per_stream
    n_str = carry_g.shape[0]
    ts = tm // n_str

    @pl.when(i == 0)
    def _():
        carry_g[...] = c0g_ref[...]
        carry_v[...] = c0v_ref[...]

    h = h_ref[...]
    row = lax.broadcasted_iota(jnp.int32, (tm, 1), 0)

    def conv(u, carry_ref, cw_ref, cb_ref, out_ref):
        u1 = pltpu.roll(u, 1, axis=0)
        u2 = pltpu.roll(u, 2, axis=0)
        for s in range(n_str):
            p = carry_ref[s]
            u1 = jnp.where(row == s * ts, p[1:2], u1)
            u2 = jnp.where(row == s * ts, p[0:1], jnp.where(row == s * ts + 1, p[1:2], u2))
            last = u[(s + 1) * ts - 2:(s + 1) * ts]
            carry_ref[s] = last
            out_ref[s] = last
        cw = cw_ref[...]
        return u2 * cw[0:1] + u1 * cw[1:2] + u * cw[2:3] + cb_ref[...]

    gate = conv(jnp.dot(h, wg_ref[...], preferred_element_type=F32), carry_g, cwg_ref, cbg_ref, cg_ref)
    val = conv(jnp.dot(h, wv_ref[...], preferred_element_type=F32), carry_v, cwv_ref, cbv_ref, cv_ref)
    a_ref[...] = (_gelu_tanh(gate) * val).astype(a_ref.dtype)


def _ffn_up(h, w_up, conv0, conv_w, conv_b, n_streams, tm, tn):
    m, d = h.shape
    f2 = w_up.shape[1]
    f = f2 // 2
    nj = f // tn
    nr = m // tm
    t = m // n_streams
    spb = max(1, tm // t)
    bps = max(1, t // tm)
    isz = jnp.dtype(MXU_DTYPE).itemsize
    vmem = 2 * (tm * d * isz + 2 * d * tn * isz + tm * tn * isz) + 10 * tm * tn * 4 + (4 << 20)
    col = lambda off: (lambda j, r: (0, off + j))
    st = lambda off: (lambda j, r: (r // bps, 0, off + j))
    a, cg, cv = pl.pallas_call(
        functools.partial(_ffn_up_kernel, blocks_per_stream=bps, tm=tm),
        grid=(nj, nr),
        in_specs=[
            pl.BlockSpec((tm, d), lambda j, r: (r, 0)),
            pl.BlockSpec((d, tn), col(0)),
            pl.BlockSpec((d, tn), col(nj)),
            pl.BlockSpec((spb, CONV_W - 1, tn), st(0)),
            pl.BlockSpec((spb, CONV_W - 1, tn), st(nj)),
            pl.BlockSpec((CONV_W, tn), col(0)),
            pl.BlockSpec((CONV_W, tn), col(nj)),
            pl.BlockSpec((1, tn), col(0)),
            pl.BlockSpec((1, tn), col(nj)),
        ],
        out_specs=[
            pl.BlockSpec((tm, tn), lambda j, r: (r, j)),
            pl.BlockSpec((spb, CONV_W - 1, tn), st(0)),
            pl.BlockSpec((spb, CONV_W - 1, tn), st(0)),
        ],
        out_shape=[
            jax.ShapeDtypeStruct((m, f), MXU_DTYPE),
            jax.ShapeDtypeStruct((n_streams, CONV_W - 1, f), F32),
            jax.ShapeDtypeStruct((n_streams, CONV_W - 1, f), F32),
        ],
        scratch_shapes=[pltpu.VMEM((spb, CONV_W - 1, tn), F32)] * 2,
        compiler_params=_cparams(2, vmem),
        name="ffn_up_conv",
    )(h, w_up, w_up, conv0, conv0, conv_w, conv_w, conv_b, conv_b)
    return a, jnp.concatenate([cg, cv], axis=-1)


RW_GROUP = 8 * LANES
RW_SMALL = 4 * LANES


def _softplus(z):
    return jnp.maximum(z, 0.0) + jnp.log(1.0 + jnp.exp(-jnp.abs(z)))


def _sigmoid(z):
    return 1.0 / (1.0 + jnp.exp(-z))


def _split(x):
    hi = x.astype(MXU_DTYPE)
    return hi, (x - hi.astype(F32)).astype(MXU_DTYPE)


def _dg(a, b, dims):
    return lax.dot_general(_mx(a), _mx(b), (dims, ((), ())), preferred_element_type=F32)


_NN = ((1,), (0,))
_NT = ((1,), (1,))
_TN = ((0,), (0,))


def _segsum(x, seg):
    hi, lo = _split(x)
    return jnp.dot(hi, seg, preferred_element_type=F32) + jnp.dot(lo, seg, preferred_element_type=F32)


def _each(f, *lists):
    return [f(*args) for args in zip(*lists)]


def _wkv_chunk(r, k, v, kk, b, cum, logd, s_bd, consts):
    m0, m1, tri_mask = consts
    c = r[0].shape[0]
    c2 = 2 * c
    stack = lambda x: jnp.concatenate([x * m0, x * m1], axis=0)
    g_end = _each(lambda cm: jnp.exp(cm[c - 1:c]), cum)
    e_inc = _each(jnp.exp, cum)
    e_prev = _each(lambda cm, ld: jnp.exp(cm - ld), cum, logd)
    e_neg = _each(lambda cm: jnp.exp(-cm), cum)
    e_tail = _each(lambda g, e: g * e, g_end, e_neg)
    vs = _each(stack, v)
    lhs = _each(lambda kk_, r_, ep, ei: jnp.concatenate([stack(kk_ * ep), stack(r_ * ei)], axis=0),
                kk, r, e_prev, e_inc)
    rhs = _each(lambda k_, b_, en: jnp.concatenate([stack(k_ * en), stack(b_ * en)], axis=0), k, b, e_neg)
    tails = _each(lambda k_, b_, et: jnp.concatenate([stack(k_ * et), stack(b_ * et)], axis=0), k, b, e_tail)
    amat = _each(lambda x, y: jnp.where(tri_mask, _dg(x, y, _NT), 0.0), lhs, rhs)
    sp = _each(lambda x, s: _dg(x, s, _NT), lhs, s_bd)
    av = _each(lambda a_, v_: _dg(a_[:, :c2], v_, _NN), amat, vs)
    pw = _each(lambda a_: -a_[:c2, c2:], amat)
    x = _each(lambda s_, a_: s_[:c2] + a_[:c2], sp, av)
    n_lvl = int(np.log2(c))
    for lvl in range(n_lvl - 1):
        z = _each(lambda p_, x_: _dg(p_, jnp.concatenate([p_, x_], axis=1), _NN), pw, x)
        pw = _each(lambda z_: z_[:, :c2], z)
        x = _each(lambda x_, z_: x_ + z_[:, c2:], x, z)
    u = _each(lambda p_, x_: x_ + _dg(p_, x_, _NN), pw, x)
    ys = _each(lambda s_, a_, am, u_: s_[c2:] + a_[c2:] - _dg(am[c2:, c2:], u_, _NN), sp, av, amat, u)
    y = _each(lambda y_: y_[:c] + y_[c:], ys)
    s_new = _each(lambda s, g, v_, u_, t_: s * g + _dg(jnp.concatenate([v_, -u_], axis=0), t_, _TN),
                  s_bd, g_end, vs, u, tails)
    return y, s_new


def _rwkv_kernel(pr_ref, pk_ref, pv_ref, ps_ref, sr_ref, sk_ref, sv_ref, ss_ref, wkv0_ref,
                 mur_ref, muk_ref, muv_ref, mus_ref, wdec_ref, dbias_ref, wa_ref, abias_ref, wg_ref,
                 kk_ref, ka_ref, rk_ref, lnw_ref, lnb_ref,
                 o_ref, wkv1_ref,
                 cr_ref, ck_ref, cv_ref, cs_ref, state_ref, *, n_chunks):
    c = pl.program_id(2)
    C = CHUNK
    npairs = RW_GROUP // LANES

    lane = lax.broadcasted_iota(jnp.int32, (1, LANES), 1)
    m0 = (lane < RW_HEAD).astype(F32)
    m1 = 1.0 - m0
    ri = lax.broadcasted_iota(jnp.int32, (4 * C, 4 * C), 0)
    ci = lax.broadcasted_iota(jnp.int32, (4 * C, 4 * C), 1)
    same_head = ((ri // C) % 2) == ((ci // C) % 2)
    tri_mask = same_head & ((ri % C) + ri // (2 * C) > (ci % C))
    li = lax.broadcasted_iota(jnp.int32, (LANES, LANES), 0)
    lj = lax.broadcasted_iota(jnp.int32, (LANES, LANES), 1)
    seg = ((li // RW_HEAD) == (lj // RW_HEAD)).astype(MXU_DTYPE)
    consts = (m0, m1, tri_mask)

    @pl.when(c == 0)
    def _():
        cr_ref[...] = sr_ref[...]
        ck_ref[...] = sk_ref[...]
        cv_ref[...] = sv_ref[...]
        cs_ref[...] = ss_ref[...]
        z = jnp.zeros((RW_HEAD, RW_HEAD), F32)
        for p in range(npairs):
            s0 = wkv0_ref[2 * p]
            s1 = wkv0_ref[2 * p + 1]
            state_ref[p] = jnp.concatenate(
                [jnp.concatenate([s0, z], axis=1), jnp.concatenate([z, s1], axis=1)], axis=0)

    row = lax.broadcasted_iota(jnp.int32, (C, 1), 0)

    def shifted(p_ref, carry_ref, mu_ref):
        p = p_ref[...]
        prev = jnp.where(row == 0, carry_ref[...], pltpu.roll(p, 1, axis=0))
        carry_ref[...] = p[C - 1:C]
        return p + (prev - p) * mu_ref[...]

    r = shifted(pr_ref, cr_ref, mur_ref)
    k = shifted(pk_ref, ck_ref, muk_ref)
    v = shifted(pv_ref, cv_ref, muv_ref)
    sm = shifted(ps_ref, cs_ref, mus_ref)
    wd, ad, gd = sm[:, 0:LANES], sm[:, LANES:2 * LANES], sm[:, 2 * LANES:4 * LANES]

    dec_in = dbias_ref[...] + jnp.dot(_mx(jnp.tanh(wd)), wdec_ref[...], preferred_element_type=F32)
    w_log = -_softplus(-dec_in) - 0.5
    logd = -jnp.exp(w_log)
    a = _sigmoid(abias_ref[...] + jnp.dot(_mx(ad), wa_ref[...], preferred_element_type=F32))
    g = jnp.dot(_mx(_sigmoid(gd)), wg_ref[...], preferred_element_type=F32)
    kk = k * kk_ref[...]
    k2 = k * (1.0 + (a - 1.0) * ka_ref[...])
    rkr = r * k2 * rk_ref[...]
    cum = logd
    for sft in (1, 2, 4, 8, 16, 32):
        cum = cum + jnp.where(row >= sft, pltpu.roll(cum, sft, axis=0), 0.0)

    pairs = lambda x: [x[:, p * LANES:(p + 1) * LANES] for p in range(npairs)]
    r_p, k_p, v_p, a_p, cum_p, logd_p = pairs(r), pairs(k2), pairs(v), pairs(a), pairs(cum), pairs(logd)
    sums = _each(lambda kk_, rkr_: _segsum(jnp.concatenate([kk_ * kk_, rkr_], axis=0), seg), pairs(kk), pairs(rkr))
    kk_p = _each(lambda kk_, s_: kk_ * lax.rsqrt(jnp.maximum(s_[:C], 1e-24)), pairs(kk), sums)
    b_p = _each(lambda kk_, a_: kk_ * a_, kk_p, a_p)
    y_p, s_new = _wkv_chunk(r_p, k_p, v_p, kk_p, b_p, cum_p, logd_p, [state_ref[p] for p in range(npairs)], consts)
    for p in range(npairs):
        state_ref[p] = s_new[p]
    yc = _each(lambda y_: y_ - _segsum(y_, seg) * (1.0 / RW_HEAD), y_p)
    var = _each(lambda yc_: _segsum(yc_ * yc_, seg) * (1.0 / RW_HEAD), yc)
    for p in range(npairs):
        sl = slice(p * LANES, (p + 1) * LANES)
        yn = yc[p] * lax.rsqrt(var[p] + LNX_EPS) * lnw_ref[:, sl] + lnb_ref[:, sl]
        o_ref[:, sl] = ((yn + sums[p][C:] * v_p[p]) * g[:, sl]).astype(o_ref.dtype)

    @pl.when(c == n_chunks - 1)
    def _():
        for p in range(npairs):
            s = state_ref[p]
            wkv1_ref[2 * p] = s[:RW_HEAD, :RW_HEAD]
            wkv1_ref[2 * p + 1] = s[RW_HEAD:, RW_HEAD:]


def _rwkv(pfull, cols, shift0, wkv0, w, n_streams, t):
    m = pfull.shape[0]
    rw = w["k_k"].shape[1]
    ng = rw // RW_GROUP
    nc = t // CHUNK
    hg = RW_GROUP // RW_HEAD
    rowblk = lambda off: (lambda b, g, c: (b * nc + c, off + g))
    fixed = lambda off: (lambda b, g, c: (b * nc + c, off))
    st = lambda b, g, c: (b, 0, g)
    st0 = lambda b, g, c: (b, 0, 0)
    wcol = lambda b, g, c: (0, g)
    in_specs = [
        pl.BlockSpec((CHUNK, RW_GROUP), rowblk(cols["r"])),
        pl.BlockSpec((CHUNK, RW_GROUP), rowblk(cols["k"])),
        pl.BlockSpec((CHUNK, RW_GROUP), rowblk(cols["v"])),
        pl.BlockSpec((CHUNK, RW_SMALL), fixed(cols["small"])),
        pl.BlockSpec((None, 1, RW_GROUP), st),
        pl.BlockSpec((None, 1, RW_GROUP), st),
        pl.BlockSpec((None, 1, RW_GROUP), st),
        pl.BlockSpec((None, 1, RW_SMALL), st0),
        pl.BlockSpec((None, hg, RW_HEAD, RW_HEAD), lambda b, g, c: (b, g, 0, 0)),
        pl.BlockSpec((1, RW_GROUP), wcol), pl.BlockSpec((1, RW_GROUP), wcol), pl.BlockSpec((1, RW_GROUP), wcol),
        pl.BlockSpec((1, RW_SMALL), lambda b, g, c: (0, 0)),
        pl.BlockSpec((LANES, RW_GROUP), wcol), pl.BlockSpec((1, RW_GROUP), wcol),
        pl.BlockSpec((LANES, RW_GROUP), wcol), pl.BlockSpec((1, RW_GROUP), wcol),
        pl.BlockSpec((2 * LANES, RW_GROUP), wcol),
        pl.BlockSpec((1, RW_GROUP), wcol), pl.BlockSpec((1, RW_GROUP), wcol), pl.BlockSpec((1, RW_GROUP), wcol),
        pl.BlockSpec((1, RW_GROUP), wcol), pl.BlockSpec((1, RW_GROUP), wcol),
    ]
    o, wkv1 = pl.pallas_call(
        functools.partial(_rwkv_kernel, n_chunks=nc),
        grid=(n_streams, ng, nc),
        in_specs=in_specs,
        out_specs=[
            pl.BlockSpec((CHUNK, RW_GROUP), lambda b, g, c: (b * nc + c, g)),
            pl.BlockSpec((None, hg, RW_HEAD, RW_HEAD), lambda b, g, c: (b, g, 0, 0)),
        ],
        out_shape=[
            jax.ShapeDtypeStruct((m, rw), MXU_DTYPE),
            jax.ShapeDtypeStruct((n_streams, rw // RW_HEAD, RW_HEAD, RW_HEAD), F32),
        ],
        scratch_shapes=[pltpu.VMEM((1, RW_GROUP), F32)] * 3 + [pltpu.VMEM((1, RW_SMALL), F32)]
        + [pltpu.VMEM((RW_GROUP // LANES, LANES, LANES), F32)],
        compiler_params=_cparams(3, 32 << 20),
        name="rwkv7_chunked",
    )(pfull, pfull, pfull, pfull, shift0["r"], shift0["k"], shift0["v"], shift0["small"], wkv0,
      w["mu_r"], w["mu_k"], w["mu_v"], w["mu_small"], w["w_decay_up"], w["decay_bias"], w["w_a_up"], w["a_bias"],
      w["w_gate_up"], w["k_k"], w["k_a"], w["r_k"], w["lnx_w"], w["lnx_b"])
    return o, wkv1


def _rope_kernel(q_ref, k_ref, qi_ref, kiw_ref, ca_ref, sa_ref, cb_ref, sb1_ref, sb2_ref,
                 qo_ref, ko_ref, qio_ref, kiwo_ref):
    ca, sa = ca_ref[...], sa_ref[...]
    cb, sb1, sb2 = cb_ref[...], sb1_ref[...], sb2_ref[...]

    def rope_head(x):
        return x * ca + pltpu.roll(x, SA_HEAD // 2, axis=1) * sa

    def rope_idx(x):
        return (x * cb + pltpu.roll(x, LANES - IDX_DIM // 2, axis=1) * sb1
                + pltpu.roll(x, IDX_DIM // 2, axis=1) * sb2)

    q_scale = float(SA_HEAD ** -0.5 * np.log2(np.e))
    for h in range(q_ref.shape[1] // LANES):
        sl = slice(h * LANES, (h + 1) * LANES)
        qo_ref[:, sl] = (rope_head(q_ref[:, sl]) * q_scale).astype(qo_ref.dtype)
    for h in range(k_ref.shape[1] // LANES):
        sl = slice(h * LANES, (h + 1) * LANES)
        ko_ref[:, sl] = rope_head(k_ref[:, sl])
    for hp in range(qi_ref.shape[1] // LANES):
        sl = slice(hp * LANES, (hp + 1) * LANES)
        qio_ref[:, sl] = rope_idx(qi_ref[:, sl]).astype(qio_ref.dtype)
    x = kiw_ref[...]
    lane = lax.broadcasted_iota(jnp.int32, (1, LANES), 1)
    kiwo_ref[...] = jnp.where(lane < IDX_DIM, rope_idx(x), x * float((IDX_HEADS * IDX_DIM) ** -0.5))


def _rope_tables(pos):
    def ang(half):
        inv = ROPE_THETA ** (-jnp.arange(half, dtype=F32) / half)
        return pos.astype(F32)[:, None] * inv[None, :]
    aa = ang(SA_HEAD // 2)
    ca = jnp.concatenate([jnp.cos(aa), jnp.cos(aa)], axis=1)
    sa = jnp.concatenate([-jnp.sin(aa), jnp.sin(aa)], axis=1)
    ab = ang(IDX_DIM // 2)
    z = jnp.zeros_like(ab)
    cb = jnp.concatenate([jnp.cos(ab)] * 4, axis=1)
    sb1 = jnp.concatenate([-jnp.sin(ab), z, -jnp.sin(ab), z], axis=1)
    sb2 = jnp.concatenate([z, jnp.sin(ab), z, jnp.sin(ab)], axis=1)
    return ca, sa, cb, sb1, sb2


def _rope(pfull, cols, tables, n_streams, t, tm, sa_w, kv_w, qi_w):
    m = pfull.shape[0]
    bps = t // tm
    tab = pl.BlockSpec((tm, LANES), lambda r: (r % bps, 0))
    return pl.pallas_call(
        _rope_kernel,
        grid=(m // tm,),
        in_specs=[
            pl.BlockSpec((tm, sa_w), lambda r: (r, cols["q"])),
            pl.BlockSpec((tm, kv_w), lambda r: (r, cols["ksa"])),
            pl.BlockSpec((tm, qi_w), lambda r: (r, cols["qi"])),
            pl.BlockSpec((tm, LANES), lambda r: (r, cols["kiw"])),
            tab, tab, tab, tab, tab,
        ],
        out_specs=[
            pl.BlockSpec((tm, sa_w), lambda r: (r, 0)),
            pl.BlockSpec((tm, kv_w), lambda r: (r, 0)),
            pl.BlockSpec((tm, qi_w), lambda r: (r, 0)),
            pl.BlockSpec((tm, LANES), lambda r: (r, 0)),
        ],
        out_shape=[
            jax.ShapeDtypeStruct((m, sa_w), MXU_DTYPE),
            jax.ShapeDtypeStruct((m, kv_w), F32),
            jax.ShapeDtypeStruct((m, qi_w), MXU_DTYPE),
            jax.ShapeDtypeStruct((m, LANES), F32),
        ],
        compiler_params=_cparams(1, 32 << 20),
        name="rope",
    )(pfull, pfull, pfull, pfull, *tables)


IDX_HEAD_GROUP = 4
IDX_VALUE_PASSES = 16


FOLD_CHAINS = 4


def _fold_sublane_tiles(x, op):
    parts = [x[r:r + 8] for r in range(0, x.shape[0], 8)]
    chains = [functools.reduce(op, parts[c::FOLD_CHAINS]) for c in range(min(FOLD_CHAINS, len(parts)))]
    return functools.reduce(op, chains)


def _f32_to_key(x):
    bits = lax.bitcast_convert_type(x, jnp.int32)
    return bits ^ ((bits >> 31) & jnp.int32(0x7FFFFFFF))


def _key_to_f32(key):
    return lax.bitcast_convert_type(key ^ ((key >> 31) & jnp.int32(0x7FFFFFFF)), F32)


def _index_kernel(qit_ref, w_ref, ki_ref, mask_ref, keys_ref, *, tq, kb, nkb, past, topk):
    i = pl.program_id(1)
    n_adm = jnp.minimum(nkb, (past + (i + 1) * tq + kb - 1) // kb)
    chunk_bits = CHUNK.bit_length() - 1
    qpos = past + i * tq + lax.broadcasted_iota(jnp.int32, (1, tq), 1)
    lim = ((qpos >> chunk_bits) + 1) << chunk_bits
    krow = lax.broadcasted_iota(jnp.int32, (kb, 1), 0)
    hg = IDX_HEAD_GROUP
    fold = _fold_sublane_tiles

    def score_block(j, carry):
        smax, smin = carry
        kblk = ki_ref[j]
        acc = jnp.zeros((kb, tq), F32)
        for g0 in range(0, IDX_HEADS, hg):
            s = jnp.dot(kblk, qit_ref[:, g0 * tq:(g0 + hg) * tq], preferred_element_type=F32)
            for g in range(hg):
                acc = acc + jnp.maximum(s[:, g * tq:(g + 1) * tq], 0.0) * w_ref[g0 + g:g0 + g + 1, :]
        score = acc + 0.0
        adm = j * kb + krow < lim
        keys_ref[j] = jnp.where(adm, _f32_to_key(score), jnp.int32(INT_MIN))
        smax = jnp.maximum(smax, fold(jnp.where(adm, score, -jnp.inf), jnp.maximum))
        smin = jnp.minimum(smin, fold(jnp.where(adm, score, jnp.inf), jnp.minimum))
        return smax, smin

    smax, smin = lax.fori_loop(0, n_adm, score_block,
                               (jnp.full((8, tq), -jnp.inf, F32), jnp.full((8, tq), jnp.inf, F32)))
    lo0 = _f32_to_key(jnp.min(smin, axis=0, keepdims=True))
    hi0 = _f32_to_key(jnp.max(smax, axis=0, keepdims=True)) + 1

    def count_ge(thr):
        def body(j, cnt):
            return cnt + fold(jnp.where(keys_ref[j] >= thr, 1.0, 0.0), jnp.add)
        return jnp.sum(lax.fori_loop(0, n_adm, body, jnp.zeros((8, tq), F32)), axis=0, keepdims=True)

    def midpoint(lo, hi):
        return (lo >> 1) + (hi >> 1) + (lo & hi & 1)

    def unfinished(lo, hi, cnt_lo):
        active = (midpoint(lo, hi) != lo) & (cnt_lo != float(topk))
        return jnp.max(jnp.where(active, 1.0, 0.0))

    def bisect(state):
        p, lo, hi, cnt_lo, _ = state
        mid_v = _f32_to_key(0.5 * _key_to_f32(lo) + 0.5 * _key_to_f32(hi))
        by_value = (p < IDX_VALUE_PASSES) & (mid_v > lo) & (mid_v < hi)
        mid = jnp.where(by_value, mid_v, midpoint(lo, hi))
        cnt = count_ge(mid)
        ge = cnt >= float(topk)
        lo, hi, cnt_lo = jnp.where(ge, mid, lo), jnp.where(ge, hi, mid), jnp.where(ge, cnt, cnt_lo)
        return p + 1, lo, hi, cnt_lo, unfinished(lo, hi, cnt_lo)

    cnt0 = jnp.full((1, tq), -1.0, F32)
    state = (jnp.int32(0), lo0, hi0, cnt0, unfinished(lo0, hi0, cnt0))
    thr = lax.while_loop(lambda st: st[4] > 0.5, bisect, state)[1]

    def write(j, carry):
        mask_ref[j] = jnp.where(keys_ref[j] >= thr, 1.0, 0.0).T.astype(mask_ref.dtype)
        return carry

    lax.fori_loop(0, n_adm, write, 0)

    def clear(j, carry):
        mask_ref[j] = jnp.zeros((tq, kb), mask_ref.dtype)
        return carry

    lax.fori_loop(n_adm, nkb, clear, 0)


def _index_mask(qit, w, ki, n_streams, t, tq, kb, past, topk):
    nkb = ki.shape[1]
    nq = t // tq
    isz = jnp.dtype(MXU_DTYPE).itemsize
    vmem = (nkb * kb * tq * 4 + 2 * nkb * kb * tq + 2 * nkb * kb * LANES * isz + 2 * IDX_DIM * IDX_HEADS * tq * isz
            + 6 * kb * IDX_HEAD_GROUP * tq * 4 + (8 << 20))
    return pl.pallas_call(
        functools.partial(_index_kernel, tq=tq, kb=kb, nkb=nkb, past=past, topk=topk),
        grid=(n_streams, nq),
        in_specs=[
            pl.BlockSpec((None, None, IDX_DIM, IDX_HEADS * tq), lambda b, i: (b, i, 0, 0)),
            pl.BlockSpec((None, IDX_HEADS, tq), lambda b, i: (b, 0, i)),
            pl.BlockSpec((None, nkb, kb, IDX_DIM), lambda b, i: (b, 0, 0, 0)),
        ],
        out_specs=pl.BlockSpec((None, nkb, tq, kb), lambda b, i: (b, 0, i, 0)),
        out_shape=jax.ShapeDtypeStruct((n_streams, nkb, t, kb), jnp.int8),
        scratch_shapes=[pltpu.VMEM((nkb, kb, tq), jnp.int32)],
        compiler_params=_cparams(2, vmem),
        name="indexer_topk_mask",
    )(qit, w, ki)


def _attn_kernel(it_ref, jt_ref, last_ref, q_ref, k_ref, v_ref, m_ref, o_ref, qs_ref, mx_ref, l_ref, acc_ref,
                 *, group):
    step = pl.program_id(1)
    tq = q_ref.shape[0]
    tk = k_ref.shape[0]
    n_kv = k_ref.shape[1] // SA_HEAD
    rows = group * tq
    n_lane_tiles = tk // LANES

    @pl.when(jt_ref[step] == 0)
    def _():
        for n in range(n_kv):
            for g in range(group):
                h = n * group + g
                qs_ref[n, g * tq:(g + 1) * tq, :] = q_ref[:, h * SA_HEAD:(h + 1) * SA_HEAD]
        mx_ref[...] = jnp.full(mx_ref.shape, NEG_BIG, F32)
        l_ref[...] = jnp.zeros(l_ref.shape, F32)
        acc_ref[...] = jnp.zeros(acc_ref.shape, F32)

    bias = jnp.where(m_ref[...].astype(F32) > 0.0, 0.0, NEG_BIG)
    kv = range(n_kv)
    s = [_nt(qs_ref[n], k_ref[:, n * SA_HEAD:(n + 1) * SA_HEAD]) for n in kv]
    s = [(x.reshape(group, tq, tk) + bias[None]).reshape(rows, tk) for x in s]
    tiles = [[x[:, c * LANES:(c + 1) * LANES] for c in range(n_lane_tiles)] for x in s]
    m_old = [mx_ref[n] for n in kv]
    m_new = [jnp.maximum(mo, jnp.max(functools.reduce(jnp.maximum, t), axis=1, keepdims=True))
             for mo, t in zip(m_old, tiles)]
    alpha = [jnp.exp2(mo - mn) for mo, mn in zip(m_old, m_new)]
    p = [[jnp.exp2(x - mn) for x in t] for t, mn in zip(tiles, m_new)]
    for n in kv:
        mx_ref[n] = m_new[n]
        l_ref[n] = alpha[n] * l_ref[n] + functools.reduce(jnp.add, p[n])
    pv = [jnp.dot(_mx(jnp.concatenate(p[n], axis=1)), v_ref[:, n * SA_HEAD:(n + 1) * SA_HEAD],
                  preferred_element_type=F32) for n in kv]
    for n in kv:
        acc_ref[n] = alpha[n] * acc_ref[n] + pv[n]

    @pl.when(last_ref[step] == 1)
    def _():
        for n in range(n_kv):
            o = acc_ref[n] / jnp.sum(l_ref[n], axis=1, keepdims=True)
            for g in range(group):
                h = n * group + g
                o_ref[:, h * SA_HEAD:(h + 1) * SA_HEAD] = o[g * tq:(g + 1) * tq].astype(o_ref.dtype)


def _attn_tiles(t, tq, tk, nkb, past):
    ii, jj, last = [], [], []
    for i in range(t // tq):
        n_adm = min(nkb, -(-(past + (i + 1) * tq) // tk))
        for j in range(n_adm):
            ii.append(i)
            jj.append(j)
            last.append(int(j == n_adm - 1))
    return tuple(jnp.asarray(np.array(x, np.int32)) for x in (ii, jj, last))


def _attention(q, k_all, v_all, mask, n_streams, t, tq, tk, past):
    m, sa_w = q.shape
    kv_w = k_all.shape[2]
    nkb = mask.shape[1]
    nq = t // tq
    n_heads = sa_w // SA_HEAD
    group = n_heads // SA_KV_HEADS
    it, jt, last = _attn_tiles(t, tq, tk, nkb, past)
    grid_spec = pltpu.PrefetchScalarGridSpec(
        num_scalar_prefetch=3,
        grid=(n_streams, int(it.shape[0])),
        in_specs=[
            pl.BlockSpec((tq, sa_w), lambda b, s, it, jt, lt: (b * nq + it[s], 0)),
            pl.BlockSpec((None, tk, kv_w), lambda b, s, it, jt, lt: (b, jt[s], 0)),
            pl.BlockSpec((None, tk, kv_w), lambda b, s, it, jt, lt: (b, jt[s], 0)),
            pl.BlockSpec((None, None, tq, tk), lambda b, s, it, jt, lt: (b, jt[s], it[s], 0)),
        ],
        out_specs=pl.BlockSpec((tq, sa_w), lambda b, s, it, jt, lt: (b * nq + it[s], 0)),
        scratch_shapes=[pltpu.VMEM((SA_KV_HEADS, group * tq, SA_HEAD), MXU_DTYPE)]
        + [pltpu.VMEM((SA_KV_HEADS, group * tq, SA_HEAD), F32)] * 3,
    )
    return pl.pallas_call(
        functools.partial(_attn_kernel, group=group),
        grid_spec=grid_spec,
        out_shape=jax.ShapeDtypeStruct((m, sa_w), MXU_DTYPE),
        compiler_params=_cparams(2, 40 << 20),
        name="masked_flash_attention",
    )(it, jt, last, q, k_all, v_all, mask)


def _prepare_weights(w_in, mu_shift, w_decay_up, decay_bias, w_a_up, a_bias, w_gate_up, k_k, k_a, r_k,
                     lnx_w, lnx_b, w_out, w_up, conv_w, conv_b, w_down, d_model):
    rw = w_decay_up.shape[1]
    rw_cols = 3 * rw + D_DECAY + D_AAA + D_GATE
    sa_w = d_model - rw
    kv_w = SA_KV_HEADS * SA_HEAD
    qi_w = IDX_HEADS * IDX_DIM
    o = {}
    o["r"] = 0
    o["wd"] = rw
    o["k"] = rw + D_DECAY
    o["v"] = 2 * rw + D_DECAY
    o["ad"] = 3 * rw + D_DECAY
    o["gd"] = 3 * rw + D_DECAY + D_AAA
    o["q"] = rw_cols
    o["ksa"] = rw_cols + sa_w
    o["vsa"] = o["ksa"] + kv_w
    o["qi"] = o["vsa"] + kv_w
    o["kiw"] = o["qi"] + qi_w
    kiw_w = IDX_DIM + IDX_HEADS

    def seg(x, name, width, pad_to=None):
        s = x[..., o[name]:o[name] + width]
        if pad_to is not None and pad_to > width:
            s = jnp.pad(s, [(0, 0)] * (s.ndim - 1) + [(0, pad_to - width)])
        return s

    def rw_small(x):
        return jnp.concatenate([seg(x, "wd", D_DECAY, LANES), seg(x, "ad", D_AAA, LANES), seg(x, "gd", D_GATE)], -1)

    w_in_l = jnp.concatenate([
        seg(w_in, "r", rw), seg(w_in, "k", rw), seg(w_in, "v", rw), seg(w_in, "q", sa_w), seg(w_in, "qi", qi_w),
        seg(w_in, "ksa", kv_w), seg(w_in, "vsa", kv_w), rw_small(w_in), seg(w_in, "kiw", kiw_w, LANES)], axis=1)
    w_in_l = jnp.pad(w_in_l, ((0, 0), (0, -w_in_l.shape[1] % IN_PROJ_TN)))
    starts = {"r": 0, "k": rw, "v": 2 * rw, "q": 3 * rw, "qi": 3 * rw + sa_w}
    starts["ksa"] = starts["qi"] + qi_w
    starts["vsa"] = starts["ksa"] + kv_w
    starts["small"] = starts["vsa"] + kv_w
    starts["kiw"] = starts["small"] + 4 * LANES
    pad_rows = lambda x, n: jnp.pad(x, ((0, n - x.shape[0]), (0, 0)))
    row = lambda x: x.reshape(1, -1).astype(F32)
    mu = mu_shift.reshape(1, -1)
    w = {
        "w_in": _mx(w_in_l),
        "mu_r": seg(mu, "r", rw), "mu_k": seg(mu, "k", rw), "mu_v": seg(mu, "v", rw), "mu_small": rw_small(mu),
        "w_decay_up": _mx(pad_rows(w_decay_up, LANES)), "decay_bias": row(decay_bias),
        "w_a_up": _mx(pad_rows(w_a_up, LANES)), "a_bias": row(a_bias),
        "w_gate_up": _mx(w_gate_up),
        "k_k": row(k_k), "k_a": row(k_a), "r_k": row(r_k), "lnx_w": row(lnx_w), "lnx_b": row(lnx_b),
        "w_out": _mx(w_out), "w_up": _mx(w_up), "conv_w": conv_w, "conv_b": row(conv_b), "w_down": _mx(w_down),
    }
    dims = {"rw": rw, "rw_cols": rw_cols, "sa_w": sa_w, "kv_w": kv_w, "qi_w": qi_w, "orig": o, "starts": starts,
            "seg": seg, "rw_small": rw_small}
    return w, dims


def _layer(x, past_k, past_v, past_ki, wkv0, shift0, conv0, norms, w, dims, tiles):
    bsz, t, d = x.shape
    past = past_k.shape[1]
    m = bsz * t
    rw, sa_w, kv_w, qi_w = dims["rw"], dims["sa_w"], dims["kv_w"], dims["qi_w"]
    starts, seg, rw_small = dims["starts"], dims["seg"], dims["rw_small"]
    assert t % CHUNK == 0 and past % CHUNK == 0
    tm, tq, kb = tiles["tm"], tiles["tq"], tiles["kb"]
    g_mix_pre, g_mix_post, g_ffn_pre, g_ffn_post = norms

    x2 = x.reshape(m, d)
    h1 = _rmsnorm(x2, g_mix_pre, tiles["tm_norm"])
    pfull = _matmul(h1, w["w_in"], tiles["tm_in"], tiles["tn_in"])

    sh = shift0.reshape(bsz, 1, -1)
    shift_l = {"r": seg(sh, "r", rw), "k": seg(sh, "k", rw), "v": seg(sh, "v", rw), "small": rw_small(sh)}
    cols_rw = {n: starts[n] // RW_GROUP for n in ("r", "k", "v")}
    cols_rw["small"] = starts["small"] // RW_SMALL
    o_rw, wkv1 = _rwkv(pfull, cols_rw, shift_l, wkv0, w, bsz, t)

    pos = past + jnp.arange(t, dtype=jnp.int32)
    cols_sa = {"q": starts["q"] // sa_w, "ksa": starts["ksa"] // kv_w, "qi": starts["qi"] // qi_w,
               "kiw": starts["kiw"] // LANES}
    q_r, k_r, qi_r, kiw_r = _rope(pfull, cols_sa, _rope_tables(pos), bsz, t, tq, sa_w, kv_w, qi_w)
    k_new = k_r.reshape(bsz, t, kv_w)
    v_new = pfull[:, starts["vsa"]:starts["vsa"] + kv_w].reshape(bsz, t, kv_w)
    ki_new = kiw_r[:, :IDX_DIM].reshape(bsz, t, IDX_DIM)
    l_all = past + t
    lp = -(-l_all // kb) * kb
    nkb = lp // kb
    cat = lambda old, new: jnp.pad(
        jnp.concatenate([_mx(old.reshape(bsz, past, new.shape[-1])), _mx(new)], axis=1),
        ((0, 0), (0, lp - l_all), (0, 0)))
    k_all, v_all, ki_all = cat(past_k, k_new), cat(past_v, v_new), cat(past_ki, ki_new)
    nq = t // tq
    qit = (qi_r.reshape(bsz, nq, tq, IDX_HEADS, IDX_DIM).transpose(0, 1, 4, 3, 2)
           .reshape(bsz, nq, IDX_DIM, IDX_HEADS * tq))
    w_idx = kiw_r[:, IDX_DIM:IDX_DIM + IDX_HEADS].reshape(bsz, t, IDX_HEADS).transpose(0, 2, 1)
    topk = min(TOPK_MAX, l_all // 4)
    mask = _index_mask(qit, w_idx, ki_all.reshape(bsz, nkb, kb, IDX_DIM), bsz, t, tq, kb, past, topk)
    o_sa = _attention(q_r, k_all, v_all, mask, bsz, t, tq, kb, past)

    mix_in = jnp.concatenate([o_rw, o_sa], axis=1)
    x1 = _matmul_norm_residual(mix_in, w["w_out"], x2, g_mix_post, tm, tiles["tn_out"], "out_proj")
    h2 = _rmsnorm(x1, g_ffn_pre, tiles["tm_norm"])
    act, conv1 = _ffn_up(h2, w["w_up"], conv0, w["conv_w"], w["conv_b"], bsz, tiles["tm_up"], tiles["tn_up"])
    x_out = _matmul_norm_residual(act, w["w_down"], x1, g_ffn_post, tiles["tm_down"], tiles["tn_out"], "ffn_down")

    orig = dims["orig"]
    last = pfull.reshape(bsz, t, -1)[:, t - 1:t]
    sm = last[..., starts["small"]:starts["small"] + 4 * LANES]
    shift1 = jnp.concatenate([
        last[..., starts["r"]:starts["r"] + rw], sm[..., :D_DECAY],
        last[..., starts["k"]:starts["k"] + rw], last[..., starts["v"]:starts["v"] + rw],
        sm[..., LANES:LANES + D_AAA], sm[..., 2 * LANES:2 * LANES + D_GATE]], axis=-1)
    del orig
    new = (k_new.reshape(bsz, t, SA_KV_HEADS, SA_HEAD), v_new.reshape(bsz, t, SA_KV_HEADS, SA_HEAD), ki_new,
           wkv1, shift1, conv1)
    return x_out.reshape(bsz, t, d), new


def _tiles(bsz, t):
    big = t >= 1024
    return {
        "tm_norm": 256 if big else CHUNK,
        "tm_in": 1024 if big else bsz * t,
        "tn_in": IN_PROJ_TN,
        "tm": 512 if big else bsz * t,
        "tn_out": 512,
        "tm_down": min(256, bsz * t),
        "tm_up": 512 if big else bsz * t,
        "tn_up": 512,
        "tq": 256 if big else CHUNK,
        "kb": 512,
    }


def _run_stream(x, past_k, past_v, past_ki, wkv0, shift0, conv0, norm_w, weights, dims, depth):
    outs = []
    tiles = _tiles(x.shape[0], x.shape[1])
    for l in range(depth):
        w_l = {n: (v[l] if n != "conv_w" else v[l]) for n, v in weights.items()}
        norms = tuple(g[l].reshape(1, -1) for g in norm_w)
        x, st = _layer(x, past_k[l], past_v[l], past_ki[l], wkv0[l], shift0[l], conv0[l], norms, w_l, dims[l], tiles)
        outs.append(st)
    return x, [jnp.stack(s) for s in zip(*outs)]


def kernel(x_prompt, x_sample, cache_k, cache_v, cache_kidx, state_wkv, state_shift, state_conv, norm_mix_pre, norm_mix_post, norm_ffn_pre, norm_ffn_post, w_in, mu_shift, w_decay_up, decay_bias, w_a_up, a_bias, w_gate_up, k_k, k_a, r_k, lnx_w, lnx_b, w_out, w_up, conv_w, conv_b, w_down):
    depth, d_model = norm_mix_pre.shape
    per_layer = [_prepare_weights(w_in[l], mu_shift[l], w_decay_up[l], decay_bias[l], w_a_up[l], a_bias[l],
                                  w_gate_up[l], k_k[l], k_a[l], r_k[l], lnx_w[l], lnx_b[l], w_out[l], w_up[l],
                                  conv_w[l], conv_b[l], w_down[l], d_model) for l in range(depth)]
    weights = {n: [pw[0][n] for pw in per_layer] for n in per_layer[0][0]}
    dims = [pw[1] for pw in per_layer]
    norm_w = (norm_mix_pre, norm_mix_post, norm_ffn_pre, norm_ffn_post)

    dt = x_prompt.dtype
    bp = x_prompt.shape[0]
    rw_heads = state_wkv.shape[2]
    zk = jnp.zeros((depth, bp, 0, SA_KV_HEADS, SA_HEAD), dt)
    zki = jnp.zeros((depth, bp, 0, IDX_DIM), dt)
    zwkv = jnp.zeros((depth, bp, rw_heads, RW_HEAD, RW_HEAD), dt)
    zshift = jnp.zeros((depth, bp, 1, state_shift.shape[-1]), dt)
    zconv = jnp.zeros((depth, bp, CONV_W - 1, state_conv.shape[-1]), dt)
    y_prompt, p_new = _run_stream(x_prompt, zk, zk, zki, zwkv, zshift, zconv, norm_w, weights, dims, depth)
    y_sample, s_new = _run_stream(x_sample, cache_k, cache_v, cache_kidx, state_wkv, state_shift, state_conv,
                                  norm_w, weights, dims, depth)
    return (y_prompt, y_sample, *p_new, *s_new)
```

```python
import functools

import numpy as np
import jax
import jax.numpy as jnp
from jax import lax
from jax.experimental import pallas as pl
from jax.experimental.pallas import tpu as pltpu

F32 = jnp.float32
MXU_DTYPE = jnp.bfloat16
HI = lax.Precision.HIGHEST

CHUNK = 64
RW_HEAD = 64
D_DECAY = 96
D_AAA = 96
D_GATE = 256
SA_HEAD = 128
SA_KV_HEADS = 4
IDX_HEADS = 16
IDX_DIM = 64
TOPK_MAX = 256
CONV_W = 3
ROPE_THETA = 10000.0
NORM_EPS = 1e-6
LNX_EPS = 64e-5

LANES = 128
V7X_VMEM_LIMIT_BYTES = 60000 * 1024
INT_MIN = -(2 ** 31)
NEG_BIG = -1e30


def _cparams(n_grid, vmem_bytes):
    limit = int(min(V7X_VMEM_LIMIT_BYTES, max(32 * 1024 * 1024, vmem_bytes)))
    return pltpu.CompilerParams(dimension_semantics=("arbitrary",) * n_grid, vmem_limit_bytes=limit)


def _nt(a, b, precision=None):
    return lax.dot_general(a, b, (((1,), (1,)), ((), ())), precision=precision, preferred_element_type=F32)


def _tn(a, b, precision=None):
    return lax.dot_general(a, b, (((0,), (0,)), ((), ())), precision=precision, preferred_element_type=F32)


def _mx(x):
    return x.astype(MXU_DTYPE)


def _rmsnorm_kernel(x_ref, g_ref, o_ref):
    x = x_ref[...]
    y = x * lax.rsqrt(jnp.mean(x * x, axis=-1, keepdims=True) + NORM_EPS)
    o_ref[...] = (y * g_ref[...]).astype(o_ref.dtype)


def _rmsnorm(x, g, tm):
    m, d = x.shape
    return pl.pallas_call(
        _rmsnorm_kernel,
        grid=(m // tm,),
        in_specs=[pl.BlockSpec((tm, d), lambda i: (i, 0)), pl.BlockSpec((1, d), lambda i: (0, 0))],
        out_specs=pl.BlockSpec((tm, d), lambda i: (i, 0)),
        out_shape=jax.ShapeDtypeStruct((m, d), MXU_DTYPE),
        compiler_params=_cparams(1, 6 * tm * d * 4),
        name="rmsnorm",
    )(x, g)


def _mm_kernel(a_ref, b_ref, o_ref):
    o_ref[...] = jnp.dot(a_ref[...], b_ref[...], preferred_element_type=F32)


IN_PROJ_TN = 1024


def _matmul(a, b, tm, tn):
    m, k = a.shape
    n = b.shape[1]
    isz = jnp.dtype(MXU_DTYPE).itemsize
    vmem = 2 * (tm * k * isz + k * tn * isz + tm * tn * 4) + 2 * tm * tn * 4 + (8 << 20)
    return pl.pallas_call(
        _mm_kernel,
        grid=(m // tm, n // tn),
        in_specs=[pl.BlockSpec((tm, k), lambda i, j: (i, 0)), pl.BlockSpec((k, tn), lambda i, j: (0, j))],
        out_specs=pl.BlockSpec((tm, tn), lambda i, j: (i, j)),
        out_shape=jax.ShapeDtypeStruct((m, n), F32),
        compiler_params=_cparams(2, vmem),
        name="in_proj",
    )(a, b)


MM_NORM_ROWS = 128


def _mm_norm_kernel(a_ref, w_ref, x_ref, gpost_ref, x1_ref, *, nn):
    n = pl.program_id(1)
    tm, d = x1_ref.shape
    tn = w_ref.shape[1]
    x1_ref[:, pl.ds(pl.multiple_of(n * tn, LANES), tn)] = jnp.dot(a_ref[...], w_ref[...],
                                                                  preferred_element_type=F32)

    @pl.when(n == nn - 1)
    def _():
        step = min(tm, MM_NORM_ROWS)
        for r in range(0, tm, step):
            rows = slice(r, r + step)
            y = x1_ref[rows, :]
            y = y * lax.rsqrt(jnp.mean(y * y, axis=-1, keepdims=True) + NORM_EPS) * gpost_ref[...]
            x1_ref[rows, :] = x_ref[rows, :] + y


def _matmul_norm_residual(a, w, x, g_post, tm, tn, name):
    m, kdim = a.shape
    d = w.shape[1]
    nn = d // tn
    isz = jnp.dtype(MXU_DTYPE).itemsize
    fixed = 2 * kdim * tn * isz + tm * d * 4 + 2 * tm * tn * 4 + 6 * MM_NORM_ROWS * d * 4 + (4 << 20)
    per_copy = tm * kdim * isz + tm * d * 4
    row_bufs = 2 if fixed + 2 * per_copy <= V7X_VMEM_LIMIT_BYTES - (4 << 20) else 1
    vmem = fixed + row_bufs * per_copy
    return pl.pallas_call(
        functools.partial(_mm_norm_kernel, nn=nn),
        grid=(m // tm, nn),
        in_specs=[
            pl.BlockSpec((tm, kdim), lambda i, n: (i, 0), pipeline_mode=pl.Buffered(row_bufs)),
            pl.BlockSpec((kdim, tn), lambda i, n: (0, n)),
            pl.BlockSpec((tm, d), lambda i, n: (i, 0), pipeline_mode=pl.Buffered(1)),
            pl.BlockSpec((1, d), lambda i, n: (0, 0)),
        ],
        out_specs=pl.BlockSpec((tm, d), lambda i, n: (i, 0), pipeline_mode=pl.Buffered(row_bufs)),
        out_shape=jax.ShapeDtypeStruct((m, d), F32),
        compiler_params=_cparams(2, vmem),
        name=name,
    )(a, w, x, g_post)


def _gelu_tanh(x):
    return 0.5 * x * (1.0 + jnp.tanh(0.7978845608028654 * (x + 0.044715 * x * x * x)))


def _ffn_up_kernel(h_ref, wg_ref, wv_ref, c0g_ref, c0v_ref, cwg_ref, cwv_ref, cbg_ref, cbv_ref,
                   a_ref, cg_ref, cv_ref, carry_g, carry_v, *, blocks_per_stream, tm):
    i = pl.program_id(1) % blocks_per_stream
    n_str = carry_g.shape[0]
    ts = tm // n_str

    @pl.when(i == 0)
    def _():
        carry_g[...] = c0g_ref[...]
        carry_v[...] = c0v_ref[...]

    h = h_ref[...]
    row = lax.broadcasted_iota(jnp.int32, (tm, 1), 0)

    def conv(u, carry_ref, cw_ref, cb_ref, out_ref):
        u1 = pltpu.roll(u, 1, axis=0)
        u2 = pltpu.roll(u, 2, axis=0)
        for s in range(n_str):
            p = carry_ref[s]
            u1 = jnp.where(row == s * ts, p[1:2], u1)
            u2 = jnp.where(row == s * ts, p[0:1], jnp.where(row == s * ts + 1, p[1:2], u2))
            last = u[(s + 1) * ts - 2:(s + 1) * ts]
            carry_ref[s] = last
            out_ref[s] = last
        cw = cw_ref[...]
        return u2 * cw[0:1] + u1 * cw[1:2] + u * cw[2:3] + cb_ref[...]

    gate = conv(jnp.dot(h, wg_ref[...], preferred_element_type=F32), carry_g, cwg_ref, cbg_ref, cg_ref)
    val = conv(jnp.dot(h, wv_ref[...], preferred_element_type=F32), carry_v, cwv_ref, cbv_ref, cv_ref)
    a_ref[...] = (_gelu_tanh(gate) * val).astype(a_ref.dtype)


def _ffn_up(h, w_up, conv0, conv_w, conv_b, n_streams, tm, tn):
    m, d = h.shape
    f2 = w_up.shape[1]
    f = f2 // 2
    nj = f // tn
    nr = m // tm
    t = m // n_streams
    spb = max(1, tm // t)
    bps = max(1, t // tm)
    isz = jnp.dtype(MXU_DTYPE).itemsize
    vmem = 2 * (tm * d * isz + 2 * d * tn * isz + tm * tn * isz) + 10 * tm * tn * 4 + (4 << 20)
    col = lambda off: (lambda j, r: (0, off + j))
    st = lambda off: (lambda j, r: (r // bps, 0, off + j))
    a, cg, cv = pl.pallas_call(
        functools.partial(_ffn_up_kernel, blocks_per_stream=bps, tm=tm),
        grid=(nj, nr),
        in_specs=[
            pl.BlockSpec((tm, d), lambda j, r: (r, 0)),
            pl.BlockSpec((d, tn), col(0)),
            pl.BlockSpec((d, tn), col(nj)),
            pl.BlockSpec((spb, CONV_W - 1, tn), st(0)),
            pl.BlockSpec((spb, CONV_W - 1, tn), st(nj)),
            pl.BlockSpec((CONV_W, tn), col(0)),
            pl.BlockSpec((CONV_W, tn), col(nj)),
            pl.BlockSpec((1, tn), col(0)),
            pl.BlockSpec((1, tn), col(nj)),
        ],
        out_specs=[
            pl.BlockSpec((tm, tn), lambda j, r: (r, j)),
            pl.BlockSpec((spb, CONV_W - 1, tn), st(0)),
            pl.BlockSpec((spb, CONV_W - 1, tn), st(0)),
        ],
        out_shape=[
            jax.ShapeDtypeStruct((m, f), MXU_DTYPE),
            jax.ShapeDtypeStruct((n_streams, CONV_W - 1, f), F32),
            jax.ShapeDtypeStruct((n_streams, CONV_W - 1, f), F32),
        ],
        scratch_shapes=[pltpu.VMEM((spb, CONV_W - 1, tn), F32)] * 2,
        compiler_params=_cparams(2, vmem),
        name="ffn_up_conv",
    )(h, w_up, w_up, conv0, conv0, conv_w, conv_w, conv_b, conv_b)
    return a, jnp.concatenate([cg, cv], axis=-1)


RW_GROUP = 8 * LANES
RW_SMALL = 4 * LANES


def _softplus(z):
    return jnp.maximum(z, 0.0) + jnp.log(1.0 + jnp.exp(-jnp.abs(z)))


def _sigmoid(z):
    return 1.0 / (1.0 + jnp.exp(-z))


def _split(x):
    hi = x.astype(MXU_DTYPE)
    return hi, (x - hi.astype(F32)).astype(MXU_DTYPE)


def _dg(a, b, dims):
    return lax.dot_general(_mx(a), _mx(b), (dims, ((), ())), preferred_element_type=F32)


_NN = ((1,), (0,))
_NT = ((1,), (1,))
_TN = ((0,), (0,))


def _segsum(x, seg):
    hi, lo = _split(x)
    return jnp.dot(hi, seg, preferred_element_type=F32) + jnp.dot(lo, seg, preferred_element_type=F32)


def _each(f, *lists):
    return [f(*args) for args in zip(*lists)]


def _wkv_chunk(r, k, v, kk, b, cum, logd, s_bd, consts):
    m0, m1, tri_mask = consts
    c = r[0].shape[0]
    c2 = 2 * c
    stack = lambda x: jnp.concatenate([x * m0, x * m1], axis=0)
    g_end = _each(lambda cm: jnp.exp(cm[c - 1:c]), cum)
    e_inc = _each(jnp.exp, cum)
    e_prev = _each(lambda cm, ld: jnp.exp(cm - ld), cum, logd)
    e_neg = _each(lambda cm: jnp.exp(-cm), cum)
    e_tail = _each(lambda g, e: g * e, g_end, e_neg)
    vs = _each(stack, v)
    lhs = _each(lambda kk_, r_, ep, ei: jnp.concatenate([stack(kk_ * ep), stack(r_ * ei)], axis=0),
                kk, r, e_prev, e_inc)
    rhs = _each(lambda k_, b_, en: jnp.concatenate([stack(k_ * en), stack(b_ * en)], axis=0), k, b, e_neg)
    tails = _each(lambda k_, b_, et: jnp.concatenate([stack(k_ * et), stack(b_ * et)], axis=0), k, b, e_tail)
    amat = _each(lambda x, y: jnp.where(tri_mask, _dg(x, y, _NT), 0.0), lhs, rhs)
    sp = _each(lambda x, s: _dg(x, s, _NT), lhs, s_bd)
    av = _each(lambda a_, v_: _dg(a_[:, :c2], v_, _NN), amat, vs)
    pw = _each(lambda a_: -a_[:c2, c2:], amat)
    x = _each(lambda s_, a_: s_[:c2] + a_[:c2], sp, av)
    n_lvl = int(np.log2(c))
    for lvl in range(n_lvl - 1):
        z = _each(lambda p_, x_: _dg(p_, jnp.concatenate([p_, x_], axis=1), _NN), pw, x)
        pw = _each(lambda z_: z_[:, :c2], z)
        x = _each(lambda x_, z_: x_ + z_[:, c2:], x, z)
    u = _each(lambda p_, x_: x_ + _dg(p_, x_, _NN), pw, x)
    ys = _each(lambda s_, a_, am, u_: s_[c2:] + a_[c2:] - _dg(am[c2:, c2:], u_, _NN), sp, av, amat, u)
    y = _each(lambda y_: y_[:c] + y_[c:], ys)
    s_new = _each(lambda s, g, v_, u_, t_: s * g + _dg(jnp.concatenate([v_, -u_], axis=0), t_, _TN),
                  s_bd, g_end, vs, u, tails)
    return y, s_new


def _rwkv_kernel(pr_ref, pk_ref, pv_ref, ps_ref, sr_ref, sk_ref, sv_ref, ss_ref, wkv0_ref,
                 mur_ref, muk_ref, muv_ref, mus_ref, wdec_ref, dbias_ref, wa_ref, abias_ref, wg_ref,
                 kk_ref, ka_ref, rk_ref, lnw_ref, lnb_ref,
                 o_ref, wkv1_ref,
                 cr_ref, ck_ref, cv_ref, cs_ref, state_ref, *, n_chunks):
    c = pl.program_id(2)
    C = CHUNK
    npairs = RW_GROUP // LANES

    lane = lax.broadcasted_iota(jnp.int32, (1, LANES), 1)
    m0 = (lane < RW_HEAD).astype(F32)
    m1 = 1.0 - m0
    ri = lax.broadcasted_iota(jnp.int32, (4 * C, 4 * C), 0)
    ci = lax.broadcasted_iota(jnp.int32, (4 * C, 4 * C), 1)
    same_head = ((ri // C) % 2) == ((ci // C) % 2)
    tri_mask = same_head & ((ri % C) + ri // (2 * C) > (ci % C))
    li = lax.broadcasted_iota(jnp.int32, (LANES, LANES), 0)
    lj = lax.broadcasted_iota(jnp.int32, (LANES, LANES), 1)
    seg = ((li // RW_HEAD) == (lj // RW_HEAD)).astype(MXU_DTYPE)
    consts = (m0, m1, tri_mask)

    @pl.when(c == 0)
    def _():
        cr_ref[...] = sr_ref[...]
        ck_ref[...] = sk_ref[...]
        cv_ref[...] = sv_ref[...]
        cs_ref[...] = ss_ref[...]
        z = jnp.zeros((RW_HEAD, RW_HEAD), F32)
        for p in range(npairs):
            s0 = wkv0_ref[2 * p]
            s1 = wkv0_ref[2 * p + 1]
            state_ref[p] = jnp.concatenate(
                [jnp.concatenate([s0, z], axis=1), jnp.concatenate([z, s1], axis=1)], axis=0)

    row = lax.broadcasted_iota(jnp.int32, (C, 1), 0)

    def shifted(p_ref, carry_ref, mu_ref):
        p = p_ref[...]
        prev = jnp.where(row == 0, carry_ref[...], pltpu.roll(p, 1, axis=0))
        carry_ref[...] = p[C - 1:C]
        return p + (prev - p) * mu_ref[...]

    r = shifted(pr_ref, cr_ref, mur_ref)
    k = shifted(pk_ref, ck_ref, muk_ref)
    v = shifted(pv_ref, cv_ref, muv_ref)
    sm = shifted(ps_ref, cs_ref, mus_ref)
    wd, ad, gd = sm[:, 0:LANES], sm[:, LANES:2 * LANES], sm[:, 2 * LANES:4 * LANES]

    dec_in = dbias_ref[...] + jnp.dot(_mx(jnp.tanh(wd)), wdec_ref[...], preferred_element_type=F32)
    w_log = -_softplus(-dec_in) - 0.5
    logd = -jnp.exp(w_log)
    a = _sigmoid(abias_ref[...] + jnp.dot(_mx(ad), wa_ref[...], preferred_element_type=F32))
    g = jnp.dot(_mx(_sigmoid(gd)), wg_ref[...], preferred_element_type=F32)
    kk = k * kk_ref[...]
    k2 = k * (1.0 + (a - 1.0) * ka_ref[...])
    rkr = r * k2 * rk_ref[...]
    cum = logd
    for sft in (1, 2, 4, 8, 16, 32):
        cum = cum + jnp.where(row >= sft, pltpu.roll(cum, sft, axis=0), 0.0)

    pairs = lambda x: [x[:, p * LANES:(p + 1) * LANES] for p in range(npairs)]
    r_p, k_p, v_p, a_p, cum_p, logd_p = pairs(r), pairs(k2), pairs(v), pairs(a), pairs(cum), pairs(logd)
    sums = _each(lambda kk_, rkr_: _segsum(jnp.concatenate([kk_ * kk_, rkr_], axis=0), seg), pairs(kk), pairs(rkr))
    kk_p = _each(lambda kk_, s_: kk_ * lax.rsqrt(jnp.maximum(s_[:C], 1e-24)), pairs(kk), sums)
    b_p = _each(lambda kk_, a_: kk_ * a_, kk_p, a_p)
    y_p, s_new = _wkv_chunk(r_p, k_p, v_p, kk_p, b_p, cum_p, logd_p, [state_ref[p] for p in range(npairs)], consts)
    for p in range(npairs):
        state_ref[p] = s_new[p]
    yc = _each(lambda y_: y_ - _segsum(y_, seg) * (1.0 / RW_HEAD), y_p)
    var = _each(lambda yc_: _segsum(yc_ * yc_, seg) * (1.0 / RW_HEAD), yc)
    for p in range(npairs):
        sl = slice(p * LANES, (p + 1) * LANES)
        yn = yc[p] * lax.rsqrt(var[p] + LNX_EPS) * lnw_ref[:, sl] + lnb_ref[:, sl]
        o_ref[:, sl] = ((yn + sums[p][C:] * v_p[p]) * g[:, sl]).astype(o_ref.dtype)

    @pl.when(c == n_chunks - 1)
    def _():
        for p in range(npairs):
            s = state_ref[p]
            wkv1_ref[2 * p] = s[:RW_HEAD, :RW_HEAD]
            wkv1_ref[2 * p + 1] = s[RW_HEAD:, RW_HEAD:]


def _rwkv(pfull, cols, shift0, wkv0, w, n_streams, t):
    m = pfull.shape[0]
    rw = w["k_k"].shape[1]
    ng = rw // RW_GROUP
    nc = t // CHUNK
    hg = RW_GROUP // RW_HEAD
    rowblk = lambda off: (lambda b, g, c: (b * nc + c, off + g))
    fixed = lambda off: (lambda b, g, c: (b * nc + c, off))
    st = lambda b, g, c: (b, 0, g)
    st0 = lambda b, g, c: (b, 0, 0)
    wcol = lambda b, g, c: (0, g)
    in_specs = [
        pl.BlockSpec((CHUNK, RW_GROUP), rowblk(cols["r"])),
        pl.BlockSpec((CHUNK, RW_GROUP), rowblk(cols["k"])),
        pl.BlockSpec((CHUNK, RW_GROUP), rowblk(cols["v"])),
        pl.BlockSpec((CHUNK, RW_SMALL), fixed(cols["small"])),
        pl.BlockSpec((None, 1, RW_GROUP), st),
        pl.BlockSpec((None, 1, RW_GROUP), st),
        pl.BlockSpec((None, 1, RW_GROUP), st),
        pl.BlockSpec((None, 1, RW_SMALL), st0),
        pl.BlockSpec((None, hg, RW_HEAD, RW_HEAD), lambda b, g, c: (b, g, 0, 0)),
        pl.BlockSpec((1, RW_GROUP), wcol), pl.BlockSpec((1, RW_GROUP), wcol), pl.BlockSpec((1, RW_GROUP), wcol),
        pl.BlockSpec((1, RW_SMALL), lambda b, g, c: (0, 0)),
        pl.BlockSpec((LANES, RW_GROUP), wcol), pl.BlockSpec((1, RW_GROUP), wcol),
        pl.BlockSpec((LANES, RW_GROUP), wcol), pl.BlockSpec((1, RW_GROUP), wcol),
        pl.BlockSpec((2 * LANES, RW_GROUP), wcol),
        pl.BlockSpec((1, RW_GROUP), wcol), pl.BlockSpec((1, RW_GROUP), wcol), pl.BlockSpec((1, RW_GROUP), wcol),
        pl.BlockSpec((1, RW_GROUP), wcol), pl.BlockSpec((1, RW_GROUP), wcol),
    ]
    o, wkv1 = pl.pallas_call(
        functools.partial(_rwkv_kernel, n_chunks=nc),
        grid=(n_streams, ng, nc),
        in_specs=in_specs,
        out_specs=[
            pl.BlockSpec((CHUNK, RW_GROUP), lambda b, g, c: (b * nc + c, g)),
            pl.BlockSpec((None, hg, RW_HEAD, RW_HEAD), lambda b, g, c: (b, g, 0, 0)),
        ],
        out_shape=[
            jax.ShapeDtypeStruct((m, rw), MXU_DTYPE),
            jax.ShapeDtypeStruct((n_streams, rw // RW_HEAD, RW_HEAD, RW_HEAD), F32),
        ],
        scratch_shapes=[pltpu.VMEM((1, RW_GROUP), F32)] * 3 + [pltpu.VMEM((1, RW_SMALL), F32)]
        + [pltpu.VMEM((RW_GROUP // LANES, LANES, LANES), F32)],
        compiler_params=_cparams(3, 32 << 20),
        name="rwkv7_chunked",
    )(pfull, pfull, pfull, pfull, shift0["r"], shift0["k"], shift0["v"], shift0["small"], wkv0,
      w["mu_r"], w["mu_k"], w["mu_v"], w["mu_small"], w["w_decay_up"], w["decay_bias"], w["w_a_up"], w["a_bias"],
      w["w_gate_up"], w["k_k"], w["k_a"], w["r_k"], w["lnx_w"], w["lnx_b"])
    return o, wkv1


def _rope_kernel(q_ref, k_ref, qi_ref, kiw_ref, ca_ref, sa_ref, cb_ref, sb1_ref, sb2_ref,
                 qo_ref, ko_ref, qio_ref, kiwo_ref):
    ca, sa = ca_ref[...], sa_ref[...]
    cb, sb1, sb2 = cb_ref[...], sb1_ref[...], sb2_ref[...]

    def rope_head(x):
        return x * ca + pltpu.roll(x, SA_HEAD // 2, axis=1) * sa

    def rope_idx(x):
        return (x * cb + pltpu.roll(x, LANES - IDX_DIM // 2, axis=1) * sb1
                + pltpu.roll(x, IDX_DIM // 2, axis=1) * sb2)

    q_scale = float(SA_HEAD ** -0.5 * np.log2(np.e))
    for h in range(q_ref.shape[1] // LANES):
        sl = slice(h * LANES, (h + 1) * LANES)
        qo_ref[:, sl] = (rope_head(q_ref[:, sl]) * q_scale).astype(qo_ref.dtype)
    for h in range(k_ref.shape[1] // LANES):
        sl = slice(h * LANES, (h + 1) * LANES)
        ko_ref[:, sl] = rope_head(k_ref[:, sl])
    for hp in range(qi_ref.shape[1] // LANES):
        sl = slice(hp * LANES, (hp + 1) * LANES)
        qio_ref[:, sl] = rope_idx(qi_ref[:, sl]).astype(qio_ref.dtype)
    x = kiw_ref[...]
    lane = lax.broadcasted_iota(jnp.int32, (1, LANES), 1)
    kiwo_ref[...] = jnp.where(lane < IDX_DIM, rope_idx(x), x * float((IDX_HEADS * IDX_DIM) ** -0.5))


def _rope_tables(pos):
    def ang(half):
        inv = ROPE_THETA ** (-jnp.arange(half, dtype=F32) / half)
        return pos.astype(F32)[:, None] * inv[None, :]
    aa = ang(SA_HEAD // 2)
    ca = jnp.concatenate([jnp.cos(aa), jnp.cos(aa)], axis=1)
    sa = jnp.concatenate([-jnp.sin(aa), jnp.sin(aa)], axis=1)
    ab = ang(IDX_DIM // 2)
    z = jnp.zeros_like(ab)
    cb = jnp.concatenate([jnp.cos(ab)] * 4, axis=1)
    sb1 = jnp.concatenate([-jnp.sin(ab), z, -jnp.sin(ab), z], axis=1)
    sb2 = jnp.concatenate([z, jnp.sin(ab), z, jnp.sin(ab)], axis=1)
    return ca, sa, cb, sb1, sb2


def _rope(pfull, cols, tables, n_streams, t, tm, sa_w, kv_w, qi_w):
    m = pfull.shape[0]
    bps = t // tm
    tab = pl.BlockSpec((tm, LANES), lambda r: (r % bps, 0))
    return pl.pallas_call(
        _rope_kernel,
        grid=(m // tm,),
        in_specs=[
            pl.BlockSpec((tm, sa_w), lambda r: (r, cols["q"])),
            pl.BlockSpec((tm, kv_w), lambda r: (r, cols["ksa"])),
            pl.BlockSpec((tm, qi_w), lambda r: (r, cols["qi"])),
            pl.BlockSpec((tm, LANES), lambda r: (r, cols["kiw"])),
            tab, tab, tab, tab, tab,
        ],
        out_specs=[
            pl.BlockSpec((tm, sa_w), lambda r: (r, 0)),
            pl.BlockSpec((tm, kv_w), lambda r: (r, 0)),
            pl.BlockSpec((tm, qi_w), lambda r: (r, 0)),
            pl.BlockSpec((tm, LANES), lambda r: (r, 0)),
        ],
        out_shape=[
            jax.ShapeDtypeStruct((m, sa_w), MXU_DTYPE),
            jax.ShapeDtypeStruct((m, kv_w), F32),
            jax.ShapeDtypeStruct((m, qi_w), MXU_DTYPE),
            jax.ShapeDtypeStruct((m, LANES), F32),
        ],
        compiler_params=_cparams(1, 32 << 20),
        name="rope",
    )(pfull, pfull, pfull, pfull, *tables)


IDX_HEAD_GROUP = 4
IDX_VALUE_PASSES = 16


FOLD_CHAINS = 4


def _fold_sublane_tiles(x, op):
    parts = [x[r:r + 8] for r in range(0, x.shape[0], 8)]
    chains = [functools.reduce(op, parts[c::FOLD_CHAINS]) for c in range(min(FOLD_CHAINS, len(parts)))]
    return functools.reduce(op, chains)


def _f32_to_key(x):
    bits = lax.bitcast_convert_type(x, jnp.int32)
    return bits ^ ((bits >> 31) & jnp.int32(0x7FFFFFFF))


def _key_to_f32(key):
    return lax.bitcast_convert_type(key ^ ((key >> 31) & jnp.int32(0x7FFFFFFF)), F32)


def _index_kernel(qit_ref, w_ref, ki_ref, mask_ref, keys_ref, *, tq, kb, nkb, past, topk):
    i = pl.program_id(1)
    n_adm = jnp.minimum(nkb, (past + (i + 1) * tq + kb - 1) // kb)
    chunk_bits = CHUNK.bit_length() - 1
    qpos = past + i * tq + lax.broadcasted_iota(jnp.int32, (1, tq), 1)
    lim = ((qpos >> chunk_bits) + 1) << chunk_bits
    krow = lax.broadcasted_iota(jnp.int32, (kb, 1), 0)
    hg = IDX_HEAD_GROUP
    fold = _fold_sublane_tiles

    def score_block(j, carry):
        smax, smin = carry
        kblk = ki_ref[j]
        acc = jnp.zeros((kb, tq), F32)
        for g0 in range(0, IDX_HEADS, hg):
            s = jnp.dot(kblk, qit_ref[:, g0 * tq:(g0 + hg) * tq], preferred_element_type=F32)
            for g in range(hg):
                acc = acc + jnp.maximum(s[:, g * tq:(g + 1) * tq], 0.0) * w_ref[g0 + g:g0 + g + 1, :]
        score = acc + 0.0
        adm = j * kb + krow < lim
        keys_ref[j] = jnp.where(adm, _f32_to_key(score), jnp.int32(INT_MIN))
        smax = jnp.maximum(smax, fold(jnp.where(adm, score, -jnp.inf), jnp.maximum))
        smin = jnp.minimum(smin, fold(jnp.where(adm, score, jnp.inf), jnp.minimum))
        return smax, smin

    smax, smin = lax.fori_loop(0, n_adm, score_block,
                               (jnp.full((8, tq), -jnp.inf, F32), jnp.full((8, tq), jnp.inf, F32)))
    lo0 = _f32_to_key(jnp.min(smin, axis=0, keepdims=True))
    hi0 = _f32_to_key(jnp.max(smax, axis=0, keepdims=True)) + 1

    def count_ge(thr):
        def body(j, cnt):
            return cnt + fold(jnp.where(keys_ref[j] >= thr, 1.0, 0.0), jnp.add)
        return jnp.sum(lax.fori_loop(0, n_adm, body, jnp.zeros((8, tq), F32)), axis=0, keepdims=True)

    def midpoint(lo, hi):
        return (lo >> 1) + (hi >> 1) + (lo & hi & 1)

    def unfinished(lo, hi, cnt_lo):
        active = (midpoint(lo, hi) != lo) & (cnt_lo != float(topk))
        return jnp.max(jnp.where(active, 1.0, 0.0))

    def bisect(state):
        p, lo, hi, cnt_lo, cnt_hi, _ = state
        mid_v = _f32_to_key(0.5 * _key_to_f32(lo) + 0.5 * _key_to_f32(hi))
        by_value = (p < IDX_VALUE_PASSES) & (mid_v > lo) & (mid_v < hi)
        mid = jnp.where(by_value, mid_v, midpoint(lo, hi))
        cnt = count_ge(mid)
        ge = cnt >= float(topk)
        lo, cnt_lo = jnp.where(ge, mid, lo), jnp.where(ge, cnt, cnt_lo)
        hi, cnt_hi = jnp.where(ge, hi, mid), jnp.where(ge, cnt_hi, cnt)
        return p + 1, lo, hi, cnt_lo, cnt_hi, unfinished(lo, hi, cnt_lo)

    cnt_lo0 = lim.astype(F32)
    state = (jnp.int32(0), lo0, hi0, cnt_lo0, jnp.zeros((1, tq), F32), unfinished(lo0, hi0, cnt_lo0))
    _, thr, _, n_ge, n_gt, _ = lax.while_loop(lambda st: st[5] > 0.5, bisect, state)

    tied = n_ge > float(topk)
    need = float(topk) - n_gt
    end_all = jnp.int32(nkb * kb)

    def count_tied_below(bound):
        def body(j, cnt):
            hit = (keys_ref[j] == thr) & (j * kb + krow < bound)
            return cnt + fold(jnp.where(hit, 1.0, 0.0), jnp.add)
        return jnp.sum(lax.fori_loop(0, n_adm, body, jnp.zeros((8, tq), F32)), axis=0, keepdims=True)

    def tie_bisect(state):
        p, lo, hi = state
        mid = (lo + hi) >> 1
        enough = count_tied_below(mid) >= need
        return p + 1, jnp.where(enough, lo, mid), jnp.where(enough, mid, hi)

    any_tied = jnp.max(jnp.where(tied, 1.0, 0.0))
    n_steps = (nkb * kb).bit_length()
    tie_state = (jnp.int32(0), jnp.zeros((1, tq), jnp.int32), jnp.full((1, tq), end_all, jnp.int32))
    tie_hi = lax.while_loop(lambda st: (any_tied > 0.5) & (st[0] < n_steps), tie_bisect, tie_state)[2]
    tie_end = jnp.where(tied, tie_hi, end_all)

    def write(j, carry):
        k = keys_ref[j]
        keep = (k > thr) | ((k == thr) & (j * kb + krow < tie_end))
        mask_ref[j] = jnp.where(keep, 1.0, 0.0).T.astype(mask_ref.dtype)
        return carry

    lax.fori_loop(0, n_adm, write, 0)

    def clear(j, carry):
        mask_ref[j] = jnp.zeros((tq, kb), mask_ref.dtype)
        return carry

    lax.fori_loop(n_adm, nkb, clear, 0)


def _index_mask(qit, w, ki, n_streams, t, tq, kb, past, topk):
    nkb = ki.shape[1]
    nq = t // tq
    isz = jnp.dtype(MXU_DTYPE).itemsize
    vmem = (nkb * kb * tq * 4 + 2 * nkb * kb * tq + 2 * nkb * kb * LANES * isz + 2 * IDX_DIM * IDX_HEADS * tq * isz
            + 6 * kb * IDX_HEAD_GROUP * tq * 4 + (8 << 20))
    return pl.pallas_call(
        functools.partial(_index_kernel, tq=tq, kb=kb, nkb=nkb, past=past, topk=topk),
        grid=(n_streams, nq),
        in_specs=[
            pl.BlockSpec((None, None, IDX_DIM, IDX_HEADS * tq), lambda b, i: (b, i, 0, 0)),
            pl.BlockSpec((None, IDX_HEADS, tq), lambda b, i: (b, 0, i)),
            pl.BlockSpec((None, nkb, kb, IDX_DIM), lambda b, i: (b, 0, 0, 0)),
        ],
        out_specs=pl.BlockSpec((None, nkb, tq, kb), lambda b, i: (b, 0, i, 0)),
        out_shape=jax.ShapeDtypeStruct((n_streams, nkb, t, kb), jnp.int8),
        scratch_shapes=[pltpu.VMEM((nkb, kb, tq), jnp.int32)],
        compiler_params=_cparams(2, vmem),
        name="indexer_topk_mask",
    )(qit, w, ki)


def _attn_kernel(it_ref, jt_ref, last_ref, q_ref, k_ref, v_ref, m_ref, o_ref, qs_ref, mx_ref, l_ref, acc_ref,
                 *, group):
    step = pl.program_id(1)
    tq = q_ref.shape[0]
    tk = k_ref.shape[0]
    n_kv = k_ref.shape[1] // SA_HEAD
    rows = group * tq
    n_lane_tiles = tk // LANES

    @pl.when(jt_ref[step] == 0)
    def _():
        for n in range(n_kv):
            for g in range(group):
                h = n * group + g
                qs_ref[n, g * tq:(g + 1) * tq, :] = q_ref[:, h * SA_HEAD:(h + 1) * SA_HEAD]
        mx_ref[...] = jnp.full(mx_ref.shape, NEG_BIG, F32)
        l_ref[...] = jnp.zeros(l_ref.shape, F32)
        acc_ref[...] = jnp.zeros(acc_ref.shape, F32)

    bias = jnp.where(m_ref[...].astype(F32) > 0.0, 0.0, NEG_BIG)
    kv = range(n_kv)
    s = [_nt(qs_ref[n], k_ref[:, n * SA_HEAD:(n + 1) * SA_HEAD]) for n in kv]
    s = [(x.reshape(group, tq, tk) + bias[None]).reshape(rows, tk) for x in s]
    tiles = [[x[:, c * LANES:(c + 1) * LANES] for c in range(n_lane_tiles)] for x in s]
    m_old = [mx_ref[n] for n in kv]
    m_new = [jnp.maximum(mo, jnp.max(functools.reduce(jnp.maximum, t), axis=1, keepdims=True))
             for mo, t in zip(m_old, tiles)]
    alpha = [jnp.exp2(mo - mn) for mo, mn in zip(m_old, m_new)]
    p = [[jnp.exp2(x - mn) for x in t] for t, mn in zip(tiles, m_new)]
    for n in kv:
        mx_ref[n] = m_new[n]
        l_ref[n] = alpha[n] * l_ref[n] + functools.reduce(jnp.add, p[n])
    pv = [jnp.dot(_mx(jnp.concatenate(p[n], axis=1)), v_ref[:, n * SA_HEAD:(n + 1) * SA_HEAD],
                  preferred_element_type=F32) for n in kv]
    for n in kv:
        acc_ref[n] = alpha[n] * acc_ref[n] + pv[n]

    @pl.when(last_ref[step] == 1)
    def _():
        for n in range(n_kv):
            o = acc_ref[n] / jnp.sum(l_ref[n], axis=1, keepdims=True)
            for g in range(group):
                h = n * group + g
                o_ref[:, h * SA_HEAD:(h + 1) * SA_HEAD] = o[g * tq:(g + 1) * tq].astype(o_ref.dtype)


def _attn_tiles(t, tq, tk, nkb, past):
    ii, jj, last = [], [], []
    for i in range(t // tq):
        n_adm = min(nkb, -(-(past + (i + 1) * tq) // tk))
        for j in range(n_adm):
            ii.append(i)
            jj.append(j)
            last.append(int(j == n_adm - 1))
    return tuple(jnp.asarray(np.array(x, np.int32)) for x in (ii, jj, last))


def _attention(q, k_all, v_all, mask, n_streams, t, tq, tk, past):
    m, sa_w = q.shape
    kv_w = k_all.shape[2]
    nkb = mask.shape[1]
    nq = t // tq
    n_heads = sa_w // SA_HEAD
    group = n_heads // SA_KV_HEADS
    it, jt, last = _attn_tiles(t, tq, tk, nkb, past)
    grid_spec = pltpu.PrefetchScalarGridSpec(
        num_scalar_prefetch=3,
        grid=(n_streams, int(it.shape[0])),
        in_specs=[
            pl.BlockSpec((tq, sa_w), lambda b, s, it, jt, lt: (b * nq + it[s], 0)),
            pl.BlockSpec((None, tk, kv_w), lambda b, s, it, jt, lt: (b, jt[s], 0)),
            pl.BlockSpec((None, tk, kv_w), lambda b, s, it, jt, lt: (b, jt[s], 0)),
            pl.BlockSpec((None, None, tq, tk), lambda b, s, it, jt, lt: (b, jt[s], it[s], 0)),
        ],
        out_specs=pl.BlockSpec((tq, sa_w), lambda b, s, it, jt, lt: (b * nq + it[s], 0)),
        scratch_shapes=[pltpu.VMEM((SA_KV_HEADS, group * tq, SA_HEAD), MXU_DTYPE)]
        + [pltpu.VMEM((SA_KV_HEADS, group * tq, SA_HEAD), F32)] * 3,
    )
    return pl.pallas_call(
        functools.partial(_attn_kernel, group=group),
        grid_spec=grid_spec,
        out_shape=jax.ShapeDtypeStruct((m, sa_w), MXU_DTYPE),
        compiler_params=_cparams(2, 40 << 20),
        name="masked_flash_attention",
    )(it, jt, last, q, k_all, v_all, mask)


def _prepare_weights(w_in, mu_shift, w_decay_up, decay_bias, w_a_up, a_bias, w_gate_up, k_k, k_a, r_k,
                     lnx_w, lnx_b, w_out, w_up, conv_w, conv_b, w_down, d_model):
    rw = w_decay_up.shape[1]
    rw_cols = 3 * rw + D_DECAY + D_AAA + D_GATE
    sa_w = d_model - rw
    kv_w = SA_KV_HEADS * SA_HEAD
    qi_w = IDX_HEADS * IDX_DIM
    o = {}
    o["r"] = 0
    o["wd"] = rw
    o["k"] = rw + D_DECAY
    o["v"] = 2 * rw + D_DECAY
    o["ad"] = 3 * rw + D_DECAY
    o["gd"] = 3 * rw + D_DECAY + D_AAA
    o["q"] = rw_cols
    o["ksa"] = rw_cols + sa_w
    o["vsa"] = o["ksa"] + kv_w
    o["qi"] = o["vsa"] + kv_w
    o["kiw"] = o["qi"] + qi_w
    kiw_w = IDX_DIM + IDX_HEADS

    def seg(x, name, width, pad_to=None):
        s = x[..., o[name]:o[name] + width]
        if pad_to is not None and pad_to > width:
            s = jnp.pad(s, [(0, 0)] * (s.ndim - 1) + [(0, pad_to - width)])
        return s

    def rw_small(x):
        return jnp.concatenate([seg(x, "wd", D_DECAY, LANES), seg(x, "ad", D_AAA, LANES), seg(x, "gd", D_GATE)], -1)

    w_in_l = jnp.concatenate([
        seg(w_in, "r", rw), seg(w_in, "k", rw), seg(w_in, "v", rw), seg(w_in, "q", sa_w), seg(w_in, "qi", qi_w),
        seg(w_in, "ksa", kv_w), seg(w_in, "vsa", kv_w), rw_small(w_in), seg(w_in, "kiw", kiw_w, LANES)], axis=1)
    w_in_l = jnp.pad(w_in_l, ((0, 0), (0, -w_in_l.shape[1] % IN_PROJ_TN)))
    starts = {"r": 0, "k": rw, "v": 2 * rw, "q": 3 * rw, "qi": 3 * rw + sa_w}
    starts["ksa"] = starts["qi"] + qi_w
    starts["vsa"] = starts["ksa"] + kv_w
    starts["small"] = starts["vsa"] + kv_w
    starts["kiw"] = starts["small"] + 4 * LANES
    pad_rows = lambda x, n: jnp.pad(x, ((0, n - x.shape[0]), (0, 0)))
    row = lambda x: x.reshape(1, -1).astype(F32)
    mu = mu_shift.reshape(1, -1)
    w = {
        "w_in": _mx(w_in_l),
        "mu_r": seg(mu, "r", rw), "mu_k": seg(mu, "k", rw), "mu_v": seg(mu, "v", rw), "mu_small": rw_small(mu),
        "w_decay_up": _mx(pad_rows(w_decay_up, LANES)), "decay_bias": row(decay_bias),
        "w_a_up": _mx(pad_rows(w_a_up, LANES)), "a_bias": row(a_bias),
        "w_gate_up": _mx(w_gate_up),
        "k_k": row(k_k), "k_a": row(k_a), "r_k": row(r_k), "lnx_w": row(lnx_w), "lnx_b": row(lnx_b),
        "w_out": _mx(w_out), "w_up": _mx(w_up), "conv_w": conv_w, "conv_b": row(conv_b), "w_down": _mx(w_down),
    }
    dims = {"rw": rw, "rw_cols": rw_cols, "sa_w": sa_w, "kv_w": kv_w, "qi_w": qi_w, "orig": o, "starts": starts,
            "seg": seg, "rw_small": rw_small}
    return w, dims


def _layer(x, past_k, past_v, past_ki, wkv0, shift0, conv0, norms, w, dims, tiles):
    bsz, t, d = x.shape
    past = past_k.shape[1]
    m = bsz * t
    rw, sa_w, kv_w, qi_w = dims["rw"], dims["sa_w"], dims["kv_w"], dims["qi_w"]
    starts, seg, rw_small = dims["starts"], dims["seg"], dims["rw_small"]
    assert t % CHUNK == 0 and past % CHUNK == 0
    tm, tq, kb = tiles["tm"], tiles["tq"], tiles["kb"]
    g_mix_pre, g_mix_post, g_ffn_pre, g_ffn_post = norms

    x2 = x.reshape(m, d)
    h1 = _rmsnorm(x2, g_mix_pre, tiles["tm_norm"])
    pfull = _matmul(h1, w["w_in"], tiles["tm_in"], tiles["tn_in"])

    sh = shift0.reshape(bsz, 1, -1)
    shift_l = {"r": seg(sh, "r", rw), "k": seg(sh, "k", rw), "v": seg(sh, "v", rw), "small": rw_small(sh)}
    cols_rw = {n: starts[n] // RW_GROUP for n in ("r", "k", "v")}
    cols_rw["small"] = starts["small"] // RW_SMALL
    o_rw, wkv1 = _rwkv(pfull, cols_rw, shift_l, wkv0, w, bsz, t)

    pos = past + jnp.arange(t, dtype=jnp.int32)
    cols_sa = {"q": starts["q"] // sa_w, "ksa": starts["ksa"] // kv_w, "qi": starts["qi"] // qi_w,
               "kiw": starts["kiw"] // LANES}
    q_r, k_r, qi_r, kiw_r = _rope(pfull, cols_sa, _rope_tables(pos), bsz, t, tq, sa_w, kv_w, qi_w)
    k_new = k_r.reshape(bsz, t, kv_w)
    v_new = pfull[:, starts["vsa"]:starts["vsa"] + kv_w].reshape(bsz, t, kv_w)
    ki_new = kiw_r[:, :IDX_DIM].reshape(bsz, t, IDX_DIM)
    l_all = past + t
    lp = -(-l_all // kb) * kb
    nkb = lp // kb
    cat = lambda old, new: jnp.pad(
        jnp.concatenate([_mx(old.reshape(bsz, past, new.shape[-1])), _mx(new)], axis=1),
        ((0, 0), (0, lp - l_all), (0, 0)))
    k_all, v_all, ki_all = cat(past_k, k_new), cat(past_v, v_new), cat(past_ki, ki_new)
    nq = t // tq
    qit = (qi_r.reshape(bsz, nq, tq, IDX_HEADS, IDX_DIM).transpose(0, 1, 4, 3, 2)
           .reshape(bsz, nq, IDX_DIM, IDX_HEADS * tq))
    w_idx = kiw_r[:, IDX_DIM:IDX_DIM + IDX_HEADS].reshape(bsz, t, IDX_HEADS).transpose(0, 2, 1)
    topk = min(TOPK_MAX, l_all // 4)
    mask = _index_mask(qit, w_idx, ki_all.reshape(bsz, nkb, kb, IDX_DIM), bsz, t, tq, kb, past, topk)
    o_sa = _attention(q_r, k_all, v_all, mask, bsz, t, tq, kb, past)

    mix_in = jnp.concatenate([o_rw, o_sa], axis=1)
    x1 = _matmul_norm_residual(mix_in, w["w_out"], x2, g_mix_post, tm, tiles["tn_out"], "out_proj")
    h2 = _rmsnorm(x1, g_ffn_pre, tiles["tm_norm"])
    act, conv1 = _ffn_up(h2, w["w_up"], conv0, w["conv_w"], w["conv_b"], bsz, tiles["tm_up"], tiles["tn_up"])
    x_out = _matmul_norm_residual(act, w["w_down"], x1, g_ffn_post, tm, tiles["tn_out"], "ffn_down")

    orig = dims["orig"]
    last = pfull.reshape(bsz, t, -1)[:, t - 1:t]
    sm = last[..., starts["small"]:starts["small"] + 4 * LANES]
    shift1 = jnp.concatenate([
        last[..., starts["r"]:starts["r"] + rw], sm[..., :D_DECAY],
        last[..., starts["k"]:starts["k"] + rw], last[..., starts["v"]:starts["v"] + rw],
        sm[..., LANES:LANES + D_AAA], sm[..., 2 * LANES:2 * LANES + D_GATE]], axis=-1)
    del orig
    new = (k_new.reshape(bsz, t, SA_KV_HEADS, SA_HEAD), v_new.reshape(bsz, t, SA_KV_HEADS, SA_HEAD), ki_new,
           wkv1, shift1, conv1)
    return x_out.reshape(bsz, t, d), new


def _tiles(bsz, t):
    big = t >= 1024
    return {
        "tm_norm": 256 if big else CHUNK,
        "tm_in": 1024 if big else bsz * t,
        "tn_in": IN_PROJ_TN,
        "tm": 512 if big else bsz * t,
        "tn_out": 512,
        "tm_up": 512 if big else bsz * t,
        "tn_up": 512,
        "tq": 256 if big else CHUNK,
        "kb": 512,
    }


def _run_stream(x, past_k, past_v, past_ki, wkv0, shift0, conv0, norm_w, weights, dims, depth):
    outs = []
    tiles = _tiles(x.shape[0], x.shape[1])
    for l in range(depth):
        w_l = {n: (v[l] if n != "conv_w" else v[l]) for n, v in weights.items()}
        norms = tuple(g[l].reshape(1, -1) for g in norm_w)
        x, st = _layer(x, past_k[l], past_v[l], past_ki[l], wkv0[l], shift0[l], conv0[l], norms, w_l, dims[l], tiles)
        outs.append(st)
    return x, [jnp.stack(s) for s in zip(*outs)]


def kernel(x_prompt, x_sample, cache_k, cache_v, cache_kidx, state_wkv, state_shift, state_conv, norm_mix_pre, norm_mix_post, norm_ffn_pre, norm_ffn_post, w_in, mu_shift, w_decay_up, decay_bias, w_a_up, a_bias, w_gate_up, k_k, k_a, r_k, lnx_w, lnx_b, w_out, w_up, conv_w, conv_b, w_down):
    depth, d_model = norm_mix_pre.shape
    per_layer = [_prepare_weights(w_in[l], mu_shift[l], w_decay_up[l], decay_bias[l], w_a_up[l], a_bias[l],
                                  w_gate_up[l], k_k[l], k_a[l], r_k[l], lnx_w[l], lnx_b[l], w_out[l], w_up[l],
                                  conv_w[l], conv_b[l], w_down[l], d_model) for l in range(depth)]
    weights = {n: [pw[0][n] for pw in per_layer] for n in per_layer[0][0]}
    dims = [pw[1] for pw in per_layer]
    norm_w = (norm_mix_pre, norm_mix_post, norm_ffn_pre, norm_ffn_post)

    dt = x_prompt.dtype
    bp = x_prompt.shape[0]
    rw_heads = state_wkv.shape[2]
    zk = jnp.zeros((depth, bp, 0, SA_KV_HEADS, SA_HEAD), dt)
    zki = jnp.zeros((depth, bp, 0, IDX_DIM), dt)
    zwkv = jnp.zeros((depth, bp, rw_heads, RW_HEAD, RW_HEAD), dt)
    zshift = jnp.zeros((depth, bp, 1, state_shift.shape[-1]), dt)
    zconv = jnp.zeros((depth, bp, CONV_W - 1, state_conv.shape[-1]), dt)
    y_prompt, p_new = _run_stream(x_prompt, zk, zk, zki, zwkv, zshift, zconv, norm_w, weights, dims, depth)
    y_sample, s_new = _run_stream(x_sample, cache_k, cache_v, cache_kidx, state_wkv, state_shift, state_conv,
                                  norm_w, weights, dims, depth)
    return (y_prompt, y_sample, *p_new, *s_new)
```

```python
import functools

import numpy as np
import jax
import jax.numpy as jnp
from jax import lax
from jax.experimental import pallas as pl
from jax.experimental.pallas import tpu as pltpu

F32 = jnp.float32
MXU_DTYPE = jnp.bfloat16
HI = lax.Precision.HIGHEST

CHUNK = 64
RW_HEAD = 64
D_DECAY = 96
D_AAA = 96
D_GATE = 256
SA_HEAD = 128
SA_KV_HEADS = 4
IDX_HEADS = 16
IDX_DIM = 64
TOPK_MAX = 256
CONV_W = 3
ROPE_THETA = 10000.0
NORM_EPS = 1e-6
LNX_EPS = 64e-5

LANES = 128
V7X_VMEM_LIMIT_BYTES = 60000 * 1024
INT_MIN = -(2 ** 31)
NEG_BIG = -1e30


def _cparams(n_grid, vmem_bytes):
    limit = int(min(V7X_VMEM_LIMIT_BYTES, max(32 * 1024 * 1024, vmem_bytes)))
    return pltpu.CompilerParams(dimension_semantics=("arbitrary",) * n_grid, vmem_limit_bytes=limit)


def _nt(a, b, precision=None):
    return lax.dot_general(a, b, (((1,), (1,)), ((), ())), precision=precision, preferred_element_type=F32)


def _tn(a, b, precision=None):
    return lax.dot_general(a, b, (((0,), (0,)), ((), ())), precision=precision, preferred_element_type=F32)


def _mx(x):
    return x.astype(MXU_DTYPE)


def _rmsnorm_kernel(x_ref, g_ref, o_ref):
    x = x_ref[...]
    y = x * lax.rsqrt(jnp.mean(x * x, axis=-1, keepdims=True) + NORM_EPS)
    o_ref[...] = (y * g_ref[...]).astype(o_ref.dtype)


def _rmsnorm(x, g, tm):
    m, d = x.shape
    return pl.pallas_call(
        _rmsnorm_kernel,
        grid=(m // tm,),
        in_specs=[pl.BlockSpec((tm, d), lambda i: (i, 0)), pl.BlockSpec((1, d), lambda i: (0, 0))],
        out_specs=pl.BlockSpec((tm, d), lambda i: (i, 0)),
        out_shape=jax.ShapeDtypeStruct((m, d), MXU_DTYPE),
        compiler_params=_cparams(1, 6 * tm * d * 4),
        name="rmsnorm",
    )(x, g)


def _mm_kernel(a_ref, b_ref, o_ref):
    o_ref[...] = jnp.dot(a_ref[...], b_ref[...], preferred_element_type=F32)


IN_PROJ_TN = 1024


PROJ_TN = 512
UP_TN = 512


def _col_tiles(w, tn):
    k, n = w.shape
    return w.reshape(k, n // tn, tn).transpose(1, 0, 2)


def _matmul(a, b, tm):
    m, k = a.shape
    tn = b.shape[2]
    n = b.shape[0] * tn
    isz = jnp.dtype(MXU_DTYPE).itemsize
    vmem = 2 * (tm * k * isz + k * tn * isz + tm * tn * 4) + 2 * tm * tn * 4 + (8 << 20)
    return pl.pallas_call(
        _mm_kernel,
        grid=(m // tm, n // tn),
        in_specs=[pl.BlockSpec((tm, k), lambda i, j: (i, 0)), pl.BlockSpec((None, k, tn), lambda i, j: (j, 0, 0))],
        out_specs=pl.BlockSpec((tm, tn), lambda i, j: (i, j)),
        out_shape=jax.ShapeDtypeStruct((m, n), F32),
        compiler_params=_cparams(2, vmem),
        name="in_proj",
    )(a, b)


MM_NORM_ROWS = 128


def _mm_norm_kernel(a_ref, w_ref, x_ref, gpost_ref, x1_ref, *, nn):
    n = pl.program_id(1)
    tm, d = x1_ref.shape
    tn = w_ref.shape[1]
    x1_ref[:, pl.ds(pl.multiple_of(n * tn, LANES), tn)] = jnp.dot(a_ref[...], w_ref[...],
                                                                  preferred_element_type=F32)

    @pl.when(n == nn - 1)
    def _():
        step = min(tm, MM_NORM_ROWS)
        for r in range(0, tm, step):
            rows = slice(r, r + step)
            y = x1_ref[rows, :]
            y = y * lax.rsqrt(jnp.mean(y * y, axis=-1, keepdims=True) + NORM_EPS) * gpost_ref[...]
            x1_ref[rows, :] = x_ref[rows, :] + y


def _matmul_norm_residual(a, w, x, g_post, tm, name):
    m, kdim = a.shape
    nn, _, tn = w.shape
    d = nn * tn
    isz = jnp.dtype(MXU_DTYPE).itemsize
    fixed = 2 * kdim * tn * isz + tm * d * 4 + 2 * tm * tn * 4 + 6 * MM_NORM_ROWS * d * 4 + (4 << 20)
    per_copy = tm * kdim * isz + tm * d * 4
    row_bufs = 2 if fixed + 2 * per_copy <= V7X_VMEM_LIMIT_BYTES - (4 << 20) else 1
    vmem = fixed + row_bufs * per_copy
    return pl.pallas_call(
        functools.partial(_mm_norm_kernel, nn=nn),
        grid=(m // tm, nn),
        in_specs=[
            pl.BlockSpec((tm, kdim), lambda i, n: (i, 0), pipeline_mode=pl.Buffered(row_bufs)),
            pl.BlockSpec((None, kdim, tn), lambda i, n: (n, 0, 0)),
            pl.BlockSpec((tm, d), lambda i, n: (i, 0), pipeline_mode=pl.Buffered(1)),
            pl.BlockSpec((1, d), lambda i, n: (0, 0)),
        ],
        out_specs=pl.BlockSpec((tm, d), lambda i, n: (i, 0), pipeline_mode=pl.Buffered(row_bufs)),
        out_shape=jax.ShapeDtypeStruct((m, d), F32),
        compiler_params=_cparams(2, vmem),
        name=name,
    )(a, w, x, g_post)


def _gelu_tanh(x):
    return 0.5 * x * (1.0 + jnp.tanh(0.7978845608028654 * (x + 0.044715 * x * x * x)))


def _ffn_up_kernel(h_ref, wg_ref, wv_ref, c0g_ref, c0v_ref, cwg_ref, cwv_ref, cbg_ref, cbv_ref,
                   a_ref, cg_ref, cv_ref, carry_g, carry_v, *, blocks_per_stream, tm):
    i = pl.program_id(1) % blocks_per_stream
    n_str = carry_g.shape[0]
    ts = tm // n_str

    @pl.when(i == 0)
    def _():
        carry_g[...] = c0g_ref[...]
        carry_v[...] = c0v_ref[...]

    h = h_ref[...]
    row = lax.broadcasted_iota(jnp.int32, (tm, 1), 0)

    def conv(u, carry_ref, cw_ref, cb_ref, out_ref):
        u1 = pltpu.roll(u, 1, axis=0)
        u2 = pltpu.roll(u, 2, axis=0)
        for s in range(n_str):
            p = carry_ref[s]
            u1 = jnp.where(row == s * ts, p[1:2], u1)
            u2 = jnp.where(row == s * ts, p[0:1], jnp.where(row == s * ts + 1, p[1:2], u2))
            last = u[(s + 1) * ts - 2:(s + 1) * ts]
            carry_ref[s] = last
            out_ref[s] = last
        cw = cw_ref[...]
        return u2 * cw[0:1] + u1 * cw[1:2] + u * cw[2:3] + cb_ref[...]

    gate = conv(jnp.dot(h, wg_ref[...], preferred_element_type=F32), carry_g, cwg_ref, cbg_ref, cg_ref)
    val = conv(jnp.dot(h, wv_ref[...], preferred_element_type=F32), carry_v, cwv_ref, cbv_ref, cv_ref)
    a_ref[...] = (_gelu_tanh(gate) * val).astype(a_ref.dtype)


def _ffn_up(h, w_up, conv0, conv_w, conv_b, n_streams, tm):
    m, d = h.shape
    tn = w_up.shape[2]
    nj = w_up.shape[0] // 2
    f = nj * tn
    nr = m // tm
    t = m // n_streams
    spb = max(1, tm // t)
    bps = max(1, t // tm)
    isz = jnp.dtype(MXU_DTYPE).itemsize
    vmem = 2 * (tm * d * isz + 2 * d * tn * isz + tm * tn * isz) + 10 * tm * tn * 4 + (4 << 20)
    col = lambda off: (lambda j, r: (0, off + j))
    st = lambda off: (lambda j, r: (r // bps, 0, off + j))
    a, cg, cv = pl.pallas_call(
        functools.partial(_ffn_up_kernel, blocks_per_stream=bps, tm=tm),
        grid=(nj, nr),
        in_specs=[
            pl.BlockSpec((tm, d), lambda j, r: (r, 0)),
            pl.BlockSpec((None, d, tn), lambda j, r: (j, 0, 0)),
            pl.BlockSpec((None, d, tn), lambda j, r: (nj + j, 0, 0)),
            pl.BlockSpec((spb, CONV_W - 1, tn), st(0)),
            pl.BlockSpec((spb, CONV_W - 1, tn), st(nj)),
            pl.BlockSpec((CONV_W, tn), col(0)),
            pl.BlockSpec((CONV_W, tn), col(nj)),
            pl.BlockSpec((1, tn), col(0)),
            pl.BlockSpec((1, tn), col(nj)),
        ],
        out_specs=[
            pl.BlockSpec((tm, tn), lambda j, r: (r, j)),
            pl.BlockSpec((spb, CONV_W - 1, tn), st(0)),
            pl.BlockSpec((spb, CONV_W - 1, tn), st(0)),
        ],
        out_shape=[
            jax.ShapeDtypeStruct((m, f), MXU_DTYPE),
            jax.ShapeDtypeStruct((n_streams, CONV_W - 1, f), F32),
            jax.ShapeDtypeStruct((n_streams, CONV_W - 1, f), F32),
        ],
        scratch_shapes=[pltpu.VMEM((spb, CONV_W - 1, tn), F32)] * 2,
        compiler_params=_cparams(2, vmem),
        name="ffn_up_conv",
    )(h, w_up, w_up, conv0, conv0, conv_w, conv_w, conv_b, conv_b)
    return a, jnp.concatenate([cg, cv], axis=-1)


RW_GROUP = 8 * LANES
RW_SMALL = 4 * LANES


def _softplus(z):
    return jnp.maximum(z, 0.0) + jnp.log(1.0 + jnp.exp(-jnp.abs(z)))


def _sigmoid(z):
    return 1.0 / (1.0 + jnp.exp(-z))


def _split(x):
    hi = x.astype(MXU_DTYPE)
    return hi, (x - hi.astype(F32)).astype(MXU_DTYPE)


def _dg(a, b, dims):
    return lax.dot_general(_mx(a), _mx(b), (dims, ((), ())), preferred_element_type=F32)


_NN = ((1,), (0,))
_NT = ((1,), (1,))
_TN = ((0,), (0,))


def _segsum(x, seg):
    hi, lo = _split(x)
    return jnp.dot(hi, seg, preferred_element_type=F32) + jnp.dot(lo, seg, preferred_element_type=F32)


def _each(f, *lists):
    return [f(*args) for args in zip(*lists)]


def _wkv_chunk(r, k, v, kk, b, cum, logd, s_bd, consts):
    m0, m1, tri_mask = consts
    c = r[0].shape[0]
    c2 = 2 * c
    stack = lambda x: jnp.concatenate([x * m0, x * m1], axis=0)
    g_end = _each(lambda cm: jnp.exp(cm[c - 1:c]), cum)
    e_inc = _each(jnp.exp, cum)
    e_prev = _each(lambda cm, ld: jnp.exp(cm - ld), cum, logd)
    e_neg = _each(lambda cm: jnp.exp(-cm), cum)
    e_tail = _each(lambda g, e: g * e, g_end, e_neg)
    vs = _each(stack, v)
    lhs = _each(lambda kk_, r_, ep, ei: jnp.concatenate([stack(kk_ * ep), stack(r_ * ei)], axis=0),
                kk, r, e_prev, e_inc)
    rhs = _each(lambda k_, b_, en: jnp.concatenate([stack(k_ * en), stack(b_ * en)], axis=0), k, b, e_neg)
    tails = _each(lambda k_, b_, et: jnp.concatenate([stack(k_ * et), stack(b_ * et)], axis=0), k, b, e_tail)
    amat = _each(lambda x, y: jnp.where(tri_mask, _dg(x, y, _NT), 0.0), lhs, rhs)
    sp = _each(lambda x, s: _dg(x, s, _NT), lhs, s_bd)
    av = _each(lambda a_, v_: _dg(a_[:, :c2], v_, _NN), amat, vs)
    pw = _each(lambda a_: -a_[:c2, c2:], amat)
    x = _each(lambda s_, a_: s_[:c2] + a_[:c2], sp, av)
    n_lvl = int(np.log2(c))
    for lvl in range(n_lvl - 1):
        z = _each(lambda p_, x_: _dg(p_, jnp.concatenate([p_, x_], axis=1), _NN), pw, x)
        pw = _each(lambda z_: z_[:, :c2], z)
        x = _each(lambda x_, z_: x_ + z_[:, c2:], x, z)
    u = _each(lambda p_, x_: x_ + _dg(p_, x_, _NN), pw, x)
    ys = _each(lambda s_, a_, am, u_: s_[c2:] + a_[c2:] - _dg(am[c2:, c2:], u_, _NN), sp, av, amat, u)
    y = _each(lambda y_: y_[:c] + y_[c:], ys)
    s_new = _each(lambda s, g, v_, u_, t_: s * g + _dg(jnp.concatenate([v_, -u_], axis=0), t_, _TN),
                  s_bd, g_end, vs, u, tails)
    return y, s_new


def _rwkv_kernel(pr_ref, pk_ref, pv_ref, ps_ref, sr_ref, sk_ref, sv_ref, ss_ref, wkv0_ref,
                 mur_ref, muk_ref, muv_ref, mus_ref, wdec_ref, dbias_ref, wa_ref, abias_ref, wg_ref,
                 kk_ref, ka_ref, rk_ref, lnw_ref, lnb_ref,
                 o_ref, wkv1_ref,
                 cr_ref, ck_ref, cv_ref, cs_ref, state_ref, *, n_chunks):
    c = pl.program_id(2)
    C = CHUNK
    npairs = RW_GROUP // LANES

    lane = lax.broadcasted_iota(jnp.int32, (1, LANES), 1)
    m0 = (lane < RW_HEAD).astype(F32)
    m1 = 1.0 - m0
    ri = lax.broadcasted_iota(jnp.int32, (4 * C, 4 * C), 0)
    ci = lax.broadcasted_iota(jnp.int32, (4 * C, 4 * C), 1)
    same_head = ((ri // C) % 2) == ((ci // C) % 2)
    tri_mask = same_head & ((ri % C) + ri // (2 * C) > (ci % C))
    li = lax.broadcasted_iota(jnp.int32, (LANES, LANES), 0)
    lj = lax.broadcasted_iota(jnp.int32, (LANES, LANES), 1)
    seg = ((li // RW_HEAD) == (lj // RW_HEAD)).astype(MXU_DTYPE)
    consts = (m0, m1, tri_mask)

    @pl.when(c == 0)
    def _():
        cr_ref[...] = sr_ref[...]
        ck_ref[...] = sk_ref[...]
        cv_ref[...] = sv_ref[...]
        cs_ref[...] = ss_ref[...]
        z = jnp.zeros((RW_HEAD, RW_HEAD), F32)
        for p in range(npairs):
            s0 = wkv0_ref[2 * p]
            s1 = wkv0_ref[2 * p + 1]
            state_ref[p] = jnp.concatenate(
                [jnp.concatenate([s0, z], axis=1), jnp.concatenate([z, s1], axis=1)], axis=0)

    row = lax.broadcasted_iota(jnp.int32, (C, 1), 0)

    def shifted(p_ref, carry_ref, mu_ref):
        p = p_ref[...]
        prev = jnp.where(row == 0, carry_ref[...], pltpu.roll(p, 1, axis=0))
        carry_ref[...] = p[C - 1:C]
        return p + (prev - p) * mu_ref[...]

    r = shifted(pr_ref, cr_ref, mur_ref)
    k = shifted(pk_ref, ck_ref, muk_ref)
    v = shifted(pv_ref, cv_ref, muv_ref)
    sm = shifted(ps_ref, cs_ref, mus_ref)
    wd, ad, gd = sm[:, 0:LANES], sm[:, LANES:2 * LANES], sm[:, 2 * LANES:4 * LANES]

    dec_in = dbias_ref[...] + jnp.dot(_mx(jnp.tanh(wd)), wdec_ref[...], preferred_element_type=F32)
    w_log = -_softplus(-dec_in) - 0.5
    logd = -jnp.exp(w_log)
    a = _sigmoid(abias_ref[...] + jnp.dot(_mx(ad), wa_ref[...], preferred_element_type=F32))
    g = jnp.dot(_mx(_sigmoid(gd)), wg_ref[...], preferred_element_type=F32)
    kk = k * kk_ref[...]
    k2 = k * (1.0 + (a - 1.0) * ka_ref[...])
    rkr = r * k2 * rk_ref[...]
    cum = logd
    for sft in (1, 2, 4, 8, 16, 32):
        cum = cum + jnp.where(row >= sft, pltpu.roll(cum, sft, axis=0), 0.0)

    pairs = lambda x: [x[:, p * LANES:(p + 1) * LANES] for p in range(npairs)]
    r_p, k_p, v_p, a_p, cum_p, logd_p = pairs(r), pairs(k2), pairs(v), pairs(a), pairs(cum), pairs(logd)
    sums = _each(lambda kk_, rkr_: _segsum(jnp.concatenate([kk_ * kk_, rkr_], axis=0), seg), pairs(kk), pairs(rkr))
    kk_p = _each(lambda kk_, s_: kk_ * lax.rsqrt(jnp.maximum(s_[:C], 1e-24)), pairs(kk), sums)
    b_p = _each(lambda kk_, a_: kk_ * a_, kk_p, a_p)
    y_p, s_new = _wkv_chunk(r_p, k_p, v_p, kk_p, b_p, cum_p, logd_p, [state_ref[p] for p in range(npairs)], consts)
    for p in range(npairs):
        state_ref[p] = s_new[p]
    yc = _each(lambda y_: y_ - _segsum(y_, seg) * (1.0 / RW_HEAD), y_p)
    var = _each(lambda yc_: _segsum(yc_ * yc_, seg) * (1.0 / RW_HEAD), yc)
    for p in range(npairs):
        sl = slice(p * LANES, (p + 1) * LANES)
        yn = yc[p] * lax.rsqrt(var[p] + LNX_EPS) * lnw_ref[:, sl] + lnb_ref[:, sl]
        o_ref[:, sl] = ((yn + sums[p][C:] * v_p[p]) * g[:, sl]).astype(o_ref.dtype)

    @pl.when(c == n_chunks - 1)
    def _():
        for p in range(npairs):
            s = state_ref[p]
            wkv1_ref[2 * p] = s[:RW_HEAD, :RW_HEAD]
            wkv1_ref[2 * p + 1] = s[RW_HEAD:, RW_HEAD:]


def _rwkv(pfull, cols, shift0, wkv0, w, n_streams, t):
    m = pfull.shape[0]
    rw = w["k_k"].shape[1]
    ng = rw // RW_GROUP
    nc = t // CHUNK
    hg = RW_GROUP // RW_HEAD
    rowblk = lambda off: (lambda b, g, c: (b * nc + c, off + g))
    fixed = lambda off: (lambda b, g, c: (b * nc + c, off))
    st = lambda b, g, c: (b, 0, g)
    st0 = lambda b, g, c: (b, 0, 0)
    wcol = lambda b, g, c: (0, g)
    in_specs = [
        pl.BlockSpec((CHUNK, RW_GROUP), rowblk(cols["r"])),
        pl.BlockSpec((CHUNK, RW_GROUP), rowblk(cols["k"])),
        pl.BlockSpec((CHUNK, RW_GROUP), rowblk(cols["v"])),
        pl.BlockSpec((CHUNK, RW_SMALL), fixed(cols["small"])),
        pl.BlockSpec((None, 1, RW_GROUP), st),
        pl.BlockSpec((None, 1, RW_GROUP), st),
        pl.BlockSpec((None, 1, RW_GROUP), st),
        pl.BlockSpec((None, 1, RW_SMALL), st0),
        pl.BlockSpec((None, hg, RW_HEAD, RW_HEAD), lambda b, g, c: (b, g, 0, 0)),
        pl.BlockSpec((1, RW_GROUP), wcol), pl.BlockSpec((1, RW_GROUP), wcol), pl.BlockSpec((1, RW_GROUP), wcol),
        pl.BlockSpec((1, RW_SMALL), lambda b, g, c: (0, 0)),
        pl.BlockSpec((LANES, RW_GROUP), wcol), pl.BlockSpec((1, RW_GROUP), wcol),
        pl.BlockSpec((LANES, RW_GROUP), wcol), pl.BlockSpec((1, RW_GROUP), wcol),
        pl.BlockSpec((2 * LANES, RW_GROUP), wcol),
        pl.BlockSpec((1, RW_GROUP), wcol), pl.BlockSpec((1, RW_GROUP), wcol), pl.BlockSpec((1, RW_GROUP), wcol),
        pl.BlockSpec((1, RW_GROUP), wcol), pl.BlockSpec((1, RW_GROUP), wcol),
    ]
    o, wkv1 = pl.pallas_call(
        functools.partial(_rwkv_kernel, n_chunks=nc),
        grid=(n_streams, ng, nc),
        in_specs=in_specs,
        out_specs=[
            pl.BlockSpec((CHUNK, RW_GROUP), lambda b, g, c: (b * nc + c, g)),
            pl.BlockSpec((None, hg, RW_HEAD, RW_HEAD), lambda b, g, c: (b, g, 0, 0)),
        ],
        out_shape=[
            jax.ShapeDtypeStruct((m, rw), MXU_DTYPE),
            jax.ShapeDtypeStruct((n_streams, rw // RW_HEAD, RW_HEAD, RW_HEAD), F32),
        ],
        scratch_shapes=[pltpu.VMEM((1, RW_GROUP), F32)] * 3 + [pltpu.VMEM((1, RW_SMALL), F32)]
        + [pltpu.VMEM((RW_GROUP // LANES, LANES, LANES), F32)],
        compiler_params=_cparams(3, 32 << 20),
        name="rwkv7_chunked",
    )(pfull, pfull, pfull, pfull, shift0["r"], shift0["k"], shift0["v"], shift0["small"], wkv0,
      w["mu_r"], w["mu_k"], w["mu_v"], w["mu_small"], w["w_decay_up"], w["decay_bias"], w["w_a_up"], w["a_bias"],
      w["w_gate_up"], w["k_k"], w["k_a"], w["r_k"], w["lnx_w"], w["lnx_b"])
    return o, wkv1


def _rope_kernel(q_ref, k_ref, qi_ref, kiw_ref, ca_ref, sa_ref, cb_ref, sb1_ref, sb2_ref,
                 qo_ref, ko_ref, qio_ref, kiwo_ref):
    ca, sa = ca_ref[...], sa_ref[...]
    cb, sb1, sb2 = cb_ref[...], sb1_ref[...], sb2_ref[...]

    def rope_head(x):
        return x * ca + pltpu.roll(x, SA_HEAD // 2, axis=1) * sa

    def rope_idx(x):
        return (x * cb + pltpu.roll(x, LANES - IDX_DIM // 2, axis=1) * sb1
                + pltpu.roll(x, IDX_DIM // 2, axis=1) * sb2)

    q_scale = float(SA_HEAD ** -0.5 * np.log2(np.e))
    for h in range(q_ref.shape[1] // LANES):
        sl = slice(h * LANES, (h + 1) * LANES)
        qo_ref[:, sl] = (rope_head(q_ref[:, sl]) * q_scale).astype(qo_ref.dtype)
    for h in range(k_ref.shape[1] // LANES):
        sl = slice(h * LANES, (h + 1) * LANES)
        ko_ref[:, sl] = rope_head(k_ref[:, sl])
    for hp in range(qi_ref.shape[1] // LANES):
        sl = slice(hp * LANES, (hp + 1) * LANES)
        qio_ref[:, sl] = rope_idx(qi_ref[:, sl]).astype(qio_ref.dtype)
    x = kiw_ref[...]
    lane = lax.broadcasted_iota(jnp.int32, (1, LANES), 1)
    kiwo_ref[...] = jnp.where(lane < IDX_DIM, rope_idx(x), x * float((IDX_HEADS * IDX_DIM) ** -0.5))


def _rope_tables(pos):
    def ang(half):
        inv = ROPE_THETA ** (-jnp.arange(half, dtype=F32) / half)
        return pos.astype(F32)[:, None] * inv[None, :]
    aa = ang(SA_HEAD // 2)
    ca = jnp.concatenate([jnp.cos(aa), jnp.cos(aa)], axis=1)
    sa = jnp.concatenate([-jnp.sin(aa), jnp.sin(aa)], axis=1)
    ab = ang(IDX_DIM // 2)
    z = jnp.zeros_like(ab)
    cb = jnp.concatenate([jnp.cos(ab)] * 4, axis=1)
    sb1 = jnp.concatenate([-jnp.sin(ab), z, -jnp.sin(ab), z], axis=1)
    sb2 = jnp.concatenate([z, jnp.sin(ab), z, jnp.sin(ab)], axis=1)
    return ca, sa, cb, sb1, sb2


def _rope(pfull, cols, tables, n_streams, t, tm, sa_w, kv_w, qi_w):
    m = pfull.shape[0]
    bps = t // tm
    tab = pl.BlockSpec((tm, LANES), lambda r: (r % bps, 0))
    return pl.pallas_call(
        _rope_kernel,
        grid=(m // tm,),
        in_specs=[
            pl.BlockSpec((tm, sa_w), lambda r: (r, cols["q"])),
            pl.BlockSpec((tm, kv_w), lambda r: (r, cols["ksa"])),
            pl.BlockSpec((tm, qi_w), lambda r: (r, cols["qi"])),
            pl.BlockSpec((tm, LANES), lambda r: (r, cols["kiw"])),
            tab, tab, tab, tab, tab,
        ],
        out_specs=[
            pl.BlockSpec((tm, sa_w), lambda r: (r, 0)),
            pl.BlockSpec((tm, kv_w), lambda r: (r, 0)),
            pl.BlockSpec((tm, qi_w), lambda r: (r, 0)),
            pl.BlockSpec((tm, LANES), lambda r: (r, 0)),
        ],
        out_shape=[
            jax.ShapeDtypeStruct((m, sa_w), MXU_DTYPE),
            jax.ShapeDtypeStruct((m, kv_w), F32),
            jax.ShapeDtypeStruct((m, qi_w), MXU_DTYPE),
            jax.ShapeDtypeStruct((m, LANES), F32),
        ],
        compiler_params=_cparams(1, 32 << 20),
        name="rope",
    )(pfull, pfull, pfull, pfull, *tables)


IDX_HEAD_GROUP = 4
IDX_VALUE_PASSES = 16


FOLD_CHAINS = 4


def _fold_sublane_tiles(x, op):
    parts = [x[r:r + 8] for r in range(0, x.shape[0], 8)]
    chains = [functools.reduce(op, parts[c::FOLD_CHAINS]) for c in range(min(FOLD_CHAINS, len(parts)))]
    return functools.reduce(op, chains)


def _f32_to_key(x):
    bits = lax.bitcast_convert_type(x, jnp.int32)
    return bits ^ ((bits >> 31) & jnp.int32(0x7FFFFFFF))


def _key_to_f32(key):
    return lax.bitcast_convert_type(key ^ ((key >> 31) & jnp.int32(0x7FFFFFFF)), F32)


def _index_kernel(qit_ref, w_ref, ki_ref, mask_ref, keys_ref, *, tq, kb, nkb, past, topk):
    i = pl.program_id(1)
    n_adm = jnp.minimum(nkb, (past + (i + 1) * tq + kb - 1) // kb)
    chunk_bits = CHUNK.bit_length() - 1
    qpos = past + i * tq + lax.broadcasted_iota(jnp.int32, (1, tq), 1)
    lim = ((qpos >> chunk_bits) + 1) << chunk_bits
    krow = lax.broadcasted_iota(jnp.int32, (kb, 1), 0)
    hg = IDX_HEAD_GROUP
    fold = _fold_sublane_tiles

    def score_block(j, carry):
        smax, smin = carry
        kblk = ki_ref[j]
        acc = jnp.zeros((kb, tq), F32)
        for g0 in range(0, IDX_HEADS, hg):
            s = jnp.dot(kblk, qit_ref[:, g0 * tq:(g0 + hg) * tq], preferred_element_type=F32)
            for g in range(hg):
                acc = acc + jnp.maximum(s[:, g * tq:(g + 1) * tq], 0.0) * w_ref[g0 + g:g0 + g + 1, :]
        score = acc + 0.0
        adm = j * kb + krow < lim
        keys_ref[j] = jnp.where(adm, _f32_to_key(score), jnp.int32(INT_MIN))
        smax = jnp.maximum(smax, fold(jnp.where(adm, score, -jnp.inf), jnp.maximum))
        smin = jnp.minimum(smin, fold(jnp.where(adm, score, jnp.inf), jnp.minimum))
        return smax, smin

    smax, smin = lax.fori_loop(0, n_adm, score_block,
                               (jnp.full((8, tq), -jnp.inf, F32), jnp.full((8, tq), jnp.inf, F32)))
    lo0 = _f32_to_key(jnp.min(smin, axis=0, keepdims=True))
    hi0 = _f32_to_key(jnp.max(smax, axis=0, keepdims=True)) + 1

    def count_ge(thr):
        def body(j, cnt):
            return cnt + fold(jnp.where(keys_ref[j] >= thr, 1.0, 0.0), jnp.add)
        return jnp.sum(lax.fori_loop(0, n_adm, body, jnp.zeros((8, tq), F32)), axis=0, keepdims=True)

    def midpoint(lo, hi):
        return (lo >> 1) + (hi >> 1) + (lo & hi & 1)

    def unfinished(lo, hi, cnt_lo):
        active = (midpoint(lo, hi) != lo) & (cnt_lo != float(topk))
        return jnp.max(jnp.where(active, 1.0, 0.0))

    def bisect(state):
        p, lo, hi, cnt_lo, cnt_hi, _ = state
        mid_v = _f32_to_key(0.5 * _key_to_f32(lo) + 0.5 * _key_to_f32(hi))
        by_value = (p < IDX_VALUE_PASSES) & (mid_v > lo) & (mid_v < hi)
        mid = jnp.where(by_value, mid_v, midpoint(lo, hi))
        cnt = count_ge(mid)
        ge = cnt >= float(topk)
        lo, cnt_lo = jnp.where(ge, mid, lo), jnp.where(ge, cnt, cnt_lo)
        hi, cnt_hi = jnp.where(ge, hi, mid), jnp.where(ge, cnt_hi, cnt)
        return p + 1, lo, hi, cnt_lo, cnt_hi, unfinished(lo, hi, cnt_lo)

    cnt_lo0 = lim.astype(F32)
    state = (jnp.int32(0), lo0, hi0, cnt_lo0, jnp.zeros((1, tq), F32), unfinished(lo0, hi0, cnt_lo0))
    _, thr, _, n_ge, n_gt, _ = lax.while_loop(lambda st: st[5] > 0.5, bisect, state)

    tied = n_ge > float(topk)
    need = float(topk) - n_gt
    end_all = jnp.int32(nkb * kb)

    def count_tied_below(bound):
        def body(j, cnt):
            hit = (keys_ref[j] == thr) & (j * kb + krow < bound)
            return cnt + fold(jnp.where(hit, 1.0, 0.0), jnp.add)
        return jnp.sum(lax.fori_loop(0, n_adm, body, jnp.zeros((8, tq), F32)), axis=0, keepdims=True)

    def tie_bisect(state):
        p, lo, hi = state
        mid = (lo + hi) >> 1
        enough = count_tied_below(mid) >= need
        return p + 1, jnp.where(enough, lo, mid), jnp.where(enough, mid, hi)

    any_tied = jnp.max(jnp.where(tied, 1.0, 0.0))
    n_steps = (nkb * kb).bit_length()
    tie_state = (jnp.int32(0), jnp.zeros((1, tq), jnp.int32), jnp.full((1, tq), end_all, jnp.int32))
    tie_hi = lax.while_loop(lambda st: (any_tied > 0.5) & (st[0] < n_steps), tie_bisect, tie_state)[2]
    tie_end = jnp.where(tied, tie_hi, end_all)

    def write(j, carry):
        k = keys_ref[j]
        keep = (k > thr) | ((k == thr) & (j * kb + krow < tie_end))
        mask_ref[j] = jnp.where(keep, 1.0, 0.0).T.astype(mask_ref.dtype)
        return carry

    lax.fori_loop(0, n_adm, write, 0)

    def clear(j, carry):
        mask_ref[j] = jnp.zeros((tq, kb), mask_ref.dtype)
        return carry

    lax.fori_loop(n_adm, nkb, clear, 0)


def _index_mask(qit, w, ki, n_streams, t, tq, kb, past, topk):
    nkb = ki.shape[1]
    nq = t // tq
    isz = jnp.dtype(MXU_DTYPE).itemsize
    vmem = (nkb * kb * tq * 4 + 2 * nkb * kb * tq + 2 * nkb * kb * LANES * isz + 2 * IDX_DIM * IDX_HEADS * tq * isz
            + 6 * kb * IDX_HEAD_GROUP * tq * 4 + (8 << 20))
    return pl.pallas_call(
        functools.partial(_index_kernel, tq=tq, kb=kb, nkb=nkb, past=past, topk=topk),
        grid=(n_streams, nq),
        in_specs=[
            pl.BlockSpec((None, None, IDX_DIM, IDX_HEADS * tq), lambda b, i: (b, i, 0, 0)),
            pl.BlockSpec((None, IDX_HEADS, tq), lambda b, i: (b, 0, i)),
            pl.BlockSpec((None, nkb, kb, IDX_DIM), lambda b, i: (b, 0, 0, 0)),
        ],
        out_specs=pl.BlockSpec((None, nkb, tq, kb), lambda b, i: (b, 0, i, 0)),
        out_shape=jax.ShapeDtypeStruct((n_streams, nkb, t, kb), jnp.int8),
        scratch_shapes=[pltpu.VMEM((nkb, kb, tq), jnp.int32)],
        compiler_params=_cparams(2, vmem),
        name="indexer_topk_mask",
    )(qit, w, ki)


def _attn_kernel(it_ref, jt_ref, last_ref, q_ref, k_ref, v_ref, m_ref, o_ref, qs_ref, mx_ref, l_ref, acc_ref,
                 *, group):
    step = pl.program_id(1)
    tq = q_ref.shape[0]
    tk = k_ref.shape[0]
    n_kv = k_ref.shape[1] // SA_HEAD
    rows = group * tq
    n_lane_tiles = tk // LANES

    @pl.when(jt_ref[step] == 0)
    def _():
        for n in range(n_kv):
            for g in range(group):
                h = n * group + g
                qs_ref[n, g * tq:(g + 1) * tq, :] = q_ref[:, h * SA_HEAD:(h + 1) * SA_HEAD]
        mx_ref[...] = jnp.full(mx_ref.shape, NEG_BIG, F32)
        l_ref[...] = jnp.zeros(l_ref.shape, F32)
        acc_ref[...] = jnp.zeros(acc_ref.shape, F32)

    bias = jnp.where(m_ref[...].astype(F32) > 0.0, 0.0, NEG_BIG)
    kv = range(n_kv)
    s = [_nt(qs_ref[n], k_ref[:, n * SA_HEAD:(n + 1) * SA_HEAD]) for n in kv]
    s = [(x.reshape(group, tq, tk) + bias[None]).reshape(rows, tk) for x in s]
    tiles = [[x[:, c * LANES:(c + 1) * LANES] for c in range(n_lane_tiles)] for x in s]
    m_old = [mx_ref[n] for n in kv]
    m_new = [jnp.maximum(mo, jnp.max(functools.reduce(jnp.maximum, t), axis=1, keepdims=True))
             for mo, t in zip(m_old, tiles)]
    alpha = [jnp.exp2(mo - mn) for mo, mn in zip(m_old, m_new)]
    p = [[jnp.exp2(x - mn) for x in t] for t, mn in zip(tiles, m_new)]
    for n in kv:
        mx_ref[n] = m_new[n]
        l_ref[n] = alpha[n] * l_ref[n] + functools.reduce(jnp.add, p[n])
    pv = [jnp.dot(_mx(jnp.concatenate(p[n], axis=1)), v_ref[:, n * SA_HEAD:(n + 1) * SA_HEAD],
                  preferred_element_type=F32) for n in kv]
    for n in kv:
        acc_ref[n] = alpha[n] * acc_ref[n] + pv[n]

    @pl.when(last_ref[step] == 1)
    def _():
        for n in range(n_kv):
            o = acc_ref[n] / jnp.sum(l_ref[n], axis=1, keepdims=True)
            for g in range(group):
                h = n * group + g
                o_ref[:, h * SA_HEAD:(h + 1) * SA_HEAD] = o[g * tq:(g + 1) * tq].astype(o_ref.dtype)


def _attn_tiles(t, tq, tk, nkb, past):
    ii, jj, last = [], [], []
    for i in range(t // tq):
        n_adm = min(nkb, -(-(past + (i + 1) * tq) // tk))
        for j in range(n_adm):
            ii.append(i)
            jj.append(j)
            last.append(int(j == n_adm - 1))
    return tuple(jnp.asarray(np.array(x, np.int32)) for x in (ii, jj, last))


def _attention(q, k_all, v_all, mask, n_streams, t, tq, tk, past):
    m, sa_w = q.shape
    kv_w = k_all.shape[2]
    nkb = mask.shape[1]
    nq = t // tq
    n_heads = sa_w // SA_HEAD
    group = n_heads // SA_KV_HEADS
    it, jt, last = _attn_tiles(t, tq, tk, nkb, past)
    grid_spec = pltpu.PrefetchScalarGridSpec(
        num_scalar_prefetch=3,
        grid=(n_streams, int(it.shape[0])),
        in_specs=[
            pl.BlockSpec((tq, sa_w), lambda b, s, it, jt, lt: (b * nq + it[s], 0)),
            pl.BlockSpec((None, tk, kv_w), lambda b, s, it, jt, lt: (b, jt[s], 0)),
            pl.BlockSpec((None, tk, kv_w), lambda b, s, it, jt, lt: (b, jt[s], 0)),
            pl.BlockSpec((None, None, tq, tk), lambda b, s, it, jt, lt: (b, jt[s], it[s], 0)),
        ],
        out_specs=pl.BlockSpec((tq, sa_w), lambda b, s, it, jt, lt: (b * nq + it[s], 0)),
        scratch_shapes=[pltpu.VMEM((SA_KV_HEADS, group * tq, SA_HEAD), MXU_DTYPE)]
        + [pltpu.VMEM((SA_KV_HEADS, group * tq, SA_HEAD), F32)] * 3,
    )
    return pl.pallas_call(
        functools.partial(_attn_kernel, group=group),
        grid_spec=grid_spec,
        out_shape=jax.ShapeDtypeStruct((m, sa_w), MXU_DTYPE),
        compiler_params=_cparams(2, 40 << 20),
        name="masked_flash_attention",
    )(it, jt, last, q, k_all, v_all, mask)


def _prepare_weights(w_in, mu_shift, w_decay_up, decay_bias, w_a_up, a_bias, w_gate_up, k_k, k_a, r_k,
                     lnx_w, lnx_b, w_out, w_up, conv_w, conv_b, w_down, d_model):
    rw = w_decay_up.shape[1]
    rw_cols = 3 * rw + D_DECAY + D_AAA + D_GATE
    sa_w = d_model - rw
    kv_w = SA_KV_HEADS * SA_HEAD
    qi_w = IDX_HEADS * IDX_DIM
    o = {}
    o["r"] = 0
    o["wd"] = rw
    o["k"] = rw + D_DECAY
    o["v"] = 2 * rw + D_DECAY
    o["ad"] = 3 * rw + D_DECAY
    o["gd"] = 3 * rw + D_DECAY + D_AAA
    o["q"] = rw_cols
    o["ksa"] = rw_cols + sa_w
    o["vsa"] = o["ksa"] + kv_w
    o["qi"] = o["vsa"] + kv_w
    o["kiw"] = o["qi"] + qi_w
    kiw_w = IDX_DIM + IDX_HEADS

    def seg(x, name, width, pad_to=None):
        s = x[..., o[name]:o[name] + width]
        if pad_to is not None and pad_to > width:
            s = jnp.pad(s, [(0, 0)] * (s.ndim - 1) + [(0, pad_to - width)])
        return s

    def rw_small(x):
        return jnp.concatenate([seg(x, "wd", D_DECAY, LANES), seg(x, "ad", D_AAA, LANES), seg(x, "gd", D_GATE)], -1)

    w_in_l = jnp.concatenate([
        seg(w_in, "r", rw), seg(w_in, "k", rw), seg(w_in, "v", rw), seg(w_in, "q", sa_w), seg(w_in, "qi", qi_w),
        seg(w_in, "ksa", kv_w), seg(w_in, "vsa", kv_w), rw_small(w_in), seg(w_in, "kiw", kiw_w, LANES)], axis=1)
    w_in_l = jnp.pad(w_in_l, ((0, 0), (0, -w_in_l.shape[1] % IN_PROJ_TN)))
    starts = {"r": 0, "k": rw, "v": 2 * rw, "q": 3 * rw, "qi": 3 * rw + sa_w}
    starts["ksa"] = starts["qi"] + qi_w
    starts["vsa"] = starts["ksa"] + kv_w
    starts["small"] = starts["vsa"] + kv_w
    starts["kiw"] = starts["small"] + 4 * LANES
    pad_rows = lambda x, n: jnp.pad(x, ((0, n - x.shape[0]), (0, 0)))
    row = lambda x: x.reshape(1, -1).astype(F32)
    mu = mu_shift.reshape(1, -1)
    w = {
        "w_in": _col_tiles(_mx(w_in_l), IN_PROJ_TN),
        "mu_r": seg(mu, "r", rw), "mu_k": seg(mu, "k", rw), "mu_v": seg(mu, "v", rw), "mu_small": rw_small(mu),
        "w_decay_up": _mx(pad_rows(w_decay_up, LANES)), "decay_bias": row(decay_bias),
        "w_a_up": _mx(pad_rows(w_a_up, LANES)), "a_bias": row(a_bias),
        "w_gate_up": _mx(w_gate_up),
        "k_k": row(k_k), "k_a": row(k_a), "r_k": row(r_k), "lnx_w": row(lnx_w), "lnx_b": row(lnx_b),
        "w_out": _col_tiles(_mx(w_out), PROJ_TN), "w_up": _col_tiles(_mx(w_up), UP_TN),
        "conv_w": conv_w, "conv_b": row(conv_b), "w_down": _col_tiles(_mx(w_down), PROJ_TN),
    }
    dims = {"rw": rw, "rw_cols": rw_cols, "sa_w": sa_w, "kv_w": kv_w, "qi_w": qi_w, "orig": o, "starts": starts,
            "seg": seg, "rw_small": rw_small}
    return w, dims


def _layer(x, past_k, past_v, past_ki, wkv0, shift0, conv0, norms, w, dims, tiles):
    bsz, t, d = x.shape
    past = past_k.shape[1]
    m = bsz * t
    rw, sa_w, kv_w, qi_w = dims["rw"], dims["sa_w"], dims["kv_w"], dims["qi_w"]
    starts, seg, rw_small = dims["starts"], dims["seg"], dims["rw_small"]
    assert t % CHUNK == 0 and past % CHUNK == 0
    tm, tq, kb = tiles["tm"], tiles["tq"], tiles["kb"]
    g_mix_pre, g_mix_post, g_ffn_pre, g_ffn_post = norms

    x2 = x.reshape(m, d)
    h1 = _rmsnorm(x2, g_mix_pre, tiles["tm_norm"])
    pfull = _matmul(h1, w["w_in"], tiles["tm_in"])

    sh = shift0.reshape(bsz, 1, -1)
    shift_l = {"r": seg(sh, "r", rw), "k": seg(sh, "k", rw), "v": seg(sh, "v", rw), "small": rw_small(sh)}
    cols_rw = {n: starts[n] // RW_GROUP for n in ("r", "k", "v")}
    cols_rw["small"] = starts["small"] // RW_SMALL
    o_rw, wkv1 = _rwkv(pfull, cols_rw, shift_l, wkv0, w, bsz, t)

    pos = past + jnp.arange(t, dtype=jnp.int32)
    cols_sa = {"q": starts["q"] // sa_w, "ksa": starts["ksa"] // kv_w, "qi": starts["qi"] // qi_w,
               "kiw": starts["kiw"] // LANES}
    q_r, k_r, qi_r, kiw_r = _rope(pfull, cols_sa, _rope_tables(pos), bsz, t, tq, sa_w, kv_w, qi_w)
    k_new = k_r.reshape(bsz, t, kv_w)
    v_new = pfull[:, starts["vsa"]:starts["vsa"] + kv_w].reshape(bsz, t, kv_w)
    ki_new = kiw_r[:, :IDX_DIM].reshape(bsz, t, IDX_DIM)
    l_all = past + t
    lp = -(-l_all // kb) * kb
    nkb = lp // kb
    cat = lambda old, new: jnp.pad(
        jnp.concatenate([_mx(old.reshape(bsz, past, new.shape[-1])), _mx(new)], axis=1),
        ((0, 0), (0, lp - l_all), (0, 0)))
    k_all, v_all, ki_all = cat(past_k, k_new), cat(past_v, v_new), cat(past_ki, ki_new)
    nq = t // tq
    qit = (qi_r.reshape(bsz, nq, tq, IDX_HEADS, IDX_DIM).transpose(0, 1, 4, 3, 2)
           .reshape(bsz, nq, IDX_DIM, IDX_HEADS * tq))
    w_idx = kiw_r[:, IDX_DIM:IDX_DIM + IDX_HEADS].reshape(bsz, t, IDX_HEADS).transpose(0, 2, 1)
    topk = min(TOPK_MAX, l_all // 4)
    mask = _index_mask(qit, w_idx, ki_all.reshape(bsz, nkb, kb, IDX_DIM), bsz, t, tq, kb, past, topk)
    o_sa = _attention(q_r, k_all, v_all, mask, bsz, t, tq, kb, past)

    mix_in = jnp.concatenate([o_rw, o_sa], axis=1)
    x1 = _matmul_norm_residual(mix_in, w["w_out"], x2, g_mix_post, tm, "out_proj")
    h2 = _rmsnorm(x1, g_ffn_pre, tiles["tm_norm"])
    act, conv1 = _ffn_up(h2, w["w_up"], conv0, w["conv_w"], w["conv_b"], bsz, tiles["tm_up"])
    x_out = _matmul_norm_residual(act, w["w_down"], x1, g_ffn_post, tm, "ffn_down")

    orig = dims["orig"]
    last = pfull.reshape(bsz, t, -1)[:, t - 1:t]
    sm = last[..., starts["small"]:starts["small"] + 4 * LANES]
    shift1 = jnp.concatenate([
        last[..., starts["r"]:starts["r"] + rw], sm[..., :D_DECAY],
        last[..., starts["k"]:starts["k"] + rw], last[..., starts["v"]:starts["v"] + rw],
        sm[..., LANES:LANES + D_AAA], sm[..., 2 * LANES:2 * LANES + D_GATE]], axis=-1)
    del orig
    new = (k_new.reshape(bsz, t, SA_KV_HEADS, SA_HEAD), v_new.reshape(bsz, t, SA_KV_HEADS, SA_HEAD), ki_new,
           wkv1, shift1, conv1)
    return x_out.reshape(bsz, t, d), new


def _tiles(bsz, t):
    big = t >= 1024
    return {
        "tm_norm": 256 if big else CHUNK,
        "tm_in": 1024 if big else bsz * t,
        "tm": 512 if big else bsz * t,
        "tm_up": 512 if big else bsz * t,
        "tq": 256 if big else CHUNK,
        "kb": 512,
    }


def _run_stream(x, past_k, past_v, past_ki, wkv0, shift0, conv0, norm_w, weights, dims, depth):
    outs = []
    tiles = _tiles(x.shape[0], x.shape[1])
    for l in range(depth):
        w_l = {n: v[l] for n, v in weights.items()}
        norms = tuple(g[l].reshape(1, -1) for g in norm_w)
        x, st = _layer(x, past_k[l], past_v[l], past_ki[l], wkv0[l], shift0[l], conv0[l], norms, w_l, dims[l], tiles)
        outs.append(st)
    return x, [jnp.stack(s) for s in zip(*outs)]


def kernel(x_prompt, x_sample, cache_k, cache_v, cache_kidx, state_wkv, state_shift, state_conv, norm_mix_pre, norm_mix_post, norm_ffn_pre, norm_ffn_post, w_in, mu_shift, w_decay_up, decay_bias, w_a_up, a_bias, w_gate_up, k_k, k_a, r_k, lnx_w, lnx_b, w_out, w_up, conv_w, conv_b, w_down):
    depth, d_model = norm_mix_pre.shape
    per_layer = [_prepare_weights(w_in[l], mu_shift[l], w_decay_up[l], decay_bias[l], w_a_up[l], a_bias[l],
                                  w_gate_up[l], k_k[l], k_a[l], r_k[l], lnx_w[l], lnx_b[l], w_out[l], w_up[l],
                                  conv_w[l], conv_b[l], w_down[l], d_model) for l in range(depth)]
    weights = {n: [pw[0][n] for pw in per_layer] for n in per_layer[0][0]}
    dims = [pw[1] for pw in per_layer]
    norm_w = (norm_mix_pre, norm_mix_post, norm_ffn_pre, norm_ffn_post)

    dt = x_prompt.dtype
    bp = x_prompt.shape[0]
    rw_heads = state_wkv.shape[2]
    zk = jnp.zeros((depth, bp, 0, SA_KV_HEADS, SA_HEAD), dt)
    zki = jnp.zeros((depth, bp, 0, IDX_DIM), dt)
    zwkv = jnp.zeros((depth, bp, rw_heads, RW_HEAD, RW_HEAD), dt)
    zshift = jnp.zeros((depth, bp, 1, state_shift.shape[-1]), dt)
    zconv = jnp.zeros((depth, bp, CONV_W - 1, state_conv.shape[-1]), dt)
    y_prompt, p_new = _run_stream(x_prompt, zk, zk, zki, zwkv, zshift, zconv, norm_w, weights, dims, depth)
    y_sample, s_new = _run_stream(x_sample, cache_k, cache_v, cache_kidx, state_wkv, state_shift, state_conv,
                                  norm_w, weights, dims, depth)
    return (y_prompt, y_sample, *p_new, *s_new)
```

```python
import functools

import numpy as np
import jax
import jax.numpy as jnp
from jax import lax
from jax.experimental import pallas as pl
from jax.experimental.pallas import tpu as pltpu

F32 = jnp.float32
MXU_DTYPE = jnp.bfloat16
HI = lax.Precision.HIGHEST

CHUNK = 64
RW_HEAD = 64
D_DECAY = 96
D_AAA = 96
D_GATE = 256
SA_HEAD = 128
SA_KV_HEADS = 4
IDX_HEADS = 16
IDX_DIM = 64
TOPK_MAX = 256
CONV_W = 3
ROPE_THETA = 10000.0
NORM_EPS = 1e-6
LNX_EPS = 64e-5

LANES = 128
V7X_VMEM_LIMIT_BYTES = 60000 * 1024
INT_MIN = -(2 ** 31)
NEG_BIG = -1e30


def _cparams(n_grid, vmem_bytes):
    limit = int(min(V7X_VMEM_LIMIT_BYTES, max(32 * 1024 * 1024, vmem_bytes)))
    return pltpu.CompilerParams(dimension_semantics=("arbitrary",) * n_grid, vmem_limit_bytes=limit)


def _nt(a, b, precision=None):
    return lax.dot_general(a, b, (((1,), (1,)), ((), ())), precision=precision, preferred_element_type=F32)


def _tn(a, b, precision=None):
    return lax.dot_general(a, b, (((0,), (0,)), ((), ())), precision=precision, preferred_element_type=F32)


def _mx(x):
    return x.astype(MXU_DTYPE)


def _rmsnorm_kernel(x_ref, g_ref, o_ref):
    x = x_ref[...]
    y = x * lax.rsqrt(jnp.mean(x * x, axis=-1, keepdims=True) + NORM_EPS)
    o_ref[...] = (y * g_ref[...]).astype(o_ref.dtype)


def _rmsnorm(x, g, tm):
    m, d = x.shape
    return pl.pallas_call(
        _rmsnorm_kernel,
        grid=(m // tm,),
        in_specs=[pl.BlockSpec((tm, d), lambda i: (i, 0)), pl.BlockSpec((1, d), lambda i: (0, 0))],
        out_specs=pl.BlockSpec((tm, d), lambda i: (i, 0)),
        out_shape=jax.ShapeDtypeStruct((m, d), MXU_DTYPE),
        compiler_params=_cparams(1, 6 * tm * d * 4),
        name="rmsnorm",
    )(x, g)


def _mm_kernel(a_ref, b_ref, o_ref):
    o_ref[...] = jnp.dot(a_ref[...], b_ref[...], preferred_element_type=F32)


IN_PROJ_TN = 1024


def _matmul(a, b, tm, tn):
    m, k = a.shape
    n = b.shape[1]
    isz = jnp.dtype(MXU_DTYPE).itemsize
    vmem = 2 * (tm * k * isz + k * tn * isz + tm * tn * 4) + 2 * tm * tn * 4 + (8 << 20)
    return pl.pallas_call(
        _mm_kernel,
        grid=(m // tm, n // tn),
        in_specs=[pl.BlockSpec((tm, k), lambda i, j: (i, 0)), pl.BlockSpec((k, tn), lambda i, j: (0, j))],
        out_specs=pl.BlockSpec((tm, tn), lambda i, j: (i, j)),
        out_shape=jax.ShapeDtypeStruct((m, n), F32),
        compiler_params=_cparams(2, vmem),
        name="in_proj",
    )(a, b)


MM_NORM_ROWS = 128


def _mm_norm_kernel(a_ref, w_ref, x_ref, gpost_ref, x1_ref, *, nn):
    n = pl.program_id(1)
    tm, d = x1_ref.shape
    tn = w_ref.shape[1]
    x1_ref[:, pl.ds(pl.multiple_of(n * tn, LANES), tn)] = jnp.dot(a_ref[...], w_ref[...],
                                                                  preferred_element_type=F32)

    @pl.when(n == nn - 1)
    def _():
        step = min(tm, MM_NORM_ROWS)
        for r in range(0, tm, step):
            rows = slice(r, r + step)
            y = x1_ref[rows, :]
            y = y * lax.rsqrt(jnp.mean(y * y, axis=-1, keepdims=True) + NORM_EPS) * gpost_ref[...]
            x1_ref[rows, :] = x_ref[rows, :] + y


def _matmul_norm_residual(a, w, x, g_post, tm, tn, name):
    m, kdim = a.shape
    d = w.shape[1]
    nn = d // tn
    isz = jnp.dtype(MXU_DTYPE).itemsize
    fixed = 2 * kdim * tn * isz + tm * d * 4 + 2 * tm * tn * 4 + 6 * MM_NORM_ROWS * d * 4 + (4 << 20)
    per_copy = tm * kdim * isz + tm * d * 4
    row_bufs = 2 if fixed + 2 * per_copy <= V7X_VMEM_LIMIT_BYTES - (4 << 20) else 1
    vmem = fixed + row_bufs * per_copy
    return pl.pallas_call(
        functools.partial(_mm_norm_kernel, nn=nn),
        grid=(m // tm, nn),
        in_specs=[
            pl.BlockSpec((tm, kdim), lambda i, n: (i, 0), pipeline_mode=pl.Buffered(row_bufs)),
            pl.BlockSpec((kdim, tn), lambda i, n: (0, n)),
            pl.BlockSpec((tm, d), lambda i, n: (i, 0), pipeline_mode=pl.Buffered(1)),
            pl.BlockSpec((1, d), lambda i, n: (0, 0)),
        ],
        out_specs=pl.BlockSpec((tm, d), lambda i, n: (i, 0), pipeline_mode=pl.Buffered(row_bufs)),
        out_shape=jax.ShapeDtypeStruct((m, d), F32),
        compiler_params=_cparams(2, vmem),
        name=name,
    )(a, w, x, g_post)


def _gelu_tanh(x):
    return 0.5 * x * (1.0 + jnp.tanh(0.7978845608028654 * (x + 0.044715 * x * x * x)))


def _ffn_up_kernel(h_ref, wg_ref, wv_ref, c0g_ref, c0v_ref, cwg_ref, cwv_ref, cbg_ref, cbv_ref,
                   a_ref, cg_ref, cv_ref, carry_g, carry_v, *, blocks_per_stream, tm):
    i = pl.program_id(1) % blocks_per_stream
    n_str = carry_g.shape[0]
    ts = tm // n_str

    @pl.when(i == 0)
    def _():
        carry_g[...] = c0g_ref[...]
        carry_v[...] = c0v_ref[...]

    h = h_ref[...]
    row = lax.broadcasted_iota(jnp.int32, (tm, 1), 0)

    def conv(u, carry_ref, cw_ref, cb_ref, out_ref):
        u1 = pltpu.roll(u, 1, axis=0)
        u2 = pltpu.roll(u, 2, axis=0)
        for s in range(n_str):
            p = carry_ref[s]
            u1 = jnp.where(row == s * ts, p[1:2], u1)
            u2 = jnp.where(row == s * ts, p[0:1], jnp.where(row == s * ts + 1, p[1:2], u2))
            last = u[(s + 1) * ts - 2:(s + 1) * ts]
            carry_ref[s] = last
            out_ref[s] = last
        cw = cw_ref[...]
        return u2 * cw[0:1] + u1 * cw[1:2] + u * cw[2:3] + cb_ref[...]

    gate = conv(jnp.dot(h, wg_ref[...], preferred_element_type=F32), carry_g, cwg_ref, cbg_ref, cg_ref)
    val = conv(jnp.dot(h, wv_ref[...], preferred_element_type=F32), carry_v, cwv_ref, cbv_ref, cv_ref)
    a_ref[...] = (_gelu_tanh(gate) * val).astype(a_ref.dtype)


def _ffn_up(h, w_up, conv0, conv_w, conv_b, n_streams, tm, tn):
    m, d = h.shape
    f2 = w_up.shape[1]
    f = f2 // 2
    nj = f // tn
    nr = m // tm
    t = m // n_streams
    spb = max(1, tm // t)
    bps = max(1, t // tm)
    isz = jnp.dtype(MXU_DTYPE).itemsize
    vmem = 2 * (tm * d * isz + 2 * d * tn * isz + tm * tn * isz) + 10 * tm * tn * 4 + (4 << 20)
    col = lambda off: (lambda j, r: (0, off + j))
    st = lambda off: (lambda j, r: (r // bps, 0, off + j))
    a, cg, cv = pl.pallas_call(
        functools.partial(_ffn_up_kernel, blocks_per_stream=bps, tm=tm),
        grid=(nj, nr),
        in_specs=[
            pl.BlockSpec((tm, d), lambda j, r: (r, 0)),
            pl.BlockSpec((d, tn), col(0)),
            pl.BlockSpec((d, tn), col(nj)),
            pl.BlockSpec((spb, CONV_W - 1, tn), st(0)),
            pl.BlockSpec((spb, CONV_W - 1, tn), st(nj)),
            pl.BlockSpec((CONV_W, tn), col(0)),
            pl.BlockSpec((CONV_W, tn), col(nj)),
            pl.BlockSpec((1, tn), col(0)),
            pl.BlockSpec((1, tn), col(nj)),
        ],
        out_specs=[
            pl.BlockSpec((tm, tn), lambda j, r: (r, j)),
            pl.BlockSpec((spb, CONV_W - 1, tn), st(0)),
            pl.BlockSpec((spb, CONV_W - 1, tn), st(0)),
        ],
        out_shape=[
            jax.ShapeDtypeStruct((m, f), MXU_DTYPE),
            jax.ShapeDtypeStruct((n_streams, CONV_W - 1, f), F32),
            jax.ShapeDtypeStruct((n_streams, CONV_W - 1, f), F32),
        ],
        scratch_shapes=[pltpu.VMEM((spb, CONV_W - 1, tn), F32)] * 2,
        compiler_params=_cparams(2, vmem),
        name="ffn_up_conv",
    )(h, w_up, w_up, conv0, conv0, conv_w, conv_w, conv_b, conv_b)
    return a, jnp.concatenate([cg, cv], axis=-1)


RW_GROUP = 16 * LANES
RW_SMALL = 4 * LANES


def _softplus(z):
    return jnp.maximum(z, 0.0) + jnp.log(1.0 + jnp.exp(-jnp.abs(z)))


def _sigmoid(z):
    return 1.0 / (1.0 + jnp.exp(-z))


def _dg(a, b, dims):
    return lax.dot_general(_mx(a), _mx(b), (dims, ((), ())), preferred_element_type=F32)


_NN = ((1,), (0,))
_NT = ((1,), (1,))
_TN = ((0,), (0,))


def _segsum(x, seg):
    return jnp.dot(_mx(x), seg, preferred_element_type=F32)


def _each(f, *lists):
    return [f(*args) for args in zip(*lists)]


def _wkv_chunk(r, k, v, kk, b, cum, logd, s_bd, consts):
    m0, m1, tri_mask = consts
    c = r[0].shape[0]
    c2 = 2 * c
    stack = lambda x: jnp.concatenate([x * m0, x * m1], axis=0)
    g_end = _each(lambda cm: jnp.exp(cm[c - 1:c]), cum)
    e_inc = _each(jnp.exp, cum)
    e_prev = _each(lambda cm, ld: jnp.exp(cm - ld), cum, logd)
    e_neg = _each(lambda cm: jnp.exp(-cm), cum)
    e_tail = _each(lambda g, e: g * e, g_end, e_neg)
    vs = _each(stack, v)
    lhs = _each(lambda kk_, r_, ep, ei: jnp.concatenate([stack(kk_ * ep), stack(r_ * ei)], axis=0),
                kk, r, e_prev, e_inc)
    rhs = _each(lambda k_, b_, en: jnp.concatenate([stack(k_ * en), stack(b_ * en)], axis=0), k, b, e_neg)
    tails = _each(lambda k_, b_, et: jnp.concatenate([stack(k_ * et), stack(b_ * et)], axis=0), k, b, e_tail)
    amat = _each(lambda x, y: jnp.where(tri_mask, _dg(x, y, _NT), 0.0), lhs, rhs)
    sp = _each(lambda x, s: _dg(x, s, _NT), lhs, s_bd)
    av = _each(lambda a_, v_: _dg(a_[:, :c2], v_, _NN), amat, vs)
    pw = _each(lambda a_: -a_[:c2, c2:], amat)
    x = _each(lambda s_, a_: s_[:c2] + a_[:c2], sp, av)
    n_lvl = int(np.log2(c))
    for lvl in range(n_lvl - 1):
        z = _each(lambda p_, x_: _dg(p_, jnp.concatenate([p_, x_], axis=1), _NN), pw, x)
        pw = _each(lambda z_: z_[:, :c2], z)
        x = _each(lambda x_, z_: x_ + z_[:, c2:], x, z)
    u = _each(lambda p_, x_: x_ + _dg(p_, x_, _NN), pw, x)
    ys = _each(lambda s_, a_, am, u_: s_[c2:] + a_[c2:] - _dg(am[c2:, c2:], u_, _NN), sp, av, amat, u)
    y = _each(lambda y_: y_[:c] + y_[c:], ys)
    s_new = _each(lambda s, g, v_, u_, t_: s * g + _dg(jnp.concatenate([v_, -u_], axis=0), t_, _TN),
                  s_bd, g_end, vs, u, tails)
    return y, s_new


def _rwkv_kernel(pr_ref, pk_ref, pv_ref, ps_ref, sr_ref, sk_ref, sv_ref, ss_ref, wkv0_ref,
                 mur_ref, muk_ref, muv_ref, mus_ref, wdec_ref, dbias_ref, wa_ref, abias_ref, wg_ref,
                 kk_ref, ka_ref, rk_ref, lnw_ref, lnb_ref,
                 o_ref, wkv1_ref,
                 cr_ref, ck_ref, cv_ref, cs_ref, state_ref, *, n_chunks):
    c = pl.program_id(2)
    C = CHUNK
    npairs = RW_GROUP // LANES

    lane = lax.broadcasted_iota(jnp.int32, (1, LANES), 1)
    m0 = (lane < RW_HEAD).astype(F32)
    m1 = 1.0 - m0
    ri = lax.broadcasted_iota(jnp.int32, (4 * C, 4 * C), 0)
    ci = lax.broadcasted_iota(jnp.int32, (4 * C, 4 * C), 1)
    same_head = ((ri // C) % 2) == ((ci // C) % 2)
    tri_mask = same_head & ((ri % C) + ri // (2 * C) > (ci % C))
    li = lax.broadcasted_iota(jnp.int32, (LANES, LANES), 0)
    lj = lax.broadcasted_iota(jnp.int32, (LANES, LANES), 1)
    seg = ((li // RW_HEAD) == (lj // RW_HEAD)).astype(MXU_DTYPE)
    consts = (m0, m1, tri_mask)

    @pl.when(c == 0)
    def _():
        cr_ref[...] = sr_ref[...]
        ck_ref[...] = sk_ref[...]
        cv_ref[...] = sv_ref[...]
        cs_ref[...] = ss_ref[...]
        z = jnp.zeros((RW_HEAD, RW_HEAD), F32)
        for p in range(npairs):
            s0 = wkv0_ref[2 * p]
            s1 = wkv0_ref[2 * p + 1]
            state_ref[p] = jnp.concatenate(
                [jnp.concatenate([s0, z], axis=1), jnp.concatenate([z, s1], axis=1)], axis=0)

    row = lax.broadcasted_iota(jnp.int32, (C, 1), 0)

    def shifted(p_ref, carry_ref, mu_ref):
        p = p_ref[...]
        prev = jnp.where(row == 0, carry_ref[...], pltpu.roll(p, 1, axis=0))
        carry_ref[...] = p[C - 1:C]
        return p + (prev - p) * mu_ref[...]

    r = shifted(pr_ref, cr_ref, mur_ref)
    k = shifted(pk_ref, ck_ref, muk_ref)
    v = shifted(pv_ref, cv_ref, muv_ref)
    sm = shifted(ps_ref, cs_ref, mus_ref)
    wd, ad, gd = sm[:, 0:LANES], sm[:, LANES:2 * LANES], sm[:, 2 * LANES:4 * LANES]

    dec_in = dbias_ref[...] + jnp.dot(_mx(jnp.tanh(wd)), wdec_ref[...], preferred_element_type=F32)
    w_log = -_softplus(-dec_in) - 0.5
    logd = -jnp.exp(w_log)
    a = _sigmoid(abias_ref[...] + jnp.dot(_mx(ad), wa_ref[...], preferred_element_type=F32))
    g = jnp.dot(_mx(_sigmoid(gd)), wg_ref[...], preferred_element_type=F32)
    kk = k * kk_ref[...]
    k2 = k * (1.0 + (a - 1.0) * ka_ref[...])
    rkr = r * k2 * rk_ref[...]
    cum = logd
    for sft in (1, 2, 4, 8, 16, 32):
        cum = cum + jnp.where(row >= sft, pltpu.roll(cum, sft, axis=0), 0.0)

    pairs = lambda x: [x[:, p * LANES:(p + 1) * LANES] for p in range(npairs)]
    r_p, k_p, v_p, a_p, cum_p, logd_p = pairs(r), pairs(k2), pairs(v), pairs(a), pairs(cum), pairs(logd)
    sums = _each(lambda kk_, rkr_: _segsum(jnp.concatenate([kk_ * kk_, rkr_], axis=0), seg), pairs(kk), pairs(rkr))
    kk_p = _each(lambda kk_, s_: kk_ * lax.rsqrt(jnp.maximum(s_[:C], 1e-24)), pairs(kk), sums)
    b_p = _each(lambda kk_, a_: kk_ * a_, kk_p, a_p)
    y_p, s_new = _wkv_chunk(r_p, k_p, v_p, kk_p, b_p, cum_p, logd_p, [state_ref[p] for p in range(npairs)], consts)
    for p in range(npairs):
        state_ref[p] = s_new[p]
    yc = _each(lambda y_: y_ - _segsum(y_, seg) * (1.0 / RW_HEAD), y_p)
    var = _each(lambda yc_: _segsum(yc_ * yc_, seg) * (1.0 / RW_HEAD), yc)
    for p in range(npairs):
        sl = slice(p * LANES, (p + 1) * LANES)
        yn = yc[p] * lax.rsqrt(var[p] + LNX_EPS) * lnw_ref[:, sl] + lnb_ref[:, sl]
        o_ref[:, sl] = ((yn + sums[p][C:] * v_p[p]) * g[:, sl]).astype(o_ref.dtype)

    @pl.when(c == n_chunks - 1)
    def _():
        for p in range(npairs):
            s = state_ref[p]
            wkv1_ref[2 * p] = s[:RW_HEAD, :RW_HEAD]
            wkv1_ref[2 * p + 1] = s[RW_HEAD:, RW_HEAD:]


def _rwkv(pfull, cols, shift0, wkv0, w, n_streams, t):
    m = pfull.shape[0]
    rw = w["k_k"].shape[1]
    ng = rw // RW_GROUP
    nc = t // CHUNK
    hg = RW_GROUP // RW_HEAD
    rowblk = lambda off: (lambda b, g, c: (b * nc + c, off + g))
    fixed = lambda off: (lambda b, g, c: (b * nc + c, off))
    st = lambda b, g, c: (b, 0, g)
    st0 = lambda b, g, c: (b, 0, 0)
    wcol = lambda b, g, c: (0, g)
    in_specs = [
        pl.BlockSpec((CHUNK, RW_GROUP), rowblk(cols["r"])),
        pl.BlockSpec((CHUNK, RW_GROUP), rowblk(cols["k"])),
        pl.BlockSpec((CHUNK, RW_GROUP), rowblk(cols["v"])),
        pl.BlockSpec((CHUNK, RW_SMALL), fixed(cols["small"])),
        pl.BlockSpec((None, 1, RW_GROUP), st),
        pl.BlockSpec((None, 1, RW_GROUP), st),
        pl.BlockSpec((None, 1, RW_GROUP), st),
        pl.BlockSpec((None, 1, RW_SMALL), st0),
        pl.BlockSpec((None, hg, RW_HEAD, RW_HEAD), lambda b, g, c: (b, g, 0, 0)),
        pl.BlockSpec((1, RW_GROUP), wcol), pl.BlockSpec((1, RW_GROUP), wcol), pl.BlockSpec((1, RW_GROUP), wcol),
        pl.BlockSpec((1, RW_SMALL), lambda b, g, c: (0, 0)),
        pl.BlockSpec((LANES, RW_GROUP), wcol), pl.BlockSpec((1, RW_GROUP), wcol),
        pl.BlockSpec((LANES, RW_GROUP), wcol), pl.BlockSpec((1, RW_GROUP), wcol),
        pl.BlockSpec((2 * LANES, RW_GROUP), wcol),
        pl.BlockSpec((1, RW_GROUP), wcol), pl.BlockSpec((1, RW_GROUP), wcol), pl.BlockSpec((1, RW_GROUP), wcol),
        pl.BlockSpec((1, RW_GROUP), wcol), pl.BlockSpec((1, RW_GROUP), wcol),
    ]
    o, wkv1 = pl.pallas_call(
        functools.partial(_rwkv_kernel, n_chunks=nc),
        grid=(n_streams, ng, nc),
        in_specs=in_specs,
        out_specs=[
            pl.BlockSpec((CHUNK, RW_GROUP), lambda b, g, c: (b * nc + c, g)),
            pl.BlockSpec((None, hg, RW_HEAD, RW_HEAD), lambda b, g, c: (b, g, 0, 0)),
        ],
        out_shape=[
            jax.ShapeDtypeStruct((m, rw), MXU_DTYPE),
            jax.ShapeDtypeStruct((n_streams, rw // RW_HEAD, RW_HEAD, RW_HEAD), F32),
        ],
        scratch_shapes=[pltpu.VMEM((1, RW_GROUP), F32)] * 3 + [pltpu.VMEM((1, RW_SMALL), F32)]
        + [pltpu.VMEM((RW_GROUP // LANES, LANES, LANES), F32)],
        compiler_params=_cparams(3, 32 << 20),
        name="rwkv7_chunked",
    )(pfull, pfull, pfull, pfull, shift0["r"], shift0["k"], shift0["v"], shift0["small"], wkv0,
      w["mu_r"], w["mu_k"], w["mu_v"], w["mu_small"], w["w_decay_up"], w["decay_bias"], w["w_a_up"], w["a_bias"],
      w["w_gate_up"], w["k_k"], w["k_a"], w["r_k"], w["lnx_w"], w["lnx_b"])
    return o, wkv1


def _rope_kernel(q_ref, k_ref, qi_ref, kiw_ref, ca_ref, sa_ref, cb_ref, sb1_ref, sb2_ref,
                 qo_ref, ko_ref, qio_ref, kiwo_ref):
    ca, sa = ca_ref[...], sa_ref[...]
    cb, sb1, sb2 = cb_ref[...], sb1_ref[...], sb2_ref[...]

    def rope_head(x):
        return x * ca + pltpu.roll(x, SA_HEAD // 2, axis=1) * sa

    def rope_idx(x):
        return (x * cb + pltpu.roll(x, LANES - IDX_DIM // 2, axis=1) * sb1
                + pltpu.roll(x, IDX_DIM // 2, axis=1) * sb2)

    q_scale = float(SA_HEAD ** -0.5 * np.log2(np.e))
    for h in range(q_ref.shape[1] // LANES):
        sl = slice(h * LANES, (h + 1) * LANES)
        qo_ref[:, sl] = (rope_head(q_ref[:, sl]) * q_scale).astype(qo_ref.dtype)
    for h in range(k_ref.shape[1] // LANES):
        sl = slice(h * LANES, (h + 1) * LANES)
        ko_ref[:, sl] = rope_head(k_ref[:, sl])
    for hp in range(qi_ref.shape[1] // LANES):
        sl = slice(hp * LANES, (hp + 1) * LANES)
        qio_ref[:, sl] = rope_idx(qi_ref[:, sl]).astype(qio_ref.dtype)
    x = kiw_ref[...]
    lane = lax.broadcasted_iota(jnp.int32, (1, LANES), 1)
    kiwo_ref[...] = jnp.where(lane < IDX_DIM, rope_idx(x), x * float((IDX_HEADS * IDX_DIM) ** -0.5))


def _rope_tables(pos):
    def ang(half):
        inv = ROPE_THETA ** (-jnp.arange(half, dtype=F32) / half)
        return pos.astype(F32)[:, None] * inv[None, :]
    aa = ang(SA_HEAD // 2)
    ca = jnp.concatenate([jnp.cos(aa), jnp.cos(aa)], axis=1)
    sa = jnp.concatenate([-jnp.sin(aa), jnp.sin(aa)], axis=1)
    ab = ang(IDX_DIM // 2)
    z = jnp.zeros_like(ab)
    cb = jnp.concatenate([jnp.cos(ab)] * 4, axis=1)
    sb1 = jnp.concatenate([-jnp.sin(ab), z, -jnp.sin(ab), z], axis=1)
    sb2 = jnp.concatenate([z, jnp.sin(ab), z, jnp.sin(ab)], axis=1)
    return ca, sa, cb, sb1, sb2


def _rope(pfull, cols, tables, n_streams, t, tm, sa_w, kv_w, qi_w):
    m = pfull.shape[0]
    bps = t // tm
    tab = pl.BlockSpec((tm, LANES), lambda r: (r % bps, 0))
    return pl.pallas_call(
        _rope_kernel,
        grid=(m // tm,),
        in_specs=[
            pl.BlockSpec((tm, sa_w), lambda r: (r, cols["q"])),
            pl.BlockSpec((tm, kv_w), lambda r: (r, cols["ksa"])),
            pl.BlockSpec((tm, qi_w), lambda r: (r, cols["qi"])),
            pl.BlockSpec((tm, LANES), lambda r: (r, cols["kiw"])),
            tab, tab, tab, tab, tab,
        ],
        out_specs=[
            pl.BlockSpec((tm, sa_w), lambda r: (r, 0)),
            pl.BlockSpec((tm, kv_w), lambda r: (r, 0)),
            pl.BlockSpec((tm, qi_w), lambda r: (r, 0)),
            pl.BlockSpec((tm, LANES), lambda r: (r, 0)),
        ],
        out_shape=[
            jax.ShapeDtypeStruct((m, sa_w), MXU_DTYPE),
            jax.ShapeDtypeStruct((m, kv_w), F32),
            jax.ShapeDtypeStruct((m, qi_w), MXU_DTYPE),
            jax.ShapeDtypeStruct((m, LANES), F32),
        ],
        compiler_params=_cparams(1, 32 << 20),
        name="rope",
    )(pfull, pfull, pfull, pfull, *tables)


IDX_HEAD_GROUP = 4
IDX_VALUE_PASSES = 16


FOLD_CHAINS = 2


def _fold_sublane_tiles(x, op):
    parts = [x[r:r + 8] for r in range(0, x.shape[0], 8)]
    chains = [functools.reduce(op, parts[c::FOLD_CHAINS]) for c in range(min(FOLD_CHAINS, len(parts)))]
    return functools.reduce(op, chains)


def _f32_to_key(x):
    bits = lax.bitcast_convert_type(x, jnp.int32)
    return bits ^ ((bits >> 31) & jnp.int32(0x7FFFFFFF))


def _key_to_f32(key):
    return lax.bitcast_convert_type(key ^ ((key >> 31) & jnp.int32(0x7FFFFFFF)), F32)


def _index_kernel(qit_ref, w_ref, ki_ref, mask_ref, keys_ref, *, tq, kb, nkb, past, topk):
    i = pl.program_id(1)
    n_adm = jnp.minimum(nkb, (past + (i + 1) * tq + kb - 1) // kb)
    chunk_bits = CHUNK.bit_length() - 1
    qpos = past + i * tq + lax.broadcasted_iota(jnp.int32, (1, tq), 1)
    lim = ((qpos >> chunk_bits) + 1) << chunk_bits
    krow = lax.broadcasted_iota(jnp.int32, (kb, 1), 0)
    hg = IDX_HEAD_GROUP
    fold = _fold_sublane_tiles

    def score_block(j, carry):
        smax, smin = carry
        kblk = ki_ref[j]
        acc = jnp.zeros((kb, tq), F32)
        for g0 in range(0, IDX_HEADS, hg):
            s = jnp.dot(kblk, qit_ref[:, g0 * tq:(g0 + hg) * tq], preferred_element_type=F32)
            for g in range(hg):
                acc = acc + jnp.maximum(s[:, g * tq:(g + 1) * tq], 0.0) * w_ref[g0 + g:g0 + g + 1, :]
        score = acc + 0.0
        adm = j * kb + krow < lim
        keys_ref[j] = jnp.where(adm, _f32_to_key(score), jnp.int32(INT_MIN))
        smax = jnp.maximum(smax, fold(jnp.where(adm, score, -jnp.inf), jnp.maximum))
        smin = jnp.minimum(smin, fold(jnp.where(adm, score, jnp.inf), jnp.minimum))
        return smax, smin

    smax, smin = lax.fori_loop(0, n_adm, score_block,
                               (jnp.full((8, tq), -jnp.inf, F32), jnp.full((8, tq), jnp.inf, F32)))
    lo0 = _f32_to_key(jnp.min(smin, axis=0, keepdims=True))
    hi0 = _f32_to_key(jnp.max(smax, axis=0, keepdims=True)) + 1

    def count_ge(thr):
        def body(j, cnt):
            return cnt + fold(jnp.where(keys_ref[j] >= thr, 1.0, 0.0), jnp.add)
        return jnp.sum(lax.fori_loop(0, n_adm, body, jnp.zeros((8, tq), F32)), axis=0, keepdims=True)

    def midpoint(lo, hi):
        return (lo >> 1) + (hi >> 1) + (lo & hi & 1)

    def unfinished(lo, hi, cnt_lo):
        active = (midpoint(lo, hi) != lo) & (cnt_lo != float(topk))
        return jnp.max(jnp.where(active, 1.0, 0.0))

    def bisect(state):
        p, lo, hi, cnt_lo, cnt_hi, _ = state
        mid_v = _f32_to_key(0.5 * _key_to_f32(lo) + 0.5 * _key_to_f32(hi))
        by_value = (p < IDX_VALUE_PASSES) & (mid_v > lo) & (mid_v < hi)
        mid = jnp.where(by_value, mid_v, midpoint(lo, hi))
        cnt = count_ge(mid)
        ge = cnt >= float(topk)
        lo, cnt_lo = jnp.where(ge, mid, lo), jnp.where(ge, cnt, cnt_lo)
        hi, cnt_hi = jnp.where(ge, hi, mid), jnp.where(ge, cnt_hi, cnt)
        return p + 1, lo, hi, cnt_lo, cnt_hi, unfinished(lo, hi, cnt_lo)

    cnt_lo0 = lim.astype(F32)
    state = (jnp.int32(0), lo0, hi0, cnt_lo0, jnp.zeros((1, tq), F32), unfinished(lo0, hi0, cnt_lo0))
    _, thr, _, n_ge, n_gt, _ = lax.while_loop(lambda st: st[5] > 0.5, bisect, state)

    tied = n_ge > float(topk)
    need = float(topk) - n_gt
    end_all = jnp.int32(nkb * kb)

    def count_tied_below(bound):
        def body(j, cnt):
            hit = (keys_ref[j] == thr) & (j * kb + krow < bound)
            return cnt + fold(jnp.where(hit, 1.0, 0.0), jnp.add)
        return jnp.sum(lax.fori_loop(0, n_adm, body, jnp.zeros((8, tq), F32)), axis=0, keepdims=True)

    def tie_bisect(state):
        p, lo, hi = state
        mid = (lo + hi) >> 1
        enough = count_tied_below(mid) >= need
        return p + 1, jnp.where(enough, lo, mid), jnp.where(enough, mid, hi)

    any_tied = jnp.max(jnp.where(tied, 1.0, 0.0))
    n_steps = (nkb * kb).bit_length()
    tie_state = (jnp.int32(0), jnp.zeros((1, tq), jnp.int32), jnp.full((1, tq), end_all, jnp.int32))
    tie_hi = lax.while_loop(lambda st: (any_tied > 0.5) & (st[0] < n_steps), tie_bisect, tie_state)[2]
    tie_end = jnp.where(tied, tie_hi, end_all)

    def write(j, carry):
        k = keys_ref[j]
        keep = (k > thr) | ((k == thr) & (j * kb + krow < tie_end))
        mask_ref[j] = jnp.where(keep, 1.0, 0.0).T.astype(mask_ref.dtype)
        return carry

    lax.fori_loop(0, n_adm, write, 0)

    def clear(j, carry):
        mask_ref[j] = jnp.zeros((tq, kb), mask_ref.dtype)
        return carry

    lax.fori_loop(n_adm, nkb, clear, 0)


def _index_mask(qit, w, ki, n_streams, t, tq, kb, past, topk):
    nkb = ki.shape[1]
    nq = t // tq
    isz = jnp.dtype(MXU_DTYPE).itemsize
    vmem = (nkb * kb * tq * 4 + 2 * nkb * kb * tq + 2 * nkb * kb * LANES * isz + 2 * IDX_DIM * IDX_HEADS * tq * isz
            + 6 * kb * IDX_HEAD_GROUP * tq * 4 + (8 << 20))
    return pl.pallas_call(
        functools.partial(_index_kernel, tq=tq, kb=kb, nkb=nkb, past=past, topk=topk),
        grid=(n_streams, nq),
        in_specs=[
            pl.BlockSpec((None, None, IDX_DIM, IDX_HEADS * tq), lambda b, i: (b, i, 0, 0)),
            pl.BlockSpec((None, IDX_HEADS, tq), lambda b, i: (b, 0, i)),
            pl.BlockSpec((None, nkb, kb, IDX_DIM), lambda b, i: (b, 0, 0, 0)),
        ],
        out_specs=pl.BlockSpec((None, nkb, tq, kb), lambda b, i: (b, 0, i, 0)),
        out_shape=jax.ShapeDtypeStruct((n_streams, nkb, t, kb), jnp.int8),
        scratch_shapes=[pltpu.VMEM((nkb, kb, tq), jnp.int32)],
        compiler_params=_cparams(2, vmem),
        name="indexer_topk_mask",
    )(qit, w, ki)


def _attn_kernel(it_ref, jt_ref, last_ref, q_ref, k_ref, v_ref, m_ref, o_ref, qs_ref, mx_ref, l_ref, acc_ref,
                 *, group):
    step = pl.program_id(1)
    tq = q_ref.shape[0]
    tk = k_ref.shape[0]
    n_kv = k_ref.shape[1] // SA_HEAD
    rows = group * tq
    n_lane_tiles = tk // LANES

    @pl.when(jt_ref[step] == 0)
    def _():
        for n in range(n_kv):
            for g in range(group):
                h = n * group + g
                qs_ref[n, g * tq:(g + 1) * tq, :] = q_ref[:, h * SA_HEAD:(h + 1) * SA_HEAD]
        mx_ref[...] = jnp.full(mx_ref.shape, NEG_BIG, F32)
        l_ref[...] = jnp.zeros(l_ref.shape, F32)
        acc_ref[...] = jnp.zeros(acc_ref.shape, F32)

    bias = jnp.where(m_ref[...].astype(F32) > 0.0, 0.0, NEG_BIG)
    kv = range(n_kv)
    s = [_nt(qs_ref[n], k_ref[:, n * SA_HEAD:(n + 1) * SA_HEAD]) for n in kv]
    s = [(x.reshape(group, tq, tk) + bias[None]).reshape(rows, tk) for x in s]
    tiles = [[x[:, c * LANES:(c + 1) * LANES] for c in range(n_lane_tiles)] for x in s]
    m_old = [mx_ref[n] for n in kv]
    m_new = [jnp.maximum(mo, jnp.max(functools.reduce(jnp.maximum, t), axis=1, keepdims=True))
             for mo, t in zip(m_old, tiles)]
    alpha = [jnp.exp2(mo - mn) for mo, mn in zip(m_old, m_new)]
    p = [[jnp.exp2(x - mn) for x in t] for t, mn in zip(tiles, m_new)]
    for n in kv:
        mx_ref[n] = m_new[n]
        l_ref[n] = alpha[n] * l_ref[n] + functools.reduce(jnp.add, p[n])
    pv = [jnp.dot(_mx(jnp.concatenate(p[n], axis=1)), v_ref[:, n * SA_HEAD:(n + 1) * SA_HEAD],
                  preferred_element_type=F32) for n in kv]
    for n in kv:
        acc_ref[n] = alpha[n] * acc_ref[n] + pv[n]

    @pl.when(last_ref[step] == 1)
    def _():
        for n in range(n_kv):
            o = acc_ref[n] / jnp.sum(l_ref[n], axis=1, keepdims=True)
            for g in range(group):
                h = n * group + g
                o_ref[:, h * SA_HEAD:(h + 1) * SA_HEAD] = o[g * tq:(g + 1) * tq].astype(o_ref.dtype)


def _attn_tiles(t, tq, tk, nkb, past):
    ii, jj, last = [], [], []
    for i in range(t // tq):
        n_adm = min(nkb, -(-(past + (i + 1) * tq) // tk))
        for j in range(n_adm):
            ii.append(i)
            jj.append(j)
            last.append(int(j == n_adm - 1))
    return tuple(jnp.asarray(np.array(x, np.int32)) for x in (ii, jj, last))


def _attention(q, k_all, v_all, mask, n_streams, t, tq, tk, past):
    m, sa_w = q.shape
    kv_w = k_all.shape[2]
    nkb = mask.shape[1]
    nq = t // tq
    n_heads = sa_w // SA_HEAD
    group = n_heads // SA_KV_HEADS
    it, jt, last = _attn_tiles(t, tq, tk, nkb, past)
    grid_spec = pltpu.PrefetchScalarGridSpec(
        num_scalar_prefetch=3,
        grid=(n_streams, int(it.shape[0])),
        in_specs=[
            pl.BlockSpec((tq, sa_w), lambda b, s, it, jt, lt: (b * nq + it[s], 0)),
            pl.BlockSpec((None, tk, kv_w), lambda b, s, it, jt, lt: (b, jt[s], 0)),
            pl.BlockSpec((None, tk, kv_w), lambda b, s, it, jt, lt: (b, jt[s], 0)),
            pl.BlockSpec((None, None, tq, tk), lambda b, s, it, jt, lt: (b, jt[s], it[s], 0)),
        ],
        out_specs=pl.BlockSpec((tq, sa_w), lambda b, s, it, jt, lt: (b * nq + it[s], 0)),
        scratch_shapes=[pltpu.VMEM((SA_KV_HEADS, group * tq, SA_HEAD), MXU_DTYPE)]
        + [pltpu.VMEM((SA_KV_HEADS, group * tq, SA_HEAD), F32)] * 3,
    )
    return pl.pallas_call(
        functools.partial(_attn_kernel, group=group),
        grid_spec=grid_spec,
        out_shape=jax.ShapeDtypeStruct((m, sa_w), MXU_DTYPE),
        compiler_params=_cparams(2, 40 << 20),
        name="masked_flash_attention",
    )(it, jt, last, q, k_all, v_all, mask)


def _prepare_weights(w_in, mu_shift, w_decay_up, decay_bias, w_a_up, a_bias, w_gate_up, k_k, k_a, r_k,
                     lnx_w, lnx_b, w_out, w_up, conv_w, conv_b, w_down, d_model):
    rw = w_decay_up.shape[1]
    rw_cols = 3 * rw + D_DECAY + D_AAA + D_GATE
    sa_w = d_model - rw
    kv_w = SA_KV_HEADS * SA_HEAD
    qi_w = IDX_HEADS * IDX_DIM
    o = {}
    o["r"] = 0
    o["wd"] = rw
    o["k"] = rw + D_DECAY
    o["v"] = 2 * rw + D_DECAY
    o["ad"] = 3 * rw + D_DECAY
    o["gd"] = 3 * rw + D_DECAY + D_AAA
    o["q"] = rw_cols
    o["ksa"] = rw_cols + sa_w
    o["vsa"] = o["ksa"] + kv_w
    o["qi"] = o["vsa"] + kv_w
    o["kiw"] = o["qi"] + qi_w
    kiw_w = IDX_DIM + IDX_HEADS

    def seg(x, name, width, pad_to=None):
        s = x[..., o[name]:o[name] + width]
        if pad_to is not None and pad_to > width:
            s = jnp.pad(s, [(0, 0)] * (s.ndim - 1) + [(0, pad_to - width)])
        return s

    def rw_small(x):
        return jnp.concatenate([seg(x, "wd", D_DECAY, LANES), seg(x, "ad", D_AAA, LANES), seg(x, "gd", D_GATE)], -1)

    w_in_l = jnp.concatenate([
        seg(w_in, "r", rw), seg(w_in, "k", rw), seg(w_in, "v", rw), seg(w_in, "q", sa_w), seg(w_in, "qi", qi_w),
        seg(w_in, "ksa", kv_w), seg(w_in, "vsa", kv_w), rw_small(w_in), seg(w_in, "kiw", kiw_w, LANES)], axis=1)
    w_in_l = jnp.pad(w_in_l, ((0, 0), (0, -w_in_l.shape[1] % IN_PROJ_TN)))
    starts = {"r": 0, "k": rw, "v": 2 * rw, "q": 3 * rw, "qi": 3 * rw + sa_w}
    starts["ksa"] = starts["qi"] + qi_w
    starts["vsa"] = starts["ksa"] + kv_w
    starts["small"] = starts["vsa"] + kv_w
    starts["kiw"] = starts["small"] + 4 * LANES
    pad_rows = lambda x, n: jnp.pad(x, ((0, n - x.shape[0]), (0, 0)))
    row = lambda x: x.reshape(1, -1).astype(F32)
    mu = mu_shift.reshape(1, -1)
    w = {
        "w_in": _mx(w_in_l),
        "mu_r": seg(mu, "r", rw), "mu_k": seg(mu, "k", rw), "mu_v": seg(mu, "v", rw), "mu_small": rw_small(mu),
        "w_decay_up": _mx(pad_rows(w_decay_up, LANES)), "decay_bias": row(decay_bias),
        "w_a_up": _mx(pad_rows(w_a_up, LANES)), "a_bias": row(a_bias),
        "w_gate_up": _mx(w_gate_up),
        "k_k": row(k_k), "k_a": row(k_a), "r_k": row(r_k), "lnx_w": row(lnx_w), "lnx_b": row(lnx_b),
        "w_out": _mx(w_out), "w_up": _mx(w_up), "conv_w": conv_w, "conv_b": row(conv_b), "w_down": _mx(w_down),
    }
    dims = {"rw": rw, "rw_cols": rw_cols, "sa_w": sa_w, "kv_w": kv_w, "qi_w": qi_w, "orig": o, "starts": starts,
            "seg": seg, "rw_small": rw_small}
    return w, dims


def _layer(x, past_k, past_v, past_ki, wkv0, shift0, conv0, norms, w, dims, tiles):
    bsz, t, d = x.shape
    past = past_k.shape[1]
    m = bsz * t
    rw, sa_w, kv_w, qi_w = dims["rw"], dims["sa_w"], dims["kv_w"], dims["qi_w"]
    starts, seg, rw_small = dims["starts"], dims["seg"], dims["rw_small"]
    assert t % CHUNK == 0 and past % CHUNK == 0
    tm, tq, kb = tiles["tm"], tiles["tq"], tiles["kb"]
    g_mix_pre, g_mix_post, g_ffn_pre, g_ffn_post = norms

    x2 = x.reshape(m, d)
    h1 = _rmsnorm(x2, g_mix_pre, tiles["tm_norm"])
    pfull = _matmul(h1, w["w_in"], tiles["tm_in"], tiles["tn_in"])

    sh = shift0.reshape(bsz, 1, -1)
    shift_l = {"r": seg(sh, "r", rw), "k": seg(sh, "k", rw), "v": seg(sh, "v", rw), "small": rw_small(sh)}
    cols_rw = {n: starts[n] // RW_GROUP for n in ("r", "k", "v")}
    cols_rw["small"] = starts["small"] // RW_SMALL
    o_rw, wkv1 = _rwkv(pfull, cols_rw, shift_l, wkv0, w, bsz, t)

    pos = past + jnp.arange(t, dtype=jnp.int32)
    cols_sa = {"q": starts["q"] // sa_w, "ksa": starts["ksa"] // kv_w, "qi": starts["qi"] // qi_w,
               "kiw": starts["kiw"] // LANES}
    q_r, k_r, qi_r, kiw_r = _rope(pfull, cols_sa, _rope_tables(pos), bsz, t, tq, sa_w, kv_w, qi_w)
    k_new = k_r.reshape(bsz, t, kv_w)
    v_new = pfull[:, starts["vsa"]:starts["vsa"] + kv_w].reshape(bsz, t, kv_w)
    ki_new = kiw_r[:, :IDX_DIM].reshape(bsz, t, IDX_DIM)
    l_all = past + t
    lp = -(-l_all // kb) * kb
    nkb = lp // kb
    cat = lambda old, new: jnp.pad(
        jnp.concatenate([_mx(old.reshape(bsz, past, new.shape[-1])), _mx(new)], axis=1),
        ((0, 0), (0, lp - l_all), (0, 0)))
    k_all, v_all, ki_all = cat(past_k, k_new), cat(past_v, v_new), cat(past_ki, ki_new)
    nq = t // tq
    qit = (qi_r.reshape(bsz, nq, tq, IDX_HEADS, IDX_DIM).transpose(0, 1, 4, 3, 2)
           .reshape(bsz, nq, IDX_DIM, IDX_HEADS * tq))
    w_idx = kiw_r[:, IDX_DIM:IDX_DIM + IDX_HEADS].reshape(bsz, t, IDX_HEADS).transpose(0, 2, 1)
    topk = min(TOPK_MAX, l_all // 4)
    mask = _index_mask(qit, w_idx, ki_all.reshape(bsz, nkb, kb, IDX_DIM), bsz, t, tq, kb, past, topk)
    o_sa = _attention(q_r, k_all, v_all, mask, bsz, t, tq, kb, past)

    mix_in = jnp.concatenate([o_rw, o_sa], axis=1)
    x1 = _matmul_norm_residual(mix_in, w["w_out"], x2, g_mix_post, tm, tiles["tn_out"], "out_proj")
    h2 = _rmsnorm(x1, g_ffn_pre, tiles["tm_norm"])
    act, conv1 = _ffn_up(h2, w["w_up"], conv0, w["conv_w"], w["conv_b"], bsz, tiles["tm_up"], tiles["tn_up"])
    x_out = _matmul_norm_residual(act, w["w_down"], x1, g_ffn_post, tm, tiles["tn_out"], "ffn_down")

    orig = dims["orig"]
    last = pfull.reshape(bsz, t, -1)[:, t - 1:t]
    sm = last[..., starts["small"]:starts["small"] + 4 * LANES]
    shift1 = jnp.concatenate([
        last[..., starts["r"]:starts["r"] + rw], sm[..., :D_DECAY],
        last[..., starts["k"]:starts["k"] + rw], last[..., starts["v"]:starts["v"] + rw],
        sm[..., LANES:LANES + D_AAA], sm[..., 2 * LANES:2 * LANES + D_GATE]], axis=-1)
    del orig
    new = (k_new.reshape(bsz, t, SA_KV_HEADS, SA_HEAD), v_new.reshape(bsz, t, SA_KV_HEADS, SA_HEAD), ki_new,
           wkv1, shift1, conv1)
    return x_out.reshape(bsz, t, d), new


def _tiles(bsz, t):
    big = t >= 1024
    return {
        "tm_norm": 256 if big else CHUNK,
        "tm_in": 1024 if big else bsz * t,
        "tn_in": IN_PROJ_TN,
        "tm": 512 if big else bsz * t,
        "tn_out": 512,
        "tm_up": 512 if big else bsz * t,
        "tn_up": 512,
        "tq": 256 if big else CHUNK,
        "kb": 512,
    }


def _run_stream(x, past_k, past_v, past_ki, wkv0, shift0, conv0, norm_w, weights, dims, depth):
    outs = []
    tiles = _tiles(x.shape[0], x.shape[1])
    for l in range(depth):
        w_l = {n: v[l] for n, v in weights.items()}
        norms = tuple(g[l].reshape(1, -1) for g in norm_w)
        x, st = _layer(x, past_k[l], past_v[l], past_ki[l], wkv0[l], shift0[l], conv0[l], norms, w_l, dims[l], tiles)
        outs.append(st)
    return x, [jnp.stack(s) for s in zip(*outs)]


def kernel(x_prompt, x_sample, cache_k, cache_v, cache_kidx, state_wkv, state_shift, state_conv, norm_mix_pre, norm_mix_post, norm_ffn_pre, norm_ffn_post, w_in, mu_shift, w_decay_up, decay_bias, w_a_up, a_bias, w_gate_up, k_k, k_a, r_k, lnx_w, lnx_b, w_out, w_up, conv_w, conv_b, w_down):
    depth, d_model = norm_mix_pre.shape
    per_layer = [_prepare_weights(w_in[l], mu_shift[l], w_decay_up[l], decay_bias[l], w_a_up[l], a_bias[l],
                                  w_gate_up[l], k_k[l], k_a[l], r_k[l], lnx_w[l], lnx_b[l], w_out[l], w_up[l],
                                  conv_w[l], conv_b[l], w_down[l], d_model) for l in range(depth)]
    weights = {n: [pw[0][n] for pw in per_layer] for n in per_layer[0][0]}
    dims = [pw[1] for pw in per_layer]
    norm_w = (norm_mix_pre, norm_mix_post, norm_ffn_pre, norm_ffn_post)

    dt = x_prompt.dtype
    bp = x_prompt.shape[0]
    rw_heads = state_wkv.shape[2]
    zk = jnp.zeros((depth, bp, 0, SA_KV_HEADS, SA_HEAD), dt)
    zki = jnp.zeros((depth, bp, 0, IDX_DIM), dt)
    zwkv = jnp.zeros((depth, bp, rw_heads, RW_HEAD, RW_HEAD), dt)
    zshift = jnp.zeros((depth, bp, 1, state_shift.shape[-1]), dt)
    zconv = jnp.zeros((depth, bp, CONV_W - 1, state_conv.shape[-1]), dt)
    y_prompt, p_new = _run_stream(x_prompt, zk, zk, zki, zwkv, zshift, zconv, norm_w, weights, dims, depth)
    y_sample, s_new = _run_stream(x_sample, cache_k, cache_v, cache_kidx, state_wkv, state_shift, state_conv,
                                  norm_w, weights, dims, depth)
    return (y_prompt, y_sample, *p_new, *s_new)
```

```python
import functools

import numpy as np
import jax
import jax.numpy as jnp
from jax import lax
from jax.experimental import pallas as pl
from jax.experimental.pallas import tpu as pltpu

F32 = jnp.float32
MXU_DTYPE = jnp.bfloat16
HI = lax.Precision.HIGHEST

CHUNK = 64
RW_HEAD = 64
D_DECAY = 96
D_AAA = 96
D_GATE = 256
SA_HEAD = 128
SA_KV_HEADS = 4
IDX_HEADS = 16
IDX_DIM = 64
TOPK_MAX = 256
CONV_W = 3
ROPE_THETA = 10000.0
NORM_EPS = 1e-6
LNX_EPS = 64e-5

LANES = 128
V7X_VMEM_LIMIT_BYTES = 60000 * 1024
INT_MIN = -(2 ** 31)
NEG_BIG = -1e30


def _cparams(n_grid, vmem_bytes):
    limit = int(min(V7X_VMEM_LIMIT_BYTES, max(32 * 1024 * 1024, vmem_bytes)))
    return pltpu.CompilerParams(dimension_semantics=("arbitrary",) * n_grid, vmem_limit_bytes=limit)


def _nt(a, b, precision=None):
    return lax.dot_general(a, b, (((1,), (1,)), ((), ())), precision=precision, preferred_element_type=F32)


def _tn(a, b, precision=None):
    return lax.dot_general(a, b, (((0,), (0,)), ((), ())), precision=precision, preferred_element_type=F32)


def _mx(x):
    return x.astype(MXU_DTYPE)


def _rmsnorm_kernel(x_ref, g_ref, o_ref):
    x = x_ref[...]
    y = x * lax.rsqrt(jnp.mean(x * x, axis=-1, keepdims=True) + NORM_EPS)
    o_ref[...] = (y * g_ref[...]).astype(o_ref.dtype)


def _rmsnorm(x, g, tm):
    m, d = x.shape
    return pl.pallas_call(
        _rmsnorm_kernel,
        grid=(m // tm,),
        in_specs=[pl.BlockSpec((tm, d), lambda i: (i, 0)), pl.BlockSpec((1, d), lambda i: (0, 0))],
        out_specs=pl.BlockSpec((tm, d), lambda i: (i, 0)),
        out_shape=jax.ShapeDtypeStruct((m, d), MXU_DTYPE),
        compiler_params=_cparams(1, 6 * tm * d * 4),
        name="rmsnorm",
    )(x, g)


def _mm_kernel(a_ref, b_ref, o_ref):
    o_ref[...] = jnp.dot(a_ref[...], b_ref[...], preferred_element_type=F32)


IN_PROJ_TN = 1024


def _matmul(a, b, tm, tn):
    m, k = a.shape
    n = b.shape[1]
    isz = jnp.dtype(MXU_DTYPE).itemsize
    vmem = 2 * (tm * k * isz + k * tn * isz + tm * tn * 4) + 2 * tm * tn * 4 + (8 << 20)
    return pl.pallas_call(
        _mm_kernel,
        grid=(m // tm, n // tn),
        in_specs=[pl.BlockSpec((tm, k), lambda i, j: (i, 0)), pl.BlockSpec((k, tn), lambda i, j: (0, j))],
        out_specs=pl.BlockSpec((tm, tn), lambda i, j: (i, j)),
        out_shape=jax.ShapeDtypeStruct((m, n), F32),
        compiler_params=_cparams(2, vmem),
        name="in_proj",
    )(a, b)


MM_NORM_ROWS = 128


def _mm_norm_kernel(a_ref, w_ref, x_ref, gpost_ref, x1_ref, *, nn):
    n = pl.program_id(1)
    tm, d = x1_ref.shape
    tn = w_ref.shape[1]
    x1_ref[:, pl.ds(pl.multiple_of(n * tn, LANES), tn)] = jnp.dot(a_ref[...], w_ref[...],
                                                                  preferred_element_type=F32)

    @pl.when(n == nn - 1)
    def _():
        step = min(tm, MM_NORM_ROWS)
        for r in range(0, tm, step):
            rows = slice(r, r + step)
            y = x1_ref[rows, :]
            y = y * lax.rsqrt(jnp.mean(y * y, axis=-1, keepdims=True) + NORM_EPS) * gpost_ref[...]
            x1_ref[rows, :] = x_ref[rows, :] + y


def _matmul_norm_residual(a, w, x, g_post, tm, tn, name):
    m, kdim = a.shape
    d = w.shape[1]
    nn = d // tn
    isz = jnp.dtype(MXU_DTYPE).itemsize
    fixed = 2 * kdim * tn * isz + tm * d * 4 + 2 * tm * tn * 4 + 6 * MM_NORM_ROWS * d * 4 + (4 << 20)
    per_copy = tm * kdim * isz + tm * d * 4
    row_bufs = 2 if fixed + 2 * per_copy <= V7X_VMEM_LIMIT_BYTES - (4 << 20) else 1
    vmem = fixed + row_bufs * per_copy
    return pl.pallas_call(
        functools.partial(_mm_norm_kernel, nn=nn),
        grid=(m // tm, nn),
        in_specs=[
            pl.BlockSpec((tm, kdim), lambda i, n: (i, 0), pipeline_mode=pl.Buffered(row_bufs)),
            pl.BlockSpec((kdim, tn), lambda i, n: (0, n)),
            pl.BlockSpec((tm, d), lambda i, n: (i, 0), pipeline_mode=pl.Buffered(1)),
            pl.BlockSpec((1, d), lambda i, n: (0, 0)),
        ],
        out_specs=pl.BlockSpec((tm, d), lambda i, n: (i, 0), pipeline_mode=pl.Buffered(row_bufs)),
        out_shape=jax.ShapeDtypeStruct((m, d), F32),
        compiler_params=_cparams(2, vmem),
        name=name,
    )(a, w, x, g_post)


def _gelu_tanh(x):
    return 0.5 * x * (1.0 + jnp.tanh(0.7978845608028654 * (x + 0.044715 * x * x * x)))


def _ffn_up_kernel(h_ref, wg_ref, wv_ref, c0g_ref, c0v_ref, cwg_ref, cwv_ref, cbg_ref, cbv_ref,
                   a_ref, cg_ref, cv_ref, carry_g, carry_v, *, blocks_per_stream, tm):
    i = pl.program_id(1) % blocks_per_stream
    n_str = carry_g.shape[0]
    ts = tm // n_str

    @pl.when(i == 0)
    def _():
        carry_g[...] = c0g_ref[...]
        carry_v[...] = c0v_ref[...]

    h = h_ref[...]
    row = lax.broadcasted_iota(jnp.int32, (tm, 1), 0)

    def conv(u, carry_ref, cw_ref, cb_ref, out_ref):
        u1 = pltpu.roll(u, 1, axis=0)
        u2 = pltpu.roll(u, 2, axis=0)
        for s in range(n_str):
            p = carry_ref[s]
            u1 = jnp.where(row == s * ts, p[1:2], u1)
            u2 = jnp.where(row == s * ts, p[0:1], jnp.where(row == s * ts + 1, p[1:2], u2))
            last = u[(s + 1) * ts - 2:(s + 1) * ts]
            carry_ref[s] = last
            out_ref[s] = last
        cw = cw_ref[...]
        return u2 * cw[0:1] + u1 * cw[1:2] + u * cw[2:3] + cb_ref[...]

    gate = conv(jnp.dot(h, wg_ref[...], preferred_element_type=F32), carry_g, cwg_ref, cbg_ref, cg_ref)
    val = conv(jnp.dot(h, wv_ref[...], preferred_element_type=F32), carry_v, cwv_ref, cbv_ref, cv_ref)
    a_ref[...] = (_gelu_tanh(gate) * val).astype(a_ref.dtype)


def _ffn_up(h, w_up, conv0, conv_w, conv_b, n_streams, tm, tn):
    m, d = h.shape
    f2 = w_up.shape[1]
    f = f2 // 2
    nj = f // tn
    nr = m // tm
    t = m // n_streams
    spb = max(1, tm // t)
    bps = max(1, t // tm)
    isz = jnp.dtype(MXU_DTYPE).itemsize
    vmem = 2 * (tm * d * isz + 2 * d * tn * isz + tm * tn * isz) + 10 * tm * tn * 4 + (4 << 20)
    col = lambda off: (lambda j, r: (0, off + j))
    st = lambda off: (lambda j, r: (r // bps, 0, off + j))
    a, cg, cv = pl.pallas_call(
        functools.partial(_ffn_up_kernel, blocks_per_stream=bps, tm=tm),
        grid=(nj, nr),
        in_specs=[
            pl.BlockSpec((tm, d), lambda j, r: (r, 0)),
            pl.BlockSpec((d, tn), col(0)),
            pl.BlockSpec((d, tn), col(nj)),
            pl.BlockSpec((spb, CONV_W - 1, tn), st(0)),
            pl.BlockSpec((spb, CONV_W - 1, tn), st(nj)),
            pl.BlockSpec((CONV_W, tn), col(0)),
            pl.BlockSpec((CONV_W, tn), col(nj)),
            pl.BlockSpec((1, tn), col(0)),
            pl.BlockSpec((1, tn), col(nj)),
        ],
        out_specs=[
            pl.BlockSpec((tm, tn), lambda j, r: (r, j)),
            pl.BlockSpec((spb, CONV_W - 1, tn), st(0)),
            pl.BlockSpec((spb, CONV_W - 1, tn), st(0)),
        ],
        out_shape=[
            jax.ShapeDtypeStruct((m, f), MXU_DTYPE),
            jax.ShapeDtypeStruct((n_streams, CONV_W - 1, f), F32),
            jax.ShapeDtypeStruct((n_streams, CONV_W - 1, f), F32),
        ],
        scratch_shapes=[pltpu.VMEM((spb, CONV_W - 1, tn), F32)] * 2,
        compiler_params=_cparams(2, vmem),
        name="ffn_up_conv",
    )(h, w_up, w_up, conv0, conv0, conv_w, conv_w, conv_b, conv_b)
    return a, jnp.concatenate([cg, cv], axis=-1)


RW_GROUP = 16 * LANES
RW_SMALL = 4 * LANES


def _softplus(z):
    return jnp.maximum(z, 0.0) + jnp.log(1.0 + jnp.exp(-jnp.abs(z)))


def _sigmoid(z):
    return 1.0 / (1.0 + jnp.exp(-z))


def _dg(a, b, dims):
    return lax.dot_general(_mx(a), _mx(b), (dims, ((), ())), preferred_element_type=F32)


_NN = ((1,), (0,))
_NT = ((1,), (1,))
_TN = ((0,), (0,))


def _segsum(x, seg):
    return jnp.dot(_mx(x), seg, preferred_element_type=F32)


def _each(f, *lists):
    return [f(*args) for args in zip(*lists)]


def _wkv_chunk(r, k, v, kk, b, cum, logd, s_bd, consts):
    m0, m1, tri_mask = consts
    c = r[0].shape[0]
    c2 = 2 * c
    stack = lambda x: jnp.concatenate([x * m0, x * m1], axis=0)
    g_end = _each(lambda cm: jnp.exp(cm[c - 1:c]), cum)
    e_inc = _each(jnp.exp, cum)
    e_prev = _each(lambda cm, ld: jnp.exp(cm - ld), cum, logd)
    e_neg = _each(lambda cm: jnp.exp(-cm), cum)
    e_tail = _each(lambda g, e: g * e, g_end, e_neg)
    vs = _each(stack, v)
    lhs = _each(lambda kk_, r_, ep, ei: jnp.concatenate([stack(kk_ * ep), stack(r_ * ei)], axis=0),
                kk, r, e_prev, e_inc)
    rhs = _each(lambda k_, b_, en: jnp.concatenate([stack(k_ * en), stack(b_ * en)], axis=0), k, b, e_neg)
    tails = _each(lambda k_, b_, et: jnp.concatenate([stack(k_ * et), stack(b_ * et)], axis=0), k, b, e_tail)
    amat = _each(lambda x, y: jnp.where(tri_mask, _dg(x, y, _NT), 0.0), lhs, rhs)
    sp = _each(lambda x, s: _dg(x, s, _NT), lhs, s_bd)
    av = _each(lambda a_, v_: _dg(a_[:, :c2], v_, _NN), amat, vs)
    pw = _each(lambda a_: -a_[:c2, c2:], amat)
    x = _each(lambda s_, a_: s_[:c2] + a_[:c2], sp, av)
    n_lvl = int(np.log2(c))
    for lvl in range(n_lvl - 1):
        z = _each(lambda p_, x_: _dg(p_, jnp.concatenate([p_, x_], axis=1), _NN), pw, x)
        pw = _each(lambda z_: z_[:, :c2], z)
        x = _each(lambda x_, z_: x_ + z_[:, c2:], x, z)
    u = _each(lambda p_, x_: x_ + _dg(p_, x_, _NN), pw, x)
    ys = _each(lambda s_, a_, am, u_: s_[c2:] + a_[c2:] - _dg(am[c2:, c2:], u_, _NN), sp, av, amat, u)
    y = _each(lambda y_: y_[:c] + y_[c:], ys)
    s_new = _each(lambda s, g, v_, u_, t_: s * g + _dg(jnp.concatenate([v_, -u_], axis=0), t_, _TN),
                  s_bd, g_end, vs, u, tails)
    return y, s_new


def _rwkv_kernel(pr_ref, pk_ref, pv_ref, ps_ref, sr_ref, sk_ref, sv_ref, ss_ref, wkv0_ref,
                 mur_ref, muk_ref, muv_ref, mus_ref, wdec_ref, dbias_ref, wa_ref, abias_ref, wg_ref,
                 kk_ref, ka_ref, rk_ref, lnw_ref, lnb_ref,
                 o_ref, wkv1_ref,
                 cr_ref, ck_ref, cv_ref, cs_ref, state_ref, *, n_chunks):
    c = pl.program_id(2)
    C = CHUNK
    npairs = RW_GROUP // LANES

    lane = lax.broadcasted_iota(jnp.int32, (1, LANES), 1)
    m0 = (lane < RW_HEAD).astype(F32)
    m1 = 1.0 - m0
    ri = lax.broadcasted_iota(jnp.int32, (4 * C, 4 * C), 0)
    ci = lax.broadcasted_iota(jnp.int32, (4 * C, 4 * C), 1)
    same_head = ((ri // C) % 2) == ((ci // C) % 2)
    tri_mask = same_head & ((ri % C) + ri // (2 * C) > (ci % C))
    li = lax.broadcasted_iota(jnp.int32, (LANES, LANES), 0)
    lj = lax.broadcasted_iota(jnp.int32, (LANES, LANES), 1)
    seg = ((li // RW_HEAD) == (lj // RW_HEAD)).astype(MXU_DTYPE)
    consts = (m0, m1, tri_mask)

    @pl.when(c == 0)
    def _():
        cr_ref[...] = sr_ref[...]
        ck_ref[...] = sk_ref[...]
        cv_ref[...] = sv_ref[...]
        cs_ref[...] = ss_ref[...]
        z = jnp.zeros((RW_HEAD, RW_HEAD), F32)
        for p in range(npairs):
            s0 = wkv0_ref[2 * p]
            s1 = wkv0_ref[2 * p + 1]
            state_ref[p] = jnp.concatenate(
                [jnp.concatenate([s0, z], axis=1), jnp.concatenate([z, s1], axis=1)], axis=0)

    row = lax.broadcasted_iota(jnp.int32, (C, 1), 0)

    def shifted(p_ref, carry_ref, mu_ref):
        p = p_ref[...]
        prev = jnp.where(row == 0, carry_ref[...], pltpu.roll(p, 1, axis=0))
        carry_ref[...] = p[C - 1:C]
        return p + (prev - p) * mu_ref[...]

    r = shifted(pr_ref, cr_ref, mur_ref)
    k = shifted(pk_ref, ck_ref, muk_ref)
    v = shifted(pv_ref, cv_ref, muv_ref)
    sm = shifted(ps_ref, cs_ref, mus_ref)
    wd, ad, gd = sm[:, 0:LANES], sm[:, LANES:2 * LANES], sm[:, 2 * LANES:4 * LANES]

    dec_in = dbias_ref[...] + jnp.dot(_mx(jnp.tanh(wd)), wdec_ref[...], preferred_element_type=F32)
    w_log = -_softplus(-dec_in) - 0.5
    logd = -jnp.exp(w_log)
    a = _sigmoid(abias_ref[...] + jnp.dot(_mx(ad), wa_ref[...], preferred_element_type=F32))
    g = jnp.dot(_mx(_sigmoid(gd)), wg_ref[...], preferred_element_type=F32)
    kk = k * kk_ref[...]
    k2 = k * (1.0 + (a - 1.0) * ka_ref[...])
    rkr = r * k2 * rk_ref[...]
    cum = logd
    for sft in (1, 2, 4, 8, 16, 32):
        cum = cum + jnp.where(row >= sft, pltpu.roll(cum, sft, axis=0), 0.0)

    pairs = lambda x: [x[:, p * LANES:(p + 1) * LANES] for p in range(npairs)]
    r_p, k_p, v_p, a_p, cum_p, logd_p = pairs(r), pairs(k2), pairs(v), pairs(a), pairs(cum), pairs(logd)
    sums = _each(lambda kk_, rkr_: _segsum(jnp.concatenate([kk_ * kk_, rkr_], axis=0), seg), pairs(kk), pairs(rkr))
    kk_p = _each(lambda kk_, s_: kk_ * lax.rsqrt(jnp.maximum(s_[:C], 1e-24)), pairs(kk), sums)
    b_p = _each(lambda kk_, a_: kk_ * a_, kk_p, a_p)
    y_p, s_new = _wkv_chunk(r_p, k_p, v_p, kk_p, b_p, cum_p, logd_p, [state_ref[p] for p in range(npairs)], consts)
    for p in range(npairs):
        state_ref[p] = s_new[p]
    yc = _each(lambda y_: y_ - _segsum(y_, seg) * (1.0 / RW_HEAD), y_p)
    var = _each(lambda yc_: _segsum(yc_ * yc_, seg) * (1.0 / RW_HEAD), yc)
    for p in range(npairs):
        sl = slice(p * LANES, (p + 1) * LANES)
        yn = yc[p] * lax.rsqrt(var[p] + LNX_EPS) * lnw_ref[:, sl] + lnb_ref[:, sl]
        o_ref[:, sl] = ((yn + sums[p][C:] * v_p[p]) * g[:, sl]).astype(o_ref.dtype)

    @pl.when(c == n_chunks - 1)
    def _():
        for p in range(npairs):
            s = state_ref[p]
            wkv1_ref[2 * p] = s[:RW_HEAD, :RW_HEAD]
            wkv1_ref[2 * p + 1] = s[RW_HEAD:, RW_HEAD:]


def _rwkv(pfull, cols, shift0, wkv0, w, n_streams, t):
    m = pfull.shape[0]
    rw = w["k_k"].shape[1]
    ng = rw // RW_GROUP
    nc = t // CHUNK
    hg = RW_GROUP // RW_HEAD
    rowblk = lambda off: (lambda b, g, c: (b * nc + c, off + g))
    fixed = lambda off: (lambda b, g, c: (b * nc + c, off))
    st = lambda b, g, c: (b, 0, g)
    st0 = lambda b, g, c: (b, 0, 0)
    wcol = lambda b, g, c: (0, g)
    in_specs = [
        pl.BlockSpec((CHUNK, RW_GROUP), rowblk(cols["r"])),
        pl.BlockSpec((CHUNK, RW_GROUP), rowblk(cols["k"])),
        pl.BlockSpec((CHUNK, RW_GROUP), rowblk(cols["v"])),
        pl.BlockSpec((CHUNK, RW_SMALL), fixed(cols["small"])),
        pl.BlockSpec((None, 1, RW_GROUP), st),
        pl.BlockSpec((None, 1, RW_GROUP), st),
        pl.BlockSpec((None, 1, RW_GROUP), st),
        pl.BlockSpec((None, 1, RW_SMALL), st0),
        pl.BlockSpec((None, hg, RW_HEAD, RW_HEAD), lambda b, g, c: (b, g, 0, 0)),
        pl.BlockSpec((1, RW_GROUP), wcol), pl.BlockSpec((1, RW_GROUP), wcol), pl.BlockSpec((1, RW_GROUP), wcol),
        pl.BlockSpec((1, RW_SMALL), lambda b, g, c: (0, 0)),
        pl.BlockSpec((LANES, RW_GROUP), wcol), pl.BlockSpec((1, RW_GROUP), wcol),
        pl.BlockSpec((LANES, RW_GROUP), wcol), pl.BlockSpec((1, RW_GROUP), wcol),
        pl.BlockSpec((2 * LANES, RW_GROUP), wcol),
        pl.BlockSpec((1, RW_GROUP), wcol), pl.BlockSpec((1, RW_GROUP), wcol), pl.BlockSpec((1, RW_GROUP), wcol),
        pl.BlockSpec((1, RW_GROUP), wcol), pl.BlockSpec((1, RW_GROUP), wcol),
    ]
    o, wkv1 = pl.pallas_call(
        functools.partial(_rwkv_kernel, n_chunks=nc),
        grid=(n_streams, ng, nc),
        in_specs=in_specs,
        out_specs=[
            pl.BlockSpec((CHUNK, RW_GROUP), lambda b, g, c: (b * nc + c, g)),
            pl.BlockSpec((None, hg, RW_HEAD, RW_HEAD), lambda b, g, c: (b, g, 0, 0)),
        ],
        out_shape=[
            jax.ShapeDtypeStruct((m, rw), MXU_DTYPE),
            jax.ShapeDtypeStruct((n_streams, rw // RW_HEAD, RW_HEAD, RW_HEAD), F32),
        ],
        scratch_shapes=[pltpu.VMEM((1, RW_GROUP), F32)] * 3 + [pltpu.VMEM((1, RW_SMALL), F32)]
        + [pltpu.VMEM((RW_GROUP // LANES, LANES, LANES), F32)],
        compiler_params=_cparams(3, 32 << 20),
        name="rwkv7_chunked",
    )(pfull, pfull, pfull, pfull, shift0["r"], shift0["k"], shift0["v"], shift0["small"], wkv0,
      w["mu_r"], w["mu_k"], w["mu_v"], w["mu_small"], w["w_decay_up"], w["decay_bias"], w["w_a_up"], w["a_bias"],
      w["w_gate_up"], w["k_k"], w["k_a"], w["r_k"], w["lnx_w"], w["lnx_b"])
    return o, wkv1


def _rope_kernel(q_ref, k_ref, qi_ref, kiw_ref, ca_ref, sa_ref, cb_ref, sb1_ref, sb2_ref,
                 qo_ref, ko_ref, qio_ref, kiwo_ref):
    ca, sa = ca_ref[...], sa_ref[...]
    cb, sb1, sb2 = cb_ref[...], sb1_ref[...], sb2_ref[...]

    def rope_head(x):
        return x * ca + pltpu.roll(x, SA_HEAD // 2, axis=1) * sa

    def rope_idx(x):
        return (x * cb + pltpu.roll(x, LANES - IDX_DIM // 2, axis=1) * sb1
                + pltpu.roll(x, IDX_DIM // 2, axis=1) * sb2)

    q_scale = float(SA_HEAD ** -0.5 * np.log2(np.e))
    for h in range(q_ref.shape[1] // LANES):
        sl = slice(h * LANES, (h + 1) * LANES)
        qo_ref[:, sl] = (rope_head(q_ref[:, sl]) * q_scale).astype(qo_ref.dtype)
    for h in range(k_ref.shape[1] // LANES):
        sl = slice(h * LANES, (h + 1) * LANES)
        ko_ref[:, sl] = rope_head(k_ref[:, sl])
    for hp in range(qi_ref.shape[1] // LANES):
        sl = slice(hp * LANES, (hp + 1) * LANES)
        qio_ref[:, sl] = rope_idx(qi_ref[:, sl]).astype(qio_ref.dtype)
    x = kiw_ref[...]
    lane = lax.broadcasted_iota(jnp.int32, (1, LANES), 1)
    kiwo_ref[...] = jnp.where(lane < IDX_DIM, rope_idx(x), x * float((IDX_HEADS * IDX_DIM) ** -0.5))


def _rope_tables(pos):
    def ang(half):
        inv = ROPE_THETA ** (-jnp.arange(half, dtype=F32) / half)
        return pos.astype(F32)[:, None] * inv[None, :]
    aa = ang(SA_HEAD // 2)
    ca = jnp.concatenate([jnp.cos(aa), jnp.cos(aa)], axis=1)
    sa = jnp.concatenate([-jnp.sin(aa), jnp.sin(aa)], axis=1)
    ab = ang(IDX_DIM // 2)
    z = jnp.zeros_like(ab)
    cb = jnp.concatenate([jnp.cos(ab)] * 4, axis=1)
    sb1 = jnp.concatenate([-jnp.sin(ab), z, -jnp.sin(ab), z], axis=1)
    sb2 = jnp.concatenate([z, jnp.sin(ab), z, jnp.sin(ab)], axis=1)
    return ca, sa, cb, sb1, sb2


def _rope(pfull, cols, tables, n_streams, t, tm, sa_w, kv_w, qi_w):
    m = pfull.shape[0]
    bps = t // tm
    tab = pl.BlockSpec((tm, LANES), lambda r: (r % bps, 0))
    return pl.pallas_call(
        _rope_kernel,
        grid=(m // tm,),
        in_specs=[
            pl.BlockSpec((tm, sa_w), lambda r: (r, cols["q"])),
            pl.BlockSpec((tm, kv_w), lambda r: (r, cols["ksa"])),
            pl.BlockSpec((tm, qi_w), lambda r: (r, cols["qi"])),
            pl.BlockSpec((tm, LANES), lambda r: (r, cols["kiw"])),
            tab, tab, tab, tab, tab,
        ],
        out_specs=[
            pl.BlockSpec((tm, sa_w), lambda r: (r, 0)),
            pl.BlockSpec((tm, kv_w), lambda r: (r, 0)),
            pl.BlockSpec((tm, qi_w), lambda r: (r, 0)),
            pl.BlockSpec((tm, LANES), lambda r: (r, 0)),
        ],
        out_shape=[
            jax.ShapeDtypeStruct((m, sa_w), MXU_DTYPE),
            jax.ShapeDtypeStruct((m, kv_w), F32),
            jax.ShapeDtypeStruct((m, qi_w), MXU_DTYPE),
            jax.ShapeDtypeStruct((m, LANES), F32),
        ],
        compiler_params=_cparams(1, 32 << 20),
        name="rope",
    )(pfull, pfull, pfull, pfull, *tables)


IDX_HEAD_GROUP = 4
IDX_SCORE_ROWS = 128
IDX_VALUE_PASSES = 16


FOLD_CHAINS = 2


def _fold_sublane_tiles(x, op):
    parts = [x[r:r + 8] for r in range(0, x.shape[0], 8)]
    chains = [functools.reduce(op, parts[c::FOLD_CHAINS]) for c in range(min(FOLD_CHAINS, len(parts)))]
    return functools.reduce(op, chains)


def _f32_to_key(x):
    bits = lax.bitcast_convert_type(x, jnp.int32)
    return bits ^ ((bits >> 31) & jnp.int32(0x7FFFFFFF))


def _key_to_f32(key):
    return lax.bitcast_convert_type(key ^ ((key >> 31) & jnp.int32(0x7FFFFFFF)), F32)


def _index_kernel(qit_ref, w_ref, ki_ref, mask_ref, keys_ref, *, tq, kb, nkb, past, topk):
    i = pl.program_id(1)
    n_adm = jnp.minimum(nkb, (past + (i + 1) * tq + kb - 1) // kb)
    chunk_bits = CHUNK.bit_length() - 1
    qpos = past + i * tq + lax.broadcasted_iota(jnp.int32, (1, tq), 1)
    lim = ((qpos >> chunk_bits) + 1) << chunk_bits
    krow = lax.broadcasted_iota(jnp.int32, (kb, 1), 0)
    hg = IDX_HEAD_GROUP
    fold = _fold_sublane_tiles

    def score_block(j, carry):
        smax, smin = carry
        for r0 in range(0, kb, IDX_SCORE_ROWS):
            rows = slice(r0, r0 + IDX_SCORE_ROWS)
            kblk = ki_ref[j, rows, :]
            acc = jnp.zeros((IDX_SCORE_ROWS, tq), F32)
            for g0 in range(0, IDX_HEADS, hg):
                s = jnp.dot(kblk, qit_ref[:, g0 * tq:(g0 + hg) * tq], preferred_element_type=F32)
                for g in range(hg):
                    acc = acc + jnp.maximum(s[:, g * tq:(g + 1) * tq], 0.0) * w_ref[g0 + g:g0 + g + 1, :]
            score = acc + 0.0
            adm = j * kb + r0 + krow[:IDX_SCORE_ROWS] < lim
            keys_ref[j, rows, :] = jnp.where(adm, _f32_to_key(score), jnp.int32(INT_MIN))
            smax = jnp.maximum(smax, fold(jnp.where(adm, score, -jnp.inf), jnp.maximum))
            smin = jnp.minimum(smin, fold(jnp.where(adm, score, jnp.inf), jnp.minimum))
        return smax, smin

    smax, smin = lax.fori_loop(0, n_adm, score_block,
                               (jnp.full((8, tq), -jnp.inf, F32), jnp.full((8, tq), jnp.inf, F32)))
    lo0 = _f32_to_key(jnp.min(smin, axis=0, keepdims=True))
    hi0 = _f32_to_key(jnp.max(smax, axis=0, keepdims=True)) + 1

    def count_ge(thr):
        def body(j, cnt):
            return cnt + fold(jnp.where(keys_ref[j] >= thr, 1.0, 0.0), jnp.add)
        return jnp.sum(lax.fori_loop(0, n_adm, body, jnp.zeros((8, tq), F32)), axis=0, keepdims=True)

    def midpoint(lo, hi):
        return (lo >> 1) + (hi >> 1) + (lo & hi & 1)

    def unfinished(lo, hi, cnt_lo):
        active = (midpoint(lo, hi) != lo) & (cnt_lo != float(topk))
        return jnp.max(jnp.where(active, 1.0, 0.0))

    def bisect(state):
        p, lo, hi, cnt_lo, cnt_hi, _ = state
        mid_v = _f32_to_key(0.5 * _key_to_f32(lo) + 0.5 * _key_to_f32(hi))
        by_value = (p < IDX_VALUE_PASSES) & (mid_v > lo) & (mid_v < hi)
        mid = jnp.where(by_value, mid_v, midpoint(lo, hi))
        cnt = count_ge(mid)
        ge = cnt >= float(topk)
        lo, cnt_lo = jnp.where(ge, mid, lo), jnp.where(ge, cnt, cnt_lo)
        hi, cnt_hi = jnp.where(ge, hi, mid), jnp.where(ge, cnt_hi, cnt)
        return p + 1, lo, hi, cnt_lo, cnt_hi, unfinished(lo, hi, cnt_lo)

    cnt_lo0 = lim.astype(F32)
    state = (jnp.int32(0), lo0, hi0, cnt_lo0, jnp.zeros((1, tq), F32), unfinished(lo0, hi0, cnt_lo0))
    _, thr, _, n_ge, n_gt, _ = lax.while_loop(lambda st: st[5] > 0.5, bisect, state)

    tied = n_ge > float(topk)
    need = float(topk) - n_gt
    end_all = jnp.int32(nkb * kb)

    def count_tied_below(bound):
        def body(j, cnt):
            hit = (keys_ref[j] == thr) & (j * kb + krow < bound)
            return cnt + fold(jnp.where(hit, 1.0, 0.0), jnp.add)
        return jnp.sum(lax.fori_loop(0, n_adm, body, jnp.zeros((8, tq), F32)), axis=0, keepdims=True)

    def tie_bisect(state):
        p, lo, hi = state
        mid = (lo + hi) >> 1
        enough = count_tied_below(mid) >= need
        return p + 1, jnp.where(enough, lo, mid), jnp.where(enough, mid, hi)

    any_tied = jnp.max(jnp.where(tied, 1.0, 0.0))
    n_steps = (nkb * kb).bit_length()
    tie_state = (jnp.int32(0), jnp.zeros((1, tq), jnp.int32), jnp.full((1, tq), end_all, jnp.int32))
    tie_hi = lax.while_loop(lambda st: (any_tied > 0.5) & (st[0] < n_steps), tie_bisect, tie_state)[2]
    tie_end = jnp.where(tied, tie_hi, end_all)

    def write(j, carry):
        k = keys_ref[j]
        keep = (k > thr) | ((k == thr) & (j * kb + krow < tie_end))
        mask_ref[j] = jnp.where(keep, 1.0, 0.0).T.astype(mask_ref.dtype)
        return carry

    lax.fori_loop(0, n_adm, write, 0)

    def clear(j, carry):
        mask_ref[j] = jnp.zeros((tq, kb), mask_ref.dtype)
        return carry

    lax.fori_loop(n_adm, nkb, clear, 0)


def _index_mask(qit, w, ki, n_streams, t, tq, kb, past, topk):
    nkb = ki.shape[1]
    nq = t // tq
    isz = jnp.dtype(MXU_DTYPE).itemsize
    vmem = (nkb * kb * tq * 4 + 2 * nkb * kb * tq + 2 * nkb * kb * LANES * isz + 2 * IDX_DIM * IDX_HEADS * tq * isz
            + 6 * kb * IDX_HEAD_GROUP * tq * 4 + (8 << 20))
    return pl.pallas_call(
        functools.partial(_index_kernel, tq=tq, kb=kb, nkb=nkb, past=past, topk=topk),
        grid=(n_streams, nq),
        in_specs=[
            pl.BlockSpec((None, None, IDX_DIM, IDX_HEADS * tq), lambda b, i: (b, i, 0, 0)),
            pl.BlockSpec((None, IDX_HEADS, tq), lambda b, i: (b, 0, i)),
            pl.BlockSpec((None, nkb, kb, IDX_DIM), lambda b, i: (b, 0, 0, 0)),
        ],
        out_specs=pl.BlockSpec((None, nkb, tq, kb), lambda b, i: (b, 0, i, 0)),
        out_shape=jax.ShapeDtypeStruct((n_streams, nkb, t, kb), jnp.int8),
        scratch_shapes=[pltpu.VMEM((nkb, kb, tq), jnp.int32)],
        compiler_params=_cparams(2, vmem),
        name="indexer_topk_mask",
    )(qit, w, ki)


def _attn_kernel(it_ref, jt_ref, last_ref, q_ref, k_ref, v_ref, m_ref, o_ref, qs_ref, mx_ref, l_ref, acc_ref,
                 *, group):
    step = pl.program_id(1)
    tq = q_ref.shape[0]
    tk = k_ref.shape[0]
    n_kv = k_ref.shape[1] // SA_HEAD
    rows = group * tq
    n_lane_tiles = tk // LANES

    @pl.when(jt_ref[step] == 0)
    def _():
        for n in range(n_kv):
            for g in range(group):
                h = n * group + g
                qs_ref[n, g * tq:(g + 1) * tq, :] = q_ref[:, h * SA_HEAD:(h + 1) * SA_HEAD]
        mx_ref[...] = jnp.full(mx_ref.shape, NEG_BIG, F32)
        l_ref[...] = jnp.zeros(l_ref.shape, F32)
        acc_ref[...] = jnp.zeros(acc_ref.shape, F32)

    bias = jnp.where(m_ref[...].astype(F32) > 0.0, 0.0, NEG_BIG)
    kv = range(n_kv)
    s = [_nt(qs_ref[n], k_ref[:, n * SA_HEAD:(n + 1) * SA_HEAD]) for n in kv]
    s = [(x.reshape(group, tq, tk) + bias[None]).reshape(rows, tk) for x in s]
    tiles = [[x[:, c * LANES:(c + 1) * LANES] for c in range(n_lane_tiles)] for x in s]
    m_old = [mx_ref[n] for n in kv]
    m_new = [jnp.maximum(mo, jnp.max(functools.reduce(jnp.maximum, t), axis=1, keepdims=True))
             for mo, t in zip(m_old, tiles)]
    alpha = [jnp.exp2(mo - mn) for mo, mn in zip(m_old, m_new)]
    p = [[jnp.exp2(x - mn) for x in t] for t, mn in zip(tiles, m_new)]
    for n in kv:
        mx_ref[n] = m_new[n]
        l_ref[n] = alpha[n] * l_ref[n] + functools.reduce(jnp.add, p[n])
    pv = [jnp.dot(_mx(jnp.concatenate(p[n], axis=1)), v_ref[:, n * SA_HEAD:(n + 1) * SA_HEAD],
                  preferred_element_type=F32) for n in kv]
    for n in kv:
        acc_ref[n] = alpha[n] * acc_ref[n] + pv[n]

    @pl.when(last_ref[step] == 1)
    def _():
        for n in range(n_kv):
            o = acc_ref[n] / jnp.sum(l_ref[n], axis=1, keepdims=True)
            for g in range(group):
                h = n * group + g
                o_ref[:, h * SA_HEAD:(h + 1) * SA_HEAD] = o[g * tq:(g + 1) * tq].astype(o_ref.dtype)


def _attn_tiles(t, tq, tk, nkb, past):
    ii, jj, last = [], [], []
    for i in range(t // tq):
        n_adm = min(nkb, -(-(past + (i + 1) * tq) // tk))
        for j in range(n_adm):
            ii.append(i)
            jj.append(j)
            last.append(int(j == n_adm - 1))
    return tuple(jnp.asarray(np.array(x, np.int32)) for x in (ii, jj, last))


def _attention(q, k_all, v_all, mask, n_streams, t, tq, tk, past):
    m, sa_w = q.shape
    kv_w = k_all.shape[2]
    nkb = mask.shape[1]
    nq = t // tq
    n_heads = sa_w // SA_HEAD
    group = n_heads // SA_KV_HEADS
    it, jt, last = _attn_tiles(t, tq, tk, nkb, past)
    grid_spec = pltpu.PrefetchScalarGridSpec(
        num_scalar_prefetch=3,
        grid=(n_streams, int(it.shape[0])),
        in_specs=[
            pl.BlockSpec((tq, sa_w), lambda b, s, it, jt, lt: (b * nq + it[s], 0)),
            pl.BlockSpec((None, tk, kv_w), lambda b, s, it, jt, lt: (b, jt[s], 0)),
            pl.BlockSpec((None, tk, kv_w), lambda b, s, it, jt, lt: (b, jt[s], 0)),
            pl.BlockSpec((None, None, tq, tk), lambda b, s, it, jt, lt: (b, jt[s], it[s], 0)),
        ],
        out_specs=pl.BlockSpec((tq, sa_w), lambda b, s, it, jt, lt: (b * nq + it[s], 0)),
        scratch_shapes=[pltpu.VMEM((SA_KV_HEADS, group * tq, SA_HEAD), MXU_DTYPE)]
        + [pltpu.VMEM((SA_KV_HEADS, group * tq, SA_HEAD), F32)] * 3,
    )
    return pl.pallas_call(
        functools.partial(_attn_kernel, group=group),
        grid_spec=grid_spec,
        out_shape=jax.ShapeDtypeStruct((m, sa_w), MXU_DTYPE),
        compiler_params=_cparams(2, 40 << 20),
        name="masked_flash_attention",
    )(it, jt, last, q, k_all, v_all, mask)


def _prepare_weights(w_in, mu_shift, w_decay_up, decay_bias, w_a_up, a_bias, w_gate_up, k_k, k_a, r_k,
                     lnx_w, lnx_b, w_out, w_up, conv_w, conv_b, w_down, d_model):
    rw = w_decay_up.shape[1]
    rw_cols = 3 * rw + D_DECAY + D_AAA + D_GATE
    sa_w = d_model - rw
    kv_w = SA_KV_HEADS * SA_HEAD
    qi_w = IDX_HEADS * IDX_DIM
    o = {}
    o["r"] = 0
    o["wd"] = rw
    o["k"] = rw + D_DECAY
    o["v"] = 2 * rw + D_DECAY
    o["ad"] = 3 * rw + D_DECAY
    o["gd"] = 3 * rw + D_DECAY + D_AAA
    o["q"] = rw_cols
    o["ksa"] = rw_cols + sa_w
    o["vsa"] = o["ksa"] + kv_w
    o["qi"] = o["vsa"] + kv_w
    o["kiw"] = o["qi"] + qi_w
    kiw_w = IDX_DIM + IDX_HEADS

    def seg(x, name, width, pad_to=None):
        s = x[..., o[name]:o[name] + width]
        if pad_to is not None and pad_to > width:
            s = jnp.pad(s, [(0, 0)] * (s.ndim - 1) + [(0, pad_to - width)])
        return s

    def rw_small(x):
        return jnp.concatenate([seg(x, "wd", D_DECAY, LANES), seg(x, "ad", D_AAA, LANES), seg(x, "gd", D_GATE)], -1)

    w_in_l = jnp.concatenate([
        seg(w_in, "r", rw), seg(w_in, "k", rw), seg(w_in, "v", rw), seg(w_in, "q", sa_w), seg(w_in, "qi", qi_w),
        seg(w_in, "ksa", kv_w), seg(w_in, "vsa", kv_w), rw_small(w_in), seg(w_in, "kiw", kiw_w, LANES)], axis=1)
    w_in_l = jnp.pad(w_in_l, ((0, 0), (0, -w_in_l.shape[1] % IN_PROJ_TN)))
    starts = {"r": 0, "k": rw, "v": 2 * rw, "q": 3 * rw, "qi": 3 * rw + sa_w}
    starts["ksa"] = starts["qi"] + qi_w
    starts["vsa"] = starts["ksa"] + kv_w
    starts["small"] = starts["vsa"] + kv_w
    starts["kiw"] = starts["small"] + 4 * LANES
    pad_rows = lambda x, n: jnp.pad(x, ((0, n - x.shape[0]), (0, 0)))
    row = lambda x: x.reshape(1, -1).astype(F32)
    mu = mu_shift.reshape(1, -1)
    w = {
        "w_in": _mx(w_in_l),
        "mu_r": seg(mu, "r", rw), "mu_k": seg(mu, "k", rw), "mu_v": seg(mu, "v", rw), "mu_small": rw_small(mu),
        "w_decay_up": _mx(pad_rows(w_decay_up, LANES)), "decay_bias": row(decay_bias),
        "w_a_up": _mx(pad_rows(w_a_up, LANES)), "a_bias": row(a_bias),
        "w_gate_up": _mx(w_gate_up),
        "k_k": row(k_k), "k_a": row(k_a), "r_k": row(r_k), "lnx_w": row(lnx_w), "lnx_b": row(lnx_b),
        "w_out": _mx(w_out), "w_up": _mx(w_up), "conv_w": conv_w, "conv_b": row(conv_b), "w_down": _mx(w_down),
    }
    dims = {"rw": rw, "rw_cols": rw_cols, "sa_w": sa_w, "kv_w": kv_w, "qi_w": qi_w, "orig": o, "starts": starts,
            "seg": seg, "rw_small": rw_small}
    return w, dims


def _layer(x, past_k, past_v, past_ki, wkv0, shift0, conv0, norms, w, dims, tiles):
    bsz, t, d = x.shape
    past = past_k.shape[1]
    m = bsz * t
    rw, sa_w, kv_w, qi_w = dims["rw"], dims["sa_w"], dims["kv_w"], dims["qi_w"]
    starts, seg, rw_small = dims["starts"], dims["seg"], dims["rw_small"]
    assert t % CHUNK == 0 and past % CHUNK == 0
    tm, tq, kb = tiles["tm"], tiles["tq"], tiles["kb"]
    g_mix_pre, g_mix_post, g_ffn_pre, g_ffn_post = norms

    x2 = x.reshape(m, d)
    h1 = _rmsnorm(x2, g_mix_pre, tiles["tm_norm"])
    pfull = _matmul(h1, w["w_in"], tiles["tm_in"], tiles["tn_in"])

    sh = shift0.reshape(bsz, 1, -1)
    shift_l = {"r": seg(sh, "r", rw), "k": seg(sh, "k", rw), "v": seg(sh, "v", rw), "small": rw_small(sh)}
    cols_rw = {n: starts[n] // RW_GROUP for n in ("r", "k", "v")}
    cols_rw["small"] = starts["small"] // RW_SMALL
    o_rw, wkv1 = _rwkv(pfull, cols_rw, shift_l, wkv0, w, bsz, t)

    pos = past + jnp.arange(t, dtype=jnp.int32)
    cols_sa = {"q": starts["q"] // sa_w, "ksa": starts["ksa"] // kv_w, "qi": starts["qi"] // qi_w,
               "kiw": starts["kiw"] // LANES}
    q_r, k_r, qi_r, kiw_r = _rope(pfull, cols_sa, _rope_tables(pos), bsz, t, tq, sa_w, kv_w, qi_w)
    k_new = k_r.reshape(bsz, t, kv_w)
    v_new = pfull[:, starts["vsa"]:starts["vsa"] + kv_w].reshape(bsz, t, kv_w)
    ki_new = kiw_r[:, :IDX_DIM].reshape(bsz, t, IDX_DIM)
    l_all = past + t
    lp = -(-l_all // kb) * kb
    nkb = lp // kb
    cat = lambda old, new: jnp.pad(
        jnp.concatenate([_mx(old.reshape(bsz, past, new.shape[-1])), _mx(new)], axis=1),
        ((0, 0), (0, lp - l_all), (0, 0)))
    k_all, v_all, ki_all = cat(past_k, k_new), cat(past_v, v_new), cat(past_ki, ki_new)
    nq = t // tq
    qit = (qi_r.reshape(bsz, nq, tq, IDX_HEADS, IDX_DIM).transpose(0, 1, 4, 3, 2)
           .reshape(bsz, nq, IDX_DIM, IDX_HEADS * tq))
    w_idx = kiw_r[:, IDX_DIM:IDX_DIM + IDX_HEADS].reshape(bsz, t, IDX_HEADS).transpose(0, 2, 1)
    topk = min(TOPK_MAX, l_all // 4)
    mask = _index_mask(qit, w_idx, ki_all.reshape(bsz, nkb, kb, IDX_DIM), bsz, t, tq, kb, past, topk)
    o_sa = _attention(q_r, k_all, v_all, mask, bsz, t, tq, kb, past)

    mix_in = jnp.concatenate([o_rw, o_sa], axis=1)
    x1 = _matmul_norm_residual(mix_in, w["w_out"], x2, g_mix_post, tm, tiles["tn_out"], "out_proj")
    h2 = _rmsnorm(x1, g_ffn_pre, tiles["tm_norm"])
    act, conv1 = _ffn_up(h2, w["w_up"], conv0, w["conv_w"], w["conv_b"], bsz, tiles["tm_up"], tiles["tn_up"])
    x_out = _matmul_norm_residual(act, w["w_down"], x1, g_ffn_post, tm, tiles["tn_out"], "ffn_down")

    orig = dims["orig"]
    last = pfull.reshape(bsz, t, -1)[:, t - 1:t]
    sm = last[..., starts["small"]:starts["small"] + 4 * LANES]
    shift1 = jnp.concatenate([
        last[..., starts["r"]:starts["r"] + rw], sm[..., :D_DECAY],
        last[..., starts["k"]:starts["k"] + rw], last[..., starts["v"]:starts["v"] + rw],
        sm[..., LANES:LANES + D_AAA], sm[..., 2 * LANES:2 * LANES + D_GATE]], axis=-1)
    del orig
    new = (k_new.reshape(bsz, t, SA_KV_HEADS, SA_HEAD), v_new.reshape(bsz, t, SA_KV_HEADS, SA_HEAD), ki_new,
           wkv1, shift1, conv1)
    return x_out.reshape(bsz, t, d), new


def _tiles(bsz, t):
    big = t >= 1024
    return {
        "tm_norm": 256 if big else CHUNK,
        "tm_in": 1024 if big else bsz * t,
        "tn_in": IN_PROJ_TN,
        "tm": 512 if big else bsz * t,
        "tn_out": 512,
        "tm_up": 1024 if big else bsz * t,
        "tn_up": 512,
        "tq": 256 if big else CHUNK,
        "kb": 1024 if big else 512,
    }


def _run_stream(x, past_k, past_v, past_ki, wkv0, shift0, conv0, norm_w, weights, dims, depth):
    outs = []
    tiles = _tiles(x.shape[0], x.shape[1])
    for l in range(depth):
        w_l = {n: v[l] for n, v in weights.items()}
        norms = tuple(g[l].reshape(1, -1) for g in norm_w)
        x, st = _layer(x, past_k[l], past_v[l], past_ki[l], wkv0[l], shift0[l], conv0[l], norms, w_l, dims[l], tiles)
        outs.append(st)
    return x, [jnp.stack(s) for s in zip(*outs)]


def kernel(x_prompt, x_sample, cache_k, cache_v, cache_kidx, state_wkv, state_shift, state_conv, norm_mix_pre, norm_mix_post, norm_ffn_pre, norm_ffn_post, w_in, mu_shift, w_decay_up, decay_bias, w_a_up, a_bias, w_gate_up, k_k, k_a, r_k, lnx_w, lnx_b, w_out, w_up, conv_w, conv_b, w_down):
    depth, d_model = norm_mix_pre.shape
    per_layer = [_prepare_weights(w_in[l], mu_shift[l], w_decay_up[l], decay_bias[l], w_a_up[l], a_bias[l],
                                  w_gate_up[l], k_k[l], k_a[l], r_k[l], lnx_w[l], lnx_b[l], w_out[l], w_up[l],
                                  conv_w[l], conv_b[l], w_down[l], d_model) for l in range(depth)]
    weights = {n: [pw[0][n] for pw in per_layer] for n in per_layer[0][0]}
    dims = [pw[1] for pw in per_layer]
    norm_w = (norm_mix_pre, norm_mix_post, norm_ffn_pre, norm_ffn_post)

    dt = x_prompt.dtype
    bp = x_prompt.shape[0]
    rw_heads = state_wkv.shape[2]
    zk = jnp.zeros((depth, bp, 0, SA_KV_HEADS, SA_HEAD), dt)
    zki = jnp.zeros((depth, bp, 0, IDX_DIM), dt)
    zwkv = jnp.zeros((depth, bp, rw_heads, RW_HEAD, RW_HEAD), dt)
    zshift = jnp.zeros((depth, bp, 1, state_shift.shape[-1]), dt)
    zconv = jnp.zeros((depth, bp, CONV_W - 1, state_conv.shape[-1]), dt)
    y_prompt, p_new = _run_stream(x_prompt, zk, zk, zki, zwkv, zshift, zconv, norm_w, weights, dims, depth)
    y_sample, s_new = _run_stream(x_sample, cache_k, cache_v, cache_kidx, state_wkv, state_shift, state_conv,
                                  norm_w, weights, dims, depth)
    return (y_prompt, y_sample, *p_new, *s_new)
```

```python
import functools

import numpy as np
import jax
import jax.numpy as jnp
from jax import lax
from jax.experimental import pallas as pl
from jax.experimental.pallas import tpu as pltpu

F32 = jnp.float32
MXU_DTYPE = jnp.bfloat16
HI = lax.Precision.HIGHEST

CHUNK = 64
RW_HEAD = 64
D_DECAY = 96
D_AAA = 96
D_GATE = 256
SA_HEAD = 128
SA_KV_HEADS = 4
IDX_HEADS = 16
IDX_DIM = 64
TOPK_MAX = 256
CONV_W = 3
ROPE_THETA = 10000.0
NORM_EPS = 1e-6
LNX_EPS = 64e-5

LANES = 128
V7X_VMEM_LIMIT_BYTES = 60000 * 1024
INT_MIN = -(2 ** 31)
NEG_BIG = -1e30


def _cparams(n_grid, vmem_bytes):
    limit = int(min(V7X_VMEM_LIMIT_BYTES, max(32 * 1024 * 1024, vmem_bytes)))
    return pltpu.CompilerParams(dimension_semantics=("arbitrary",) * n_grid, vmem_limit_bytes=limit)


def _nt(a, b, precision=None):
    return lax.dot_general(a, b, (((1,), (1,)), ((), ())), precision=precision, preferred_element_type=F32)


def _tn(a, b, precision=None):
    return lax.dot_general(a, b, (((0,), (0,)), ((), ())), precision=precision, preferred_element_type=F32)


def _mx(x):
    return x.astype(MXU_DTYPE)


def _rmsnorm_kernel(x_ref, g_ref, o_ref):
    x = x_ref[...]
    y = x * lax.rsqrt(jnp.mean(x * x, axis=-1, keepdims=True) + NORM_EPS)
    o_ref[...] = (y * g_ref[...]).astype(o_ref.dtype)


def _rmsnorm(x, g, tm):
    m, d = x.shape
    return pl.pallas_call(
        _rmsnorm_kernel,
        grid=(m // tm,),
        in_specs=[pl.BlockSpec((tm, d), lambda i: (i, 0)), pl.BlockSpec((1, d), lambda i: (0, 0))],
        out_specs=pl.BlockSpec((tm, d), lambda i: (i, 0)),
        out_shape=jax.ShapeDtypeStruct((m, d), MXU_DTYPE),
        compiler_params=_cparams(1, 6 * tm * d * 4),
        name="rmsnorm",
    )(x, g)


def _mm_kernel(a_ref, b_ref, o_ref):
    o_ref[...] = jnp.dot(a_ref[...], b_ref[...], preferred_element_type=F32)


IN_PROJ_TN = 1024


def _matmul(a, b, tm, tn):
    m, k = a.shape
    n = b.shape[1]
    isz = jnp.dtype(MXU_DTYPE).itemsize
    vmem = 2 * (tm * k * isz + k * tn * isz + tm * tn * 4) + 2 * tm * tn * 4 + (8 << 20)
    return pl.pallas_call(
        _mm_kernel,
        grid=(m // tm, n // tn),
        in_specs=[pl.BlockSpec((tm, k), lambda i, j: (i, 0)), pl.BlockSpec((k, tn), lambda i, j: (0, j))],
        out_specs=pl.BlockSpec((tm, tn), lambda i, j: (i, j)),
        out_shape=jax.ShapeDtypeStruct((m, n), F32),
        compiler_params=_cparams(2, vmem),
        name="in_proj",
    )(a, b)


MM_NORM_ROWS = 128


def _mm_norm_kernel(*refs, nn, n_ops):
    a_refs, w_refs = refs[:n_ops], refs[n_ops:2 * n_ops]
    x_ref, gpost_ref, x1_ref = refs[2 * n_ops:]
    n = pl.program_id(1)
    tm, d = x1_ref.shape
    tn = w_refs[0].shape[1]
    parts = [jnp.dot(a_ref[...], w_ref[...], preferred_element_type=F32) for a_ref, w_ref in zip(a_refs, w_refs)]
    x1_ref[:, pl.ds(pl.multiple_of(n * tn, LANES), tn)] = functools.reduce(jnp.add, parts)

    @pl.when(n == nn - 1)
    def _():
        step = min(tm, MM_NORM_ROWS)
        for r in range(0, tm, step):
            rows = slice(r, r + step)
            y = x1_ref[rows, :]
            y = y * lax.rsqrt(jnp.mean(y * y, axis=-1, keepdims=True) + NORM_EPS) * gpost_ref[...]
            x1_ref[rows, :] = x_ref[rows, :] + y


def _matmul_norm_residual(a_slabs, w, x, g_post, tm, tn, name):
    n_ops = len(a_slabs)
    m, ks = a_slabs[0].shape
    kdim = n_ops * ks
    d = w.shape[1]
    nn = d // tn
    isz = jnp.dtype(MXU_DTYPE).itemsize
    fixed = 2 * kdim * tn * isz + tm * d * 4 + 2 * tm * tn * 4 + 6 * MM_NORM_ROWS * d * 4 + (4 << 20)
    per_copy = tm * kdim * isz + tm * d * 4
    row_bufs = 2 if fixed + 2 * per_copy <= V7X_VMEM_LIMIT_BYTES - (4 << 20) else 1
    vmem = fixed + row_bufs * per_copy
    w_rows = lambda p: (lambda i, n: (p, n))
    return pl.pallas_call(
        functools.partial(_mm_norm_kernel, nn=nn, n_ops=n_ops),
        grid=(m // tm, nn),
        in_specs=[pl.BlockSpec((tm, ks), lambda i, n: (i, 0), pipeline_mode=pl.Buffered(row_bufs))] * n_ops
        + [pl.BlockSpec((ks, tn), w_rows(p)) for p in range(n_ops)]
        + [
            pl.BlockSpec((tm, d), lambda i, n: (i, 0), pipeline_mode=pl.Buffered(1)),
            pl.BlockSpec((1, d), lambda i, n: (0, 0)),
        ],
        out_specs=pl.BlockSpec((tm, d), lambda i, n: (i, 0), pipeline_mode=pl.Buffered(row_bufs)),
        out_shape=jax.ShapeDtypeStruct((m, d), F32),
        compiler_params=_cparams(2, vmem),
        name=name,
    )(*a_slabs, *([w] * n_ops), x, g_post)


def _gelu_tanh(x):
    return 0.5 * x * (1.0 + jnp.tanh(0.7978845608028654 * (x + 0.044715 * x * x * x)))


def _ffn_up_kernel(h_ref, wg_ref, wv_ref, c0g_ref, c0v_ref, cwg_ref, cwv_ref, cbg_ref, cbv_ref,
                   a_ref, cg_ref, cv_ref, carry_g, carry_v, *, blocks_per_stream, tm):
    i = pl.program_id(1) % blocks_per_stream
    n_str = carry_g.shape[0]
    ts = tm // n_str

    @pl.when(i == 0)
    def _():
        carry_g[...] = c0g_ref[...]
        carry_v[...] = c0v_ref[...]

    h = h_ref[...]
    row = lax.broadcasted_iota(jnp.int32, (tm, 1), 0)

    def conv(u, carry_ref, cw_ref, cb_ref, out_ref):
        u1 = pltpu.roll(u, 1, axis=0)
        u2 = pltpu.roll(u, 2, axis=0)
        for s in range(n_str):
            p = carry_ref[s]
            u1 = jnp.where(row == s * ts, p[1:2], u1)
            u2 = jnp.where(row == s * ts, p[0:1], jnp.where(row == s * ts + 1, p[1:2], u2))
            last = u[(s + 1) * ts - 2:(s + 1) * ts]
            carry_ref[s] = last
            out_ref[s] = last
        cw = cw_ref[...]
        return u2 * cw[0:1] + u1 * cw[1:2] + u * cw[2:3] + cb_ref[...]

    gate = conv(jnp.dot(h, wg_ref[...], preferred_element_type=F32), carry_g, cwg_ref, cbg_ref, cg_ref)
    val = conv(jnp.dot(h, wv_ref[...], preferred_element_type=F32), carry_v, cwv_ref, cbv_ref, cv_ref)
    a_ref[...] = (_gelu_tanh(gate) * val).astype(a_ref.dtype)


def _ffn_up(h, w_up, conv0, conv_w, conv_b, n_streams, tm, tn):
    m, d = h.shape
    f2 = w_up.shape[1]
    f = f2 // 2
    nj = f // tn
    nr = m // tm
    t = m // n_streams
    spb = max(1, tm // t)
    bps = max(1, t // tm)
    isz = jnp.dtype(MXU_DTYPE).itemsize
    vmem = 2 * (tm * d * isz + 2 * d * tn * isz + tm * tn * isz) + 10 * tm * tn * 4 + (4 << 20)
    col = lambda off: (lambda j, r: (0, off + j))
    st = lambda off: (lambda j, r: (r // bps, 0, off + j))
    a, cg, cv = pl.pallas_call(
        functools.partial(_ffn_up_kernel, blocks_per_stream=bps, tm=tm),
        grid=(nj, nr),
        in_specs=[
            pl.BlockSpec((tm, d), lambda j, r: (r, 0)),
            pl.BlockSpec((d, tn), col(0)),
            pl.BlockSpec((d, tn), col(nj)),
            pl.BlockSpec((spb, CONV_W - 1, tn), st(0)),
            pl.BlockSpec((spb, CONV_W - 1, tn), st(nj)),
            pl.BlockSpec((CONV_W, tn), col(0)),
            pl.BlockSpec((CONV_W, tn), col(nj)),
            pl.BlockSpec((1, tn), col(0)),
            pl.BlockSpec((1, tn), col(nj)),
        ],
        out_specs=[
            pl.BlockSpec((tm, tn), lambda j, r: (r, j)),
            pl.BlockSpec((spb, CONV_W - 1, tn), st(0)),
            pl.BlockSpec((spb, CONV_W - 1, tn), st(0)),
        ],
        out_shape=[
            jax.ShapeDtypeStruct((m, f), MXU_DTYPE),
            jax.ShapeDtypeStruct((n_streams, CONV_W - 1, f), F32),
            jax.ShapeDtypeStruct((n_streams, CONV_W - 1, f), F32),
        ],
        scratch_shapes=[pltpu.VMEM((spb, CONV_W - 1, tn), F32)] * 2,
        compiler_params=_cparams(2, vmem),
        name="ffn_up_conv",
    )(h, w_up, w_up, conv0, conv0, conv_w, conv_w, conv_b, conv_b)
    return a, jnp.concatenate([cg, cv], axis=-1)


RW_GROUP = 16 * LANES
RW_SMALL = 4 * LANES


def _softplus(z):
    return jnp.maximum(z, 0.0) + jnp.log(1.0 + jnp.exp(-jnp.abs(z)))


def _sigmoid(z):
    return 1.0 / (1.0 + jnp.exp(-z))


def _dg(a, b, dims):
    return lax.dot_general(_mx(a), _mx(b), (dims, ((), ())), preferred_element_type=F32)


_NN = ((1,), (0,))
_NT = ((1,), (1,))
_TN = ((0,), (0,))


def _segsum(x, seg):
    return jnp.dot(_mx(x), seg, preferred_element_type=F32)


def _each(f, *lists):
    return [f(*args) for args in zip(*lists)]


def _wkv_chunk(r, k, v, kk, b, cum, logd, s_bd, consts):
    m0, m1, tri_mask = consts
    c = r[0].shape[0]
    c2 = 2 * c
    stack = lambda x: jnp.concatenate([x * m0, x * m1], axis=0)
    g_end = _each(lambda cm: jnp.exp(cm[c - 1:c]), cum)
    e_inc = _each(jnp.exp, cum)
    e_prev = _each(lambda cm, ld: jnp.exp(cm - ld), cum, logd)
    e_neg = _each(lambda cm: jnp.exp(-cm), cum)
    e_tail = _each(lambda g, e: g * e, g_end, e_neg)
    vs = _each(stack, v)
    lhs = _each(lambda kk_, r_, ep, ei: jnp.concatenate([stack(kk_ * ep), stack(r_ * ei)], axis=0),
                kk, r, e_prev, e_inc)
    rhs = _each(lambda k_, b_, en: jnp.concatenate([stack(k_ * en), stack(b_ * en)], axis=0), k, b, e_neg)
    tails = _each(lambda k_, b_, et: jnp.concatenate([stack(k_ * et), stack(b_ * et)], axis=0), k, b, e_tail)
    amat = _each(lambda x, y: jnp.where(tri_mask, _dg(x, y, _NT), 0.0), lhs, rhs)
    sp = _each(lambda x, s: _dg(x, s, _NT), lhs, s_bd)
    av = _each(lambda a_, v_: _dg(a_[:, :c2], v_, _NN), amat, vs)
    pw = _each(lambda a_: -a_[:c2, c2:], amat)
    x = _each(lambda s_, a_: s_[:c2] + a_[:c2], sp, av)
    n_lvl = int(np.log2(c))
    for lvl in range(n_lvl - 1):
        z = _each(lambda p_, x_: _dg(p_, jnp.concatenate([p_, x_], axis=1), _NN), pw, x)
        pw = _each(lambda z_: z_[:, :c2], z)
        x = _each(lambda x_, z_: x_ + z_[:, c2:], x, z)
    u = _each(lambda p_, x_: x_ + _dg(p_, x_, _NN), pw, x)
    ys = _each(lambda s_, a_, am, u_: s_[c2:] + a_[c2:] - _dg(am[c2:, c2:], u_, _NN), sp, av, amat, u)
    y = _each(lambda y_: y_[:c] + y_[c:], ys)
    s_new = _each(lambda s, g, v_, u_, t_: s * g + _dg(jnp.concatenate([v_, -u_], axis=0), t_, _TN),
                  s_bd, g_end, vs, u, tails)
    return y, s_new


def _rwkv_kernel(pr_ref, pk_ref, pv_ref, ps_ref, sr_ref, sk_ref, sv_ref, ss_ref, wkv0_ref,
                 mur_ref, muk_ref, muv_ref, mus_ref, wdec_ref, dbias_ref, wa_ref, abias_ref, wg_ref,
                 kk_ref, ka_ref, rk_ref, lnw_ref, lnb_ref,
                 o_ref, wkv1_ref,
                 cr_ref, ck_ref, cv_ref, cs_ref, state_ref, *, n_chunks):
    c = pl.program_id(2)
    C = CHUNK
    npairs = RW_GROUP // LANES

    lane = lax.broadcasted_iota(jnp.int32, (1, LANES), 1)
    m0 = (lane < RW_HEAD).astype(F32)
    m1 = 1.0 - m0
    ri = lax.broadcasted_iota(jnp.int32, (4 * C, 4 * C), 0)
    ci = lax.broadcasted_iota(jnp.int32, (4 * C, 4 * C), 1)
    same_head = ((ri // C) % 2) == ((ci // C) % 2)
    tri_mask = same_head & ((ri % C) + ri // (2 * C) > (ci % C))
    li = lax.broadcasted_iota(jnp.int32, (LANES, LANES), 0)
    lj = lax.broadcasted_iota(jnp.int32, (LANES, LANES), 1)
    seg = ((li // RW_HEAD) == (lj // RW_HEAD)).astype(MXU_DTYPE)
    consts = (m0, m1, tri_mask)

    @pl.when(c == 0)
    def _():
        cr_ref[...] = sr_ref[...]
        ck_ref[...] = sk_ref[...]
        cv_ref[...] = sv_ref[...]
        cs_ref[...] = ss_ref[...]
        z = jnp.zeros((RW_HEAD, RW_HEAD), F32)
        for p in range(npairs):
            s0 = wkv0_ref[2 * p]
            s1 = wkv0_ref[2 * p + 1]
            state_ref[p] = jnp.concatenate(
                [jnp.concatenate([s0, z], axis=1), jnp.concatenate([z, s1], axis=1)], axis=0)

    row = lax.broadcasted_iota(jnp.int32, (C, 1), 0)

    def shifted(p_ref, carry_ref, mu_ref):
        p = p_ref[...]
        prev = jnp.where(row == 0, carry_ref[...], pltpu.roll(p, 1, axis=0))
        carry_ref[...] = p[C - 1:C]
        return p + (prev - p) * mu_ref[...]

    r = shifted(pr_ref, cr_ref, mur_ref)
    k = shifted(pk_ref, ck_ref, muk_ref)
    v = shifted(pv_ref, cv_ref, muv_ref)
    sm = shifted(ps_ref, cs_ref, mus_ref)
    wd, ad, gd = sm[:, 0:LANES], sm[:, LANES:2 * LANES], sm[:, 2 * LANES:4 * LANES]

    dec_in = dbias_ref[...] + jnp.dot(_mx(jnp.tanh(wd)), wdec_ref[...], preferred_element_type=F32)
    w_log = -_softplus(-dec_in) - 0.5
    logd = -jnp.exp(w_log)
    a = _sigmoid(abias_ref[...] + jnp.dot(_mx(ad), wa_ref[...], preferred_element_type=F32))
    g = jnp.dot(_mx(_sigmoid(gd)), wg_ref[...], preferred_element_type=F32)
    kk = k * kk_ref[...]
    k2 = k * (1.0 + (a - 1.0) * ka_ref[...])
    rkr = r * k2 * rk_ref[...]
    cum = logd
    for sft in (1, 2, 4, 8, 16, 32):
        cum = cum + jnp.where(row >= sft, pltpu.roll(cum, sft, axis=0), 0.0)

    pairs = lambda x: [x[:, p * LANES:(p + 1) * LANES] for p in range(npairs)]
    r_p, k_p, v_p, a_p, cum_p, logd_p = pairs(r), pairs(k2), pairs(v), pairs(a), pairs(cum), pairs(logd)
    sums = _each(lambda kk_, rkr_: _segsum(jnp.concatenate([kk_ * kk_, rkr_], axis=0), seg), pairs(kk), pairs(rkr))
    kk_p = _each(lambda kk_, s_: kk_ * lax.rsqrt(jnp.maximum(s_[:C], 1e-24)), pairs(kk), sums)
    b_p = _each(lambda kk_, a_: kk_ * a_, kk_p, a_p)
    y_p, s_new = _wkv_chunk(r_p, k_p, v_p, kk_p, b_p, cum_p, logd_p, [state_ref[p] for p in range(npairs)], consts)
    for p in range(npairs):
        state_ref[p] = s_new[p]
    yc = _each(lambda y_: y_ - _segsum(y_, seg) * (1.0 / RW_HEAD), y_p)
    var = _each(lambda yc_: _segsum(yc_ * yc_, seg) * (1.0 / RW_HEAD), yc)
    for p in range(npairs):
        sl = slice(p * LANES, (p + 1) * LANES)
        yn = yc[p] * lax.rsqrt(var[p] + LNX_EPS) * lnw_ref[:, sl] + lnb_ref[:, sl]
        o_ref[:, sl] = ((yn + sums[p][C:] * v_p[p]) * g[:, sl]).astype(o_ref.dtype)

    @pl.when(c == n_chunks - 1)
    def _():
        for p in range(npairs):
            s = state_ref[p]
            wkv1_ref[2 * p] = s[:RW_HEAD, :RW_HEAD]
            wkv1_ref[2 * p + 1] = s[RW_HEAD:, RW_HEAD:]


def _rwkv(pfull, cols, shift0, wkv0, w, n_streams, t):
    m = pfull.shape[0]
    rw = w["k_k"].shape[1]
    ng = rw // RW_GROUP
    nc = t // CHUNK
    hg = RW_GROUP // RW_HEAD
    rowblk = lambda off: (lambda b, g, c: (b * nc + c, off + g))
    fixed = lambda off: (lambda b, g, c: (b * nc + c, off))
    st = lambda b, g, c: (b, 0, g)
    st0 = lambda b, g, c: (b, 0, 0)
    wcol = lambda b, g, c: (0, g)
    in_specs = [
        pl.BlockSpec((CHUNK, RW_GROUP), rowblk(cols["r"])),
        pl.BlockSpec((CHUNK, RW_GROUP), rowblk(cols["k"])),
        pl.BlockSpec((CHUNK, RW_GROUP), rowblk(cols["v"])),
        pl.BlockSpec((CHUNK, RW_SMALL), fixed(cols["small"])),
        pl.BlockSpec((None, 1, RW_GROUP), st),
        pl.BlockSpec((None, 1, RW_GROUP), st),
        pl.BlockSpec((None, 1, RW_GROUP), st),
        pl.BlockSpec((None, 1, RW_SMALL), st0),
        pl.BlockSpec((None, hg, RW_HEAD, RW_HEAD), lambda b, g, c: (b, g, 0, 0)),
        pl.BlockSpec((1, RW_GROUP), wcol), pl.BlockSpec((1, RW_GROUP), wcol), pl.BlockSpec((1, RW_GROUP), wcol),
        pl.BlockSpec((1, RW_SMALL), lambda b, g, c: (0, 0)),
        pl.BlockSpec((LANES, RW_GROUP), wcol), pl.BlockSpec((1, RW_GROUP), wcol),
        pl.BlockSpec((LANES, RW_GROUP), wcol), pl.BlockSpec((1, RW_GROUP), wcol),
        pl.BlockSpec((2 * LANES, RW_GROUP), wcol),
        pl.BlockSpec((1, RW_GROUP), wcol), pl.BlockSpec((1, RW_GROUP), wcol), pl.BlockSpec((1, RW_GROUP), wcol),
        pl.BlockSpec((1, RW_GROUP), wcol), pl.BlockSpec((1, RW_GROUP), wcol),
    ]
    o, wkv1 = pl.pallas_call(
        functools.partial(_rwkv_kernel, n_chunks=nc),
        grid=(n_streams, ng, nc),
        in_specs=in_specs,
        out_specs=[
            pl.BlockSpec((CHUNK, RW_GROUP), lambda b, g, c: (b * nc + c, g)),
            pl.BlockSpec((None, hg, RW_HEAD, RW_HEAD), lambda b, g, c: (b, g, 0, 0)),
        ],
        out_shape=[
            jax.ShapeDtypeStruct((m, rw), MXU_DTYPE),
            jax.ShapeDtypeStruct((n_streams, rw // RW_HEAD, RW_HEAD, RW_HEAD), F32),
        ],
        scratch_shapes=[pltpu.VMEM((1, RW_GROUP), F32)] * 3 + [pltpu.VMEM((1, RW_SMALL), F32)]
        + [pltpu.VMEM((RW_GROUP // LANES, LANES, LANES), F32)],
        compiler_params=_cparams(3, 32 << 20),
        name="rwkv7_chunked",
    )(pfull, pfull, pfull, pfull, shift0["r"], shift0["k"], shift0["v"], shift0["small"], wkv0,
      w["mu_r"], w["mu_k"], w["mu_v"], w["mu_small"], w["w_decay_up"], w["decay_bias"], w["w_a_up"], w["a_bias"],
      w["w_gate_up"], w["k_k"], w["k_a"], w["r_k"], w["lnx_w"], w["lnx_b"])
    return o, wkv1


def _rope_kernel(q_ref, k_ref, qi_ref, kiw_ref, ca_ref, sa_ref, cb_ref, sb1_ref, sb2_ref,
                 qo_ref, ko_ref, qio_ref, kiwo_ref):
    ca, sa = ca_ref[...], sa_ref[...]
    cb, sb1, sb2 = cb_ref[...], sb1_ref[...], sb2_ref[...]

    def rope_head(x):
        return x * ca + pltpu.roll(x, SA_HEAD // 2, axis=1) * sa

    def rope_idx(x):
        return (x * cb + pltpu.roll(x, LANES - IDX_DIM // 2, axis=1) * sb1
                + pltpu.roll(x, IDX_DIM // 2, axis=1) * sb2)

    q_scale = float(SA_HEAD ** -0.5 * np.log2(np.e))
    for h in range(q_ref.shape[1] // LANES):
        sl = slice(h * LANES, (h + 1) * LANES)
        qo_ref[:, sl] = (rope_head(q_ref[:, sl]) * q_scale).astype(qo_ref.dtype)
    for h in range(k_ref.shape[1] // LANES):
        sl = slice(h * LANES, (h + 1) * LANES)
        ko_ref[:, sl] = rope_head(k_ref[:, sl])
    for hp in range(qi_ref.shape[1] // LANES):
        sl = slice(hp * LANES, (hp + 1) * LANES)
        qio_ref[:, sl] = rope_idx(qi_ref[:, sl]).astype(qio_ref.dtype)
    x = kiw_ref[...]
    lane = lax.broadcasted_iota(jnp.int32, (1, LANES), 1)
    kiwo_ref[...] = jnp.where(lane < IDX_DIM, rope_idx(x), x * float((IDX_HEADS * IDX_DIM) ** -0.5))


def _rope_tables(pos):
    def ang(half):
        inv = ROPE_THETA ** (-jnp.arange(half, dtype=F32) / half)
        return pos.astype(F32)[:, None] * inv[None, :]
    aa = ang(SA_HEAD // 2)
    ca = jnp.concatenate([jnp.cos(aa), jnp.cos(aa)], axis=1)
    sa = jnp.concatenate([-jnp.sin(aa), jnp.sin(aa)], axis=1)
    ab = ang(IDX_DIM // 2)
    z = jnp.zeros_like(ab)
    cb = jnp.concatenate([jnp.cos(ab)] * 4, axis=1)
    sb1 = jnp.concatenate([-jnp.sin(ab), z, -jnp.sin(ab), z], axis=1)
    sb2 = jnp.concatenate([z, jnp.sin(ab), z, jnp.sin(ab)], axis=1)
    return ca, sa, cb, sb1, sb2


def _rope(pfull, cols, tables, n_streams, t, tm, sa_w, kv_w, qi_w):
    m = pfull.shape[0]
    bps = t // tm
    tab = pl.BlockSpec((tm, LANES), lambda r: (r % bps, 0))
    return pl.pallas_call(
        _rope_kernel,
        grid=(m // tm,),
        in_specs=[
            pl.BlockSpec((tm, sa_w), lambda r: (r, cols["q"])),
            pl.BlockSpec((tm, kv_w), lambda r: (r, cols["ksa"])),
            pl.BlockSpec((tm, qi_w), lambda r: (r, cols["qi"])),
            pl.BlockSpec((tm, LANES), lambda r: (r, cols["kiw"])),
            tab, tab, tab, tab, tab,
        ],
        out_specs=[
            pl.BlockSpec((tm, sa_w), lambda r: (r, 0)),
            pl.BlockSpec((tm, kv_w), lambda r: (r, 0)),
            pl.BlockSpec((tm, qi_w), lambda r: (r, 0)),
            pl.BlockSpec((tm, LANES), lambda r: (r, 0)),
        ],
        out_shape=[
            jax.ShapeDtypeStruct((m, sa_w), MXU_DTYPE),
            jax.ShapeDtypeStruct((m, kv_w), F32),
            jax.ShapeDtypeStruct((m, qi_w), MXU_DTYPE),
            jax.ShapeDtypeStruct((m, LANES), F32),
        ],
        compiler_params=_cparams(1, 32 << 20),
        name="rope",
    )(pfull, pfull, pfull, pfull, *tables)


IDX_HEAD_GROUP = 4
IDX_SCORE_ROWS = 128
IDX_VALUE_PASSES = 16


FOLD_CHAINS = 2


def _fold_sublane_tiles(x, op):
    parts = [x[r:r + 8] for r in range(0, x.shape[0], 8)]
    chains = [functools.reduce(op, parts[c::FOLD_CHAINS]) for c in range(min(FOLD_CHAINS, len(parts)))]
    return functools.reduce(op, chains)


def _f32_to_key(x):
    bits = lax.bitcast_convert_type(x, jnp.int32)
    return bits ^ ((bits >> 31) & jnp.int32(0x7FFFFFFF))


def _key_to_f32(key):
    return lax.bitcast_convert_type(key ^ ((key >> 31) & jnp.int32(0x7FFFFFFF)), F32)


def _index_kernel(qit_ref, w_ref, ki_ref, mask_ref, keys_ref, *, tq, kb, nkb, past, topk):
    i = pl.program_id(1)
    n_adm = jnp.minimum(nkb, (past + (i + 1) * tq + kb - 1) // kb)
    chunk_bits = CHUNK.bit_length() - 1
    qpos = past + i * tq + lax.broadcasted_iota(jnp.int32, (1, tq), 1)
    lim = ((qpos >> chunk_bits) + 1) << chunk_bits
    krow = lax.broadcasted_iota(jnp.int32, (kb, 1), 0)
    hg = IDX_HEAD_GROUP
    fold = _fold_sublane_tiles

    def score_block(j, carry):
        smax, smin = carry
        for r0 in range(0, kb, IDX_SCORE_ROWS):
            rows = slice(r0, r0 + IDX_SCORE_ROWS)
            kblk = ki_ref[j, rows, :]
            acc = jnp.zeros((IDX_SCORE_ROWS, tq), F32)
            for g0 in range(0, IDX_HEADS, hg):
                s = jnp.dot(kblk, qit_ref[:, g0 * tq:(g0 + hg) * tq], preferred_element_type=F32)
                for g in range(hg):
                    acc = acc + jnp.maximum(s[:, g * tq:(g + 1) * tq], 0.0) * w_ref[g0 + g:g0 + g + 1, :]
            score = acc + 0.0
            adm = j * kb + r0 + krow[:IDX_SCORE_ROWS] < lim
            keys_ref[j, rows, :] = jnp.where(adm, _f32_to_key(score), jnp.int32(INT_MIN))
            smax = jnp.maximum(smax, fold(jnp.where(adm, score, -jnp.inf), jnp.maximum))
            smin = jnp.minimum(smin, fold(jnp.where(adm, score, jnp.inf), jnp.minimum))
        return smax, smin

    smax, smin = lax.fori_loop(0, n_adm, score_block,
                               (jnp.full((8, tq), -jnp.inf, F32), jnp.full((8, tq), jnp.inf, F32)))
    lo0 = _f32_to_key(jnp.min(smin, axis=0, keepdims=True))
    hi0 = _f32_to_key(jnp.max(smax, axis=0, keepdims=True)) + 1

    def count_ge(thr):
        def body(j, cnt):
            return cnt + fold(jnp.where(keys_ref[j] >= thr, 1.0, 0.0), jnp.add)
        return jnp.sum(lax.fori_loop(0, n_adm, body, jnp.zeros((8, tq), F32)), axis=0, keepdims=True)

    def midpoint(lo, hi):
        return (lo >> 1) + (hi >> 1) + (lo & hi & 1)

    def unfinished(lo, hi, cnt_lo):
        active = (midpoint(lo, hi) != lo) & (cnt_lo != float(topk))
        return jnp.max(jnp.where(active, 1.0, 0.0))

    def bisect(state):
        p, lo, hi, cnt_lo, cnt_hi, _ = state
        mid_v = _f32_to_key(0.5 * _key_to_f32(lo) + 0.5 * _key_to_f32(hi))
        by_value = (p < IDX_VALUE_PASSES) & (mid_v > lo) & (mid_v < hi)
        mid = jnp.where(by_value, mid_v, midpoint(lo, hi))
        cnt = count_ge(mid)
        ge = cnt >= float(topk)
        lo, cnt_lo = jnp.where(ge, mid, lo), jnp.where(ge, cnt, cnt_lo)
        hi, cnt_hi = jnp.where(ge, hi, mid), jnp.where(ge, cnt_hi, cnt)
        return p + 1, lo, hi, cnt_lo, cnt_hi, unfinished(lo, hi, cnt_lo)

    cnt_lo0 = lim.astype(F32)
    state = (jnp.int32(0), lo0, hi0, cnt_lo0, jnp.zeros((1, tq), F32), unfinished(lo0, hi0, cnt_lo0))
    _, thr, _, n_ge, n_gt, _ = lax.while_loop(lambda st: st[5] > 0.5, bisect, state)

    tied = n_ge > float(topk)
    need = float(topk) - n_gt
    end_all = jnp.int32(nkb * kb)

    def count_tied_below(bound):
        def body(j, cnt):
            hit = (keys_ref[j] == thr) & (j * kb + krow < bound)
            return cnt + fold(jnp.where(hit, 1.0, 0.0), jnp.add)
        return jnp.sum(lax.fori_loop(0, n_adm, body, jnp.zeros((8, tq), F32)), axis=0, keepdims=True)

    def tie_bisect(state):
        p, lo, hi = state
        mid = (lo + hi) >> 1
        enough = count_tied_below(mid) >= need
        return p + 1, jnp.where(enough, lo, mid), jnp.where(enough, mid, hi)

    any_tied = jnp.max(jnp.where(tied, 1.0, 0.0))
    n_steps = (nkb * kb).bit_length()
    tie_state = (jnp.int32(0), jnp.zeros((1, tq), jnp.int32), jnp.full((1, tq), end_all, jnp.int32))
    tie_hi = lax.while_loop(lambda st: (any_tied > 0.5) & (st[0] < n_steps), tie_bisect, tie_state)[2]
    tie_end = jnp.where(tied, tie_hi, end_all)

    def write(j, carry):
        k = keys_ref[j]
        keep = (k > thr) | ((k == thr) & (j * kb + krow < tie_end))
        mask_ref[j] = jnp.where(keep, 1.0, 0.0).T.astype(mask_ref.dtype)
        return carry

    lax.fori_loop(0, n_adm, write, 0)

    def clear(j, carry):
        mask_ref[j] = jnp.zeros((tq, kb), mask_ref.dtype)
        return carry

    lax.fori_loop(n_adm, nkb, clear, 0)


def _index_mask(qit, w, ki, n_streams, t, tq, kb, past, topk):
    nkb = ki.shape[1]
    nq = t // tq
    isz = jnp.dtype(MXU_DTYPE).itemsize
    vmem = (nkb * kb * tq * 4 + 2 * nkb * kb * tq + 2 * nkb * kb * LANES * isz + 2 * IDX_DIM * IDX_HEADS * tq * isz
            + 6 * kb * IDX_HEAD_GROUP * tq * 4 + (8 << 20))
    return pl.pallas_call(
        functools.partial(_index_kernel, tq=tq, kb=kb, nkb=nkb, past=past, topk=topk),
        grid=(n_streams, nq),
        in_specs=[
            pl.BlockSpec((None, None, IDX_DIM, IDX_HEADS * tq), lambda b, i: (b, i, 0, 0)),
            pl.BlockSpec((None, IDX_HEADS, tq), lambda b, i: (b, 0, i)),
            pl.BlockSpec((None, nkb, kb, IDX_DIM), lambda b, i: (b, 0, 0, 0)),
        ],
        out_specs=pl.BlockSpec((None, nkb, tq, kb), lambda b, i: (b, 0, i, 0)),
        out_shape=jax.ShapeDtypeStruct((n_streams, nkb, t, kb), jnp.int8),
        scratch_shapes=[pltpu.VMEM((nkb, kb, tq), jnp.int32)],
        compiler_params=_cparams(2, vmem),
        name="indexer_topk_mask",
    )(qit, w, ki)


def _attn_kernel(it_ref, jt_ref, last_ref, q_ref, k_ref, v_ref, m_ref, o_ref, qs_ref, mx_ref, l_ref, acc_ref,
                 *, group):
    step = pl.program_id(1)
    tq = q_ref.shape[0]
    tk = k_ref.shape[0]
    n_kv = k_ref.shape[1] // SA_HEAD
    rows = group * tq
    n_lane_tiles = tk // LANES

    @pl.when(jt_ref[step] == 0)
    def _():
        for n in range(n_kv):
            for g in range(group):
                h = n * group + g
                qs_ref[n, g * tq:(g + 1) * tq, :] = q_ref[:, h * SA_HEAD:(h + 1) * SA_HEAD]
        mx_ref[...] = jnp.full(mx_ref.shape, NEG_BIG, F32)
        l_ref[...] = jnp.zeros(l_ref.shape, F32)
        acc_ref[...] = jnp.zeros(acc_ref.shape, F32)

    bias = jnp.where(m_ref[...].astype(F32) > 0.0, 0.0, NEG_BIG)
    kv = range(n_kv)
    s = [_nt(qs_ref[n], k_ref[:, n * SA_HEAD:(n + 1) * SA_HEAD]) for n in kv]
    s = [(x.reshape(group, tq, tk) + bias[None]).reshape(rows, tk) for x in s]
    tiles = [[x[:, c * LANES:(c + 1) * LANES] for c in range(n_lane_tiles)] for x in s]
    m_old = [mx_ref[n] for n in kv]
    m_new = [jnp.maximum(mo, jnp.max(functools.reduce(jnp.maximum, t), axis=1, keepdims=True))
             for mo, t in zip(m_old, tiles)]
    alpha = [jnp.exp2(mo - mn) for mo, mn in zip(m_old, m_new)]
    p = [[jnp.exp2(x - mn) for x in t] for t, mn in zip(tiles, m_new)]
    for n in kv:
        mx_ref[n] = m_new[n]
        l_ref[n] = alpha[n] * l_ref[n] + functools.reduce(jnp.add, p[n])
    pv = [jnp.dot(_mx(jnp.concatenate(p[n], axis=1)), v_ref[:, n * SA_HEAD:(n + 1) * SA_HEAD],
                  preferred_element_type=F32) for n in kv]
    for n in kv:
        acc_ref[n] = alpha[n] * acc_ref[n] + pv[n]

    @pl.when(last_ref[step] == 1)
    def _():
        for n in range(n_kv):
            o = acc_ref[n] / jnp.sum(l_ref[n], axis=1, keepdims=True)
            for g in range(group):
                h = n * group + g
                o_ref[:, h * SA_HEAD:(h + 1) * SA_HEAD] = o[g * tq:(g + 1) * tq].astype(o_ref.dtype)


def _attn_tiles(t, tq, tk, nkb, past):
    ii, jj, last = [], [], []
    for i in range(t // tq):
        n_adm = min(nkb, -(-(past + (i + 1) * tq) // tk))
        for j in range(n_adm):
            ii.append(i)
            jj.append(j)
            last.append(int(j == n_adm - 1))
    return tuple(jnp.asarray(np.array(x, np.int32)) for x in (ii, jj, last))


def _attention(q, k_all, v_all, mask, n_streams, t, tq, tk, past):
    m, sa_w = q.shape
    kv_w = k_all.shape[2]
    nkb = mask.shape[1]
    nq = t // tq
    n_heads = sa_w // SA_HEAD
    group = n_heads // SA_KV_HEADS
    it, jt, last = _attn_tiles(t, tq, tk, nkb, past)
    grid_spec = pltpu.PrefetchScalarGridSpec(
        num_scalar_prefetch=3,
        grid=(n_streams, int(it.shape[0])),
        in_specs=[
            pl.BlockSpec((tq, sa_w), lambda b, s, it, jt, lt: (b * nq + it[s], 0)),
            pl.BlockSpec((None, tk, kv_w), lambda b, s, it, jt, lt: (b, jt[s], 0)),
            pl.BlockSpec((None, tk, kv_w), lambda b, s, it, jt, lt: (b, jt[s], 0)),
            pl.BlockSpec((None, None, tq, tk), lambda b, s, it, jt, lt: (b, jt[s], it[s], 0)),
        ],
        out_specs=pl.BlockSpec((tq, sa_w), lambda b, s, it, jt, lt: (b * nq + it[s], 0)),
        scratch_shapes=[pltpu.VMEM((SA_KV_HEADS, group * tq, SA_HEAD), MXU_DTYPE)]
        + [pltpu.VMEM((SA_KV_HEADS, group * tq, SA_HEAD), F32)] * 3,
    )
    return pl.pallas_call(
        functools.partial(_attn_kernel, group=group),
        grid_spec=grid_spec,
        out_shape=jax.ShapeDtypeStruct((m, sa_w), MXU_DTYPE),
        compiler_params=_cparams(2, 40 << 20),
        name="masked_flash_attention",
    )(it, jt, last, q, k_all, v_all, mask)


def _prepare_weights(w_in, mu_shift, w_decay_up, decay_bias, w_a_up, a_bias, w_gate_up, k_k, k_a, r_k,
                     lnx_w, lnx_b, w_out, w_up, conv_w, conv_b, w_down, d_model):
    rw = w_decay_up.shape[1]
    rw_cols = 3 * rw + D_DECAY + D_AAA + D_GATE
    sa_w = d_model - rw
    kv_w = SA_KV_HEADS * SA_HEAD
    qi_w = IDX_HEADS * IDX_DIM
    o = {}
    o["r"] = 0
    o["wd"] = rw
    o["k"] = rw + D_DECAY
    o["v"] = 2 * rw + D_DECAY
    o["ad"] = 3 * rw + D_DECAY
    o["gd"] = 3 * rw + D_DECAY + D_AAA
    o["q"] = rw_cols
    o["ksa"] = rw_cols + sa_w
    o["vsa"] = o["ksa"] + kv_w
    o["qi"] = o["vsa"] + kv_w
    o["kiw"] = o["qi"] + qi_w
    kiw_w = IDX_DIM + IDX_HEADS

    def seg(x, name, width, pad_to=None):
        s = x[..., o[name]:o[name] + width]
        if pad_to is not None and pad_to > width:
            s = jnp.pad(s, [(0, 0)] * (s.ndim - 1) + [(0, pad_to - width)])
        return s

    def rw_small(x):
        return jnp.concatenate([seg(x, "wd", D_DECAY, LANES), seg(x, "ad", D_AAA, LANES), seg(x, "gd", D_GATE)], -1)

    w_in_l = jnp.concatenate([
        seg(w_in, "r", rw), seg(w_in, "k", rw), seg(w_in, "v", rw), seg(w_in, "q", sa_w), seg(w_in, "qi", qi_w),
        seg(w_in, "ksa", kv_w), seg(w_in, "vsa", kv_w), rw_small(w_in), seg(w_in, "kiw", kiw_w, LANES)], axis=1)
    w_in_l = jnp.pad(w_in_l, ((0, 0), (0, -w_in_l.shape[1] % IN_PROJ_TN)))
    starts = {"r": 0, "k": rw, "v": 2 * rw, "q": 3 * rw, "qi": 3 * rw + sa_w}
    starts["ksa"] = starts["qi"] + qi_w
    starts["vsa"] = starts["ksa"] + kv_w
    starts["small"] = starts["vsa"] + kv_w
    starts["kiw"] = starts["small"] + 4 * LANES
    pad_rows = lambda x, n: jnp.pad(x, ((0, n - x.shape[0]), (0, 0)))
    row = lambda x: x.reshape(1, -1).astype(F32)
    mu = mu_shift.reshape(1, -1)
    w = {
        "w_in": _mx(w_in_l),
        "mu_r": seg(mu, "r", rw), "mu_k": seg(mu, "k", rw), "mu_v": seg(mu, "v", rw), "mu_small": rw_small(mu),
        "w_decay_up": _mx(pad_rows(w_decay_up, LANES)), "decay_bias": row(decay_bias),
        "w_a_up": _mx(pad_rows(w_a_up, LANES)), "a_bias": row(a_bias),
        "w_gate_up": _mx(w_gate_up),
        "k_k": row(k_k), "k_a": row(k_a), "r_k": row(r_k), "lnx_w": row(lnx_w), "lnx_b": row(lnx_b),
        "w_out": _mx(w_out), "w_up": _mx(w_up), "conv_w": conv_w, "conv_b": row(conv_b), "w_down": _mx(w_down),
    }
    dims = {"rw": rw, "rw_cols": rw_cols, "sa_w": sa_w, "kv_w": kv_w, "qi_w": qi_w, "orig": o, "starts": starts,
            "seg": seg, "rw_small": rw_small}
    return w, dims


def _layer(x, past_k, past_v, past_ki, wkv0, shift0, conv0, norms, w, dims, tiles):
    bsz, t, d = x.shape
    past = past_k.shape[1]
    m = bsz * t
    rw, sa_w, kv_w, qi_w = dims["rw"], dims["sa_w"], dims["kv_w"], dims["qi_w"]
    starts, seg, rw_small = dims["starts"], dims["seg"], dims["rw_small"]
    assert t % CHUNK == 0 and past % CHUNK == 0
    tm, tq, kb = tiles["tm"], tiles["tq"], tiles["kb"]
    g_mix_pre, g_mix_post, g_ffn_pre, g_ffn_post = norms

    x2 = x.reshape(m, d)
    h1 = _rmsnorm(x2, g_mix_pre, tiles["tm_norm"])
    pfull = _matmul(h1, w["w_in"], tiles["tm_in"], tiles["tn_in"])

    sh = shift0.reshape(bsz, 1, -1)
    shift_l = {"r": seg(sh, "r", rw), "k": seg(sh, "k", rw), "v": seg(sh, "v", rw), "small": rw_small(sh)}
    cols_rw = {n: starts[n] // RW_GROUP for n in ("r", "k", "v")}
    cols_rw["small"] = starts["small"] // RW_SMALL
    o_rw, wkv1 = _rwkv(pfull, cols_rw, shift_l, wkv0, w, bsz, t)

    pos = past + jnp.arange(t, dtype=jnp.int32)
    cols_sa = {"q": starts["q"] // sa_w, "ksa": starts["ksa"] // kv_w, "qi": starts["qi"] // qi_w,
               "kiw": starts["kiw"] // LANES}
    q_r, k_r, qi_r, kiw_r = _rope(pfull, cols_sa, _rope_tables(pos), bsz, t, tq, sa_w, kv_w, qi_w)
    k_new = k_r.reshape(bsz, t, kv_w)
    v_new = pfull[:, starts["vsa"]:starts["vsa"] + kv_w].reshape(bsz, t, kv_w)
    ki_new = kiw_r[:, :IDX_DIM].reshape(bsz, t, IDX_DIM)
    l_all = past + t
    lp = -(-l_all // kb) * kb
    nkb = lp // kb
    cat = lambda old, new: jnp.pad(
        jnp.concatenate([_mx(old.reshape(bsz, past, new.shape[-1])), _mx(new)], axis=1),
        ((0, 0), (0, lp - l_all), (0, 0)))
    k_all, v_all, ki_all = cat(past_k, k_new), cat(past_v, v_new), cat(past_ki, ki_new)
    nq = t // tq
    qit = (qi_r.reshape(bsz, nq, tq, IDX_HEADS, IDX_DIM).transpose(0, 1, 4, 3, 2)
           .reshape(bsz, nq, IDX_DIM, IDX_HEADS * tq))
    w_idx = kiw_r[:, IDX_DIM:IDX_DIM + IDX_HEADS].reshape(bsz, t, IDX_HEADS).transpose(0, 2, 1)
    topk = min(TOPK_MAX, l_all // 4)
    mask = _index_mask(qit, w_idx, ki_all.reshape(bsz, nkb, kb, IDX_DIM), bsz, t, tq, kb, past, topk)
    o_sa = _attention(q_r, k_all, v_all, mask, bsz, t, tq, kb, past)

    x1 = _matmul_norm_residual([o_rw, o_sa], w["w_out"], x2, g_mix_post, tm, tiles["tn_out"], "out_proj")
    h2 = _rmsnorm(x1, g_ffn_pre, tiles["tm_norm"])
    act, conv1 = _ffn_up(h2, w["w_up"], conv0, w["conv_w"], w["conv_b"], bsz, tiles["tm_up"], tiles["tn_up"])
    x_out = _matmul_norm_residual([act], w["w_down"], x1, g_ffn_post, tm, tiles["tn_out"], "ffn_down")

    orig = dims["orig"]
    last = pfull.reshape(bsz, t, -1)[:, t - 1:t]
    sm = last[..., starts["small"]:starts["small"] + 4 * LANES]
    shift1 = jnp.concatenate([
        last[..., starts["r"]:starts["r"] + rw], sm[..., :D_DECAY],
        last[..., starts["k"]:starts["k"] + rw], last[..., starts["v"]:starts["v"] + rw],
        sm[..., LANES:LANES + D_AAA], sm[..., 2 * LANES:2 * LANES + D_GATE]], axis=-1)
    del orig
    new = (k_new.reshape(bsz, t, SA_KV_HEADS, SA_HEAD), v_new.reshape(bsz, t, SA_KV_HEADS, SA_HEAD), ki_new,
           wkv1, shift1, conv1)
    return x_out.reshape(bsz, t, d), new


def _tiles(bsz, t):
    big = t >= 1024
    return {
        "tm_norm": 256 if big else CHUNK,
        "tm_in": 1024 if big else bsz * t,
        "tn_in": IN_PROJ_TN,
        "tm": 512 if big else bsz * t,
        "tn_out": 512,
        "tm_up": 1024 if big else bsz * t,
        "tn_up": 512,
        "tq": 256 if big else CHUNK,
        "kb": 512,
    }


def _run_stream(x, past_k, past_v, past_ki, wkv0, shift0, conv0, norm_w, weights, dims, depth):
    outs = []
    tiles = _tiles(x.shape[0], x.shape[1])
    for l in range(depth):
        w_l = {n: v[l] for n, v in weights.items()}
        norms = tuple(g[l].reshape(1, -1) for g in norm_w)
        x, st = _layer(x, past_k[l], past_v[l], past_ki[l], wkv0[l], shift0[l], conv0[l], norms, w_l, dims[l], tiles)
        outs.append(st)
    return x, [jnp.stack(s) for s in zip(*outs)]


def kernel(x_prompt, x_sample, cache_k, cache_v, cache_kidx, state_wkv, state_shift, state_conv, norm_mix_pre, norm_mix_post, norm_ffn_pre, norm_ffn_post, w_in, mu_shift, w_decay_up, decay_bias, w_a_up, a_bias, w_gate_up, k_k, k_a, r_k, lnx_w, lnx_b, w_out, w_up, conv_w, conv_b, w_down):
    depth, d_model = norm_mix_pre.shape
    per_layer = [_prepare_weights(w_in[l], mu_shift[l], w_decay_up[l], decay_bias[l], w_a_up[l], a_bias[l],
                                  w_gate_up[l], k_k[l], k_a[l], r_k[l], lnx_w[l], lnx_b[l], w_out[l], w_up[l],
                                  conv_w[l], conv_b[l], w_down[l], d_model) for l in range(depth)]
    weights = {n: [pw[0][n] for pw in per_layer] for n in per_layer[0][0]}
    dims = [pw[1] for pw in per_layer]
    norm_w = (norm_mix_pre, norm_mix_post, norm_ffn_pre, norm_ffn_post)

    dt = x_prompt.dtype
    bp = x_prompt.shape[0]
    rw_heads = state_wkv.shape[2]
    zk = jnp.zeros((depth, bp, 0, SA_KV_HEADS, SA_HEAD), dt)
    zki = jnp.zeros((depth, bp, 0, IDX_DIM), dt)
    zwkv = jnp.zeros((depth, bp, rw_heads, RW_HEAD, RW_HEAD), dt)
    zshift = jnp.zeros((depth, bp, 1, state_shift.shape[-1]), dt)
    zconv = jnp.zeros((depth, bp, CONV_W - 1, state_conv.shape[-1]), dt)
    y_prompt, p_new = _run_stream(x_prompt, zk, zk, zki, zwkv, zshift, zconv, norm_w, weights, dims, depth)
    y_sample, s_new = _run_stream(x_sample, cache_k, cache_v, cache_kidx, state_wkv, state_shift, state_conv,
                                  norm_w, weights, dims, depth)
    return (y_prompt, y_sample, *p_new, *s_new)
```

```python
import functools

import numpy as np
import jax
import jax.numpy as jnp
from jax import lax
from jax.experimental import pallas as pl
from jax.experimental.pallas import tpu as pltpu

F32 = jnp.float32
MXU_DTYPE = jnp.bfloat16

CHUNK = 64
RW_HEAD = 64
D_DECAY = 96
D_AAA = 96
D_GATE = 256
SA_HEAD = 128
SA_KV_HEADS = 4
IDX_HEADS = 16
IDX_DIM = 64
TOPK_MAX = 256
CONV_W = 3
ROPE_THETA = 10000.0
NORM_EPS = 1e-6
LNX_EPS = 64e-5

LANES = 128
V7X_VMEM_LIMIT_BYTES = 60000 * 1024
SMALL_CALL_VMEM_BYTES = 32 << 20
ATTN_VMEM_BYTES = 40 << 20
INT_MIN = -(2 ** 31)
NEG_BIG = -1e30


def _cparams(n_grid, vmem_bytes):
    limit = int(min(V7X_VMEM_LIMIT_BYTES, max(SMALL_CALL_VMEM_BYTES, vmem_bytes)))
    return pltpu.CompilerParams(dimension_semantics=("arbitrary",) * n_grid, vmem_limit_bytes=limit)


def _nt(a, b):
    return lax.dot_general(a, b, (((1,), (1,)), ((), ())), preferred_element_type=F32)


def _mx(x):
    return x.astype(MXU_DTYPE)


def _rmsnorm_kernel(x_ref, g_ref, o_ref):
    x = x_ref[...]
    y = x * lax.rsqrt(jnp.mean(x * x, axis=-1, keepdims=True) + NORM_EPS)
    o_ref[...] = (y * g_ref[...]).astype(o_ref.dtype)


def _rmsnorm(x, g, tm):
    m, d = x.shape
    return pl.pallas_call(
        _rmsnorm_kernel,
        grid=(m // tm,),
        in_specs=[pl.BlockSpec((tm, d), lambda i: (i, 0)), pl.BlockSpec((1, d), lambda i: (0, 0))],
        out_specs=pl.BlockSpec((tm, d), lambda i: (i, 0)),
        out_shape=jax.ShapeDtypeStruct((m, d), MXU_DTYPE),
        compiler_params=_cparams(1, 6 * tm * d * 4),
        name="rmsnorm",
    )(x, g)


def _mm_kernel(a_ref, b_ref, o_ref):
    o_ref[...] = jnp.dot(a_ref[...], b_ref[...], preferred_element_type=F32)


IN_PROJ_TN = 1024


def _matmul(a, b, tm, tn):
    m, k = a.shape
    n = b.shape[1]
    isz = jnp.dtype(MXU_DTYPE).itemsize
    vmem = 2 * (tm * k * isz + k * tn * isz + tm * tn * 4) + 2 * tm * tn * 4 + (8 << 20)
    return pl.pallas_call(
        _mm_kernel,
        grid=(m // tm, n // tn),
        in_specs=[pl.BlockSpec((tm, k), lambda i, j: (i, 0)), pl.BlockSpec((k, tn), lambda i, j: (0, j))],
        out_specs=pl.BlockSpec((tm, tn), lambda i, j: (i, j)),
        out_shape=jax.ShapeDtypeStruct((m, n), F32),
        compiler_params=_cparams(2, vmem),
        name="in_proj",
    )(a, b)


MM_NORM_ROWS = 128


def _mm_norm_kernel(*refs, nn, n_ops):
    a_refs, w_refs = refs[:n_ops], refs[n_ops:2 * n_ops]
    x_ref, gpost_ref, x1_ref = refs[2 * n_ops:]
    n = pl.program_id(1)
    tm, d = x1_ref.shape
    tn = w_refs[0].shape[1]
    parts = [jnp.dot(a_ref[...], w_ref[...], preferred_element_type=F32) for a_ref, w_ref in zip(a_refs, w_refs)]
    x1_ref[:, pl.ds(pl.multiple_of(n * tn, LANES), tn)] = functools.reduce(jnp.add, parts)

    @pl.when(n == nn - 1)
    def _():
        step = min(tm, MM_NORM_ROWS)
        for r in range(0, tm, step):
            rows = slice(r, r + step)
            y = x1_ref[rows, :]
            y = y * lax.rsqrt(jnp.mean(y * y, axis=-1, keepdims=True) + NORM_EPS) * gpost_ref[...]
            x1_ref[rows, :] = x_ref[rows, :] + y


def _matmul_norm_residual(a_slabs, w, x, g_post, tm, tn, name):
    n_ops = len(a_slabs)
    m, ks = a_slabs[0].shape
    kdim = n_ops * ks
    d = w.shape[1]
    nn = d // tn
    isz = jnp.dtype(MXU_DTYPE).itemsize
    fixed = 2 * kdim * tn * isz + tm * d * 4 + 2 * tm * tn * 4 + 6 * MM_NORM_ROWS * d * 4 + (4 << 20)
    per_copy = tm * kdim * isz + tm * d * 4
    row_bufs = 2 if fixed + 2 * per_copy <= V7X_VMEM_LIMIT_BYTES - (4 << 20) else 1
    vmem = fixed + row_bufs * per_copy
    w_rows = lambda p: (lambda i, n: (p, n))
    return pl.pallas_call(
        functools.partial(_mm_norm_kernel, nn=nn, n_ops=n_ops),
        grid=(m // tm, nn),
        in_specs=[pl.BlockSpec((tm, ks), lambda i, n: (i, 0), pipeline_mode=pl.Buffered(row_bufs))] * n_ops
        + [pl.BlockSpec((ks, tn), w_rows(p)) for p in range(n_ops)]
        + [
            pl.BlockSpec((tm, d), lambda i, n: (i, 0), pipeline_mode=pl.Buffered(1)),
            pl.BlockSpec((1, d), lambda i, n: (0, 0)),
        ],
        out_specs=pl.BlockSpec((tm, d), lambda i, n: (i, 0), pipeline_mode=pl.Buffered(row_bufs)),
        out_shape=jax.ShapeDtypeStruct((m, d), F32),
        compiler_params=_cparams(2, vmem),
        name=name,
    )(*a_slabs, *([w] * n_ops), x, g_post)


def _gelu_tanh(x):
    return 0.5 * x * (1.0 + jnp.tanh(0.7978845608028654 * (x + 0.044715 * x * x * x)))


def _ffn_up_kernel(h_ref, wg_ref, wv_ref, c0g_ref, c0v_ref, cwg_ref, cwv_ref, cbg_ref, cbv_ref,
                   a_ref, cg_ref, cv_ref, carry_g, carry_v, *, blocks_per_stream, tm):
    i = pl.program_id(1) % blocks_per_stream
    n_str = carry_g.shape[0]
    ts = tm // n_str

    @pl.when(i == 0)
    def _():
        carry_g[...] = c0g_ref[...]
        carry_v[...] = c0v_ref[...]

    h = h_ref[...]
    row = lax.broadcasted_iota(jnp.int32, (tm, 1), 0)

    def conv(u, carry_ref, cw_ref, cb_ref, out_ref):
        u1 = pltpu.roll(u, 1, axis=0)
        u2 = pltpu.roll(u, 2, axis=0)
        for s in range(n_str):
            p = carry_ref[s]
            u1 = jnp.where(row == s * ts, p[1:2], u1)
            u2 = jnp.where(row == s * ts, p[0:1], jnp.where(row == s * ts + 1, p[1:2], u2))
            last = u[(s + 1) * ts - 2:(s + 1) * ts]
            carry_ref[s] = last
            out_ref[s] = last
        cw = cw_ref[...]
        return u2 * cw[0:1] + u1 * cw[1:2] + u * cw[2:3] + cb_ref[...]

    gate = conv(jnp.dot(h, wg_ref[...], preferred_element_type=F32), carry_g, cwg_ref, cbg_ref, cg_ref)
    val = conv(jnp.dot(h, wv_ref[...], preferred_element_type=F32), carry_v, cwv_ref, cbv_ref, cv_ref)
    a_ref[...] = (_gelu_tanh(gate) * val).astype(a_ref.dtype)


def _ffn_up(h, w_up, conv0, conv_w, conv_b, n_streams, tm, tn):
    m, d = h.shape
    f2 = w_up.shape[1]
    f = f2 // 2
    nj = f // tn
    nr = m // tm
    t = m // n_streams
    spb = max(1, tm // t)
    bps = max(1, t // tm)
    isz = jnp.dtype(MXU_DTYPE).itemsize
    vmem = 2 * (tm * d * isz + 2 * d * tn * isz + tm * tn * isz) + 10 * tm * tn * 4 + (4 << 20)
    col = lambda off: (lambda j, r: (0, off + j))
    st = lambda off: (lambda j, r: (r // bps, 0, off + j))
    a, cg, cv = pl.pallas_call(
        functools.partial(_ffn_up_kernel, blocks_per_stream=bps, tm=tm),
        grid=(nj, nr),
        in_specs=[
            pl.BlockSpec((tm, d), lambda j, r: (r, 0)),
            pl.BlockSpec((d, tn), col(0)),
            pl.BlockSpec((d, tn), col(nj)),
            pl.BlockSpec((spb, CONV_W - 1, tn), st(0)),
            pl.BlockSpec((spb, CONV_W - 1, tn), st(nj)),
            pl.BlockSpec((CONV_W, tn), col(0)),
            pl.BlockSpec((CONV_W, tn), col(nj)),
            pl.BlockSpec((1, tn), col(0)),
            pl.BlockSpec((1, tn), col(nj)),
        ],
        out_specs=[
            pl.BlockSpec((tm, tn), lambda j, r: (r, j)),
            pl.BlockSpec((spb, CONV_W - 1, tn), st(0)),
            pl.BlockSpec((spb, CONV_W - 1, tn), st(0)),
        ],
        out_shape=[
            jax.ShapeDtypeStruct((m, f), MXU_DTYPE),
            jax.ShapeDtypeStruct((n_streams, CONV_W - 1, f), F32),
            jax.ShapeDtypeStruct((n_streams, CONV_W - 1, f), F32),
        ],
        scratch_shapes=[pltpu.VMEM((spb, CONV_W - 1, tn), F32)] * 2,
        compiler_params=_cparams(2, vmem),
        name="ffn_up_conv",
    )(h, w_up, w_up, conv0, conv0, conv_w, conv_w, conv_b, conv_b)
    return a, jnp.concatenate([cg, cv], axis=-1)


RW_GROUP = 16 * LANES
RW_SMALL = 4 * LANES


def _softplus(z):
    return jnp.maximum(z, 0.0) + jnp.log(1.0 + jnp.exp(-jnp.abs(z)))


def _sigmoid(z):
    return 1.0 / (1.0 + jnp.exp(-z))


def _dg(a, b, dims):
    return lax.dot_general(_mx(a), _mx(b), (dims, ((), ())), preferred_element_type=F32)


_NN = ((1,), (0,))
_NT = ((1,), (1,))
_TN = ((0,), (0,))


def _segsum(x, seg):
    return jnp.dot(_mx(x), seg, preferred_element_type=F32)


def _each(f, *lists):
    return [f(*args) for args in zip(*lists)]


def _wkv_chunk(r, k, v, kk, b, cum, logd, s_bd, consts):
    m0, m1, tri_mask = consts
    c = r[0].shape[0]
    c2 = 2 * c
    stack = lambda x: jnp.concatenate([x * m0, x * m1], axis=0)
    g_end = _each(lambda cm: jnp.exp(cm[c - 1:c]), cum)
    e_inc = _each(jnp.exp, cum)
    e_prev = _each(lambda cm, ld: jnp.exp(cm - ld), cum, logd)
    e_neg = _each(lambda cm: jnp.exp(-cm), cum)
    e_tail = _each(lambda g, e: g * e, g_end, e_neg)
    vs = _each(stack, v)
    lhs = _each(lambda kk_, r_, ep, ei: jnp.concatenate([stack(kk_ * ep), stack(r_ * ei)], axis=0),
                kk, r, e_prev, e_inc)
    rhs = _each(lambda k_, b_, en: jnp.concatenate([stack(k_ * en), stack(b_ * en)], axis=0), k, b, e_neg)
    tails = _each(lambda k_, b_, et: jnp.concatenate([stack(k_ * et), stack(b_ * et)], axis=0), k, b, e_tail)
    amat = _each(lambda x, y: jnp.where(tri_mask, _dg(x, y, _NT), 0.0), lhs, rhs)
    sp = _each(lambda x, s: _dg(x, s, _NT), lhs, s_bd)
    av = _each(lambda a_, v_: _dg(a_[:, :c2], v_, _NN), amat, vs)
    pw = _each(lambda a_: -a_[:c2, c2:], amat)
    x = _each(lambda s_, a_: s_[:c2] + a_[:c2], sp, av)
    n_lvl = int(np.log2(c))
    for lvl in range(n_lvl - 1):
        z = _each(lambda p_, x_: _dg(p_, jnp.concatenate([p_, x_], axis=1), _NN), pw, x)
        pw = _each(lambda z_: z_[:, :c2], z)
        x = _each(lambda x_, z_: x_ + z_[:, c2:], x, z)
    u = _each(lambda p_, x_: x_ + _dg(p_, x_, _NN), pw, x)
    ys = _each(lambda s_, a_, am, u_: s_[c2:] + a_[c2:] - _dg(am[c2:, c2:], u_, _NN), sp, av, amat, u)
    y = _each(lambda y_: y_[:c] + y_[c:], ys)
    s_new = _each(lambda s, g, v_, u_, t_: s * g + _dg(jnp.concatenate([v_, -u_], axis=0), t_, _TN),
                  s_bd, g_end, vs, u, tails)
    return y, s_new


def _rwkv_kernel(pr_ref, pk_ref, pv_ref, ps_ref, sr_ref, sk_ref, sv_ref, ss_ref, wkv0_ref,
                 mur_ref, muk_ref, muv_ref, mus_ref, wdec_ref, dbias_ref, wa_ref, abias_ref, wg_ref,
                 kk_ref, ka_ref, rk_ref, lnw_ref, lnb_ref,
                 o_ref, wkv1_ref,
                 cr_ref, ck_ref, cv_ref, cs_ref, state_ref, *, n_chunks):
    c = pl.program_id(2)
    C = CHUNK
    npairs = RW_GROUP // LANES

    lane = lax.broadcasted_iota(jnp.int32, (1, LANES), 1)
    m0 = (lane < RW_HEAD).astype(F32)
    m1 = 1.0 - m0
    ri = lax.broadcasted_iota(jnp.int32, (4 * C, 4 * C), 0)
    ci = lax.broadcasted_iota(jnp.int32, (4 * C, 4 * C), 1)
    same_head = ((ri // C) % 2) == ((ci // C) % 2)
    tri_mask = same_head & ((ri % C) + ri // (2 * C) > (ci % C))
    li = lax.broadcasted_iota(jnp.int32, (LANES, LANES), 0)
    lj = lax.broadcasted_iota(jnp.int32, (LANES, LANES), 1)
    seg = ((li // RW_HEAD) == (lj // RW_HEAD)).astype(MXU_DTYPE)
    consts = (m0, m1, tri_mask)

    @pl.when(c == 0)
    def _():
        cr_ref[...] = sr_ref[...]
        ck_ref[...] = sk_ref[...]
        cv_ref[...] = sv_ref[...]
        cs_ref[...] = ss_ref[...]
        z = jnp.zeros((RW_HEAD, RW_HEAD), F32)
        for p in range(npairs):
            s0 = wkv0_ref[2 * p]
            s1 = wkv0_ref[2 * p + 1]
            state_ref[p] = jnp.concatenate(
                [jnp.concatenate([s0, z], axis=1), jnp.concatenate([z, s1], axis=1)], axis=0)

    row = lax.broadcasted_iota(jnp.int32, (C, 1), 0)

    def shifted(p_ref, carry_ref, mu_ref):
        p = p_ref[...]
        prev = jnp.where(row == 0, carry_ref[...], pltpu.roll(p, 1, axis=0))
        carry_ref[...] = p[C - 1:C]
        return p + (prev - p) * mu_ref[...]

    r = shifted(pr_ref, cr_ref, mur_ref)
    k = shifted(pk_ref, ck_ref, muk_ref)
    v = shifted(pv_ref, cv_ref, muv_ref)
    sm = shifted(ps_ref, cs_ref, mus_ref)
    wd, ad, gd = sm[:, 0:LANES], sm[:, LANES:2 * LANES], sm[:, 2 * LANES:4 * LANES]

    dec_in = dbias_ref[...] + jnp.dot(_mx(jnp.tanh(wd)), wdec_ref[...], preferred_element_type=F32)
    w_log = -_softplus(-dec_in) - 0.5
    logd = -jnp.exp(w_log)
    a = _sigmoid(abias_ref[...] + jnp.dot(_mx(ad), wa_ref[...], preferred_element_type=F32))
    g = jnp.dot(_mx(_sigmoid(gd)), wg_ref[...], preferred_element_type=F32)
    kk = k * kk_ref[...]
    k2 = k * (1.0 + (a - 1.0) * ka_ref[...])
    rkr = r * k2 * rk_ref[...]
    cum = logd
    for sft in (1, 2, 4, 8, 16, 32):
        cum = cum + jnp.where(row >= sft, pltpu.roll(cum, sft, axis=0), 0.0)

    pairs = lambda x: [x[:, p * LANES:(p + 1) * LANES] for p in range(npairs)]
    r_p, k_p, v_p, a_p, cum_p, logd_p = pairs(r), pairs(k2), pairs(v), pairs(a), pairs(cum), pairs(logd)
    sums = _each(lambda kk_, rkr_: _segsum(jnp.concatenate([kk_ * kk_, rkr_], axis=0), seg), pairs(kk), pairs(rkr))
    kk_p = _each(lambda kk_, s_: kk_ * lax.rsqrt(jnp.maximum(s_[:C], 1e-24)), pairs(kk), sums)
    b_p = _each(lambda kk_, a_: kk_ * a_, kk_p, a_p)
    y_p, s_new = _wkv_chunk(r_p, k_p, v_p, kk_p, b_p, cum_p, logd_p, [state_ref[p] for p in range(npairs)], consts)
    for p in range(npairs):
        state_ref[p] = s_new[p]
    yc = _each(lambda y_: y_ - _segsum(y_, seg) * (1.0 / RW_HEAD), y_p)
    var = _each(lambda yc_: _segsum(yc_ * yc_, seg) * (1.0 / RW_HEAD), yc)
    for p in range(npairs):
        sl = slice(p * LANES, (p + 1) * LANES)
        yn = yc[p] * lax.rsqrt(var[p] + LNX_EPS) * lnw_ref[:, sl] + lnb_ref[:, sl]
        o_ref[:, sl] = ((yn + sums[p][C:] * v_p[p]) * g[:, sl]).astype(o_ref.dtype)

    @pl.when(c == n_chunks - 1)
    def _():
        for p in range(npairs):
            s = state_ref[p]
            wkv1_ref[2 * p] = s[:RW_HEAD, :RW_HEAD]
            wkv1_ref[2 * p + 1] = s[RW_HEAD:, RW_HEAD:]


def _rwkv(pfull, cols, shift0, wkv0, w, n_streams, t):
    m = pfull.shape[0]
    rw = w["k_k"].shape[1]
    ng = rw // RW_GROUP
    nc = t // CHUNK
    hg = RW_GROUP // RW_HEAD
    rowblk = lambda off: (lambda b, g, c: (b * nc + c, off + g))
    fixed = lambda off: (lambda b, g, c: (b * nc + c, off))
    st = lambda b, g, c: (b, 0, g)
    st0 = lambda b, g, c: (b, 0, 0)
    wcol = lambda b, g, c: (0, g)
    in_specs = [
        pl.BlockSpec((CHUNK, RW_GROUP), rowblk(cols["r"])),
        pl.BlockSpec((CHUNK, RW_GROUP), rowblk(cols["k"])),
        pl.BlockSpec((CHUNK, RW_GROUP), rowblk(cols["v"])),
        pl.BlockSpec((CHUNK, RW_SMALL), fixed(cols["small"])),
        pl.BlockSpec((None, 1, RW_GROUP), st),
        pl.BlockSpec((None, 1, RW_GROUP), st),
        pl.BlockSpec((None, 1, RW_GROUP), st),
        pl.BlockSpec((None, 1, RW_SMALL), st0),
        pl.BlockSpec((None, hg, RW_HEAD, RW_HEAD), lambda b, g, c: (b, g, 0, 0)),
        pl.BlockSpec((1, RW_GROUP), wcol), pl.BlockSpec((1, RW_GROUP), wcol), pl.BlockSpec((1, RW_GROUP), wcol),
        pl.BlockSpec((1, RW_SMALL), lambda b, g, c: (0, 0)),
        pl.BlockSpec((LANES, RW_GROUP), wcol), pl.BlockSpec((1, RW_GROUP), wcol),
        pl.BlockSpec((LANES, RW_GROUP), wcol), pl.BlockSpec((1, RW_GROUP), wcol),
        pl.BlockSpec((2 * LANES, RW_GROUP), wcol),
        pl.BlockSpec((1, RW_GROUP), wcol), pl.BlockSpec((1, RW_GROUP), wcol), pl.BlockSpec((1, RW_GROUP), wcol),
        pl.BlockSpec((1, RW_GROUP), wcol), pl.BlockSpec((1, RW_GROUP), wcol),
    ]
    o, wkv1 = pl.pallas_call(
        functools.partial(_rwkv_kernel, n_chunks=nc),
        grid=(n_streams, ng, nc),
        in_specs=in_specs,
        out_specs=[
            pl.BlockSpec((CHUNK, RW_GROUP), lambda b, g, c: (b * nc + c, g)),
            pl.BlockSpec((None, hg, RW_HEAD, RW_HEAD), lambda b, g, c: (b, g, 0, 0)),
        ],
        out_shape=[
            jax.ShapeDtypeStruct((m, rw), MXU_DTYPE),
            jax.ShapeDtypeStruct((n_streams, rw // RW_HEAD, RW_HEAD, RW_HEAD), F32),
        ],
        scratch_shapes=[pltpu.VMEM((1, RW_GROUP), F32)] * 3 + [pltpu.VMEM((1, RW_SMALL), F32)]
        + [pltpu.VMEM((RW_GROUP // LANES, LANES, LANES), F32)],
        compiler_params=_cparams(3, SMALL_CALL_VMEM_BYTES),
        name="rwkv7_chunked",
    )(pfull, pfull, pfull, pfull, shift0["r"], shift0["k"], shift0["v"], shift0["small"], wkv0,
      w["mu_r"], w["mu_k"], w["mu_v"], w["mu_small"], w["w_decay_up"], w["decay_bias"], w["w_a_up"], w["a_bias"],
      w["w_gate_up"], w["k_k"], w["k_a"], w["r_k"], w["lnx_w"], w["lnx_b"])
    return o, wkv1


def _rope_kernel(q_ref, k_ref, qi_ref, kiw_ref, ca_ref, sa_ref, cb_ref, sb1_ref, sb2_ref,
                 qo_ref, ko_ref, qio_ref, kiwo_ref):
    ca, sa = ca_ref[...], sa_ref[...]
    cb, sb1, sb2 = cb_ref[...], sb1_ref[...], sb2_ref[...]

    def rope_head(x):
        return x * ca + pltpu.roll(x, SA_HEAD // 2, axis=1) * sa

    def rope_idx(x):
        return (x * cb + pltpu.roll(x, LANES - IDX_DIM // 2, axis=1) * sb1
                + pltpu.roll(x, IDX_DIM // 2, axis=1) * sb2)

    q_scale = float(SA_HEAD ** -0.5 * np.log2(np.e))
    for h in range(q_ref.shape[1] // LANES):
        sl = slice(h * LANES, (h + 1) * LANES)
        qo_ref[:, sl] = (rope_head(q_ref[:, sl]) * q_scale).astype(qo_ref.dtype)
    for h in range(k_ref.shape[1] // LANES):
        sl = slice(h * LANES, (h + 1) * LANES)
        ko_ref[:, sl] = rope_head(k_ref[:, sl])
    for hp in range(qi_ref.shape[1] // LANES):
        sl = slice(hp * LANES, (hp + 1) * LANES)
        qio_ref[:, sl] = rope_idx(qi_ref[:, sl]).astype(qio_ref.dtype)
    x = kiw_ref[...]
    lane = lax.broadcasted_iota(jnp.int32, (1, LANES), 1)
    kiwo_ref[...] = jnp.where(lane < IDX_DIM, rope_idx(x), x * float((IDX_HEADS * IDX_DIM) ** -0.5))


def _rope_tables(pos):
    def ang(half):
        inv = ROPE_THETA ** (-jnp.arange(half, dtype=F32) / half)
        return pos.astype(F32)[:, None] * inv[None, :]
    aa = ang(SA_HEAD // 2)
    ca = jnp.concatenate([jnp.cos(aa), jnp.cos(aa)], axis=1)
    sa = jnp.concatenate([-jnp.sin(aa), jnp.sin(aa)], axis=1)
    ab = ang(IDX_DIM // 2)
    z = jnp.zeros_like(ab)
    cb = jnp.concatenate([jnp.cos(ab)] * 4, axis=1)
    sb1 = jnp.concatenate([-jnp.sin(ab), z, -jnp.sin(ab), z], axis=1)
    sb2 = jnp.concatenate([z, jnp.sin(ab), z, jnp.sin(ab)], axis=1)
    return ca, sa, cb, sb1, sb2


def _rope(pfull, cols, tables, n_streams, t, tm, sa_w, kv_w, qi_w):
    m = pfull.shape[0]
    bps = t // tm
    tab = pl.BlockSpec((tm, LANES), lambda r: (r % bps, 0))
    return pl.pallas_call(
        _rope_kernel,
        grid=(m // tm,),
        in_specs=[
            pl.BlockSpec((tm, sa_w), lambda r: (r, cols["q"])),
            pl.BlockSpec((tm, kv_w), lambda r: (r, cols["ksa"])),
            pl.BlockSpec((tm, qi_w), lambda r: (r, cols["qi"])),
            pl.BlockSpec((tm, LANES), lambda r: (r, cols["kiw"])),
            tab, tab, tab, tab, tab,
        ],
        out_specs=[
            pl.BlockSpec((tm, sa_w), lambda r: (r, 0)),
            pl.BlockSpec((tm, kv_w), lambda r: (r, 0)),
            pl.BlockSpec((tm, qi_w), lambda r: (r, 0)),
            pl.BlockSpec((tm, LANES), lambda r: (r, 0)),
        ],
        out_shape=[
            jax.ShapeDtypeStruct((m, sa_w), MXU_DTYPE),
            jax.ShapeDtypeStruct((m, kv_w), F32),
            jax.ShapeDtypeStruct((m, qi_w), MXU_DTYPE),
            jax.ShapeDtypeStruct((m, LANES), F32),
        ],
        compiler_params=_cparams(1, SMALL_CALL_VMEM_BYTES),
        name="rope",
    )(pfull, pfull, pfull, pfull, *tables)


IDX_HEAD_GROUP = 4
IDX_SCORE_ROWS = 128
IDX_VALUE_PASSES = 16


FOLD_CHAINS = 2


def _fold_sublane_tiles(x, op):
    parts = [x[r:r + 8] for r in range(0, x.shape[0], 8)]
    chains = [functools.reduce(op, parts[c::FOLD_CHAINS]) for c in range(min(FOLD_CHAINS, len(parts)))]
    return functools.reduce(op, chains)


def _f32_to_key(x):
    bits = lax.bitcast_convert_type(x, jnp.int32)
    return bits ^ ((bits >> 31) & jnp.int32(0x7FFFFFFF))


def _key_to_f32(key):
    return lax.bitcast_convert_type(key ^ ((key >> 31) & jnp.int32(0x7FFFFFFF)), F32)


def _index_kernel(qit_ref, w_ref, ki_ref, mask_ref, keys_ref, *, tq, kb, nkb, past, topk):
    i = pl.program_id(1)
    n_adm = jnp.minimum(nkb, (past + (i + 1) * tq + kb - 1) // kb)
    chunk_bits = CHUNK.bit_length() - 1
    qpos = past + i * tq + lax.broadcasted_iota(jnp.int32, (1, tq), 1)
    lim = ((qpos >> chunk_bits) + 1) << chunk_bits
    krow = lax.broadcasted_iota(jnp.int32, (kb, 1), 0)
    hg = IDX_HEAD_GROUP
    fold = _fold_sublane_tiles

    def score_block(j, carry):
        smax, smin = carry
        for r0 in range(0, kb, IDX_SCORE_ROWS):
            rows = slice(r0, r0 + IDX_SCORE_ROWS)
            kblk = ki_ref[j, rows, :]
            acc = jnp.zeros((IDX_SCORE_ROWS, tq), F32)
            for g0 in range(0, IDX_HEADS, hg):
                s = jnp.dot(kblk, qit_ref[:, g0 * tq:(g0 + hg) * tq], preferred_element_type=F32)
                for g in range(hg):
                    acc = acc + jnp.maximum(s[:, g * tq:(g + 1) * tq], 0.0) * w_ref[g0 + g:g0 + g + 1, :]
            score = acc + 0.0
            adm = j * kb + r0 + krow[:IDX_SCORE_ROWS] < lim
            keys_ref[j, rows, :] = jnp.where(adm, _f32_to_key(score), jnp.int32(INT_MIN))
            smax = jnp.maximum(smax, fold(jnp.where(adm, score, -jnp.inf), jnp.maximum))
            smin = jnp.minimum(smin, fold(jnp.where(adm, score, jnp.inf), jnp.minimum))
        return smax, smin

    smax, smin = lax.fori_loop(0, n_adm, score_block,
                               (jnp.full((8, tq), -jnp.inf, F32), jnp.full((8, tq), jnp.inf, F32)))
    lo0 = _f32_to_key(jnp.min(smin, axis=0, keepdims=True))
    hi0 = _f32_to_key(jnp.max(smax, axis=0, keepdims=True)) + 1

    def count_ge(thr):
        def body(j, cnt):
            return cnt + fold(jnp.where(keys_ref[j] >= thr, 1.0, 0.0), jnp.add)
        return jnp.sum(lax.fori_loop(0, n_adm, body, jnp.zeros((8, tq), F32)), axis=0, keepdims=True)

    def midpoint(lo, hi):
        return (lo >> 1) + (hi >> 1) + (lo & hi & 1)

    def unfinished(lo, hi, cnt_lo):
        active = (midpoint(lo, hi) != lo) & (cnt_lo != float(topk))
        return jnp.max(jnp.where(active, 1.0, 0.0))

    def bisect(state):
        p, lo, hi, cnt_lo, cnt_hi, _ = state
        mid_v = _f32_to_key(0.5 * _key_to_f32(lo) + 0.5 * _key_to_f32(hi))
        by_value = (p < IDX_VALUE_PASSES) & (mid_v > lo) & (mid_v < hi)
        mid = jnp.where(by_value, mid_v, midpoint(lo, hi))
        cnt = count_ge(mid)
        ge = cnt >= float(topk)
        lo, cnt_lo = jnp.where(ge, mid, lo), jnp.where(ge, cnt, cnt_lo)
        hi, cnt_hi = jnp.where(ge, hi, mid), jnp.where(ge, cnt_hi, cnt)
        return p + 1, lo, hi, cnt_lo, cnt_hi, unfinished(lo, hi, cnt_lo)

    cnt_lo0 = lim.astype(F32)
    state = (jnp.int32(0), lo0, hi0, cnt_lo0, jnp.zeros((1, tq), F32), unfinished(lo0, hi0, cnt_lo0))
    _, thr, _, n_ge, n_gt, _ = lax.while_loop(lambda st: st[5] > 0.5, bisect, state)

    tied = n_ge > float(topk)
    need = float(topk) - n_gt
    end_all = jnp.int32(nkb * kb)

    def count_tied_below(bound):
        def body(j, cnt):
            hit = (keys_ref[j] == thr) & (j * kb + krow < bound)
            return cnt + fold(jnp.where(hit, 1.0, 0.0), jnp.add)
        return jnp.sum(lax.fori_loop(0, n_adm, body, jnp.zeros((8, tq), F32)), axis=0, keepdims=True)

    def tie_bisect(state):
        p, lo, hi = state
        mid = (lo + hi) >> 1
        enough = count_tied_below(mid) >= need
        return p + 1, jnp.where(enough, lo, mid), jnp.where(enough, mid, hi)

    any_tied = jnp.max(jnp.where(tied, 1.0, 0.0))
    n_steps = (nkb * kb).bit_length()
    tie_state = (jnp.int32(0), jnp.zeros((1, tq), jnp.int32), jnp.full((1, tq), end_all, jnp.int32))
    tie_hi = lax.while_loop(lambda st: (any_tied > 0.5) & (st[0] < n_steps), tie_bisect, tie_state)[2]
    tie_end = jnp.where(tied, tie_hi, end_all)

    def write(j, carry):
        k = keys_ref[j]
        keep = (k > thr) | ((k == thr) & (j * kb + krow < tie_end))
        mask_ref[j] = jnp.where(keep, 1.0, 0.0).T.astype(mask_ref.dtype)
        return carry

    lax.fori_loop(0, n_adm, write, 0)

    def clear(j, carry):
        mask_ref[j] = jnp.zeros((tq, kb), mask_ref.dtype)
        return carry

    lax.fori_loop(n_adm, nkb, clear, 0)


def _index_mask(qit, w, ki, n_streams, t, tq, kb, past, topk):
    nkb = ki.shape[1]
    nq = t // tq
    isz = jnp.dtype(MXU_DTYPE).itemsize
    vmem = (nkb * kb * tq * 4 + 2 * nkb * kb * tq + 2 * nkb * kb * LANES * isz + 2 * IDX_DIM * IDX_HEADS * tq * isz
            + 6 * kb * IDX_HEAD_GROUP * tq * 4 + (8 << 20))
    return pl.pallas_call(
        functools.partial(_index_kernel, tq=tq, kb=kb, nkb=nkb, past=past, topk=topk),
        grid=(n_streams, nq),
        in_specs=[
            pl.BlockSpec((None, None, IDX_DIM, IDX_HEADS * tq), lambda b, i: (b, i, 0, 0)),
            pl.BlockSpec((None, IDX_HEADS, tq), lambda b, i: (b, 0, i)),
            pl.BlockSpec((None, nkb, kb, IDX_DIM), lambda b, i: (b, 0, 0, 0)),
        ],
        out_specs=pl.BlockSpec((None, nkb, tq, kb), lambda b, i: (b, 0, i, 0)),
        out_shape=jax.ShapeDtypeStruct((n_streams, nkb, t, kb), jnp.int8),
        scratch_shapes=[pltpu.VMEM((nkb, kb, tq), jnp.int32)],
        compiler_params=_cparams(2, vmem),
        name="indexer_topk_mask",
    )(qit, w, ki)


def _attn_kernel(it_ref, jt_ref, last_ref, q_ref, k_ref, v_ref, m_ref, o_ref, qs_ref, mx_ref, l_ref, acc_ref,
                 *, group):
    step = pl.program_id(1)
    tq = q_ref.shape[0]
    tk = k_ref.shape[0]
    n_kv = k_ref.shape[1] // SA_HEAD
    rows = group * tq
    n_lane_tiles = tk // LANES

    @pl.when(jt_ref[step] == 0)
    def _():
        for n in range(n_kv):
            for g in range(group):
                h = n * group + g
                qs_ref[n, g * tq:(g + 1) * tq, :] = q_ref[:, h * SA_HEAD:(h + 1) * SA_HEAD]
        mx_ref[...] = jnp.full(mx_ref.shape, NEG_BIG, F32)
        l_ref[...] = jnp.zeros(l_ref.shape, F32)
        acc_ref[...] = jnp.zeros(acc_ref.shape, F32)

    bias = jnp.where(m_ref[...].astype(F32) > 0.0, 0.0, NEG_BIG)
    kv = range(n_kv)
    s = [_nt(qs_ref[n], k_ref[:, n * SA_HEAD:(n + 1) * SA_HEAD]) for n in kv]
    s = [(x.reshape(group, tq, tk) + bias[None]).reshape(rows, tk) for x in s]
    tiles = [[x[:, c * LANES:(c + 1) * LANES] for c in range(n_lane_tiles)] for x in s]
    m_old = [mx_ref[n] for n in kv]
    m_new = [jnp.maximum(mo, jnp.max(functools.reduce(jnp.maximum, t), axis=1, keepdims=True))
             for mo, t in zip(m_old, tiles)]
    alpha = [jnp.exp2(mo - mn) for mo, mn in zip(m_old, m_new)]
    p = [[jnp.exp2(x - mn) for x in t] for t, mn in zip(tiles, m_new)]
    for n in kv:
        mx_ref[n] = m_new[n]
        l_ref[n] = alpha[n] * l_ref[n] + functools.reduce(jnp.add, p[n])
    pv = [jnp.dot(_mx(jnp.concatenate(p[n], axis=1)), v_ref[:, n * SA_HEAD:(n + 1) * SA_HEAD],
                  preferred_element_type=F32) for n in kv]
    for n in kv:
        acc_ref[n] = alpha[n] * acc_ref[n] + pv[n]

    @pl.when(last_ref[step] == 1)
    def _():
        for n in range(n_kv):
            o = acc_ref[n] / jnp.sum(l_ref[n], axis=1, keepdims=True)
            for g in range(group):
                h = n * group + g
                o_ref[:, h * SA_HEAD:(h + 1) * SA_HEAD] = o[g * tq:(g + 1) * tq].astype(o_ref.dtype)


def _attn_tiles(t, tq, tk, nkb, past):
    ii, jj, last = [], [], []
    for i in range(t // tq):
        n_adm = min(nkb, -(-(past + (i + 1) * tq) // tk))
        for j in range(n_adm):
            ii.append(i)
            jj.append(j)
            last.append(int(j == n_adm - 1))
    return tuple(jnp.asarray(np.array(x, np.int32)) for x in (ii, jj, last))


def _attention(q, k_all, v_all, mask, n_streams, t, tq, tk, past):
    m, sa_w = q.shape
    kv_w = k_all.shape[2]
    nkb = mask.shape[1]
    nq = t // tq
    n_heads = sa_w // SA_HEAD
    group = n_heads // SA_KV_HEADS
    it, jt, last = _attn_tiles(t, tq, tk, nkb, past)
    grid_spec = pltpu.PrefetchScalarGridSpec(
        num_scalar_prefetch=3,
        grid=(n_streams, int(it.shape[0])),
        in_specs=[
            pl.BlockSpec((tq, sa_w), lambda b, s, it, jt, lt: (b * nq + it[s], 0)),
            pl.BlockSpec((None, tk, kv_w), lambda b, s, it, jt, lt: (b, jt[s], 0)),
            pl.BlockSpec((None, tk, kv_w), lambda b, s, it, jt, lt: (b, jt[s], 0)),
            pl.BlockSpec((None, None, tq, tk), lambda b, s, it, jt, lt: (b, jt[s], it[s], 0)),
        ],
        out_specs=pl.BlockSpec((tq, sa_w), lambda b, s, it, jt, lt: (b * nq + it[s], 0)),
        scratch_shapes=[pltpu.VMEM((SA_KV_HEADS, group * tq, SA_HEAD), MXU_DTYPE)]
        + [pltpu.VMEM((SA_KV_HEADS, group * tq, SA_HEAD), F32)] * 3,
    )
    return pl.pallas_call(
        functools.partial(_attn_kernel, group=group),
        grid_spec=grid_spec,
        out_shape=jax.ShapeDtypeStruct((m, sa_w), MXU_DTYPE),
        compiler_params=_cparams(2, ATTN_VMEM_BYTES),
        name="masked_flash_attention",
    )(it, jt, last, q, k_all, v_all, mask)


def _prepare_weights(w_in, mu_shift, w_decay_up, decay_bias, w_a_up, a_bias, w_gate_up, k_k, k_a, r_k,
                     lnx_w, lnx_b, w_out, w_up, conv_w, conv_b, w_down, d_model):
    rw = w_decay_up.shape[1]
    rw_cols = 3 * rw + D_DECAY + D_AAA + D_GATE
    sa_w = d_model - rw
    kv_w = SA_KV_HEADS * SA_HEAD
    qi_w = IDX_HEADS * IDX_DIM
    o = {}
    o["r"] = 0
    o["wd"] = rw
    o["k"] = rw + D_DECAY
    o["v"] = 2 * rw + D_DECAY
    o["ad"] = 3 * rw + D_DECAY
    o["gd"] = 3 * rw + D_DECAY + D_AAA
    o["q"] = rw_cols
    o["ksa"] = rw_cols + sa_w
    o["vsa"] = o["ksa"] + kv_w
    o["qi"] = o["vsa"] + kv_w
    o["kiw"] = o["qi"] + qi_w
    kiw_w = IDX_DIM + IDX_HEADS

    def seg(x, name, width, pad_to=None):
        s = x[..., o[name]:o[name] + width]
        if pad_to is not None and pad_to > width:
            s = jnp.pad(s, [(0, 0)] * (s.ndim - 1) + [(0, pad_to - width)])
        return s

    def rw_small(x):
        return jnp.concatenate([seg(x, "wd", D_DECAY, LANES), seg(x, "ad", D_AAA, LANES), seg(x, "gd", D_GATE)], -1)

    w_in_l = jnp.concatenate([
        seg(w_in, "r", rw), seg(w_in, "k", rw), seg(w_in, "v", rw), seg(w_in, "q", sa_w), seg(w_in, "qi", qi_w),
        seg(w_in, "ksa", kv_w), seg(w_in, "vsa", kv_w), rw_small(w_in), seg(w_in, "kiw", kiw_w, LANES)], axis=1)
    w_in_l = jnp.pad(w_in_l, ((0, 0), (0, -w_in_l.shape[1] % IN_PROJ_TN)))
    starts = {"r": 0, "k": rw, "v": 2 * rw, "q": 3 * rw, "qi": 3 * rw + sa_w}
    starts["ksa"] = starts["qi"] + qi_w
    starts["vsa"] = starts["ksa"] + kv_w
    starts["small"] = starts["vsa"] + kv_w
    starts["kiw"] = starts["small"] + 4 * LANES
    pad_rows = lambda x, n: jnp.pad(x, ((0, n - x.shape[0]), (0, 0)))
    row = lambda x: x.reshape(1, -1).astype(F32)
    mu = mu_shift.reshape(1, -1)
    w = {
        "w_in": _mx(w_in_l),
        "mu_r": seg(mu, "r", rw), "mu_k": seg(mu, "k", rw), "mu_v": seg(mu, "v", rw), "mu_small": rw_small(mu),
        "w_decay_up": _mx(pad_rows(w_decay_up, LANES)), "decay_bias": row(decay_bias),
        "w_a_up": _mx(pad_rows(w_a_up, LANES)), "a_bias": row(a_bias),
        "w_gate_up": _mx(w_gate_up),
        "k_k": row(k_k), "k_a": row(k_a), "r_k": row(r_k), "lnx_w": row(lnx_w), "lnx_b": row(lnx_b),
        "w_out": _mx(w_out), "w_up": _mx(w_up), "conv_w": conv_w, "conv_b": row(conv_b), "w_down": _mx(w_down),
    }
    dims = {"rw": rw, "rw_cols": rw_cols, "sa_w": sa_w, "kv_w": kv_w, "qi_w": qi_w, "orig": o, "starts": starts,
            "seg": seg, "rw_small": rw_small}
    return w, dims


def _layer(x, past_k, past_v, past_ki, wkv0, shift0, conv0, norms, w, dims, tiles):
    bsz, t, d = x.shape
    past = past_k.shape[1]
    m = bsz * t
    rw, sa_w, kv_w, qi_w = dims["rw"], dims["sa_w"], dims["kv_w"], dims["qi_w"]
    starts, seg, rw_small = dims["starts"], dims["seg"], dims["rw_small"]
    assert t % CHUNK == 0 and past % CHUNK == 0
    tm, tq, kb = tiles["tm"], tiles["tq"], tiles["kb"]
    g_mix_pre, g_mix_post, g_ffn_pre, g_ffn_post = norms

    x2 = x.reshape(m, d)
    h1 = _rmsnorm(x2, g_mix_pre, tiles["tm_norm"])
    pfull = _matmul(h1, w["w_in"], tiles["tm_in"], tiles["tn_in"])

    sh = shift0.reshape(bsz, 1, -1)
    shift_l = {"r": seg(sh, "r", rw), "k": seg(sh, "k", rw), "v": seg(sh, "v", rw), "small": rw_small(sh)}
    cols_rw = {n: starts[n] // RW_GROUP for n in ("r", "k", "v")}
    cols_rw["small"] = starts["small"] // RW_SMALL
    o_rw, wkv1 = _rwkv(pfull, cols_rw, shift_l, wkv0, w, bsz, t)

    pos = past + jnp.arange(t, dtype=jnp.int32)
    cols_sa = {"q": starts["q"] // sa_w, "ksa": starts["ksa"] // kv_w, "qi": starts["qi"] // qi_w,
               "kiw": starts["kiw"] // LANES}
    q_r, k_r, qi_r, kiw_r = _rope(pfull, cols_sa, _rope_tables(pos), bsz, t, tq, sa_w, kv_w, qi_w)
    k_new = k_r.reshape(bsz, t, kv_w)
    v_new = pfull[:, starts["vsa"]:starts["vsa"] + kv_w].reshape(bsz, t, kv_w)
    ki_new = kiw_r[:, :IDX_DIM].reshape(bsz, t, IDX_DIM)
    l_all = past + t
    lp = -(-l_all // kb) * kb
    nkb = lp // kb
    cat = lambda old, new: jnp.pad(
        jnp.concatenate([_mx(old.reshape(bsz, past, new.shape[-1])), _mx(new)], axis=1),
        ((0, 0), (0, lp - l_all), (0, 0)))
    k_all, v_all, ki_all = cat(past_k, k_new), cat(past_v, v_new), cat(past_ki, ki_new)
    nq = t // tq
    qit = (qi_r.reshape(bsz, nq, tq, IDX_HEADS, IDX_DIM).transpose(0, 1, 4, 3, 2)
           .reshape(bsz, nq, IDX_DIM, IDX_HEADS * tq))
    w_idx = kiw_r[:, IDX_DIM:IDX_DIM + IDX_HEADS].reshape(bsz, t, IDX_HEADS).transpose(0, 2, 1)
    topk = min(TOPK_MAX, l_all // 4)
    mask = _index_mask(qit, w_idx, ki_all.reshape(bsz, nkb, kb, IDX_DIM), bsz, t, tq, kb, past, topk)
    o_sa = _attention(q_r, k_all, v_all, mask, bsz, t, tq, kb, past)

    x1 = _matmul_norm_residual([o_rw, o_sa], w["w_out"], x2, g_mix_post, tm, tiles["tn_out"], "out_proj")
    h2 = _rmsnorm(x1, g_ffn_pre, tiles["tm_norm"])
    act, conv1 = _ffn_up(h2, w["w_up"], conv0, w["conv_w"], w["conv_b"], bsz, tiles["tm_up"], tiles["tn_up"])
    x_out = _matmul_norm_residual([act], w["w_down"], x1, g_ffn_post, tm, tiles["tn_out"], "ffn_down")

    last = pfull.reshape(bsz, t, -1)[:, t - 1:t]
    sm = last[..., starts["small"]:starts["small"] + 4 * LANES]
    shift1 = jnp.concatenate([
        last[..., starts["r"]:starts["r"] + rw], sm[..., :D_DECAY],
        last[..., starts["k"]:starts["k"] + rw], last[..., starts["v"]:starts["v"] + rw],
        sm[..., LANES:LANES + D_AAA], sm[..., 2 * LANES:2 * LANES + D_GATE]], axis=-1)
    new = (k_new.reshape(bsz, t, SA_KV_HEADS, SA_HEAD), v_new.reshape(bsz, t, SA_KV_HEADS, SA_HEAD), ki_new,
           wkv1, shift1, conv1)
    return x_out.reshape(bsz, t, d), new


def _tiles(bsz, t):
    big = t >= 1024
    return {
        "tm_norm": 256 if big else CHUNK,
        "tm_in": 1024 if big else bsz * t,
        "tn_in": IN_PROJ_TN,
        "tm": 512 if big else bsz * t,
        "tn_out": 512,
        "tm_up": 1024 if big else bsz * t,
        "tn_up": 512,
        "tq": 256 if big else CHUNK,
        "kb": 512,
    }


def _run_stream(x, past_k, past_v, past_ki, wkv0, shift0, conv0, norm_w, weights, dims, depth):
    outs = []
    tiles = _tiles(x.shape[0], x.shape[1])
    for l in range(depth):
        w_l = {n: v[l] for n, v in weights.items()}
        norms = tuple(g[l].reshape(1, -1) for g in norm_w)
        x, st = _layer(x, past_k[l], past_v[l], past_ki[l], wkv0[l], shift0[l], conv0[l], norms, w_l, dims[l], tiles)
        outs.append(st)
    return x, [jnp.stack(s) for s in zip(*outs)]


def kernel(x_prompt, x_sample, cache_k, cache_v, cache_kidx, state_wkv, state_shift, state_conv, norm_mix_pre, norm_mix_post, norm_ffn_pre, norm_ffn_post, w_in, mu_shift, w_decay_up, decay_bias, w_a_up, a_bias, w_gate_up, k_k, k_a, r_k, lnx_w, lnx_b, w_out, w_up, conv_w, conv_b, w_down):
    depth, d_model = norm_mix_pre.shape
    per_layer = [_prepare_weights(w_in[l], mu_shift[l], w_decay_up[l], decay_bias[l], w_a_up[l], a_bias[l],
                                  w_gate_up[l], k_k[l], k_a[l], r_k[l], lnx_w[l], lnx_b[l], w_out[l], w_up[l],
                                  conv_w[l], conv_b[l], w_down[l], d_model) for l in range(depth)]
    weights = {n: [pw[0][n] for pw in per_layer] for n in per_layer[0][0]}
    dims = [pw[1] for pw in per_layer]
    norm_w = (norm_mix_pre, norm_mix_post, norm_ffn_pre, norm_ffn_post)

    dt = x_prompt.dtype
    bp = x_prompt.shape[0]
    rw_heads = state_wkv.shape[2]
    zk = jnp.zeros((depth, bp, 0, SA_KV_HEADS, SA_HEAD), dt)
    zki = jnp.zeros((depth, bp, 0, IDX_DIM), dt)
    zwkv = jnp.zeros((depth, bp, rw_heads, RW_HEAD, RW_HEAD), dt)
    zshift = jnp.zeros((depth, bp, 1, state_shift.shape[-1]), dt)
    zconv = jnp.zeros((depth, bp, CONV_W - 1, state_conv.shape[-1]), dt)
    y_prompt, p_new = _run_stream(x_prompt, zk, zk, zki, zwkv, zshift, zconv, norm_w, weights, dims, depth)
    y_sample, s_new = _run_stream(x_sample, cache_k, cache_v, cache_kidx, state_wkv, state_shift, state_conv,
                                  norm_w, weights, dims, depth)
    return (y_prompt, y_sample, *p_new, *s_new)
```

```python
import functools

import numpy as np
import jax
import jax.numpy as jnp
from jax import lax
from jax.experimental import pallas as pl
from jax.experimental.pallas import tpu as pltpu

F32 = jnp.float32
MXU_DTYPE = jnp.bfloat16

CHUNK = 64
RW_HEAD = 64
D_DECAY = 96
D_AAA = 96
D_GATE = 256
SA_HEAD = 128
SA_KV_HEADS = 4
IDX_HEADS = 16
IDX_DIM = 64
TOPK_MAX = 256
CONV_W = 3
ROPE_THETA = 10000.0
NORM_EPS = 1e-6
LNX_EPS = 64e-5

LANES = 128
V7X_VMEM_LIMIT_BYTES = 60000 * 1024
SMALL_CALL_VMEM_BYTES = 32 << 20
ATTN_VMEM_BYTES = 40 << 20
INT_MIN = -(2 ** 31)
NEG_BIG = -1e30


def _cparams(n_grid, vmem_bytes):
    limit = int(min(V7X_VMEM_LIMIT_BYTES, max(SMALL_CALL_VMEM_BYTES, vmem_bytes)))
    return pltpu.CompilerParams(dimension_semantics=("arbitrary",) * n_grid, vmem_limit_bytes=limit)


def _nt(a, b):
    return lax.dot_general(a, b, (((1,), (1,)), ((), ())), preferred_element_type=F32)


def _mx(x):
    return x.astype(MXU_DTYPE)


def _rmsnorm_kernel(x_ref, g_ref, o_ref):
    x = x_ref[...]
    y = x * lax.rsqrt(jnp.mean(x * x, axis=-1, keepdims=True) + NORM_EPS)
    o_ref[...] = (y * g_ref[...]).astype(o_ref.dtype)


def _rmsnorm(x, g, tm):
    m, d = x.shape
    return pl.pallas_call(
        _rmsnorm_kernel,
        grid=(m // tm,),
        in_specs=[pl.BlockSpec((tm, d), lambda i: (i, 0)), pl.BlockSpec((1, d), lambda i: (0, 0))],
        out_specs=pl.BlockSpec((tm, d), lambda i: (i, 0)),
        out_shape=jax.ShapeDtypeStruct((m, d), MXU_DTYPE),
        compiler_params=_cparams(1, 6 * tm * d * 4),
        name="rmsnorm",
    )(x, g)


def _mm_kernel(a_ref, b_ref, o_ref):
    o_ref[...] = jnp.dot(a_ref[...], b_ref[...], preferred_element_type=F32)


IN_PROJ_TN = 1024


def _matmul(a, b, tm, tn):
    m, k = a.shape
    n = b.shape[1]
    isz = jnp.dtype(MXU_DTYPE).itemsize
    vmem = 2 * (tm * k * isz + k * tn * isz + tm * tn * 4) + 2 * tm * tn * 4 + (8 << 20)
    return pl.pallas_call(
        _mm_kernel,
        grid=(m // tm, n // tn),
        in_specs=[pl.BlockSpec((tm, k), lambda i, j: (i, 0)), pl.BlockSpec((k, tn), lambda i, j: (0, j))],
        out_specs=pl.BlockSpec((tm, tn), lambda i, j: (i, j)),
        out_shape=jax.ShapeDtypeStruct((m, n), F32),
        compiler_params=_cparams(2, vmem),
        name="in_proj",
    )(a, b)


MM_NORM_ROWS = 128


def _mm_norm_kernel(*refs, nn, n_ops):
    a_refs, w_refs = refs[:n_ops], refs[n_ops:2 * n_ops]
    x_ref, gpost_ref, x1_ref = refs[2 * n_ops:]
    n = pl.program_id(1)
    tm, d = x1_ref.shape
    tn = w_refs[0].shape[1]
    parts = [jnp.dot(a_ref[...], w_ref[...], preferred_element_type=F32) for a_ref, w_ref in zip(a_refs, w_refs)]
    x1_ref[:, pl.ds(pl.multiple_of(n * tn, LANES), tn)] = functools.reduce(jnp.add, parts)

    @pl.when(n == nn - 1)
    def _():
        step = min(tm, MM_NORM_ROWS)
        for r in range(0, tm, step):
            rows = slice(r, r + step)
            y = x1_ref[rows, :]
            y = y * lax.rsqrt(jnp.mean(y * y, axis=-1, keepdims=True) + NORM_EPS) * gpost_ref[...]
            x1_ref[rows, :] = x_ref[rows, :] + y


def _matmul_norm_residual(a_slabs, w, x, g_post, tm, tn, name):
    n_ops = len(a_slabs)
    m, ks = a_slabs[0].shape
    kdim = n_ops * ks
    d = w.shape[1]
    nn = d // tn
    isz = jnp.dtype(MXU_DTYPE).itemsize
    fixed = 2 * kdim * tn * isz + tm * d * 4 + 2 * tm * tn * 4 + 6 * MM_NORM_ROWS * d * 4 + (4 << 20)
    per_copy = tm * kdim * isz + tm * d * 4
    row_bufs = 2 if fixed + 2 * per_copy <= V7X_VMEM_LIMIT_BYTES - (4 << 20) else 1
    vmem = fixed + row_bufs * per_copy
    w_rows = lambda p: (lambda i, n: (p, n))
    return pl.pallas_call(
        functools.partial(_mm_norm_kernel, nn=nn, n_ops=n_ops),
        grid=(m // tm, nn),
        in_specs=[pl.BlockSpec((tm, ks), lambda i, n: (i, 0), pipeline_mode=pl.Buffered(row_bufs))] * n_ops
        + [pl.BlockSpec((ks, tn), w_rows(p)) for p in range(n_ops)]
        + [
            pl.BlockSpec((tm, d), lambda i, n: (i, 0), pipeline_mode=pl.Buffered(1)),
            pl.BlockSpec((1, d), lambda i, n: (0, 0)),
        ],
        out_specs=pl.BlockSpec((tm, d), lambda i, n: (i, 0), pipeline_mode=pl.Buffered(row_bufs)),
        out_shape=jax.ShapeDtypeStruct((m, d), F32),
        compiler_params=_cparams(2, vmem),
        name=name,
    )(*a_slabs, *([w] * n_ops), x, g_post)


def _gelu_tanh(x):
    return 0.5 * x * (1.0 + jnp.tanh(0.7978845608028654 * (x + 0.044715 * x * x * x)))


def _ffn_up_kernel(h_ref, wg_ref, wv_ref, c0g_ref, c0v_ref, cwg_ref, cwv_ref, cbg_ref, cbv_ref,
                   a_ref, cg_ref, cv_ref, carry_g, carry_v, *, blocks_per_stream, tm):
    i = pl.program_id(1) % blocks_per_stream
    n_str = carry_g.shape[0]
    ts = tm // n_str

    @pl.when(i == 0)
    def _():
        carry_g[...] = c0g_ref[...]
        carry_v[...] = c0v_ref[...]

    h = h_ref[...]
    row = lax.broadcasted_iota(jnp.int32, (tm, 1), 0)

    def conv(u, carry_ref, cw_ref, cb_ref, out_ref):
        u1 = pltpu.roll(u, 1, axis=0)
        u2 = pltpu.roll(u, 2, axis=0)
        for s in range(n_str):
            p = carry_ref[s]
            u1 = jnp.where(row == s * ts, p[1:2], u1)
            u2 = jnp.where(row == s * ts, p[0:1], jnp.where(row == s * ts + 1, p[1:2], u2))
            last = u[(s + 1) * ts - 2:(s + 1) * ts]
            carry_ref[s] = last
            out_ref[s] = last
        cw = cw_ref[...]
        return u2 * cw[0:1] + u1 * cw[1:2] + u * cw[2:3] + cb_ref[...]

    gate = conv(jnp.dot(h, wg_ref[...], preferred_element_type=F32), carry_g, cwg_ref, cbg_ref, cg_ref)
    val = conv(jnp.dot(h, wv_ref[...], preferred_element_type=F32), carry_v, cwv_ref, cbv_ref, cv_ref)
    a_ref[...] = (_gelu_tanh(gate) * val).astype(a_ref.dtype)


def _ffn_up(h, w_up, conv0, conv_w, conv_b, n_streams, tm, tn):
    m, d = h.shape
    f2 = w_up.shape[1]
    f = f2 // 2
    nj = f // tn
    nr = m // tm
    t = m // n_streams
    spb = max(1, tm // t)
    bps = max(1, t // tm)
    isz = jnp.dtype(MXU_DTYPE).itemsize
    vmem = 2 * (tm * d * isz + 2 * d * tn * isz + tm * tn * isz) + 10 * tm * tn * 4 + (4 << 20)
    col = lambda off: (lambda j, r: (0, off + j))
    st = lambda off: (lambda j, r: (r // bps, 0, off + j))
    a, cg, cv = pl.pallas_call(
        functools.partial(_ffn_up_kernel, blocks_per_stream=bps, tm=tm),
        grid=(nj, nr),
        in_specs=[
            pl.BlockSpec((tm, d), lambda j, r: (r, 0)),
            pl.BlockSpec((d, tn), col(0)),
            pl.BlockSpec((d, tn), col(nj)),
            pl.BlockSpec((spb, CONV_W - 1, tn), st(0)),
            pl.BlockSpec((spb, CONV_W - 1, tn), st(nj)),
            pl.BlockSpec((CONV_W, tn), col(0)),
            pl.BlockSpec((CONV_W, tn), col(nj)),
            pl.BlockSpec((1, tn), col(0)),
            pl.BlockSpec((1, tn), col(nj)),
        ],
        out_specs=[
            pl.BlockSpec((tm, tn), lambda j, r: (r, j)),
            pl.BlockSpec((spb, CONV_W - 1, tn), st(0)),
            pl.BlockSpec((spb, CONV_W - 1, tn), st(0)),
        ],
        out_shape=[
            jax.ShapeDtypeStruct((m, f), MXU_DTYPE),
            jax.ShapeDtypeStruct((n_streams, CONV_W - 1, f), F32),
            jax.ShapeDtypeStruct((n_streams, CONV_W - 1, f), F32),
        ],
        scratch_shapes=[pltpu.VMEM((spb, CONV_W - 1, tn), F32)] * 2,
        compiler_params=_cparams(2, vmem),
        name="ffn_up_conv",
    )(h, w_up, w_up, conv0, conv0, conv_w, conv_w, conv_b, conv_b)
    return a, jnp.concatenate([cg, cv], axis=-1)


RW_GROUP = 16 * LANES
RW_SMALL = 4 * LANES


def _softplus(z):
    return jnp.maximum(z, 0.0) + jnp.log(1.0 + jnp.exp(-jnp.abs(z)))


def _sigmoid(z):
    return 1.0 / (1.0 + jnp.exp(-z))


def _dg(a, b, dims):
    return lax.dot_general(_mx(a), _mx(b), (dims, ((), ())), preferred_element_type=F32)


_NN = ((1,), (0,))
_NT = ((1,), (1,))
_TN = ((0,), (0,))


def _segsum(x, seg):
    return jnp.dot(_mx(x), seg, preferred_element_type=F32)


def _each(f, *lists):
    return [f(*args) for args in zip(*lists)]


def _wkv_chunk(r, k, v, kk, b, cum, logd, s_bd, consts):
    m0, m1, tri_mask = consts
    c = r[0].shape[0]
    c2 = 2 * c
    stack = lambda x: jnp.concatenate([x * m0, x * m1], axis=0)
    g_end = _each(lambda cm: jnp.exp(cm[c - 1:c]), cum)
    e_inc = _each(jnp.exp, cum)
    e_prev = _each(lambda cm, ld: jnp.exp(cm - ld), cum, logd)
    e_neg = _each(lambda cm: jnp.exp(-cm), cum)
    e_tail = _each(lambda g, e: g * e, g_end, e_neg)
    vs = _each(stack, v)
    lhs = _each(lambda kk_, r_, ep, ei: jnp.concatenate([stack(kk_ * ep), stack(r_ * ei)], axis=0),
                kk, r, e_prev, e_inc)
    rhs = _each(lambda k_, b_, en: jnp.concatenate([stack(k_ * en), stack(b_ * en)], axis=0), k, b, e_neg)
    tails = _each(lambda k_, b_, et: jnp.concatenate([stack(k_ * et), stack(b_ * et)], axis=0), k, b, e_tail)
    amat = _each(lambda x, y: jnp.where(tri_mask, _dg(x, y, _NT), 0.0), lhs, rhs)
    sp = _each(lambda x, s: _dg(x, s, _NT), lhs, s_bd)
    av = _each(lambda a_, v_: _dg(a_[:, :c2], v_, _NN), amat, vs)
    pw = _each(lambda a_: -a_[:c2, c2:], amat)
    x = _each(lambda s_, a_: s_[:c2] + a_[:c2], sp, av)
    n_lvl = int(np.log2(c))
    for lvl in range(n_lvl - 1):
        z = _each(lambda p_, x_: _dg(p_, jnp.concatenate([p_, x_], axis=1), _NN), pw, x)
        pw = _each(lambda z_: z_[:, :c2], z)
        x = _each(lambda x_, z_: x_ + z_[:, c2:], x, z)
    u = _each(lambda p_, x_: x_ + _dg(p_, x_, _NN), pw, x)
    ys = _each(lambda s_, a_, am, u_: s_[c2:] + a_[c2:] - _dg(am[c2:, c2:], u_, _NN), sp, av, amat, u)
    y = _each(lambda y_: y_[:c] + y_[c:], ys)
    s_new = _each(lambda s, g, v_, u_, t_: s * g + _dg(jnp.concatenate([v_, -u_], axis=0), t_, _TN),
                  s_bd, g_end, vs, u, tails)
    return y, s_new


def _rwkv_kernel(pr_ref, pk_ref, pv_ref, ps_ref, sr_ref, sk_ref, sv_ref, ss_ref, wkv0_ref,
                 mur_ref, muk_ref, muv_ref, mus_ref, wdec_ref, dbias_ref, wa_ref, abias_ref, wg_ref,
                 kk_ref, ka_ref, rk_ref, lnw_ref, lnb_ref,
                 o_ref, wkv1_ref,
                 cr_ref, ck_ref, cv_ref, cs_ref, state_ref, *, n_chunks):
    c = pl.program_id(2)
    C = CHUNK
    npairs = RW_GROUP // LANES

    lane = lax.broadcasted_iota(jnp.int32, (1, LANES), 1)
    m0 = (lane < RW_HEAD).astype(F32)
    m1 = 1.0 - m0
    ri = lax.broadcasted_iota(jnp.int32, (4 * C, 4 * C), 0)
    ci = lax.broadcasted_iota(jnp.int32, (4 * C, 4 * C), 1)
    same_head = ((ri // C) % 2) == ((ci // C) % 2)
    tri_mask = same_head & ((ri % C) + ri // (2 * C) > (ci % C))
    li = lax.broadcasted_iota(jnp.int32, (LANES, LANES), 0)
    lj = lax.broadcasted_iota(jnp.int32, (LANES, LANES), 1)
    seg = ((li // RW_HEAD) == (lj // RW_HEAD)).astype(MXU_DTYPE)
    consts = (m0, m1, tri_mask)

    @pl.when(c == 0)
    def _():
        cr_ref[...] = sr_ref[...]
        ck_ref[...] = sk_ref[...]
        cv_ref[...] = sv_ref[...]
        cs_ref[...] = ss_ref[...]
        z = jnp.zeros((RW_HEAD, RW_HEAD), F32)
        for p in range(npairs):
            s0 = wkv0_ref[2 * p]
            s1 = wkv0_ref[2 * p + 1]
            state_ref[p] = jnp.concatenate(
                [jnp.concatenate([s0, z], axis=1), jnp.concatenate([z, s1], axis=1)], axis=0)

    row = lax.broadcasted_iota(jnp.int32, (C, 1), 0)

    def shifted(p_ref, carry_ref, mu_ref):
        p = p_ref[...]
        prev = jnp.where(row == 0, carry_ref[...], pltpu.roll(p, 1, axis=0))
        carry_ref[...] = p[C - 1:C]
        return p + (prev - p) * mu_ref[...]

    r = shifted(pr_ref, cr_ref, mur_ref)
    k = shifted(pk_ref, ck_ref, muk_ref)
    v = shifted(pv_ref, cv_ref, muv_ref)
    sm = shifted(ps_ref, cs_ref, mus_ref)
    wd, ad, gd = sm[:, 0:LANES], sm[:, LANES:2 * LANES], sm[:, 2 * LANES:4 * LANES]

    dec_in = dbias_ref[...] + jnp.dot(_mx(jnp.tanh(wd)), wdec_ref[...], preferred_element_type=F32)
    w_log = -_softplus(-dec_in) - 0.5
    logd = -jnp.exp(w_log)
    a = _sigmoid(abias_ref[...] + jnp.dot(_mx(ad), wa_ref[...], preferred_element_type=F32))
    g = jnp.dot(_mx(_sigmoid(gd)), wg_ref[...], preferred_element_type=F32)
    kk = k * kk_ref[...]
    k2 = k * (1.0 + (a - 1.0) * ka_ref[...])
    rkr = r * k2 * rk_ref[...]
    cum = logd
    for sft in (1, 2, 4, 8, 16, 32):
        cum = cum + jnp.where(row >= sft, pltpu.roll(cum, sft, axis=0), 0.0)

    pairs = lambda x: [x[:, p * LANES:(p + 1) * LANES] for p in range(npairs)]
    r_p, k_p, v_p, a_p, cum_p, logd_p = pairs(r), pairs(k2), pairs(v), pairs(a), pairs(cum), pairs(logd)
    sums = _each(lambda kk_, rkr_: _segsum(jnp.concatenate([kk_ * kk_, rkr_], axis=0), seg), pairs(kk), pairs(rkr))
    kk_p = _each(lambda kk_, s_: kk_ * lax.rsqrt(jnp.maximum(s_[:C], 1e-24)), pairs(kk), sums)
    b_p = _each(lambda kk_, a_: kk_ * a_, kk_p, a_p)
    y_p, s_new = _wkv_chunk(r_p, k_p, v_p, kk_p, b_p, cum_p, logd_p, [state_ref[p] for p in range(npairs)], consts)
    for p in range(npairs):
        state_ref[p] = s_new[p]
    yc = _each(lambda y_: y_ - _segsum(y_, seg) * (1.0 / RW_HEAD), y_p)
    var = _each(lambda yc_: _segsum(yc_ * yc_, seg) * (1.0 / RW_HEAD), yc)
    for p in range(npairs):
        sl = slice(p * LANES, (p + 1) * LANES)
        yn = yc[p] * lax.rsqrt(var[p] + LNX_EPS) * lnw_ref[:, sl] + lnb_ref[:, sl]
        o_ref[:, sl] = ((yn + sums[p][C:] * v_p[p]) * g[:, sl]).astype(o_ref.dtype)

    @pl.when(c == n_chunks - 1)
    def _():
        for p in range(npairs):
            s = state_ref[p]
            wkv1_ref[2 * p] = s[:RW_HEAD, :RW_HEAD]
            wkv1_ref[2 * p + 1] = s[RW_HEAD:, RW_HEAD:]


def _rwkv(pfull, cols, shift0, wkv0, w, n_streams, t):
    m = pfull.shape[0]
    rw = w["k_k"].shape[1]
    ng = rw // RW_GROUP
    nc = t // CHUNK
    hg = RW_GROUP // RW_HEAD
    rowblk = lambda off: (lambda b, g, c: (b * nc + c, off + g))
    fixed = lambda off: (lambda b, g, c: (b * nc + c, off))
    st = lambda b, g, c: (b, 0, g)
    st0 = lambda b, g, c: (b, 0, 0)
    wcol = lambda b, g, c: (0, g)
    in_specs = [
        pl.BlockSpec((CHUNK, RW_GROUP), rowblk(cols["r"])),
        pl.BlockSpec((CHUNK, RW_GROUP), rowblk(cols["k"])),
        pl.BlockSpec((CHUNK, RW_GROUP), rowblk(cols["v"])),
        pl.BlockSpec((CHUNK, RW_SMALL), fixed(cols["small"])),
        pl.BlockSpec((None, 1, RW_GROUP), st),
        pl.BlockSpec((None, 1, RW_GROUP), st),
        pl.BlockSpec((None, 1, RW_GROUP), st),
        pl.BlockSpec((None, 1, RW_SMALL), st0),
        pl.BlockSpec((None, hg, RW_HEAD, RW_HEAD), lambda b, g, c: (b, g, 0, 0)),
        pl.BlockSpec((1, RW_GROUP), wcol), pl.BlockSpec((1, RW_GROUP), wcol), pl.BlockSpec((1, RW_GROUP), wcol),
        pl.BlockSpec((1, RW_SMALL), lambda b, g, c: (0, 0)),
        pl.BlockSpec((LANES, RW_GROUP), wcol), pl.BlockSpec((1, RW_GROUP), wcol),
        pl.BlockSpec((LANES, RW_GROUP), wcol), pl.BlockSpec((1, RW_GROUP), wcol),
        pl.BlockSpec((2 * LANES, RW_GROUP), wcol),
        pl.BlockSpec((1, RW_GROUP), wcol), pl.BlockSpec((1, RW_GROUP), wcol), pl.BlockSpec((1, RW_GROUP), wcol),
        pl.BlockSpec((1, RW_GROUP), wcol), pl.BlockSpec((1, RW_GROUP), wcol),
    ]
    o, wkv1 = pl.pallas_call(
        functools.partial(_rwkv_kernel, n_chunks=nc),
        grid=(n_streams, ng, nc),
        in_specs=in_specs,
        out_specs=[
            pl.BlockSpec((CHUNK, RW_GROUP), lambda b, g, c: (b * nc + c, g)),
            pl.BlockSpec((None, hg, RW_HEAD, RW_HEAD), lambda b, g, c: (b, g, 0, 0)),
        ],
        out_shape=[
            jax.ShapeDtypeStruct((m, rw), MXU_DTYPE),
            jax.ShapeDtypeStruct((n_streams, rw // RW_HEAD, RW_HEAD, RW_HEAD), F32),
        ],
        scratch_shapes=[pltpu.VMEM((1, RW_GROUP), F32)] * 3 + [pltpu.VMEM((1, RW_SMALL), F32)]
        + [pltpu.VMEM((RW_GROUP // LANES, LANES, LANES), F32)],
        compiler_params=_cparams(3, SMALL_CALL_VMEM_BYTES),
        name="rwkv7_chunked",
    )(pfull, pfull, pfull, pfull, shift0["r"], shift0["k"], shift0["v"], shift0["small"], wkv0,
      w["mu_r"], w["mu_k"], w["mu_v"], w["mu_small"], w["w_decay_up"], w["decay_bias"], w["w_a_up"], w["a_bias"],
      w["w_gate_up"], w["k_k"], w["k_a"], w["r_k"], w["lnx_w"], w["lnx_b"])
    return o, wkv1


def _rope_kernel(q_ref, k_ref, qi_ref, kiw_ref, ca_ref, sa_ref, cb_ref, sb1_ref, sb2_ref,
                 qo_ref, ko_ref, qio_ref, kiwo_ref):
    ca, sa = ca_ref[...], sa_ref[...]
    cb, sb1, sb2 = cb_ref[...], sb1_ref[...], sb2_ref[...]

    def rope_head(x):
        return x * ca + pltpu.roll(x, SA_HEAD // 2, axis=1) * sa

    def rope_idx(x):
        return (x * cb + pltpu.roll(x, LANES - IDX_DIM // 2, axis=1) * sb1
                + pltpu.roll(x, IDX_DIM // 2, axis=1) * sb2)

    q_scale = float(SA_HEAD ** -0.5 * np.log2(np.e))
    for h in range(q_ref.shape[1] // LANES):
        sl = slice(h * LANES, (h + 1) * LANES)
        qo_ref[:, sl] = (rope_head(q_ref[:, sl]) * q_scale).astype(qo_ref.dtype)
    for h in range(k_ref.shape[1] // LANES):
        sl = slice(h * LANES, (h + 1) * LANES)
        ko_ref[:, sl] = rope_head(k_ref[:, sl])
    for hp in range(qi_ref.shape[1] // LANES):
        sl = slice(hp * LANES, (hp + 1) * LANES)
        qio_ref[:, sl] = rope_idx(qi_ref[:, sl]).astype(qio_ref.dtype)
    x = kiw_ref[...]
    lane = lax.broadcasted_iota(jnp.int32, (1, LANES), 1)
    kiwo_ref[...] = jnp.where(lane < IDX_DIM, rope_idx(x), x * float((IDX_HEADS * IDX_DIM) ** -0.5))


def _rope_tables(pos):
    def ang(half):
        inv = ROPE_THETA ** (-jnp.arange(half, dtype=F32) / half)
        return pos.astype(F32)[:, None] * inv[None, :]
    aa = ang(SA_HEAD // 2)
    ca = jnp.concatenate([jnp.cos(aa), jnp.cos(aa)], axis=1)
    sa = jnp.concatenate([-jnp.sin(aa), jnp.sin(aa)], axis=1)
    ab = ang(IDX_DIM // 2)
    z = jnp.zeros_like(ab)
    cb = jnp.concatenate([jnp.cos(ab)] * 4, axis=1)
    sb1 = jnp.concatenate([-jnp.sin(ab), z, -jnp.sin(ab), z], axis=1)
    sb2 = jnp.concatenate([z, jnp.sin(ab), z, jnp.sin(ab)], axis=1)
    return ca, sa, cb, sb1, sb2


def _rope(pfull, cols, tables, n_streams, t, tm, sa_w, kv_w, qi_w):
    m = pfull.shape[0]
    bps = t // tm
    tab = pl.BlockSpec((tm, LANES), lambda r: (r % bps, 0))
    return pl.pallas_call(
        _rope_kernel,
        grid=(m // tm,),
        in_specs=[
            pl.BlockSpec((tm, sa_w), lambda r: (r, cols["q"])),
            pl.BlockSpec((tm, kv_w), lambda r: (r, cols["ksa"])),
            pl.BlockSpec((tm, qi_w), lambda r: (r, cols["qi"])),
            pl.BlockSpec((tm, LANES), lambda r: (r, cols["kiw"])),
            tab, tab, tab, tab, tab,
        ],
        out_specs=[
            pl.BlockSpec((tm, sa_w), lambda r: (r, 0)),
            pl.BlockSpec((tm, kv_w), lambda r: (r, 0)),
            pl.BlockSpec((tm, qi_w), lambda r: (r, 0)),
            pl.BlockSpec((tm, LANES), lambda r: (r, 0)),
        ],
        out_shape=[
            jax.ShapeDtypeStruct((m, sa_w), MXU_DTYPE),
            jax.ShapeDtypeStruct((m, kv_w), F32),
            jax.ShapeDtypeStruct((m, qi_w), MXU_DTYPE),
            jax.ShapeDtypeStruct((m, LANES), F32),
        ],
        compiler_params=_cparams(1, SMALL_CALL_VMEM_BYTES),
        name="rope",
    )(pfull, pfull, pfull, pfull, *tables)


IDX_HEAD_GROUP = 4
IDX_SCORE_ROWS = 128
IDX_VALUE_PASSES = 16


FOLD_CHAINS = 2


def _fold_sublane_tiles(x, op):
    parts = [x[r:r + 8] for r in range(0, x.shape[0], 8)]
    chains = [functools.reduce(op, parts[c::FOLD_CHAINS]) for c in range(min(FOLD_CHAINS, len(parts)))]
    return functools.reduce(op, chains)


def _f32_to_key(x):
    bits = lax.bitcast_convert_type(x, jnp.int32)
    return bits ^ ((bits >> 31) & jnp.int32(0x7FFFFFFF))


def _key_to_f32(key):
    return lax.bitcast_convert_type(key ^ ((key >> 31) & jnp.int32(0x7FFFFFFF)), F32)


def _index_kernel(qit_ref, w_ref, ki_ref, mask_ref, keys_ref, *, tq, kb, nkb, past, topk):
    i = pl.program_id(1)
    n_adm = jnp.minimum(nkb, (past + (i + 1) * tq + kb - 1) // kb)
    chunk_bits = CHUNK.bit_length() - 1
    qpos = past + i * tq + lax.broadcasted_iota(jnp.int32, (1, tq), 1)
    lim = ((qpos >> chunk_bits) + 1) << chunk_bits
    krow = lax.broadcasted_iota(jnp.int32, (kb, 1), 0)
    hg = IDX_HEAD_GROUP
    fold = _fold_sublane_tiles

    def score_block(j, carry):
        smax, smin = carry
        for r0 in range(0, kb, IDX_SCORE_ROWS):
            rows = slice(r0, r0 + IDX_SCORE_ROWS)
            kblk = ki_ref[j, rows, :]
            acc = jnp.zeros((IDX_SCORE_ROWS, tq), F32)
            for g0 in range(0, IDX_HEADS, hg):
                s = jnp.dot(kblk, qit_ref[:, g0 * tq:(g0 + hg) * tq], preferred_element_type=F32)
                for g in range(hg):
                    acc = acc + jnp.maximum(s[:, g * tq:(g + 1) * tq], 0.0) * w_ref[g0 + g:g0 + g + 1, :]
            score = acc + 0.0
            adm = j * kb + r0 + krow[:IDX_SCORE_ROWS] < lim
            keys_ref[j, rows, :] = jnp.where(adm, _f32_to_key(score), jnp.int32(INT_MIN))
            smax = jnp.maximum(smax, fold(jnp.where(adm, score, -jnp.inf), jnp.maximum))
            smin = jnp.minimum(smin, fold(jnp.where(adm, score, jnp.inf), jnp.minimum))
        return smax, smin

    smax, smin = lax.fori_loop(0, n_adm, score_block,
                               (jnp.full((8, tq), -jnp.inf, F32), jnp.full((8, tq), jnp.inf, F32)))
    lo0 = _f32_to_key(jnp.min(smin, axis=0, keepdims=True))
    hi0 = _f32_to_key(jnp.max(smax, axis=0, keepdims=True)) + 1

    def count_ge(thr):
        def body(j, cnt):
            return cnt + fold(jnp.where(keys_ref[j] >= thr, 1.0, 0.0), jnp.add)
        return jnp.sum(lax.fori_loop(0, n_adm, body, jnp.zeros((8, tq), F32)), axis=0, keepdims=True)

    def midpoint(lo, hi):
        return (lo >> 1) + (hi >> 1) + (lo & hi & 1)

    def unfinished(lo, hi, cnt_lo):
        active = (midpoint(lo, hi) != lo) & (cnt_lo != float(topk))
        return jnp.max(jnp.where(active, 1.0, 0.0))

    def bisect(state):
        p, lo, hi, cnt_lo, cnt_hi, _ = state
        mid_v = _f32_to_key(0.5 * _key_to_f32(lo) + 0.5 * _key_to_f32(hi))
        by_value = (p < IDX_VALUE_PASSES) & (mid_v > lo) & (mid_v < hi)
        mid = jnp.where(by_value, mid_v, midpoint(lo, hi))
        cnt = count_ge(mid)
        ge = cnt >= float(topk)
        lo, cnt_lo = jnp.where(ge, mid, lo), jnp.where(ge, cnt, cnt_lo)
        hi, cnt_hi = jnp.where(ge, hi, mid), jnp.where(ge, cnt_hi, cnt)
        return p + 1, lo, hi, cnt_lo, cnt_hi, unfinished(lo, hi, cnt_lo)

    cnt_lo0 = lim.astype(F32)
    state = (jnp.int32(0), lo0, hi0, cnt_lo0, jnp.zeros((1, tq), F32), unfinished(lo0, hi0, cnt_lo0))
    _, thr, _, n_ge, n_gt, _ = lax.while_loop(lambda st: st[5] > 0.5, bisect, state)

    tied = n_ge > float(topk)
    need = float(topk) - n_gt
    end_all = jnp.int32(nkb * kb)

    def count_tied_below(bound):
        def body(j, cnt):
            hit = (keys_ref[j] == thr) & (j * kb + krow < bound)
            return cnt + fold(jnp.where(hit, 1.0, 0.0), jnp.add)
        return jnp.sum(lax.fori_loop(0, n_adm, body, jnp.zeros((8, tq), F32)), axis=0, keepdims=True)

    def tie_bisect(state):
        p, lo, hi = state
        mid = (lo + hi) >> 1
        enough = count_tied_below(mid) >= need
        return p + 1, jnp.where(enough, lo, mid), jnp.where(enough, mid, hi)

    any_tied = jnp.max(jnp.where(tied, 1.0, 0.0))
    n_steps = (nkb * kb).bit_length()
    tie_state = (jnp.int32(0), jnp.zeros((1, tq), jnp.int32), jnp.full((1, tq), end_all, jnp.int32))
    tie_hi = lax.while_loop(lambda st: (any_tied > 0.5) & (st[0] < n_steps), tie_bisect, tie_state)[2]
    tie_end = jnp.where(tied, tie_hi, end_all)

    def write(j, carry):
        k = keys_ref[j]
        keep = (k > thr) | ((k == thr) & (j * kb + krow < tie_end))
        mask_ref[j] = jnp.where(keep, 1.0, 0.0).T.astype(mask_ref.dtype)
        return carry

    lax.fori_loop(0, n_adm, write, 0)

    def clear(j, carry):
        mask_ref[j] = jnp.zeros((tq, kb), mask_ref.dtype)
        return carry

    lax.fori_loop(n_adm, nkb, clear, 0)


def _index_mask(qit, w, ki, n_streams, t, tq, kb, past, topk):
    nkb = ki.shape[1]
    nq = t // tq
    isz = jnp.dtype(MXU_DTYPE).itemsize
    vmem = (nkb * kb * tq * 4 + 2 * nkb * kb * tq + 2 * nkb * kb * LANES * isz + 2 * IDX_DIM * IDX_HEADS * tq * isz
            + 6 * kb * IDX_HEAD_GROUP * tq * 4 + (8 << 20))
    return pl.pallas_call(
        functools.partial(_index_kernel, tq=tq, kb=kb, nkb=nkb, past=past, topk=topk),
        grid=(n_streams, nq),
        in_specs=[
            pl.BlockSpec((None, None, IDX_DIM, IDX_HEADS * tq), lambda b, i: (b, i, 0, 0)),
            pl.BlockSpec((None, IDX_HEADS, tq), lambda b, i: (b, 0, i)),
            pl.BlockSpec((None, nkb, kb, IDX_DIM), lambda b, i: (b, 0, 0, 0)),
        ],
        out_specs=pl.BlockSpec((None, nkb, tq, kb), lambda b, i: (b, 0, i, 0)),
        out_shape=jax.ShapeDtypeStruct((n_streams, nkb, t, kb), jnp.int8),
        scratch_shapes=[pltpu.VMEM((nkb, kb, tq), jnp.int32)],
        compiler_params=_cparams(2, vmem),
        name="indexer_topk_mask",
    )(qit, w, ki)


def _attn_kernel(it_ref, jt_ref, last_ref, q_ref, k_ref, v_ref, m_ref, o_ref, qs_ref, mx_ref, l_ref, acc_ref,
                 *, group):
    step = pl.program_id(1)
    tq = q_ref.shape[0]
    tk = k_ref.shape[0]
    n_kv = k_ref.shape[1] // SA_HEAD
    rows = group * tq
    n_lane_tiles = tk // LANES

    @pl.when(jt_ref[step] == 0)
    def _():
        for n in range(n_kv):
            for g in range(group):
                h = n * group + g
                qs_ref[n, g * tq:(g + 1) * tq, :] = q_ref[:, h * SA_HEAD:(h + 1) * SA_HEAD]
        mx_ref[...] = jnp.full(mx_ref.shape, NEG_BIG, F32)
        l_ref[...] = jnp.zeros(l_ref.shape, F32)
        acc_ref[...] = jnp.zeros(acc_ref.shape, F32)

    bias = jnp.where(m_ref[...].astype(F32) > 0.0, 0.0, NEG_BIG)
    kv = range(n_kv)
    s = [_nt(qs_ref[n], k_ref[:, n * SA_HEAD:(n + 1) * SA_HEAD]) for n in kv]
    s = [(x.reshape(group, tq, tk) + bias[None]).reshape(rows, tk) for x in s]
    tiles = [[x[:, c * LANES:(c + 1) * LANES] for c in range(n_lane_tiles)] for x in s]
    m_old = [mx_ref[n] for n in kv]
    m_new = [jnp.maximum(mo, jnp.max(functools.reduce(jnp.maximum, t), axis=1, keepdims=True))
             for mo, t in zip(m_old, tiles)]
    alpha = [jnp.exp2(mo - mn) for mo, mn in zip(m_old, m_new)]
    p = [[jnp.exp2(x - mn) for x in t] for t, mn in zip(tiles, m_new)]
    for n in kv:
        mx_ref[n] = m_new[n]
        l_ref[n] = alpha[n] * l_ref[n] + functools.reduce(jnp.add, p[n])
    pv = [jnp.dot(_mx(jnp.concatenate(p[n], axis=1)), v_ref[:, n * SA_HEAD:(n + 1) * SA_HEAD],
                  preferred_element_type=F32) for n in kv]
    for n in kv:
        acc_ref[n] = alpha[n] * acc_ref[n] + pv[n]

    @pl.when(last_ref[step] == 1)
    def _():
        for n in range(n_kv):
            o = acc_ref[n] / jnp.sum(l_ref[n], axis=1, keepdims=True)
            for g in range(group):
                h = n * group + g
                o_ref[:, h * SA_HEAD:(h + 1) * SA_HEAD] = o[g * tq:(g + 1) * tq].astype(o_ref.dtype)


def _attn_tiles(t, tq, tk, nkb, past):
    ii, jj, last = [], [], []
    for i in range(t // tq):
        n_adm = min(nkb, -(-(past + (i + 1) * tq) // tk))
        for j in range(n_adm):
            ii.append(i)
            jj.append(j)
            last.append(int(j == n_adm - 1))
    return tuple(jnp.asarray(np.array(x, np.int32)) for x in (ii, jj, last))


def _attention(q, k_all, v_all, mask, n_streams, t, tq, tk, past):
    m, sa_w = q.shape
    kv_w = k_all.shape[2]
    nkb = mask.shape[1]
    nq = t // tq
    n_heads = sa_w // SA_HEAD
    group = n_heads // SA_KV_HEADS
    it, jt, last = _attn_tiles(t, tq, tk, nkb, past)
    grid_spec = pltpu.PrefetchScalarGridSpec(
        num_scalar_prefetch=3,
        grid=(n_streams, int(it.shape[0])),
        in_specs=[
            pl.BlockSpec((tq, sa_w), lambda b, s, it, jt, lt: (b * nq + it[s], 0)),
            pl.BlockSpec((None, tk, kv_w), lambda b, s, it, jt, lt: (b, jt[s], 0)),
            pl.BlockSpec((None, tk, kv_w), lambda b, s, it, jt, lt: (b, jt[s], 0)),
            pl.BlockSpec((None, None, tq, tk), lambda b, s, it, jt, lt: (b, jt[s], it[s], 0)),
        ],
        out_specs=pl.BlockSpec((tq, sa_w), lambda b, s, it, jt, lt: (b * nq + it[s], 0)),
        scratch_shapes=[pltpu.VMEM((SA_KV_HEADS, group * tq, SA_HEAD), MXU_DTYPE)]
        + [pltpu.VMEM((SA_KV_HEADS, group * tq, SA_HEAD), F32)] * 3,
    )
    return pl.pallas_call(
        functools.partial(_attn_kernel, group=group),
        grid_spec=grid_spec,
        out_shape=jax.ShapeDtypeStruct((m, sa_w), MXU_DTYPE),
        compiler_params=_cparams(2, ATTN_VMEM_BYTES),
        name="masked_flash_attention",
    )(it, jt, last, q, k_all, v_all, mask)


def _prepare_weights(w_in, mu_shift, w_decay_up, decay_bias, w_a_up, a_bias, w_gate_up, k_k, k_a, r_k,
                     lnx_w, lnx_b, w_out, w_up, conv_w, conv_b, w_down, d_model):
    rw = w_decay_up.shape[1]
    rw_cols = 3 * rw + D_DECAY + D_AAA + D_GATE
    sa_w = d_model - rw
    kv_w = SA_KV_HEADS * SA_HEAD
    qi_w = IDX_HEADS * IDX_DIM
    o = {}
    o["r"] = 0
    o["wd"] = rw
    o["k"] = rw + D_DECAY
    o["v"] = 2 * rw + D_DECAY
    o["ad"] = 3 * rw + D_DECAY
    o["gd"] = 3 * rw + D_DECAY + D_AAA
    o["q"] = rw_cols
    o["ksa"] = rw_cols + sa_w
    o["vsa"] = o["ksa"] + kv_w
    o["qi"] = o["vsa"] + kv_w
    o["kiw"] = o["qi"] + qi_w
    kiw_w = IDX_DIM + IDX_HEADS

    def seg(x, name, width, pad_to=None):
        s = x[..., o[name]:o[name] + width]
        if pad_to is not None and pad_to > width:
            s = jnp.pad(s, [(0, 0)] * (s.ndim - 1) + [(0, pad_to - width)])
        return s

    def rw_small(x):
        return jnp.concatenate([seg(x, "wd", D_DECAY, LANES), seg(x, "ad", D_AAA, LANES), seg(x, "gd", D_GATE)], -1)

    w_in_l = jnp.concatenate([
        seg(w_in, "r", rw), seg(w_in, "k", rw), seg(w_in, "v", rw), seg(w_in, "q", sa_w), seg(w_in, "qi", qi_w),
        seg(w_in, "ksa", kv_w), seg(w_in, "vsa", kv_w), rw_small(w_in), seg(w_in, "kiw", kiw_w, LANES)], axis=1)
    w_in_l = jnp.pad(w_in_l, ((0, 0), (0, -w_in_l.shape[1] % IN_PROJ_TN)))
    starts = {"r": 0, "k": rw, "v": 2 * rw, "q": 3 * rw, "qi": 3 * rw + sa_w}
    starts["ksa"] = starts["qi"] + qi_w
    starts["vsa"] = starts["ksa"] + kv_w
    starts["small"] = starts["vsa"] + kv_w
    starts["kiw"] = starts["small"] + 4 * LANES
    pad_rows = lambda x, n: jnp.pad(x, ((0, n - x.shape[0]), (0, 0)))
    row = lambda x: x.reshape(1, -1).astype(F32)
    mu = mu_shift.reshape(1, -1)
    w = {
        "w_in": _mx(w_in_l),
        "mu_r": seg(mu, "r", rw), "mu_k": seg(mu, "k", rw), "mu_v": seg(mu, "v", rw), "mu_small": rw_small(mu),
        "w_decay_up": _mx(pad_rows(w_decay_up, LANES)), "decay_bias": row(decay_bias),
        "w_a_up": _mx(pad_rows(w_a_up, LANES)), "a_bias": row(a_bias),
        "w_gate_up": _mx(w_gate_up),
        "k_k": row(k_k), "k_a": row(k_a), "r_k": row(r_k), "lnx_w": row(lnx_w), "lnx_b": row(lnx_b),
        "w_out": _mx(w_out), "w_up": _mx(w_up), "conv_w": conv_w, "conv_b": row(conv_b), "w_down": _mx(w_down),
    }
    dims = {"rw": rw, "rw_cols": rw_cols, "sa_w": sa_w, "kv_w": kv_w, "qi_w": qi_w, "orig": o, "starts": starts,
            "seg": seg, "rw_small": rw_small}
    return w, dims


def _layer(x, past_k, past_v, past_ki, wkv0, shift0, conv0, norms, w, dims, tiles):
    bsz, t, d = x.shape
    past = past_k.shape[1]
    m = bsz * t
    rw, sa_w, kv_w, qi_w = dims["rw"], dims["sa_w"], dims["kv_w"], dims["qi_w"]
    starts, seg, rw_small = dims["starts"], dims["seg"], dims["rw_small"]
    assert t % CHUNK == 0 and past % CHUNK == 0
    tm, tq, kb = tiles["tm"], tiles["tq"], tiles["kb"]
    g_mix_pre, g_mix_post, g_ffn_pre, g_ffn_post = norms

    x2 = x.reshape(m, d)
    h1 = _rmsnorm(x2, g_mix_pre, tiles["tm_norm"])
    pfull = _matmul(h1, w["w_in"], tiles["tm_in"], tiles["tn_in"])

    sh = shift0.reshape(bsz, 1, -1)
    shift_l = {"r": seg(sh, "r", rw), "k": seg(sh, "k", rw), "v": seg(sh, "v", rw), "small": rw_small(sh)}
    cols_rw = {n: starts[n] // RW_GROUP for n in ("r", "k", "v")}
    cols_rw["small"] = starts["small"] // RW_SMALL
    o_rw, wkv1 = _rwkv(pfull, cols_rw, shift_l, wkv0, w, bsz, t)

    pos = past + jnp.arange(t, dtype=jnp.int32)
    cols_sa = {"q": starts["q"] // sa_w, "ksa": starts["ksa"] // kv_w, "qi": starts["qi"] // qi_w,
               "kiw": starts["kiw"] // LANES}
    q_r, k_r, qi_r, kiw_r = _rope(pfull, cols_sa, _rope_tables(pos), bsz, t, tq, sa_w, kv_w, qi_w)
    k_new = k_r.reshape(bsz, t, kv_w)
    v_new = pfull[:, starts["vsa"]:starts["vsa"] + kv_w].reshape(bsz, t, kv_w)
    ki_new = kiw_r[:, :IDX_DIM].reshape(bsz, t, IDX_DIM)
    l_all = past + t
    lp = -(-l_all // kb) * kb
    nkb = lp // kb
    cat = lambda old, new: jnp.pad(
        jnp.concatenate([_mx(old.reshape(bsz, past, new.shape[-1])), _mx(new)], axis=1),
        ((0, 0), (0, lp - l_all), (0, 0)))
    k_all, v_all, ki_all = cat(past_k, k_new), cat(past_v, v_new), cat(past_ki, ki_new)
    nq = t // tq
    qit = (qi_r.reshape(bsz, nq, tq, IDX_HEADS, IDX_DIM).transpose(0, 1, 4, 3, 2)
           .reshape(bsz, nq, IDX_DIM, IDX_HEADS * tq))
    w_idx = kiw_r[:, IDX_DIM:IDX_DIM + IDX_HEADS].reshape(bsz, t, IDX_HEADS).transpose(0, 2, 1)
    topk = min(TOPK_MAX, l_all // 4)
    mask = _index_mask(qit, w_idx, ki_all.reshape(bsz, nkb, kb, IDX_DIM), bsz, t, tq, kb, past, topk)
    o_sa = _attention(q_r, k_all, v_all, mask, bsz, t, tiles["tq_attn"], kb, past)

    x1 = _matmul_norm_residual([o_rw, o_sa], w["w_out"], x2, g_mix_post, tm, tiles["tn_out"], "out_proj")
    h2 = _rmsnorm(x1, g_ffn_pre, tiles["tm_norm"])
    act, conv1 = _ffn_up(h2, w["w_up"], conv0, w["conv_w"], w["conv_b"], bsz, tiles["tm_up"], tiles["tn_up"])
    x_out = _matmul_norm_residual([act], w["w_down"], x1, g_ffn_post, tm, tiles["tn_out"], "ffn_down")

    last = pfull.reshape(bsz, t, -1)[:, t - 1:t]
    sm = last[..., starts["small"]:starts["small"] + 4 * LANES]
    shift1 = jnp.concatenate([
        last[..., starts["r"]:starts["r"] + rw], sm[..., :D_DECAY],
        last[..., starts["k"]:starts["k"] + rw], last[..., starts["v"]:starts["v"] + rw],
        sm[..., LANES:LANES + D_AAA], sm[..., 2 * LANES:2 * LANES + D_GATE]], axis=-1)
    new = (k_new.reshape(bsz, t, SA_KV_HEADS, SA_HEAD), v_new.reshape(bsz, t, SA_KV_HEADS, SA_HEAD), ki_new,
           wkv1, shift1, conv1)
    return x_out.reshape(bsz, t, d), new


def _tiles(bsz, t):
    big = t >= 1024
    return {
        "tm_norm": 256 if big else CHUNK,
        "tm_in": 1024 if big else bsz * t,
        "tn_in": IN_PROJ_TN,
        "tm": 512 if big else bsz * t,
        "tn_out": 512,
        "tm_up": 1024 if big else bsz * t,
        "tn_up": 512,
        "tq": 256 if big else CHUNK,
        "tq_attn": 128 if big else CHUNK,
        "kb": 512,
    }


def _run_stream(x, past_k, past_v, past_ki, wkv0, shift0, conv0, norm_w, weights, dims, depth):
    outs = []
    tiles = _tiles(x.shape[0], x.shape[1])
    for l in range(depth):
        w_l = {n: v[l] for n, v in weights.items()}
        norms = tuple(g[l].reshape(1, -1) for g in norm_w)
        x, st = _layer(x, past_k[l], past_v[l], past_ki[l], wkv0[l], shift0[l], conv0[l], norms, w_l, dims[l], tiles)
        outs.append(st)
    return x, [jnp.stack(s) for s in zip(*outs)]


def kernel(x_prompt, x_sample, cache_k, cache_v, cache_kidx, state_wkv, state_shift, state_conv, norm_mix_pre, norm_mix_post, norm_ffn_pre, norm_ffn_post, w_in, mu_shift, w_decay_up, decay_bias, w_a_up, a_bias, w_gate_up, k_k, k_a, r_k, lnx_w, lnx_b, w_out, w_up, conv_w, conv_b, w_down):
    depth, d_model = norm_mix_pre.shape
    per_layer = [_prepare_weights(w_in[l], mu_shift[l], w_decay_up[l], decay_bias[l], w_a_up[l], a_bias[l],
                                  w_gate_up[l], k_k[l], k_a[l], r_k[l], lnx_w[l], lnx_b[l], w_out[l], w_up[l],
                                  conv_w[l], conv_b[l], w_down[l], d_model) for l in range(depth)]
    weights = {n: [pw[0][n] for pw in per_layer] for n in per_layer[0][0]}
    dims = [pw[1] for pw in per_layer]
    norm_w = (norm_mix_pre, norm_mix_post, norm_ffn_pre, norm_ffn_post)

    dt = x_prompt.dtype
    bp = x_prompt.shape[0]
    rw_heads = state_wkv.shape[2]
    zk = jnp.zeros((depth, bp, 0, SA_KV_HEADS, SA_HEAD), dt)
    zki = jnp.zeros((depth, bp, 0, IDX_DIM), dt)
    zwkv = jnp.zeros((depth, bp, rw_heads, RW_HEAD, RW_HEAD), dt)
    zshift = jnp.zeros((depth, bp, 1, state_shift.shape[-1]), dt)
    zconv = jnp.zeros((depth, bp, CONV_W - 1, state_conv.shape[-1]), dt)
    y_prompt, p_new = _run_stream(x_prompt, zk, zk, zki, zwkv, zshift, zconv, norm_w, weights, dims, depth)
    y_sample, s_new = _run_stream(x_sample, cache_k, cache_v, cache_kidx, state_wkv, state_shift, state_conv,
                                  norm_w, weights, dims, depth)
    return (y_prompt, y_sample, *p_new, *s_new)
```

```python
import functools

import numpy as np
import jax
import jax.numpy as jnp
from jax import lax
from jax.experimental import pallas as pl
from jax.experimental.pallas import tpu as pltpu

F32 = jnp.float32
MXU_DTYPE = jnp.bfloat16

CHUNK = 64
RW_HEAD = 64
D_DECAY = 96
D_AAA = 96
D_GATE = 256
SA_HEAD = 128
SA_KV_HEADS = 4
IDX_HEADS = 16
IDX_DIM = 64
TOPK_MAX = 256
CONV_W = 3
ROPE_THETA = 10000.0
NORM_EPS = 1e-6
LNX_EPS = 64e-5

LANES = 128
V7X_VMEM_LIMIT_BYTES = 60000 * 1024
SMALL_CALL_VMEM_BYTES = 32 << 20
ATTN_VMEM_BYTES = 40 << 20
INT_MIN = -(2 ** 31)
NEG_BIG = -1e30


def _cparams(n_grid, vmem_bytes):
    limit = int(min(V7X_VMEM_LIMIT_BYTES, max(SMALL_CALL_VMEM_BYTES, vmem_bytes)))
    return pltpu.CompilerParams(dimension_semantics=("arbitrary",) * n_grid, vmem_limit_bytes=limit)


def _nt(a, b):
    return lax.dot_general(a, b, (((1,), (1,)), ((), ())), preferred_element_type=F32)


def _mx(x):
    return x.astype(MXU_DTYPE)


def _rmsnorm_kernel(x_ref, g_ref, o_ref):
    x = x_ref[...]
    y = x * lax.rsqrt(jnp.mean(x * x, axis=-1, keepdims=True) + NORM_EPS)
    o_ref[...] = (y * g_ref[...]).astype(o_ref.dtype)


def _rmsnorm(x, g, tm):
    m, d = x.shape
    return pl.pallas_call(
        _rmsnorm_kernel,
        grid=(m // tm,),
        in_specs=[pl.BlockSpec((tm, d), lambda i: (i, 0)), pl.BlockSpec((1, d), lambda i: (0, 0))],
        out_specs=pl.BlockSpec((tm, d), lambda i: (i, 0)),
        out_shape=jax.ShapeDtypeStruct((m, d), MXU_DTYPE),
        compiler_params=_cparams(1, 6 * tm * d * 4),
        name="rmsnorm",
    )(x, g)


def _mm_kernel(a_ref, b_ref, o_ref):
    o_ref[...] = jnp.dot(a_ref[...], b_ref[...], preferred_element_type=F32)


IN_PROJ_TN = 1024


def _matmul(a, b, tm, tn):
    m, k = a.shape
    n = b.shape[1]
    isz = jnp.dtype(MXU_DTYPE).itemsize
    vmem = 2 * (tm * k * isz + k * tn * isz + tm * tn * 4) + 2 * tm * tn * 4 + (8 << 20)
    return pl.pallas_call(
        _mm_kernel,
        grid=(m // tm, n // tn),
        in_specs=[pl.BlockSpec((tm, k), lambda i, j: (i, 0)), pl.BlockSpec((k, tn), lambda i, j: (0, j))],
        out_specs=pl.BlockSpec((tm, tn), lambda i, j: (i, j)),
        out_shape=jax.ShapeDtypeStruct((m, n), F32),
        compiler_params=_cparams(2, vmem),
        name="in_proj",
    )(a, b)


MM_NORM_ROWS = 128


def _mm_norm_kernel(*refs, nn, n_ops):
    a_refs, w_refs = refs[:n_ops], refs[n_ops:2 * n_ops]
    x_ref, gpost_ref, x1_ref = refs[2 * n_ops:]
    n = pl.program_id(1)
    tm, d = x1_ref.shape
    tn = w_refs[0].shape[1]
    parts = [jnp.dot(a_ref[...], w_ref[...], preferred_element_type=F32) for a_ref, w_ref in zip(a_refs, w_refs)]
    x1_ref[:, pl.ds(pl.multiple_of(n * tn, LANES), tn)] = functools.reduce(jnp.add, parts)

    @pl.when(n == nn - 1)
    def _():
        step = min(tm, MM_NORM_ROWS)
        for r in range(0, tm, step):
            rows = slice(r, r + step)
            y = x1_ref[rows, :]
            y = y * lax.rsqrt(jnp.mean(y * y, axis=-1, keepdims=True) + NORM_EPS) * gpost_ref[...]
            x1_ref[rows, :] = x_ref[rows, :] + y


def _matmul_norm_residual(a_slabs, w, x, g_post, tm, tn, name):
    n_ops = len(a_slabs)
    m, ks = a_slabs[0].shape
    kdim = n_ops * ks
    d = w.shape[1]
    nn = d // tn
    isz = jnp.dtype(MXU_DTYPE).itemsize
    fixed = 2 * kdim * tn * isz + tm * d * 4 + 2 * tm * tn * 4 + 6 * MM_NORM_ROWS * d * 4 + (4 << 20)
    per_copy = tm * kdim * isz + tm * d * 4
    row_bufs = 2 if fixed + 2 * per_copy <= V7X_VMEM_LIMIT_BYTES - (4 << 20) else 1
    vmem = fixed + row_bufs * per_copy
    w_rows = lambda p: (lambda i, n: (p, n))
    return pl.pallas_call(
        functools.partial(_mm_norm_kernel, nn=nn, n_ops=n_ops),
        grid=(m // tm, nn),
        in_specs=[pl.BlockSpec((tm, ks), lambda i, n: (i, 0), pipeline_mode=pl.Buffered(row_bufs))] * n_ops
        + [pl.BlockSpec((ks, tn), w_rows(p)) for p in range(n_ops)]
        + [
            pl.BlockSpec((tm, d), lambda i, n: (i, 0), pipeline_mode=pl.Buffered(1)),
            pl.BlockSpec((1, d), lambda i, n: (0, 0)),
        ],
        out_specs=pl.BlockSpec((tm, d), lambda i, n: (i, 0), pipeline_mode=pl.Buffered(row_bufs)),
        out_shape=jax.ShapeDtypeStruct((m, d), F32),
        compiler_params=_cparams(2, vmem),
        name=name,
    )(*a_slabs, *([w] * n_ops), x, g_post)


def _gelu_tanh(x):
    return 0.5 * x * (1.0 + jnp.tanh(0.7978845608028654 * (x + 0.044715 * x * x * x)))


def _ffn_up_kernel(h_ref, wg_ref, wv_ref, c0g_ref, c0v_ref, cwg_ref, cwv_ref, cbg_ref, cbv_ref,
                   a_ref, cg_ref, cv_ref, carry_g, carry_v, *, blocks_per_stream, tm):
    i = pl.program_id(1) % blocks_per_stream
    n_str = carry_g.shape[0]
    ts = tm // n_str

    @pl.when(i == 0)
    def _():
        carry_g[...] = c0g_ref[...]
        carry_v[...] = c0v_ref[...]

    h = h_ref[...]
    row = lax.broadcasted_iota(jnp.int32, (tm, 1), 0)

    def conv(u, carry_ref, cw_ref, cb_ref, out_ref):
        u1 = pltpu.roll(u, 1, axis=0)
        u2 = pltpu.roll(u, 2, axis=0)
        for s in range(n_str):
            p = carry_ref[s]
            u1 = jnp.where(row == s * ts, p[1:2], u1)
            u2 = jnp.where(row == s * ts, p[0:1], jnp.where(row == s * ts + 1, p[1:2], u2))
            last = u[(s + 1) * ts - 2:(s + 1) * ts]
            carry_ref[s] = last
            out_ref[s] = last
        cw = cw_ref[...]
        return u2 * cw[0:1] + u1 * cw[1:2] + u * cw[2:3] + cb_ref[...]

    gate = conv(jnp.dot(h, wg_ref[...], preferred_element_type=F32), carry_g, cwg_ref, cbg_ref, cg_ref)
    val = conv(jnp.dot(h, wv_ref[...], preferred_element_type=F32), carry_v, cwv_ref, cbv_ref, cv_ref)
    a_ref[...] = (_gelu_tanh(gate) * val).astype(a_ref.dtype)


def _ffn_up(h, w_up, conv0, conv_w, conv_b, n_streams, tm, tn):
    m, d = h.shape
    f2 = w_up.shape[1]
    f = f2 // 2
    nj = f // tn
    nr = m // tm
    t = m // n_streams
    spb = max(1, tm // t)
    bps = max(1, t // tm)
    isz = jnp.dtype(MXU_DTYPE).itemsize
    vmem = 2 * (tm * d * isz + 2 * d * tn * isz + tm * tn * isz) + 10 * tm * tn * 4 + (4 << 20)
    col = lambda off: (lambda j, r: (0, off + j))
    st = lambda off: (lambda j, r: (r // bps, 0, off + j))
    a, cg, cv = pl.pallas_call(
        functools.partial(_ffn_up_kernel, blocks_per_stream=bps, tm=tm),
        grid=(nj, nr),
        in_specs=[
            pl.BlockSpec((tm, d), lambda j, r: (r, 0)),
            pl.BlockSpec((d, tn), col(0)),
            pl.BlockSpec((d, tn), col(nj)),
            pl.BlockSpec((spb, CONV_W - 1, tn), st(0)),
            pl.BlockSpec((spb, CONV_W - 1, tn), st(nj)),
            pl.BlockSpec((CONV_W, tn), col(0)),
            pl.BlockSpec((CONV_W, tn), col(nj)),
            pl.BlockSpec((1, tn), col(0)),
            pl.BlockSpec((1, tn), col(nj)),
        ],
        out_specs=[
            pl.BlockSpec((tm, tn), lambda j, r: (r, j)),
            pl.BlockSpec((spb, CONV_W - 1, tn), st(0)),
            pl.BlockSpec((spb, CONV_W - 1, tn), st(0)),
        ],
        out_shape=[
            jax.ShapeDtypeStruct((m, f), MXU_DTYPE),
            jax.ShapeDtypeStruct((n_streams, CONV_W - 1, f), F32),
            jax.ShapeDtypeStruct((n_streams, CONV_W - 1, f), F32),
        ],
        scratch_shapes=[pltpu.VMEM((spb, CONV_W - 1, tn), F32)] * 2,
        compiler_params=_cparams(2, vmem),
        name="ffn_up_conv",
    )(h, w_up, w_up, conv0, conv0, conv_w, conv_w, conv_b, conv_b)
    return a, jnp.concatenate([cg, cv], axis=-1)


RW_GROUP = 16 * LANES
RW_SMALL = 4 * LANES


def _softplus(z):
    return jnp.maximum(z, 0.0) + jnp.log(1.0 + jnp.exp(-jnp.abs(z)))


def _sigmoid(z):
    return 1.0 / (1.0 + jnp.exp(-z))


def _dg(a, b, dims):
    return lax.dot_general(_mx(a), _mx(b), (dims, ((), ())), preferred_element_type=F32)


_NN = ((1,), (0,))
_NT = ((1,), (1,))
_TN = ((0,), (0,))


def _segsum(x, seg):
    return jnp.dot(_mx(x), seg, preferred_element_type=F32)


def _each(f, *lists):
    return [f(*args) for args in zip(*lists)]


def _wkv_chunk(r, k, v, kk, b, cum, logd, s_bd, consts):
    m0, m1, tri_mask = consts
    c = r[0].shape[0]
    c2 = 2 * c
    stack = lambda x: jnp.concatenate([x * m0, x * m1], axis=0)
    g_end = _each(lambda cm: jnp.exp(cm[c - 1:c]), cum)
    e_inc = _each(jnp.exp, cum)
    e_prev = _each(lambda cm, ld: jnp.exp(cm - ld), cum, logd)
    e_neg = _each(lambda cm: jnp.exp(-cm), cum)
    e_tail = _each(lambda g, e: g * e, g_end, e_neg)
    vs = _each(stack, v)
    lhs = _each(lambda kk_, r_, ep, ei: jnp.concatenate([stack(kk_ * ep), stack(r_ * ei)], axis=0),
                kk, r, e_prev, e_inc)
    rhs = _each(lambda k_, b_, en: jnp.concatenate([stack(k_ * en), stack(b_ * en)], axis=0), k, b, e_neg)
    tails = _each(lambda k_, b_, et: jnp.concatenate([stack(k_ * et), stack(b_ * et)], axis=0), k, b, e_tail)
    amat = _each(lambda x, y: jnp.where(tri_mask, _dg(x, y, _NT), 0.0), lhs, rhs)
    sp = _each(lambda x, s: _dg(x, s, _NT), lhs, s_bd)
    av = _each(lambda a_, v_: _dg(a_[:, :c2], v_, _NN), amat, vs)
    pw = _each(lambda a_: -a_[:c2, c2:], amat)
    x = _each(lambda s_, a_: s_[:c2] + a_[:c2], sp, av)
    n_lvl = int(np.log2(c))
    for lvl in range(n_lvl - 1):
        z = _each(lambda p_, x_: _dg(p_, jnp.concatenate([p_, x_], axis=1), _NN), pw, x)
        pw = _each(lambda z_: z_[:, :c2], z)
        x = _each(lambda x_, z_: x_ + z_[:, c2:], x, z)
    u = _each(lambda p_, x_: x_ + _dg(p_, x_, _NN), pw, x)
    ys = _each(lambda s_, a_, am, u_: s_[c2:] + a_[c2:] - _dg(am[c2:, c2:], u_, _NN), sp, av, amat, u)
    y = _each(lambda y_: y_[:c] + y_[c:], ys)
    s_new = _each(lambda s, g, v_, u_, t_: s * g + _dg(jnp.concatenate([v_, -u_], axis=0), t_, _TN),
                  s_bd, g_end, vs, u, tails)
    return y, s_new


def _rwkv_kernel(pr_ref, pk_ref, pv_ref, ps_ref, sr_ref, sk_ref, sv_ref, ss_ref, wkv0_ref,
                 mur_ref, muk_ref, muv_ref, mus_ref, wdec_ref, dbias_ref, wa_ref, abias_ref, wg_ref,
                 kk_ref, ka_ref, rk_ref, lnw_ref, lnb_ref,
                 o_ref, wkv1_ref,
                 cr_ref, ck_ref, cv_ref, cs_ref, state_ref, *, n_chunks):
    c = pl.program_id(2)
    C = CHUNK
    npairs = RW_GROUP // LANES

    lane = lax.broadcasted_iota(jnp.int32, (1, LANES), 1)
    m0 = (lane < RW_HEAD).astype(F32)
    m1 = 1.0 - m0
    ri = lax.broadcasted_iota(jnp.int32, (4 * C, 4 * C), 0)
    ci = lax.broadcasted_iota(jnp.int32, (4 * C, 4 * C), 1)
    same_head = ((ri // C) % 2) == ((ci // C) % 2)
    tri_mask = same_head & ((ri % C) + ri // (2 * C) > (ci % C))
    li = lax.broadcasted_iota(jnp.int32, (LANES, LANES), 0)
    lj = lax.broadcasted_iota(jnp.int32, (LANES, LANES), 1)
    seg = ((li // RW_HEAD) == (lj // RW_HEAD)).astype(MXU_DTYPE)
    consts = (m0, m1, tri_mask)

    @pl.when(c == 0)
    def _():
        cr_ref[...] = sr_ref[...]
        ck_ref[...] = sk_ref[...]
        cv_ref[...] = sv_ref[...]
        cs_ref[...] = ss_ref[...]
        z = jnp.zeros((RW_HEAD, RW_HEAD), F32)
        for p in range(npairs):
            s0 = wkv0_ref[2 * p]
            s1 = wkv0_ref[2 * p + 1]
            state_ref[p] = jnp.concatenate(
                [jnp.concatenate([s0, z], axis=1), jnp.concatenate([z, s1], axis=1)], axis=0)

    row = lax.broadcasted_iota(jnp.int32, (C, 1), 0)

    def shifted(p_ref, carry_ref, mu_ref):
        p = p_ref[...]
        prev = jnp.where(row == 0, carry_ref[...], pltpu.roll(p, 1, axis=0))
        carry_ref[...] = p[C - 1:C]
        return p + (prev - p) * mu_ref[...]

    r = shifted(pr_ref, cr_ref, mur_ref)
    k = shifted(pk_ref, ck_ref, muk_ref)
    v = shifted(pv_ref, cv_ref, muv_ref)
    sm = shifted(ps_ref, cs_ref, mus_ref)
    wd, ad, gd = sm[:, 0:LANES], sm[:, LANES:2 * LANES], sm[:, 2 * LANES:4 * LANES]

    dec_in = dbias_ref[...] + jnp.dot(_mx(jnp.tanh(wd)), wdec_ref[...], preferred_element_type=F32)
    w_log = -_softplus(-dec_in) - 0.5
    logd = -jnp.exp(w_log)
    a = _sigmoid(abias_ref[...] + jnp.dot(_mx(ad), wa_ref[...], preferred_element_type=F32))
    g = jnp.dot(_mx(_sigmoid(gd)), wg_ref[...], preferred_element_type=F32)
    kk = k * kk_ref[...]
    k2 = k * (1.0 + (a - 1.0) * ka_ref[...])
    rkr = r * k2 * rk_ref[...]
    cum = logd
    for sft in (1, 2, 4, 8, 16, 32):
        cum = cum + jnp.where(row >= sft, pltpu.roll(cum, sft, axis=0), 0.0)

    pairs = lambda x: [x[:, p * LANES:(p + 1) * LANES] for p in range(npairs)]
    r_p, k_p, v_p, a_p, cum_p, logd_p = pairs(r), pairs(k2), pairs(v), pairs(a), pairs(cum), pairs(logd)
    sums = _each(lambda kk_, rkr_: _segsum(jnp.concatenate([kk_ * kk_, rkr_], axis=0), seg), pairs(kk), pairs(rkr))
    kk_p = _each(lambda kk_, s_: kk_ * lax.rsqrt(jnp.maximum(s_[:C], 1e-24)), pairs(kk), sums)
    b_p = _each(lambda kk_, a_: kk_ * a_, kk_p, a_p)
    y_p, s_new = _wkv_chunk(r_p, k_p, v_p, kk_p, b_p, cum_p, logd_p, [state_ref[p] for p in range(npairs)], consts)
    for p in range(npairs):
        state_ref[p] = s_new[p]
    yc = _each(lambda y_: y_ - _segsum(y_, seg) * (1.0 / RW_HEAD), y_p)
    var = _each(lambda yc_: _segsum(yc_ * yc_, seg) * (1.0 / RW_HEAD), yc)
    for p in range(npairs):
        sl = slice(p * LANES, (p + 1) * LANES)
        yn = yc[p] * lax.rsqrt(var[p] + LNX_EPS) * lnw_ref[:, sl] + lnb_ref[:, sl]
        o_ref[:, sl] = ((yn + sums[p][C:] * v_p[p]) * g[:, sl]).astype(o_ref.dtype)

    @pl.when(c == n_chunks - 1)
    def _():
        for p in range(npairs):
            s = state_ref[p]
            wkv1_ref[2 * p] = s[:RW_HEAD, :RW_HEAD]
            wkv1_ref[2 * p + 1] = s[RW_HEAD:, RW_HEAD:]


def _rwkv(pfull, cols, shift0, wkv0, w, n_streams, t):
    m = pfull.shape[0]
    rw = w["k_k"].shape[1]
    ng = rw // RW_GROUP
    nc = t // CHUNK
    hg = RW_GROUP // RW_HEAD
    rowblk = lambda off: (lambda b, g, c: (b * nc + c, off + g))
    fixed = lambda off: (lambda b, g, c: (b * nc + c, off))
    st = lambda b, g, c: (b, 0, g)
    st0 = lambda b, g, c: (b, 0, 0)
    wcol = lambda b, g, c: (0, g)
    in_specs = [
        pl.BlockSpec((CHUNK, RW_GROUP), rowblk(cols["r"])),
        pl.BlockSpec((CHUNK, RW_GROUP), rowblk(cols["k"])),
        pl.BlockSpec((CHUNK, RW_GROUP), rowblk(cols["v"])),
        pl.BlockSpec((CHUNK, RW_SMALL), fixed(cols["small"])),
        pl.BlockSpec((None, 1, RW_GROUP), st),
        pl.BlockSpec((None, 1, RW_GROUP), st),
        pl.BlockSpec((None, 1, RW_GROUP), st),
        pl.BlockSpec((None, 1, RW_SMALL), st0),
        pl.BlockSpec((None, hg, RW_HEAD, RW_HEAD), lambda b, g, c: (b, g, 0, 0)),
        pl.BlockSpec((1, RW_GROUP), wcol), pl.BlockSpec((1, RW_GROUP), wcol), pl.BlockSpec((1, RW_GROUP), wcol),
        pl.BlockSpec((1, RW_SMALL), lambda b, g, c: (0, 0)),
        pl.BlockSpec((LANES, RW_GROUP), wcol), pl.BlockSpec((1, RW_GROUP), wcol),
        pl.BlockSpec((LANES, RW_GROUP), wcol), pl.BlockSpec((1, RW_GROUP), wcol),
        pl.BlockSpec((2 * LANES, RW_GROUP), wcol),
        pl.BlockSpec((1, RW_GROUP), wcol), pl.BlockSpec((1, RW_GROUP), wcol), pl.BlockSpec((1, RW_GROUP), wcol),
        pl.BlockSpec((1, RW_GROUP), wcol), pl.BlockSpec((1, RW_GROUP), wcol),
    ]
    o, wkv1 = pl.pallas_call(
        functools.partial(_rwkv_kernel, n_chunks=nc),
        grid=(n_streams, ng, nc),
        in_specs=in_specs,
        out_specs=[
            pl.BlockSpec((CHUNK, RW_GROUP), lambda b, g, c: (b * nc + c, g)),
            pl.BlockSpec((None, hg, RW_HEAD, RW_HEAD), lambda b, g, c: (b, g, 0, 0)),
        ],
        out_shape=[
            jax.ShapeDtypeStruct((m, rw), MXU_DTYPE),
            jax.ShapeDtypeStruct((n_streams, rw // RW_HEAD, RW_HEAD, RW_HEAD), F32),
        ],
        scratch_shapes=[pltpu.VMEM((1, RW_GROUP), F32)] * 3 + [pltpu.VMEM((1, RW_SMALL), F32)]
        + [pltpu.VMEM((RW_GROUP // LANES, LANES, LANES), F32)],
        compiler_params=_cparams(3, SMALL_CALL_VMEM_BYTES),
        name="rwkv7_chunked",
    )(pfull, pfull, pfull, pfull, shift0["r"], shift0["k"], shift0["v"], shift0["small"], wkv0,
      w["mu_r"], w["mu_k"], w["mu_v"], w["mu_small"], w["w_decay_up"], w["decay_bias"], w["w_a_up"], w["a_bias"],
      w["w_gate_up"], w["k_k"], w["k_a"], w["r_k"], w["lnx_w"], w["lnx_b"])
    return o, wkv1


def _rope_kernel(q_ref, k_ref, qi_ref, kiw_ref, ca_ref, sa_ref, cb_ref, sb1_ref, sb2_ref,
                 qo_ref, ko_ref, qio_ref, kiwo_ref):
    ca, sa = ca_ref[...], sa_ref[...]
    cb, sb1, sb2 = cb_ref[...], sb1_ref[...], sb2_ref[...]

    def rope_head(x):
        return x * ca + pltpu.roll(x, SA_HEAD // 2, axis=1) * sa

    def rope_idx(x):
        return (x * cb + pltpu.roll(x, LANES - IDX_DIM // 2, axis=1) * sb1
                + pltpu.roll(x, IDX_DIM // 2, axis=1) * sb2)

    q_scale = float(SA_HEAD ** -0.5 * np.log2(np.e))
    for h in range(q_ref.shape[1] // LANES):
        sl = slice(h * LANES, (h + 1) * LANES)
        qo_ref[:, sl] = (rope_head(q_ref[:, sl]) * q_scale).astype(qo_ref.dtype)
    for h in range(k_ref.shape[1] // LANES):
        sl = slice(h * LANES, (h + 1) * LANES)
        ko_ref[:, sl] = rope_head(k_ref[:, sl])
    for hp in range(qi_ref.shape[1] // LANES):
        sl = slice(hp * LANES, (hp + 1) * LANES)
        qio_ref[:, sl] = rope_idx(qi_ref[:, sl]).astype(qio_ref.dtype)
    x = kiw_ref[...]
    lane = lax.broadcasted_iota(jnp.int32, (1, LANES), 1)
    kiwo_ref[...] = jnp.where(lane < IDX_DIM, rope_idx(x), x * float((IDX_HEADS * IDX_DIM) ** -0.5))


def _rope_tables(pos):
    def ang(half):
        inv = ROPE_THETA ** (-jnp.arange(half, dtype=F32) / half)
        return pos.astype(F32)[:, None] * inv[None, :]
    aa = ang(SA_HEAD // 2)
    ca = jnp.concatenate([jnp.cos(aa), jnp.cos(aa)], axis=1)
    sa = jnp.concatenate([-jnp.sin(aa), jnp.sin(aa)], axis=1)
    ab = ang(IDX_DIM // 2)
    z = jnp.zeros_like(ab)
    cb = jnp.concatenate([jnp.cos(ab)] * 4, axis=1)
    sb1 = jnp.concatenate([-jnp.sin(ab), z, -jnp.sin(ab), z], axis=1)
    sb2 = jnp.concatenate([z, jnp.sin(ab), z, jnp.sin(ab)], axis=1)
    return ca, sa, cb, sb1, sb2


def _rope(pfull, cols, tables, n_streams, t, tm, sa_w, kv_w, qi_w):
    m = pfull.shape[0]
    bps = t // tm
    tab = pl.BlockSpec((tm, LANES), lambda r: (r % bps, 0))
    return pl.pallas_call(
        _rope_kernel,
        grid=(m // tm,),
        in_specs=[
            pl.BlockSpec((tm, sa_w), lambda r: (r, cols["q"])),
            pl.BlockSpec((tm, kv_w), lambda r: (r, cols["ksa"])),
            pl.BlockSpec((tm, qi_w), lambda r: (r, cols["qi"])),
            pl.BlockSpec((tm, LANES), lambda r: (r, cols["kiw"])),
            tab, tab, tab, tab, tab,
        ],
        out_specs=[
            pl.BlockSpec((tm, sa_w), lambda r: (r, 0)),
            pl.BlockSpec((tm, kv_w), lambda r: (r, 0)),
            pl.BlockSpec((tm, qi_w), lambda r: (r, 0)),
            pl.BlockSpec((tm, LANES), lambda r: (r, 0)),
        ],
        out_shape=[
            jax.ShapeDtypeStruct((m, sa_w), MXU_DTYPE),
            jax.ShapeDtypeStruct((m, kv_w), F32),
            jax.ShapeDtypeStruct((m, qi_w), MXU_DTYPE),
            jax.ShapeDtypeStruct((m, LANES), F32),
        ],
        compiler_params=_cparams(1, SMALL_CALL_VMEM_BYTES),
        name="rope",
    )(pfull, pfull, pfull, pfull, *tables)


IDX_HEAD_GROUP = 4
IDX_SCORE_ROWS = 512
IDX_VALUE_PASSES = 16


FOLD_CHAINS = 2


def _fold_sublane_tiles(x, op):
    parts = [x[r:r + 8] for r in range(0, x.shape[0], 8)]
    chains = [functools.reduce(op, parts[c::FOLD_CHAINS]) for c in range(min(FOLD_CHAINS, len(parts)))]
    return functools.reduce(op, chains)


def _f32_to_key(x):
    bits = lax.bitcast_convert_type(x, jnp.int32)
    return bits ^ ((bits >> 31) & jnp.int32(0x7FFFFFFF))


def _key_to_f32(key):
    return lax.bitcast_convert_type(key ^ ((key >> 31) & jnp.int32(0x7FFFFFFF)), F32)


def _index_kernel(qit_ref, w_ref, ki_ref, mask_ref, keys_ref, *, tq, kb, nkb, past, topk):
    i = pl.program_id(1)
    n_adm = jnp.minimum(nkb, (past + (i + 1) * tq + kb - 1) // kb)
    chunk_bits = CHUNK.bit_length() - 1
    qpos = past + i * tq + lax.broadcasted_iota(jnp.int32, (1, tq), 1)
    lim = ((qpos >> chunk_bits) + 1) << chunk_bits
    krow = lax.broadcasted_iota(jnp.int32, (kb, 1), 0)
    hg = IDX_HEAD_GROUP
    fold = _fold_sublane_tiles

    def score_block(j, carry):
        smax, smin = carry
        for r0 in range(0, kb, IDX_SCORE_ROWS):
            rows = slice(r0, r0 + IDX_SCORE_ROWS)
            kblk = ki_ref[j, rows, :]
            acc = jnp.zeros((IDX_SCORE_ROWS, tq), F32)
            for g0 in range(0, IDX_HEADS, hg):
                s = jnp.dot(kblk, qit_ref[:, g0 * tq:(g0 + hg) * tq], preferred_element_type=F32)
                for g in range(hg):
                    acc = acc + jnp.maximum(s[:, g * tq:(g + 1) * tq], 0.0) * w_ref[g0 + g:g0 + g + 1, :]
            score = acc + 0.0
            adm = j * kb + r0 + krow[:IDX_SCORE_ROWS] < lim
            keys_ref[j, rows, :] = jnp.where(adm, _f32_to_key(score), jnp.int32(INT_MIN))
            smax = jnp.maximum(smax, fold(jnp.where(adm, score, -jnp.inf), jnp.maximum))
            smin = jnp.minimum(smin, fold(jnp.where(adm, score, jnp.inf), jnp.minimum))
        return smax, smin

    smax, smin = lax.fori_loop(0, n_adm, score_block,
                               (jnp.full((8, tq), -jnp.inf, F32), jnp.full((8, tq), jnp.inf, F32)))
    lo0 = _f32_to_key(jnp.min(smin, axis=0, keepdims=True))
    hi0 = _f32_to_key(jnp.max(smax, axis=0, keepdims=True)) + 1

    def count_ge(thr):
        def body(j, cnt):
            return cnt + fold(jnp.where(keys_ref[j] >= thr, 1.0, 0.0), jnp.add)
        return jnp.sum(lax.fori_loop(0, n_adm, body, jnp.zeros((8, tq), F32)), axis=0, keepdims=True)

    def midpoint(lo, hi):
        return (lo >> 1) + (hi >> 1) + (lo & hi & 1)

    def unfinished(lo, hi, cnt_lo):
        active = (midpoint(lo, hi) != lo) & (cnt_lo != float(topk))
        return jnp.max(jnp.where(active, 1.0, 0.0))

    def bisect(state):
        p, lo, hi, cnt_lo, cnt_hi, _ = state
        mid_v = _f32_to_key(0.5 * _key_to_f32(lo) + 0.5 * _key_to_f32(hi))
        by_value = (p < IDX_VALUE_PASSES) & (mid_v > lo) & (mid_v < hi)
        mid = jnp.where(by_value, mid_v, midpoint(lo, hi))
        cnt = count_ge(mid)
        ge = cnt >= float(topk)
        lo, cnt_lo = jnp.where(ge, mid, lo), jnp.where(ge, cnt, cnt_lo)
        hi, cnt_hi = jnp.where(ge, hi, mid), jnp.where(ge, cnt_hi, cnt)
        return p + 1, lo, hi, cnt_lo, cnt_hi, unfinished(lo, hi, cnt_lo)

    cnt_lo0 = lim.astype(F32)
    state = (jnp.int32(0), lo0, hi0, cnt_lo0, jnp.zeros((1, tq), F32), unfinished(lo0, hi0, cnt_lo0))
    _, thr, _, n_ge, n_gt, _ = lax.while_loop(lambda st: st[5] > 0.5, bisect, state)

    tied = n_ge > float(topk)
    need = float(topk) - n_gt
    end_all = jnp.int32(nkb * kb)

    def count_tied_below(bound):
        def body(j, cnt):
            hit = (keys_ref[j] == thr) & (j * kb + krow < bound)
            return cnt + fold(jnp.where(hit, 1.0, 0.0), jnp.add)
        return jnp.sum(lax.fori_loop(0, n_adm, body, jnp.zeros((8, tq), F32)), axis=0, keepdims=True)

    def tie_bisect(state):
        p, lo, hi = state
        mid = (lo + hi) >> 1
        enough = count_tied_below(mid) >= need
        return p + 1, jnp.where(enough, lo, mid), jnp.where(enough, mid, hi)

    any_tied = jnp.max(jnp.where(tied, 1.0, 0.0))
    n_steps = (nkb * kb).bit_length()
    tie_state = (jnp.int32(0), jnp.zeros((1, tq), jnp.int32), jnp.full((1, tq), end_all, jnp.int32))
    tie_hi = lax.while_loop(lambda st: (any_tied > 0.5) & (st[0] < n_steps), tie_bisect, tie_state)[2]
    tie_end = jnp.where(tied, tie_hi, end_all)

    def write(j, carry):
        k = keys_ref[j]
        keep = (k > thr) | ((k == thr) & (j * kb + krow < tie_end))
        mask_ref[j] = jnp.where(keep, 1.0, 0.0).T.astype(mask_ref.dtype)
        return carry

    lax.fori_loop(0, n_adm, write, 0)

    def clear(j, carry):
        mask_ref[j] = jnp.zeros((tq, kb), mask_ref.dtype)
        return carry

    lax.fori_loop(n_adm, nkb, clear, 0)


def _index_mask(qit, w, ki, n_streams, t, tq, kb, past, topk):
    nkb = ki.shape[1]
    nq = t // tq
    isz = jnp.dtype(MXU_DTYPE).itemsize
    vmem = (nkb * kb * tq * 4 + 2 * nkb * kb * tq + 2 * nkb * kb * LANES * isz + 2 * IDX_DIM * IDX_HEADS * tq * isz
            + 6 * kb * IDX_HEAD_GROUP * tq * 4 + (8 << 20))
    return pl.pallas_call(
        functools.partial(_index_kernel, tq=tq, kb=kb, nkb=nkb, past=past, topk=topk),
        grid=(n_streams, nq),
        in_specs=[
            pl.BlockSpec((None, None, IDX_DIM, IDX_HEADS * tq), lambda b, i: (b, i, 0, 0)),
            pl.BlockSpec((None, IDX_HEADS, tq), lambda b, i: (b, 0, i)),
            pl.BlockSpec((None, nkb, kb, IDX_DIM), lambda b, i: (b, 0, 0, 0)),
        ],
        out_specs=pl.BlockSpec((None, nkb, tq, kb), lambda b, i: (b, 0, i, 0)),
        out_shape=jax.ShapeDtypeStruct((n_streams, nkb, t, kb), jnp.int8),
        scratch_shapes=[pltpu.VMEM((nkb, kb, tq), jnp.int32)],
        compiler_params=_cparams(2, vmem),
        name="indexer_topk_mask",
    )(qit, w, ki)


def _attn_kernel(it_ref, jt_ref, last_ref, q_ref, k_ref, v_ref, m_ref, o_ref, qs_ref, mx_ref, l_ref, acc_ref,
                 *, group):
    step = pl.program_id(1)
    tq = q_ref.shape[0]
    tk = k_ref.shape[0]
    n_kv = k_ref.shape[1] // SA_HEAD
    rows = group * tq
    n_lane_tiles = tk // LANES

    @pl.when(jt_ref[step] == 0)
    def _():
        for n in range(n_kv):
            for g in range(group):
                h = n * group + g
                qs_ref[n, g * tq:(g + 1) * tq, :] = q_ref[:, h * SA_HEAD:(h + 1) * SA_HEAD]
        mx_ref[...] = jnp.full(mx_ref.shape, NEG_BIG, F32)
        l_ref[...] = jnp.zeros(l_ref.shape, F32)
        acc_ref[...] = jnp.zeros(acc_ref.shape, F32)

    bias = jnp.where(m_ref[...].astype(F32) > 0.0, 0.0, NEG_BIG)
    kv = range(n_kv)
    s = [_nt(qs_ref[n], k_ref[:, n * SA_HEAD:(n + 1) * SA_HEAD]) for n in kv]
    s = [(x.reshape(group, tq, tk) + bias[None]).reshape(rows, tk) for x in s]
    tiles = [[x[:, c * LANES:(c + 1) * LANES] for c in range(n_lane_tiles)] for x in s]
    m_old = [mx_ref[n] for n in kv]
    m_new = [jnp.maximum(mo, jnp.max(functools.reduce(jnp.maximum, t), axis=1, keepdims=True))
             for mo, t in zip(m_old, tiles)]
    alpha = [jnp.exp2(mo - mn) for mo, mn in zip(m_old, m_new)]
    p = [[jnp.exp2(x - mn) for x in t] for t, mn in zip(tiles, m_new)]
    for n in kv:
        mx_ref[n] = m_new[n]
        l_ref[n] = alpha[n] * l_ref[n] + functools.reduce(jnp.add, p[n])
    pv = [jnp.dot(_mx(jnp.concatenate(p[n], axis=1)), v_ref[:, n * SA_HEAD:(n + 1) * SA_HEAD],
                  preferred_element_type=F32) for n in kv]
    for n in kv:
        acc_ref[n] = alpha[n] * acc_ref[n] + pv[n]

    @pl.when(last_ref[step] == 1)
    def _():
        for n in range(n_kv):
            o = acc_ref[n] / jnp.sum(l_ref[n], axis=1, keepdims=True)
            for g in range(group):
                h = n * group + g
                o_ref[:, h * SA_HEAD:(h + 1) * SA_HEAD] = o[g * tq:(g + 1) * tq].astype(o_ref.dtype)


def _attn_tiles(t, tq, tk, nkb, past):
    ii, jj, last = [], [], []
    for i in range(t // tq):
        n_adm = min(nkb, -(-(past + (i + 1) * tq) // tk))
        for j in range(n_adm):
            ii.append(i)
            jj.append(j)
            last.append(int(j == n_adm - 1))
    return tuple(jnp.asarray(np.array(x, np.int32)) for x in (ii, jj, last))


def _attention(q, k_all, v_all, mask, n_streams, t, tq, tk, past):
    m, sa_w = q.shape
    kv_w = k_all.shape[2]
    nkb = mask.shape[1]
    nq = t // tq
    n_heads = sa_w // SA_HEAD
    group = n_heads // SA_KV_HEADS
    it, jt, last = _attn_tiles(t, tq, tk, nkb, past)
    grid_spec = pltpu.PrefetchScalarGridSpec(
        num_scalar_prefetch=3,
        grid=(n_streams, int(it.shape[0])),
        in_specs=[
            pl.BlockSpec((tq, sa_w), lambda b, s, it, jt, lt: (b * nq + it[s], 0)),
            pl.BlockSpec((None, tk, kv_w), lambda b, s, it, jt, lt: (b, jt[s], 0)),
            pl.BlockSpec((None, tk, kv_w), lambda b, s, it, jt, lt: (b, jt[s], 0)),
            pl.BlockSpec((None, None, tq, tk), lambda b, s, it, jt, lt: (b, jt[s], it[s], 0)),
        ],
        out_specs=pl.BlockSpec((tq, sa_w), lambda b, s, it, jt, lt: (b * nq + it[s], 0)),
        scratch_shapes=[pltpu.VMEM((SA_KV_HEADS, group * tq, SA_HEAD), MXU_DTYPE)]
        + [pltpu.VMEM((SA_KV_HEADS, group * tq, SA_HEAD), F32)] * 3,
    )
    return pl.pallas_call(
        functools.partial(_attn_kernel, group=group),
        grid_spec=grid_spec,
        out_shape=jax.ShapeDtypeStruct((m, sa_w), MXU_DTYPE),
        compiler_params=_cparams(2, ATTN_VMEM_BYTES),
        name="masked_flash_attention",
    )(it, jt, last, q, k_all, v_all, mask)


def _prepare_weights(w_in, mu_shift, w_decay_up, decay_bias, w_a_up, a_bias, w_gate_up, k_k, k_a, r_k,
                     lnx_w, lnx_b, w_out, w_up, conv_w, conv_b, w_down, d_model):
    rw = w_decay_up.shape[1]
    rw_cols = 3 * rw + D_DECAY + D_AAA + D_GATE
    sa_w = d_model - rw
    kv_w = SA_KV_HEADS * SA_HEAD
    qi_w = IDX_HEADS * IDX_DIM
    o = {}
    o["r"] = 0
    o["wd"] = rw
    o["k"] = rw + D_DECAY
    o["v"] = 2 * rw + D_DECAY
    o["ad"] = 3 * rw + D_DECAY
    o["gd"] = 3 * rw + D_DECAY + D_AAA
    o["q"] = rw_cols
    o["ksa"] = rw_cols + sa_w
    o["vsa"] = o["ksa"] + kv_w
    o["qi"] = o["vsa"] + kv_w
    o["kiw"] = o["qi"] + qi_w
    kiw_w = IDX_DIM + IDX_HEADS

    def seg(x, name, width, pad_to=None):
        s = x[..., o[name]:o[name] + width]
        if pad_to is not None and pad_to > width:
            s = jnp.pad(s, [(0, 0)] * (s.ndim - 1) + [(0, pad_to - width)])
        return s

    def rw_small(x):
        return jnp.concatenate([seg(x, "wd", D_DECAY, LANES), seg(x, "ad", D_AAA, LANES), seg(x, "gd", D_GATE)], -1)

    w_in_l = jnp.concatenate([
        seg(w_in, "r", rw), seg(w_in, "k", rw), seg(w_in, "v", rw), seg(w_in, "q", sa_w), seg(w_in, "qi", qi_w),
        seg(w_in, "ksa", kv_w), seg(w_in, "vsa", kv_w), rw_small(w_in), seg(w_in, "kiw", kiw_w, LANES)], axis=1)
    w_in_l = jnp.pad(w_in_l, ((0, 0), (0, -w_in_l.shape[1] % IN_PROJ_TN)))
    starts = {"r": 0, "k": rw, "v": 2 * rw, "q": 3 * rw, "qi": 3 * rw + sa_w}
    starts["ksa"] = starts["qi"] + qi_w
    starts["vsa"] = starts["ksa"] + kv_w
    starts["small"] = starts["vsa"] + kv_w
    starts["kiw"] = starts["small"] + 4 * LANES
    pad_rows = lambda x, n: jnp.pad(x, ((0, n - x.shape[0]), (0, 0)))
    row = lambda x: x.reshape(1, -1).astype(F32)
    mu = mu_shift.reshape(1, -1)
    w = {
        "w_in": _mx(w_in_l),
        "mu_r": seg(mu, "r", rw), "mu_k": seg(mu, "k", rw), "mu_v": seg(mu, "v", rw), "mu_small": rw_small(mu),
        "w_decay_up": _mx(pad_rows(w_decay_up, LANES)), "decay_bias": row(decay_bias),
        "w_a_up": _mx(pad_rows(w_a_up, LANES)), "a_bias": row(a_bias),
        "w_gate_up": _mx(w_gate_up),
        "k_k": row(k_k), "k_a": row(k_a), "r_k": row(r_k), "lnx_w": row(lnx_w), "lnx_b": row(lnx_b),
        "w_out": _mx(w_out), "w_up": _mx(w_up), "conv_w": conv_w, "conv_b": row(conv_b), "w_down": _mx(w_down),
    }
    dims = {"rw": rw, "rw_cols": rw_cols, "sa_w": sa_w, "kv_w": kv_w, "qi_w": qi_w, "orig": o, "starts": starts,
            "seg": seg, "rw_small": rw_small}
    return w, dims


def _layer(x, past_k, past_v, past_ki, wkv0, shift0, conv0, norms, w, dims, tiles):
    bsz, t, d = x.shape
    past = past_k.shape[1]
    m = bsz * t
    rw, sa_w, kv_w, qi_w = dims["rw"], dims["sa_w"], dims["kv_w"], dims["qi_w"]
    starts, seg, rw_small = dims["starts"], dims["seg"], dims["rw_small"]
    assert t % CHUNK == 0 and past % CHUNK == 0
    tm, tq, kb = tiles["tm"], tiles["tq"], tiles["kb"]
    g_mix_pre, g_mix_post, g_ffn_pre, g_ffn_post = norms

    x2 = x.reshape(m, d)
    h1 = _rmsnorm(x2, g_mix_pre, tiles["tm_norm"])
    pfull = _matmul(h1, w["w_in"], tiles["tm_in"], tiles["tn_in"])

    sh = shift0.reshape(bsz, 1, -1)
    shift_l = {"r": seg(sh, "r", rw), "k": seg(sh, "k", rw), "v": seg(sh, "v", rw), "small": rw_small(sh)}
    cols_rw = {n: starts[n] // RW_GROUP for n in ("r", "k", "v")}
    cols_rw["small"] = starts["small"] // RW_SMALL
    o_rw, wkv1 = _rwkv(pfull, cols_rw, shift_l, wkv0, w, bsz, t)

    pos = past + jnp.arange(t, dtype=jnp.int32)
    cols_sa = {"q": starts["q"] // sa_w, "ksa": starts["ksa"] // kv_w, "qi": starts["qi"] // qi_w,
               "kiw": starts["kiw"] // LANES}
    q_r, k_r, qi_r, kiw_r = _rope(pfull, cols_sa, _rope_tables(pos), bsz, t, tq, sa_w, kv_w, qi_w)
    k_new = k_r.reshape(bsz, t, kv_w)
    v_new = pfull[:, starts["vsa"]:starts["vsa"] + kv_w].reshape(bsz, t, kv_w)
    ki_new = kiw_r[:, :IDX_DIM].reshape(bsz, t, IDX_DIM)
    l_all = past + t
    lp = -(-l_all // kb) * kb
    nkb = lp // kb
    cat = lambda old, new: jnp.pad(
        jnp.concatenate([_mx(old.reshape(bsz, past, new.shape[-1])), _mx(new)], axis=1),
        ((0, 0), (0, lp - l_all), (0, 0)))
    k_all, v_all, ki_all = cat(past_k, k_new), cat(past_v, v_new), cat(past_ki, ki_new)
    nq = t // tq
    qit = (qi_r.reshape(bsz, nq, tq, IDX_HEADS, IDX_DIM).transpose(0, 1, 4, 3, 2)
           .reshape(bsz, nq, IDX_DIM, IDX_HEADS * tq))
    w_idx = kiw_r[:, IDX_DIM:IDX_DIM + IDX_HEADS].reshape(bsz, t, IDX_HEADS).transpose(0, 2, 1)
    topk = min(TOPK_MAX, l_all // 4)
    mask = _index_mask(qit, w_idx, ki_all.reshape(bsz, nkb, kb, IDX_DIM), bsz, t, tq, kb, past, topk)
    o_sa = _attention(q_r, k_all, v_all, mask, bsz, t, tq, kb, past)

    x1 = _matmul_norm_residual([o_rw, o_sa], w["w_out"], x2, g_mix_post, tm, tiles["tn_out"], "out_proj")
    h2 = _rmsnorm(x1, g_ffn_pre, tiles["tm_norm"])
    act, conv1 = _ffn_up(h2, w["w_up"], conv0, w["conv_w"], w["conv_b"], bsz, tiles["tm_up"], tiles["tn_up"])
    x_out = _matmul_norm_residual([act], w["w_down"], x1, g_ffn_post, tm, tiles["tn_out"], "ffn_down")

    last = pfull.reshape(bsz, t, -1)[:, t - 1:t]
    sm = last[..., starts["small"]:starts["small"] + 4 * LANES]
    shift1 = jnp.concatenate([
        last[..., starts["r"]:starts["r"] + rw], sm[..., :D_DECAY],
        last[..., starts["k"]:starts["k"] + rw], last[..., starts["v"]:starts["v"] + rw],
        sm[..., LANES:LANES + D_AAA], sm[..., 2 * LANES:2 * LANES + D_GATE]], axis=-1)
    new = (k_new.reshape(bsz, t, SA_KV_HEADS, SA_HEAD), v_new.reshape(bsz, t, SA_KV_HEADS, SA_HEAD), ki_new,
           wkv1, shift1, conv1)
    return x_out.reshape(bsz, t, d), new


def _tiles(bsz, t):
    big = t >= 1024
    return {
        "tm_norm": 256 if big else CHUNK,
        "tm_in": 1024 if big else bsz * t,
        "tn_in": IN_PROJ_TN,
        "tm": 512 if big else bsz * t,
        "tn_out": 512,
        "tm_up": 1024 if big else bsz * t,
        "tn_up": 512,
        "tq": 256 if big else CHUNK,
        "kb": 512,
    }


def _run_stream(x, past_k, past_v, past_ki, wkv0, shift0, conv0, norm_w, weights, dims, depth):
    outs = []
    tiles = _tiles(x.shape[0], x.shape[1])
    for l in range(depth):
        w_l = {n: v[l] for n, v in weights.items()}
        norms = tuple(g[l].reshape(1, -1) for g in norm_w)
        x, st = _layer(x, past_k[l], past_v[l], past_ki[l], wkv0[l], shift0[l], conv0[l], norms, w_l, dims[l], tiles)
        outs.append(st)
    return x, [jnp.stack(s) for s in zip(*outs)]


def kernel(x_prompt, x_sample, cache_k, cache_v, cache_kidx, state_wkv, state_shift, state_conv, norm_mix_pre, norm_mix_post, norm_ffn_pre, norm_ffn_post, w_in, mu_shift, w_decay_up, decay_bias, w_a_up, a_bias, w_gate_up, k_k, k_a, r_k, lnx_w, lnx_b, w_out, w_up, conv_w, conv_b, w_down):
    depth, d_model = norm_mix_pre.shape
    per_layer = [_prepare_weights(w_in[l], mu_shift[l], w_decay_up[l], decay_bias[l], w_a_up[l], a_bias[l],
                                  w_gate_up[l], k_k[l], k_a[l], r_k[l], lnx_w[l], lnx_b[l], w_out[l], w_up[l],
                                  conv_w[l], conv_b[l], w_down[l], d_model) for l in range(depth)]
    weights = {n: [pw[0][n] for pw in per_layer] for n in per_layer[0][0]}
    dims = [pw[1] for pw in per_layer]
    norm_w = (norm_mix_pre, norm_mix_post, norm_ffn_pre, norm_ffn_post)

    dt = x_prompt.dtype
    bp = x_prompt.shape[0]
    rw_heads = state_wkv.shape[2]
    zk = jnp.zeros((depth, bp, 0, SA_KV_HEADS, SA_HEAD), dt)
    zki = jnp.zeros((depth, bp, 0, IDX_DIM), dt)
    zwkv = jnp.zeros((depth, bp, rw_heads, RW_HEAD, RW_HEAD), dt)
    zshift = jnp.zeros((depth, bp, 1, state_shift.shape[-1]), dt)
    zconv = jnp.zeros((depth, bp, CONV_W - 1, state_conv.shape[-1]), dt)
    y_prompt, p_new = _run_stream(x_prompt, zk, zk, zki, zwkv, zshift, zconv, norm_w, weights, dims, depth)
    y_sample, s_new = _run_stream(x_sample, cache_k, cache_v, cache_kidx, state_wkv, state_shift, state_conv,
                                  norm_w, weights, dims, depth)
    return (y_prompt, y_sample, *p_new, *s_new)
```

```python
import functools

import numpy as np
import jax
import jax.numpy as jnp
from jax import lax
from jax.experimental import pallas as pl
from jax.experimental.pallas import tpu as pltpu

F32 = jnp.float32
MXU_DTYPE = jnp.bfloat16

CHUNK = 64
RW_HEAD = 64
D_DECAY = 96
D_AAA = 96
D_GATE = 256
SA_HEAD = 128
SA_KV_HEADS = 4
IDX_HEADS = 16
IDX_DIM = 64
TOPK_MAX = 256
CONV_W = 3
ROPE_THETA = 10000.0
NORM_EPS = 1e-6
LNX_EPS = 64e-5

LANES = 128
V7X_VMEM_LIMIT_BYTES = 60000 * 1024
SMALL_CALL_VMEM_BYTES = 32 << 20
ATTN_VMEM_BYTES = 40 << 20
INT_MIN = -(2 ** 31)
NEG_BIG = -1e30


def _cparams(n_grid, vmem_bytes):
    limit = int(min(V7X_VMEM_LIMIT_BYTES, max(SMALL_CALL_VMEM_BYTES, vmem_bytes)))
    return pltpu.CompilerParams(dimension_semantics=("arbitrary",) * n_grid, vmem_limit_bytes=limit)


def _nt(a, b):
    return lax.dot_general(a, b, (((1,), (1,)), ((), ())), preferred_element_type=F32)


def _mx(x):
    return x.astype(MXU_DTYPE)


def _rmsnorm_kernel(x_ref, g_ref, o_ref):
    x = x_ref[...]
    y = x * lax.rsqrt(jnp.mean(x * x, axis=-1, keepdims=True) + NORM_EPS)
    o_ref[...] = (y * g_ref[...]).astype(o_ref.dtype)


def _rmsnorm(x, g, tm):
    m, d = x.shape
    return pl.pallas_call(
        _rmsnorm_kernel,
        grid=(m // tm,),
        in_specs=[pl.BlockSpec((tm, d), lambda i: (i, 0)), pl.BlockSpec((1, d), lambda i: (0, 0))],
        out_specs=pl.BlockSpec((tm, d), lambda i: (i, 0)),
        out_shape=jax.ShapeDtypeStruct((m, d), MXU_DTYPE),
        compiler_params=_cparams(1, 6 * tm * d * 4),
        name="rmsnorm",
    )(x, g)


def _mm_kernel(a_ref, b_ref, o_ref):
    o_ref[...] = jnp.dot(a_ref[...], b_ref[...], preferred_element_type=F32)


IN_PROJ_TN = 1024


def _matmul(a, b, tm, tn):
    m, k = a.shape
    n = b.shape[1]
    isz = jnp.dtype(MXU_DTYPE).itemsize
    vmem = 2 * (tm * k * isz + k * tn * isz + tm * tn * 4) + 2 * tm * tn * 4 + (8 << 20)
    return pl.pallas_call(
        _mm_kernel,
        grid=(m // tm, n // tn),
        in_specs=[pl.BlockSpec((tm, k), lambda i, j: (i, 0)), pl.BlockSpec((k, tn), lambda i, j: (0, j))],
        out_specs=pl.BlockSpec((tm, tn), lambda i, j: (i, j)),
        out_shape=jax.ShapeDtypeStruct((m, n), F32),
        compiler_params=_cparams(2, vmem),
        name="in_proj",
    )(a, b)


MM_NORM_ROWS = 128


def _mm_norm_kernel(*refs, nn, n_ops):
    a_refs, w_refs = refs[:n_ops], refs[n_ops:2 * n_ops]
    x_ref, gpost_ref, x1_ref = refs[2 * n_ops:]
    n = pl.program_id(1)
    tm, d = x1_ref.shape
    tn = w_refs[0].shape[1]
    parts = [jnp.dot(a_ref[...], w_ref[...], preferred_element_type=F32) for a_ref, w_ref in zip(a_refs, w_refs)]
    x1_ref[:, pl.ds(pl.multiple_of(n * tn, LANES), tn)] = functools.reduce(jnp.add, parts)

    @pl.when(n == nn - 1)
    def _():
        step = min(tm, MM_NORM_ROWS)
        for r in range(0, tm, step):
            rows = slice(r, r + step)
            y = x1_ref[rows, :]
            y = y * lax.rsqrt(jnp.mean(y * y, axis=-1, keepdims=True) + NORM_EPS) * gpost_ref[...]
            x1_ref[rows, :] = x_ref[rows, :] + y


def _matmul_norm_residual(a_slabs, w, x, g_post, tm, tn, name):
    n_ops = len(a_slabs)
    m, ks = a_slabs[0].shape
    kdim = n_ops * ks
    d = w.shape[1]
    nn = d // tn
    isz = jnp.dtype(MXU_DTYPE).itemsize
    fixed = 2 * kdim * tn * isz + tm * d * 4 + 2 * tm * tn * 4 + 6 * MM_NORM_ROWS * d * 4 + (4 << 20)
    per_copy = tm * kdim * isz + tm * d * 4
    row_bufs = 2 if fixed + 2 * per_copy <= V7X_VMEM_LIMIT_BYTES - (4 << 20) else 1
    vmem = fixed + row_bufs * per_copy
    w_rows = lambda p: (lambda i, n: (p, n))
    return pl.pallas_call(
        functools.partial(_mm_norm_kernel, nn=nn, n_ops=n_ops),
        grid=(m // tm, nn),
        in_specs=[pl.BlockSpec((tm, ks), lambda i, n: (i, 0), pipeline_mode=pl.Buffered(row_bufs))] * n_ops
        + [pl.BlockSpec((ks, tn), w_rows(p)) for p in range(n_ops)]
        + [
            pl.BlockSpec((tm, d), lambda i, n: (i, 0), pipeline_mode=pl.Buffered(1)),
            pl.BlockSpec((1, d), lambda i, n: (0, 0)),
        ],
        out_specs=pl.BlockSpec((tm, d), lambda i, n: (i, 0), pipeline_mode=pl.Buffered(row_bufs)),
        out_shape=jax.ShapeDtypeStruct((m, d), F32),
        compiler_params=_cparams(2, vmem),
        name=name,
    )(*a_slabs, *([w] * n_ops), x, g_post)


def _gelu_tanh(x):
    return 0.5 * x * (1.0 + jnp.tanh(0.7978845608028654 * (x + 0.044715 * x * x * x)))


def _ffn_up_kernel(h_ref, wg_ref, wv_ref, c0g_ref, c0v_ref, cwg_ref, cwv_ref, cbg_ref, cbv_ref,
                   a_ref, cg_ref, cv_ref, carry_g, carry_v, *, blocks_per_stream, tm):
    i = pl.program_id(1) % blocks_per_stream
    n_str = carry_g.shape[0]
    ts = tm // n_str

    @pl.when(i == 0)
    def _():
        carry_g[...] = c0g_ref[...]
        carry_v[...] = c0v_ref[...]

    h = h_ref[...]
    row = lax.broadcasted_iota(jnp.int32, (tm, 1), 0)

    def conv(u, carry_ref, cw_ref, cb_ref, out_ref):
        u1 = pltpu.roll(u, 1, axis=0)
        u2 = pltpu.roll(u, 2, axis=0)
        for s in range(n_str):
            p = carry_ref[s]
            u1 = jnp.where(row == s * ts, p[1:2], u1)
            u2 = jnp.where(row == s * ts, p[0:1], jnp.where(row == s * ts + 1, p[1:2], u2))
            last = u[(s + 1) * ts - 2:(s + 1) * ts]
            carry_ref[s] = last
            out_ref[s] = last
        cw = cw_ref[...]
        return u2 * cw[0:1] + u1 * cw[1:2] + u * cw[2:3] + cb_ref[...]

    gate = conv(jnp.dot(h, wg_ref[...], preferred_element_type=F32), carry_g, cwg_ref, cbg_ref, cg_ref)
    val = conv(jnp.dot(h, wv_ref[...], preferred_element_type=F32), carry_v, cwv_ref, cbv_ref, cv_ref)
    a_ref[...] = (_gelu_tanh(gate) * val).astype(a_ref.dtype)


def _ffn_up(h, w_up, conv0, conv_w, conv_b, n_streams, tm, tn):
    m, d = h.shape
    f2 = w_up.shape[1]
    f = f2 // 2
    nj = f // tn
    nr = m // tm
    t = m // n_streams
    spb = max(1, tm // t)
    bps = max(1, t // tm)
    isz = jnp.dtype(MXU_DTYPE).itemsize
    vmem = 2 * (tm * d * isz + 2 * d * tn * isz + tm * tn * isz) + 10 * tm * tn * 4 + (4 << 20)
    col = lambda off: (lambda j, r: (0, off + j))
    st = lambda off: (lambda j, r: (r // bps, 0, off + j))
    a, cg, cv = pl.pallas_call(
        functools.partial(_ffn_up_kernel, blocks_per_stream=bps, tm=tm),
        grid=(nj, nr),
        in_specs=[
            pl.BlockSpec((tm, d), lambda j, r: (r, 0)),
            pl.BlockSpec((d, tn), col(0)),
            pl.BlockSpec((d, tn), col(nj)),
            pl.BlockSpec((spb, CONV_W - 1, tn), st(0)),
            pl.BlockSpec((spb, CONV_W - 1, tn), st(nj)),
            pl.BlockSpec((CONV_W, tn), col(0)),
            pl.BlockSpec((CONV_W, tn), col(nj)),
            pl.BlockSpec((1, tn), col(0)),
            pl.BlockSpec((1, tn), col(nj)),
        ],
        out_specs=[
            pl.BlockSpec((tm, tn), lambda j, r: (r, j)),
            pl.BlockSpec((spb, CONV_W - 1, tn), st(0)),
            pl.BlockSpec((spb, CONV_W - 1, tn), st(0)),
        ],
        out_shape=[
            jax.ShapeDtypeStruct((m, f), MXU_DTYPE),
            jax.ShapeDtypeStruct((n_streams, CONV_W - 1, f), F32),
            jax.ShapeDtypeStruct((n_streams, CONV_W - 1, f), F32),
        ],
        scratch_shapes=[pltpu.VMEM((spb, CONV_W - 1, tn), F32)] * 2,
        compiler_params=_cparams(2, vmem),
        name="ffn_up_conv",
    )(h, w_up, w_up, conv0, conv0, conv_w, conv_w, conv_b, conv_b)
    return a, jnp.concatenate([cg, cv], axis=-1)


RW_GROUP = 16 * LANES
RW_SMALL = 4 * LANES


def _softplus(z):
    return jnp.maximum(z, 0.0) + jnp.log(1.0 + jnp.exp(-jnp.abs(z)))


def _sigmoid(z):
    return 1.0 / (1.0 + jnp.exp(-z))


def _dg(a, b, dims):
    return lax.dot_general(_mx(a), _mx(b), (dims, ((), ())), preferred_element_type=F32)


_NN = ((1,), (0,))
_NT = ((1,), (1,))
_TN = ((0,), (0,))


def _segsum(x, seg):
    return jnp.dot(_mx(x), seg, preferred_element_type=F32)


def _each(f, *lists):
    return [f(*args) for args in zip(*lists)]


def _wkv_chunk(r, k, v, kk, b, cum, logd, s_bd, consts):
    m0, m1, tri_mask = consts
    c = r[0].shape[0]
    c2 = 2 * c
    stack = lambda x: jnp.concatenate([x * m0, x * m1], axis=0)
    g_end = _each(lambda cm: jnp.exp(cm[c - 1:c]), cum)
    e_inc = _each(jnp.exp, cum)
    e_prev = _each(lambda cm, ld: jnp.exp(cm - ld), cum, logd)
    e_neg = _each(lambda cm: jnp.exp(-cm), cum)
    e_tail = _each(lambda g, e: g * e, g_end, e_neg)
    vs = _each(stack, v)
    lhs = _each(lambda kk_, r_, ep, ei: jnp.concatenate([stack(kk_ * ep), stack(r_ * ei)], axis=0),
                kk, r, e_prev, e_inc)
    rhs = _each(lambda k_, b_, en: jnp.concatenate([stack(k_ * en), stack(b_ * en)], axis=0), k, b, e_neg)
    tails = _each(lambda k_, b_, et: jnp.concatenate([stack(k_ * et), stack(b_ * et)], axis=0), k, b, e_tail)
    amat = _each(lambda x, y: jnp.where(tri_mask, _dg(x, y, _NT), 0.0), lhs, rhs)
    sp = _each(lambda x, s: _dg(x, s, _NT), lhs, s_bd)
    av = _each(lambda a_, v_: _dg(a_[:, :c2], v_, _NN), amat, vs)
    pw = _each(lambda a_: -a_[:c2, c2:], amat)
    x = _each(lambda s_, a_: s_[:c2] + a_[:c2], sp, av)
    n_lvl = int(np.log2(c))
    for lvl in range(n_lvl - 1):
        z = _each(lambda p_, x_: _dg(p_, jnp.concatenate([p_, x_], axis=1), _NN), pw, x)
        pw = _each(lambda z_: z_[:, :c2], z)
        x = _each(lambda x_, z_: x_ + z_[:, c2:], x, z)
    u = _each(lambda p_, x_: x_ + _dg(p_, x_, _NN), pw, x)
    ys = _each(lambda s_, a_, am, u_: s_[c2:] + a_[c2:] - _dg(am[c2:, c2:], u_, _NN), sp, av, amat, u)
    y = _each(lambda y_: y_[:c] + y_[c:], ys)
    s_new = _each(lambda s, g, v_, u_, t_: s * g + _dg(jnp.concatenate([v_, -u_], axis=0), t_, _TN),
                  s_bd, g_end, vs, u, tails)
    return y, s_new


def _rwkv_kernel(pr_ref, pk_ref, pv_ref, ps_ref, sr_ref, sk_ref, sv_ref, ss_ref, wkv0_ref,
                 mur_ref, muk_ref, muv_ref, mus_ref, wdec_ref, dbias_ref, wa_ref, abias_ref, wg_ref,
                 kk_ref, ka_ref, rk_ref, lnw_ref, lnb_ref,
                 o_ref, wkv1_ref,
                 cr_ref, ck_ref, cv_ref, cs_ref, state_ref, *, n_chunks):
    c = pl.program_id(2)
    C = CHUNK
    npairs = RW_GROUP // LANES

    lane = lax.broadcasted_iota(jnp.int32, (1, LANES), 1)
    m0 = (lane < RW_HEAD).astype(F32)
    m1 = 1.0 - m0
    ri = lax.broadcasted_iota(jnp.int32, (4 * C, 4 * C), 0)
    ci = lax.broadcasted_iota(jnp.int32, (4 * C, 4 * C), 1)
    same_head = ((ri // C) % 2) == ((ci // C) % 2)
    tri_mask = same_head & ((ri % C) + ri // (2 * C) > (ci % C))
    li = lax.broadcasted_iota(jnp.int32, (LANES, LANES), 0)
    lj = lax.broadcasted_iota(jnp.int32, (LANES, LANES), 1)
    seg = ((li // RW_HEAD) == (lj // RW_HEAD)).astype(MXU_DTYPE)
    consts = (m0, m1, tri_mask)

    @pl.when(c == 0)
    def _():
        cr_ref[...] = sr_ref[...]
        ck_ref[...] = sk_ref[...]
        cv_ref[...] = sv_ref[...]
        cs_ref[...] = ss_ref[...]
        z = jnp.zeros((RW_HEAD, RW_HEAD), F32)
        for p in range(npairs):
            s0 = wkv0_ref[2 * p]
            s1 = wkv0_ref[2 * p + 1]
            state_ref[p] = jnp.concatenate(
                [jnp.concatenate([s0, z], axis=1), jnp.concatenate([z, s1], axis=1)], axis=0)

    row = lax.broadcasted_iota(jnp.int32, (C, 1), 0)

    def shifted(p_ref, carry_ref, mu_ref):
        p = p_ref[...]
        prev = jnp.where(row == 0, carry_ref[...], pltpu.roll(p, 1, axis=0))
        carry_ref[...] = p[C - 1:C]
        return p + (prev - p) * mu_ref[...]

    r = shifted(pr_ref, cr_ref, mur_ref)
    k = shifted(pk_ref, ck_ref, muk_ref)
    v = shifted(pv_ref, cv_ref, muv_ref)
    sm = shifted(ps_ref, cs_ref, mus_ref)
    wd, ad, gd = sm[:, 0:LANES], sm[:, LANES:2 * LANES], sm[:, 2 * LANES:4 * LANES]

    dec_in = dbias_ref[...] + jnp.dot(_mx(jnp.tanh(wd)), wdec_ref[...], preferred_element_type=F32)
    w_log = -_softplus(-dec_in) - 0.5
    logd = -jnp.exp(w_log)
    a = _sigmoid(abias_ref[...] + jnp.dot(_mx(ad), wa_ref[...], preferred_element_type=F32))
    g = jnp.dot(_mx(_sigmoid(gd)), wg_ref[...], preferred_element_type=F32)
    kk = k * kk_ref[...]
    k2 = k * (1.0 + (a - 1.0) * ka_ref[...])
    rkr = r * k2 * rk_ref[...]
    cum = logd
    for sft in (1, 2, 4, 8, 16, 32):
        cum = cum + jnp.where(row >= sft, pltpu.roll(cum, sft, axis=0), 0.0)

    pairs = lambda x: [x[:, p * LANES:(p + 1) * LANES] for p in range(npairs)]
    r_p, k_p, v_p, a_p, cum_p, logd_p = pairs(r), pairs(k2), pairs(v), pairs(a), pairs(cum), pairs(logd)
    sums = _each(lambda kk_, rkr_: _segsum(jnp.concatenate([kk_ * kk_, rkr_], axis=0), seg), pairs(kk), pairs(rkr))
    kk_p = _each(lambda kk_, s_: kk_ * lax.rsqrt(jnp.maximum(s_[:C], 1e-24)), pairs(kk), sums)
    b_p = _each(lambda kk_, a_: kk_ * a_, kk_p, a_p)
    y_p, s_new = _wkv_chunk(r_p, k_p, v_p, kk_p, b_p, cum_p, logd_p, [state_ref[p] for p in range(npairs)], consts)
    for p in range(npairs):
        state_ref[p] = s_new[p]
    yc = _each(lambda y_: y_ - _segsum(y_, seg) * (1.0 / RW_HEAD), y_p)
    var = _each(lambda yc_: _segsum(yc_ * yc_, seg) * (1.0 / RW_HEAD), yc)
    for p in range(npairs):
        sl = slice(p * LANES, (p + 1) * LANES)
        yn = yc[p] * lax.rsqrt(var[p] + LNX_EPS) * lnw_ref[:, sl] + lnb_ref[:, sl]
        o_ref[:, sl] = ((yn + sums[p][C:] * v_p[p]) * g[:, sl]).astype(o_ref.dtype)

    @pl.when(c == n_chunks - 1)
    def _():
        for p in range(npairs):
            s = state_ref[p]
            wkv1_ref[2 * p] = s[:RW_HEAD, :RW_HEAD]
            wkv1_ref[2 * p + 1] = s[RW_HEAD:, RW_HEAD:]


def _rwkv(pfull, cols, shift0, wkv0, w, n_streams, t):
    m = pfull.shape[0]
    rw = w["k_k"].shape[1]
    ng = rw // RW_GROUP
    nc = t // CHUNK
    hg = RW_GROUP // RW_HEAD
    rowblk = lambda off: (lambda b, g, c: (b * nc + c, off + g))
    fixed = lambda off: (lambda b, g, c: (b * nc + c, off))
    st = lambda b, g, c: (b, 0, g)
    st0 = lambda b, g, c: (b, 0, 0)
    wcol = lambda b, g, c: (0, g)
    in_specs = [
        pl.BlockSpec((CHUNK, RW_GROUP), rowblk(cols["r"])),
        pl.BlockSpec((CHUNK, RW_GROUP), rowblk(cols["k"])),
        pl.BlockSpec((CHUNK, RW_GROUP), rowblk(cols["v"])),
        pl.BlockSpec((CHUNK, RW_SMALL), fixed(cols["small"])),
        pl.BlockSpec((None, 1, RW_GROUP), st),
        pl.BlockSpec((None, 1, RW_GROUP), st),
        pl.BlockSpec((None, 1, RW_GROUP), st),
        pl.BlockSpec((None, 1, RW_SMALL), st0),
        pl.BlockSpec((None, hg, RW_HEAD, RW_HEAD), lambda b, g, c: (b, g, 0, 0)),
        pl.BlockSpec((1, RW_GROUP), wcol), pl.BlockSpec((1, RW_GROUP), wcol), pl.BlockSpec((1, RW_GROUP), wcol),
        pl.BlockSpec((1, RW_SMALL), lambda b, g, c: (0, 0)),
        pl.BlockSpec((LANES, RW_GROUP), wcol), pl.BlockSpec((1, RW_GROUP), wcol),
        pl.BlockSpec((LANES, RW_GROUP), wcol), pl.BlockSpec((1, RW_GROUP), wcol),
        pl.BlockSpec((2 * LANES, RW_GROUP), wcol),
        pl.BlockSpec((1, RW_GROUP), wcol), pl.BlockSpec((1, RW_GROUP), wcol), pl.BlockSpec((1, RW_GROUP), wcol),
        pl.BlockSpec((1, RW_GROUP), wcol), pl.BlockSpec((1, RW_GROUP), wcol),
    ]
    o, wkv1 = pl.pallas_call(
        functools.partial(_rwkv_kernel, n_chunks=nc),
        grid=(n_streams, ng, nc),
        in_specs=in_specs,
        out_specs=[
            pl.BlockSpec((CHUNK, RW_GROUP), lambda b, g, c: (b * nc + c, g)),
            pl.BlockSpec((None, hg, RW_HEAD, RW_HEAD), lambda b, g, c: (b, g, 0, 0)),
        ],
        out_shape=[
            jax.ShapeDtypeStruct((m, rw), MXU_DTYPE),
            jax.ShapeDtypeStruct((n_streams, rw // RW_HEAD, RW_HEAD, RW_HEAD), F32),
        ],
        scratch_shapes=[pltpu.VMEM((1, RW_GROUP), F32)] * 3 + [pltpu.VMEM((1, RW_SMALL), F32)]
        + [pltpu.VMEM((RW_GROUP // LANES, LANES, LANES), F32)],
        compiler_params=_cparams(3, SMALL_CALL_VMEM_BYTES),
        name="rwkv7_chunked",
    )(pfull, pfull, pfull, pfull, shift0["r"], shift0["k"], shift0["v"], shift0["small"], wkv0,
      w["mu_r"], w["mu_k"], w["mu_v"], w["mu_small"], w["w_decay_up"], w["decay_bias"], w["w_a_up"], w["a_bias"],
      w["w_gate_up"], w["k_k"], w["k_a"], w["r_k"], w["lnx_w"], w["lnx_b"])
    return o, wkv1


def _rope_kernel(q_ref, k_ref, qi_ref, kiw_ref, ca_ref, sa_ref, cb_ref, sb1_ref, sb2_ref,
                 qo_ref, ko_ref, qio_ref, kiwo_ref):
    ca, sa = ca_ref[...], sa_ref[...]
    cb, sb1, sb2 = cb_ref[...], sb1_ref[...], sb2_ref[...]

    def rope_head(x):
        return x * ca + pltpu.roll(x, SA_HEAD // 2, axis=1) * sa

    def rope_idx(x):
        return (x * cb + pltpu.roll(x, LANES - IDX_DIM // 2, axis=1) * sb1
                + pltpu.roll(x, IDX_DIM // 2, axis=1) * sb2)

    q_scale = float(SA_HEAD ** -0.5 * np.log2(np.e))
    for h in range(q_ref.shape[1] // LANES):
        sl = slice(h * LANES, (h + 1) * LANES)
        qo_ref[:, sl] = (rope_head(q_ref[:, sl]) * q_scale).astype(qo_ref.dtype)
    for h in range(k_ref.shape[1] // LANES):
        sl = slice(h * LANES, (h + 1) * LANES)
        ko_ref[:, sl] = rope_head(k_ref[:, sl])
    for hp in range(qi_ref.shape[1] // LANES):
        sl = slice(hp * LANES, (hp + 1) * LANES)
        qio_ref[:, sl] = rope_idx(qi_ref[:, sl]).astype(qio_ref.dtype)
    x = kiw_ref[...]
    lane = lax.broadcasted_iota(jnp.int32, (1, LANES), 1)
    kiwo_ref[...] = jnp.where(lane < IDX_DIM, rope_idx(x), x * float((IDX_HEADS * IDX_DIM) ** -0.5))


def _rope_tables(pos):
    def ang(half):
        inv = ROPE_THETA ** (-jnp.arange(half, dtype=F32) / half)
        return pos.astype(F32)[:, None] * inv[None, :]
    aa = ang(SA_HEAD // 2)
    ca = jnp.concatenate([jnp.cos(aa), jnp.cos(aa)], axis=1)
    sa = jnp.concatenate([-jnp.sin(aa), jnp.sin(aa)], axis=1)
    ab = ang(IDX_DIM // 2)
    z = jnp.zeros_like(ab)
    cb = jnp.concatenate([jnp.cos(ab)] * 4, axis=1)
    sb1 = jnp.concatenate([-jnp.sin(ab), z, -jnp.sin(ab), z], axis=1)
    sb2 = jnp.concatenate([z, jnp.sin(ab), z, jnp.sin(ab)], axis=1)
    return ca, sa, cb, sb1, sb2


def _rope(pfull, cols, tables, n_streams, t, tm, sa_w, kv_w, qi_w):
    m = pfull.shape[0]
    bps = t // tm
    tab = pl.BlockSpec((tm, LANES), lambda r: (r % bps, 0))
    return pl.pallas_call(
        _rope_kernel,
        grid=(m // tm,),
        in_specs=[
            pl.BlockSpec((tm, sa_w), lambda r: (r, cols["q"])),
            pl.BlockSpec((tm, kv_w), lambda r: (r, cols["ksa"])),
            pl.BlockSpec((tm, qi_w), lambda r: (r, cols["qi"])),
            pl.BlockSpec((tm, LANES), lambda r: (r, cols["kiw"])),
            tab, tab, tab, tab, tab,
        ],
        out_specs=[
            pl.BlockSpec((tm, sa_w), lambda r: (r, 0)),
            pl.BlockSpec((tm, kv_w), lambda r: (r, 0)),
            pl.BlockSpec((tm, qi_w), lambda r: (r, 0)),
            pl.BlockSpec((tm, LANES), lambda r: (r, 0)),
        ],
        out_shape=[
            jax.ShapeDtypeStruct((m, sa_w), MXU_DTYPE),
            jax.ShapeDtypeStruct((m, kv_w), F32),
            jax.ShapeDtypeStruct((m, qi_w), MXU_DTYPE),
            jax.ShapeDtypeStruct((m, LANES), F32),
        ],
        compiler_params=_cparams(1, SMALL_CALL_VMEM_BYTES),
        name="rope",
    )(pfull, pfull, pfull, pfull, *tables)


IDX_HEAD_GROUP = 4
IDX_SCORE_ROWS = 512
IDX_VALUE_PASSES = 16


FOLD_CHAINS = 2


def _fold_sublane_tiles(x, op):
    parts = [x[r:r + 8] for r in range(0, x.shape[0], 8)]
    chains = [functools.reduce(op, parts[c::FOLD_CHAINS]) for c in range(min(FOLD_CHAINS, len(parts)))]
    return functools.reduce(op, chains)


def _f32_to_key(x):
    bits = lax.bitcast_convert_type(x, jnp.int32)
    return bits ^ ((bits >> 31) & jnp.int32(0x7FFFFFFF))


def _key_to_f32(key):
    return lax.bitcast_convert_type(key ^ ((key >> 31) & jnp.int32(0x7FFFFFFF)), F32)


def _index_kernel(qit_ref, w_ref, ki_ref, mask_ref, keys_ref, *, tq, kb, nkb, past, topk):
    i = pl.program_id(1)
    n_adm = jnp.minimum(nkb, (past + (i + 1) * tq + kb - 1) // kb)
    chunk_bits = CHUNK.bit_length() - 1
    qpos = past + i * tq + lax.broadcasted_iota(jnp.int32, (1, tq), 1)
    lim = ((qpos >> chunk_bits) + 1) << chunk_bits
    krow = lax.broadcasted_iota(jnp.int32, (kb, 1), 0)
    hg = IDX_HEAD_GROUP
    fold = _fold_sublane_tiles

    def score_block(j, carry):
        smax, smin = carry
        for r0 in range(0, kb, IDX_SCORE_ROWS):
            rows = slice(r0, r0 + IDX_SCORE_ROWS)
            kblk = ki_ref[j, rows, :]
            acc = jnp.zeros((IDX_SCORE_ROWS, tq), F32)
            for g0 in range(0, IDX_HEADS, hg):
                s = jnp.dot(kblk, qit_ref[:, g0 * tq:(g0 + hg) * tq], preferred_element_type=F32)
                for g in range(hg):
                    acc = acc + jnp.maximum(s[:, g * tq:(g + 1) * tq], 0.0) * w_ref[g0 + g:g0 + g + 1, :]
            score = acc + 0.0
            adm = j * kb + r0 + krow[:IDX_SCORE_ROWS] < lim
            keys_ref[j, rows, :] = jnp.where(adm, _f32_to_key(score), jnp.int32(INT_MIN))
            smax = jnp.maximum(smax, fold(jnp.where(adm, score, -jnp.inf), jnp.maximum))
            smin = jnp.minimum(smin, fold(jnp.where(adm, score, jnp.inf), jnp.minimum))
        return smax, smin

    smax, smin = lax.fori_loop(0, n_adm, score_block,
                               (jnp.full((8, tq), -jnp.inf, F32), jnp.full((8, tq), jnp.inf, F32)))
    lo0 = _f32_to_key(jnp.min(smin, axis=0, keepdims=True))
    hi0 = _f32_to_key(jnp.max(smax, axis=0, keepdims=True)) + 1

    def count_ge(thr):
        def body(j, cnt):
            return cnt + fold(jnp.where(keys_ref[j] >= thr, 1.0, 0.0), jnp.add)
        return jnp.sum(lax.fori_loop(0, n_adm, body, jnp.zeros((8, tq), F32)), axis=0, keepdims=True)

    def midpoint(lo, hi):
        return (lo >> 1) + (hi >> 1) + (lo & hi & 1)

    def unfinished(lo, hi, cnt_lo):
        active = (midpoint(lo, hi) != lo) & (cnt_lo != float(topk))
        return jnp.max(jnp.where(active, 1.0, 0.0))

    def bisect(state):
        p, lo, hi, cnt_lo, cnt_hi, _ = state
        mid_v = _f32_to_key(0.5 * _key_to_f32(lo) + 0.5 * _key_to_f32(hi))
        by_value = (p < IDX_VALUE_PASSES) & (mid_v > lo) & (mid_v < hi)
        mid = jnp.where(by_value, mid_v, midpoint(lo, hi))
        cnt = count_ge(mid)
        ge = cnt >= float(topk)
        lo, cnt_lo = jnp.where(ge, mid, lo), jnp.where(ge, cnt, cnt_lo)
        hi, cnt_hi = jnp.where(ge, hi, mid), jnp.where(ge, cnt_hi, cnt)
        return p + 1, lo, hi, cnt_lo, cnt_hi, unfinished(lo, hi, cnt_lo)

    cnt_lo0 = lim.astype(F32)
    state = (jnp.int32(0), lo0, hi0, cnt_lo0, jnp.zeros((1, tq), F32), unfinished(lo0, hi0, cnt_lo0))
    _, thr, _, n_ge, n_gt, _ = lax.while_loop(lambda st: st[5] > 0.5, bisect, state)

    tied = n_ge > float(topk)
    need = float(topk) - n_gt
    end_all = jnp.int32(nkb * kb)

    def count_tied_below(bound):
        def body(j, cnt):
            hit = (keys_ref[j] == thr) & (j * kb + krow < bound)
            return cnt + fold(jnp.where(hit, 1.0, 0.0), jnp.add)
        return jnp.sum(lax.fori_loop(0, n_adm, body, jnp.zeros((8, tq), F32)), axis=0, keepdims=True)

    def tie_bisect(state):
        p, lo, hi = state
        mid = (lo + hi) >> 1
        enough = count_tied_below(mid) >= need
        return p + 1, jnp.where(enough, lo, mid), jnp.where(enough, mid, hi)

    any_tied = jnp.max(jnp.where(tied, 1.0, 0.0))
    n_steps = (nkb * kb).bit_length()
    tie_state = (jnp.int32(0), jnp.zeros((1, tq), jnp.int32), jnp.full((1, tq), end_all, jnp.int32))
    tie_hi = lax.while_loop(lambda st: (any_tied > 0.5) & (st[0] < n_steps), tie_bisect, tie_state)[2]
    tie_end = jnp.where(tied, tie_hi, end_all)

    def write(j, carry):
        k = keys_ref[j]
        keep = (k > thr) | ((k == thr) & (j * kb + krow < tie_end))
        mask_ref[j] = jnp.where(keep, 0.0, NEG_BIG).T.astype(mask_ref.dtype)
        return carry

    lax.fori_loop(0, n_adm, write, 0)

    def clear(j, carry):
        mask_ref[j] = jnp.full((tq, kb), NEG_BIG, mask_ref.dtype)
        return carry

    lax.fori_loop(n_adm, nkb, clear, 0)


def _index_mask(qit, w, ki, n_streams, t, tq, kb, past, topk):
    nkb = ki.shape[1]
    nq = t // tq
    isz = jnp.dtype(MXU_DTYPE).itemsize
    vmem = (nkb * kb * tq * 4 + 4 * nkb * kb * tq + 2 * nkb * kb * LANES * isz + 2 * IDX_DIM * IDX_HEADS * tq * isz
            + 6 * kb * IDX_HEAD_GROUP * tq * 4 + (8 << 20))
    return pl.pallas_call(
        functools.partial(_index_kernel, tq=tq, kb=kb, nkb=nkb, past=past, topk=topk),
        grid=(n_streams, nq),
        in_specs=[
            pl.BlockSpec((None, None, IDX_DIM, IDX_HEADS * tq), lambda b, i: (b, i, 0, 0)),
            pl.BlockSpec((None, IDX_HEADS, tq), lambda b, i: (b, 0, i)),
            pl.BlockSpec((None, nkb, kb, IDX_DIM), lambda b, i: (b, 0, 0, 0)),
        ],
        out_specs=pl.BlockSpec((None, nkb, tq, kb), lambda b, i: (b, 0, i, 0)),
        out_shape=jax.ShapeDtypeStruct((n_streams, nkb, t, kb), jnp.bfloat16),
        scratch_shapes=[pltpu.VMEM((nkb, kb, tq), jnp.int32)],
        compiler_params=_cparams(2, vmem),
        name="indexer_topk_mask",
    )(qit, w, ki)


def _attn_kernel(it_ref, jt_ref, last_ref, q_ref, k_ref, v_ref, m_ref, o_ref, qs_ref, mx_ref, l_ref, acc_ref,
                 *, group):
    step = pl.program_id(1)
    tq = q_ref.shape[0]
    tk = k_ref.shape[0]
    n_kv = k_ref.shape[1] // SA_HEAD
    rows = group * tq
    n_lane_tiles = tk // LANES

    @pl.when(jt_ref[step] == 0)
    def _():
        for n in range(n_kv):
            for g in range(group):
                h = n * group + g
                qs_ref[n, g * tq:(g + 1) * tq, :] = q_ref[:, h * SA_HEAD:(h + 1) * SA_HEAD]
        mx_ref[...] = jnp.full(mx_ref.shape, NEG_BIG, F32)
        l_ref[...] = jnp.zeros(l_ref.shape, F32)
        acc_ref[...] = jnp.zeros(acc_ref.shape, F32)

    bias = m_ref[...].astype(F32)
    kv = range(n_kv)
    s = [_nt(qs_ref[n], k_ref[:, n * SA_HEAD:(n + 1) * SA_HEAD]) for n in kv]
    s = [(x.reshape(group, tq, tk) + bias[None]).reshape(rows, tk) for x in s]
    tiles = [[x[:, c * LANES:(c + 1) * LANES] for c in range(n_lane_tiles)] for x in s]
    m_old = [mx_ref[n] for n in kv]
    m_new = [jnp.maximum(mo, jnp.max(functools.reduce(jnp.maximum, t), axis=1, keepdims=True))
             for mo, t in zip(m_old, tiles)]
    alpha = [jnp.exp2(mo - mn) for mo, mn in zip(m_old, m_new)]
    p = [[jnp.exp2(x - mn) for x in t] for t, mn in zip(tiles, m_new)]
    for n in kv:
        mx_ref[n] = m_new[n]
        l_ref[n] = alpha[n] * l_ref[n] + functools.reduce(jnp.add, p[n])
    pv = [jnp.dot(_mx(jnp.concatenate(p[n], axis=1)), v_ref[:, n * SA_HEAD:(n + 1) * SA_HEAD],
                  preferred_element_type=F32) for n in kv]
    for n in kv:
        acc_ref[n] = alpha[n] * acc_ref[n] + pv[n]

    @pl.when(last_ref[step] == 1)
    def _():
        for n in range(n_kv):
            o = acc_ref[n] / jnp.sum(l_ref[n], axis=1, keepdims=True)
            for g in range(group):
                h = n * group + g
                o_ref[:, h * SA_HEAD:(h + 1) * SA_HEAD] = o[g * tq:(g + 1) * tq].astype(o_ref.dtype)


def _attn_tiles(t, tq, tk, nkb, past):
    ii, jj, last = [], [], []
    for i in range(t // tq):
        n_adm = min(nkb, -(-(past + (i + 1) * tq) // tk))
        for j in range(n_adm):
            ii.append(i)
            jj.append(j)
            last.append(int(j == n_adm - 1))
    return tuple(jnp.asarray(np.array(x, np.int32)) for x in (ii, jj, last))


def _attention(q, k_all, v_all, mask, n_streams, t, tq, tk, past):
    m, sa_w = q.shape
    kv_w = k_all.shape[2]
    nkb = mask.shape[1]
    nq = t // tq
    n_heads = sa_w // SA_HEAD
    group = n_heads // SA_KV_HEADS
    it, jt, last = _attn_tiles(t, tq, tk, nkb, past)
    grid_spec = pltpu.PrefetchScalarGridSpec(
        num_scalar_prefetch=3,
        grid=(n_streams, int(it.shape[0])),
        in_specs=[
            pl.BlockSpec((tq, sa_w), lambda b, s, it, jt, lt: (b * nq + it[s], 0)),
            pl.BlockSpec((None, tk, kv_w), lambda b, s, it, jt, lt: (b, jt[s], 0)),
            pl.BlockSpec((None, tk, kv_w), lambda b, s, it, jt, lt: (b, jt[s], 0)),
            pl.BlockSpec((None, None, tq, tk), lambda b, s, it, jt, lt: (b, jt[s], it[s], 0)),
        ],
        out_specs=pl.BlockSpec((tq, sa_w), lambda b, s, it, jt, lt: (b * nq + it[s], 0)),
        scratch_shapes=[pltpu.VMEM((SA_KV_HEADS, group * tq, SA_HEAD), MXU_DTYPE)]
        + [pltpu.VMEM((SA_KV_HEADS, group * tq, SA_HEAD), F32)] * 3,
    )
    return pl.pallas_call(
        functools.partial(_attn_kernel, group=group),
        grid_spec=grid_spec,
        out_shape=jax.ShapeDtypeStruct((m, sa_w), MXU_DTYPE),
        compiler_params=_cparams(2, ATTN_VMEM_BYTES),
        name="masked_flash_attention",
    )(it, jt, last, q, k_all, v_all, mask)


def _prepare_weights(w_in, mu_shift, w_decay_up, decay_bias, w_a_up, a_bias, w_gate_up, k_k, k_a, r_k,
                     lnx_w, lnx_b, w_out, w_up, conv_w, conv_b, w_down, d_model):
    rw = w_decay_up.shape[1]
    rw_cols = 3 * rw + D_DECAY + D_AAA + D_GATE
    sa_w = d_model - rw
    kv_w = SA_KV_HEADS * SA_HEAD
    qi_w = IDX_HEADS * IDX_DIM
    o = {}
    o["r"] = 0
    o["wd"] = rw
    o["k"] = rw + D_DECAY
    o["v"] = 2 * rw + D_DECAY
    o["ad"] = 3 * rw + D_DECAY
    o["gd"] = 3 * rw + D_DECAY + D_AAA
    o["q"] = rw_cols
    o["ksa"] = rw_cols + sa_w
    o["vsa"] = o["ksa"] + kv_w
    o["qi"] = o["vsa"] + kv_w
    o["kiw"] = o["qi"] + qi_w
    kiw_w = IDX_DIM + IDX_HEADS

    def seg(x, name, width, pad_to=None):
        s = x[..., o[name]:o[name] + width]
        if pad_to is not None and pad_to > width:
            s = jnp.pad(s, [(0, 0)] * (s.ndim - 1) + [(0, pad_to - width)])
        return s

    def rw_small(x):
        return jnp.concatenate([seg(x, "wd", D_DECAY, LANES), seg(x, "ad", D_AAA, LANES), seg(x, "gd", D_GATE)], -1)

    w_in_l = jnp.concatenate([
        seg(w_in, "r", rw), seg(w_in, "k", rw), seg(w_in, "v", rw), seg(w_in, "q", sa_w), seg(w_in, "qi", qi_w),
        seg(w_in, "ksa", kv_w), seg(w_in, "vsa", kv_w), rw_small(w_in), seg(w_in, "kiw", kiw_w, LANES)], axis=1)
    w_in_l = jnp.pad(w_in_l, ((0, 0), (0, -w_in_l.shape[1] % IN_PROJ_TN)))
    starts = {"r": 0, "k": rw, "v": 2 * rw, "q": 3 * rw, "qi": 3 * rw + sa_w}
    starts["ksa"] = starts["qi"] + qi_w
    starts["vsa"] = starts["ksa"] + kv_w
    starts["small"] = starts["vsa"] + kv_w
    starts["kiw"] = starts["small"] + 4 * LANES
    pad_rows = lambda x, n: jnp.pad(x, ((0, n - x.shape[0]), (0, 0)))
    row = lambda x: x.reshape(1, -1).astype(F32)
    mu = mu_shift.reshape(1, -1)
    w = {
        "w_in": _mx(w_in_l),
        "mu_r": seg(mu, "r", rw), "mu_k": seg(mu, "k", rw), "mu_v": seg(mu, "v", rw), "mu_small": rw_small(mu),
        "w_decay_up": _mx(pad_rows(w_decay_up, LANES)), "decay_bias": row(decay_bias),
        "w_a_up": _mx(pad_rows(w_a_up, LANES)), "a_bias": row(a_bias),
        "w_gate_up": _mx(w_gate_up),
        "k_k": row(k_k), "k_a": row(k_a), "r_k": row(r_k), "lnx_w": row(lnx_w), "lnx_b": row(lnx_b),
        "w_out": _mx(w_out), "w_up": _mx(w_up), "conv_w": conv_w, "conv_b": row(conv_b), "w_down": _mx(w_down),
    }
    dims = {"rw": rw, "rw_cols": rw_cols, "sa_w": sa_w, "kv_w": kv_w, "qi_w": qi_w, "orig": o, "starts": starts,
            "seg": seg, "rw_small": rw_small}
    return w, dims


def _layer(x, past_k, past_v, past_ki, wkv0, shift0, conv0, norms, w, dims, tiles):
    bsz, t, d = x.shape
    past = past_k.shape[1]
    m = bsz * t
    rw, sa_w, kv_w, qi_w = dims["rw"], dims["sa_w"], dims["kv_w"], dims["qi_w"]
    starts, seg, rw_small = dims["starts"], dims["seg"], dims["rw_small"]
    assert t % CHUNK == 0 and past % CHUNK == 0
    tm, tq, kb = tiles["tm"], tiles["tq"], tiles["kb"]
    g_mix_pre, g_mix_post, g_ffn_pre, g_ffn_post = norms

    x2 = x.reshape(m, d)
    h1 = _rmsnorm(x2, g_mix_pre, tiles["tm_norm"])
    pfull = _matmul(h1, w["w_in"], tiles["tm_in"], tiles["tn_in"])

    sh = shift0.reshape(bsz, 1, -1)
    shift_l = {"r": seg(sh, "r", rw), "k": seg(sh, "k", rw), "v": seg(sh, "v", rw), "small": rw_small(sh)}
    cols_rw = {n: starts[n] // RW_GROUP for n in ("r", "k", "v")}
    cols_rw["small"] = starts["small"] // RW_SMALL
    o_rw, wkv1 = _rwkv(pfull, cols_rw, shift_l, wkv0, w, bsz, t)

    pos = past + jnp.arange(t, dtype=jnp.int32)
    cols_sa = {"q": starts["q"] // sa_w, "ksa": starts["ksa"] // kv_w, "qi": starts["qi"] // qi_w,
               "kiw": starts["kiw"] // LANES}
    q_r, k_r, qi_r, kiw_r = _rope(pfull, cols_sa, _rope_tables(pos), bsz, t, tq, sa_w, kv_w, qi_w)
    k_new = k_r.reshape(bsz, t, kv_w)
    v_new = pfull[:, starts["vsa"]:starts["vsa"] + kv_w].reshape(bsz, t, kv_w)
    ki_new = kiw_r[:, :IDX_DIM].reshape(bsz, t, IDX_DIM)
    l_all = past + t
    lp = -(-l_all // kb) * kb
    nkb = lp // kb
    cat = lambda old, new: jnp.pad(
        jnp.concatenate([_mx(old.reshape(bsz, past, new.shape[-1])), _mx(new)], axis=1),
        ((0, 0), (0, lp - l_all), (0, 0)))
    k_all, v_all, ki_all = cat(past_k, k_new), cat(past_v, v_new), cat(past_ki, ki_new)
    nq = t // tq
    qit = (qi_r.reshape(bsz, nq, tq, IDX_HEADS, IDX_DIM).transpose(0, 1, 4, 3, 2)
           .reshape(bsz, nq, IDX_DIM, IDX_HEADS * tq))
    w_idx = kiw_r[:, IDX_DIM:IDX_DIM + IDX_HEADS].reshape(bsz, t, IDX_HEADS).transpose(0, 2, 1)
    topk = min(TOPK_MAX, l_all // 4)
    mask = _index_mask(qit, w_idx, ki_all.reshape(bsz, nkb, kb, IDX_DIM), bsz, t, tq, kb, past, topk)
    o_sa = _attention(q_r, k_all, v_all, mask, bsz, t, tq, kb, past)

    x1 = _matmul_norm_residual([o_rw, o_sa], w["w_out"], x2, g_mix_post, tm, tiles["tn_out"], "out_proj")
    h2 = _rmsnorm(x1, g_ffn_pre, tiles["tm_norm"])
    act, conv1 = _ffn_up(h2, w["w_up"], conv0, w["conv_w"], w["conv_b"], bsz, tiles["tm_up"], tiles["tn_up"])
    x_out = _matmul_norm_residual([act], w["w_down"], x1, g_ffn_post, tm, tiles["tn_out"], "ffn_down")

    last = pfull.reshape(bsz, t, -1)[:, t - 1:t]
    sm = last[..., starts["small"]:starts["small"] + 4 * LANES]
    shift1 = jnp.concatenate([
        last[..., starts["r"]:starts["r"] + rw], sm[..., :D_DECAY],
        last[..., starts["k"]:starts["k"] + rw], last[..., starts["v"]:starts["v"] + rw],
        sm[..., LANES:LANES + D_AAA], sm[..., 2 * LANES:2 * LANES + D_GATE]], axis=-1)
    new = (k_new.reshape(bsz, t, SA_KV_HEADS, SA_HEAD), v_new.reshape(bsz, t, SA_KV_HEADS, SA_HEAD), ki_new,
           wkv1, shift1, conv1)
    return x_out.reshape(bsz, t, d), new


def _tiles(bsz, t):
    big = t >= 1024
    return {
        "tm_norm": 256 if big else CHUNK,
        "tm_in": 1024 if big else bsz * t,
        "tn_in": IN_PROJ_TN,
        "tm": 512 if big else bsz * t,
        "tn_out": 512,
        "tm_up": 1024 if big else bsz * t,
        "tn_up": 512,
        "tq": 256 if big else CHUNK,
        "kb": 512,
    }


def _run_stream(x, past_k, past_v, past_ki, wkv0, shift0, conv0, norm_w, weights, dims, depth):
    outs = []
    tiles = _tiles(x.shape[0], x.shape[1])
    for l in range(depth):
        w_l = {n: v[l] for n, v in weights.items()}
        norms = tuple(g[l].reshape(1, -1) for g in norm_w)
        x, st = _layer(x, past_k[l], past_v[l], past_ki[l], wkv0[l], shift0[l], conv0[l], norms, w_l, dims[l], tiles)
        outs.append(st)
    return x, [jnp.stack(s) for s in zip(*outs)]


def kernel(x_prompt, x_sample, cache_k, cache_v, cache_kidx, state_wkv, state_shift, state_conv, norm_mix_pre, norm_mix_post, norm_ffn_pre, norm_ffn_post, w_in, mu_shift, w_decay_up, decay_bias, w_a_up, a_bias, w_gate_up, k_k, k_a, r_k, lnx_w, lnx_b, w_out, w_up, conv_w, conv_b, w_down):
    depth, d_model = norm_mix_pre.shape
    per_layer = [_prepare_weights(w_in[l], mu_shift[l], w_decay_up[l], decay_bias[l], w_a_up[l], a_bias[l],
                                  w_gate_up[l], k_k[l], k_a[l], r_k[l], lnx_w[l], lnx_b[l], w_out[l], w_up[l],
                                  conv_w[l], conv_b[l], w_down[l], d_model) for l in range(depth)]
    weights = {n: [pw[0][n] for pw in per_layer] for n in per_layer[0][0]}
    dims = [pw[1] for pw in per_layer]
    norm_w = (norm_mix_pre, norm_mix_post, norm_ffn_pre, norm_ffn_post)

    dt = x_prompt.dtype
    bp = x_prompt.shape[0]
    rw_heads = state_wkv.shape[2]
    zk = jnp.zeros((depth, bp, 0, SA_KV_HEADS, SA_HEAD), dt)
    zki = jnp.zeros((depth, bp, 0, IDX_DIM), dt)
    zwkv = jnp.zeros((depth, bp, rw_heads, RW_HEAD, RW_HEAD), dt)
    zshift = jnp.zeros((depth, bp, 1, state_shift.shape[-1]), dt)
    zconv = jnp.zeros((depth, bp, CONV_W - 1, state_conv.shape[-1]), dt)
    y_prompt, p_new = _run_stream(x_prompt, zk, zk, zki, zwkv, zshift, zconv, norm_w, weights, dims, depth)
    y_sample, s_new = _run_stream(x_sample, cache_k, cache_v, cache_kidx, state_wkv, state_shift, state_conv,
                                  norm_w, weights, dims, depth)
    return (y_prompt, y_sample, *p_new, *s_new)
```
